```python
import math, functools
import jax, jax.numpy as jnp
from jax import lax
import numpy as np

D_MODEL = 2048
BATCH = 2
SEQ = 4096
DEPTH = 4
DEC_BATCH = 8
DEC_SEQ = 8
PAST_LEN = 16384
PAGE_SIZE = 128

N_MIXERS = 3
N_LAYERS_A = len(range(0, DEPTH, N_MIXERS))
N_LAYERS_B = len(range(1, DEPTH, N_MIXERS))
N_LAYERS_C = len(range(2, DEPTH, N_MIXERS))
N_ADA = 9
D_FF = ((8 * D_MODEL // 3 + 255) // 256) * 256
CONV_W = 4
NORM_EPS = 1e-6
NEG_INF = -1e30

GDN_DK = 128
GDN_DV = 128
GDN_HK = D_MODEL // 128
GDN_HV = 2 * GDN_HK
GDN_KEY_DIM = GDN_HK * GDN_DK
GDN_VAL_DIM = GDN_HV * GDN_DV
GDN_CONV_DIM = 2 * GDN_KEY_DIM + GDN_VAL_DIM
GDN_PROJ_DIM = GDN_CONV_DIM + GDN_VAL_DIM + 2 * GDN_HV
GDN_CHUNK = 64

ATT_HEAD_DIM = 128
ATT_HEADS = D_MODEL // ATT_HEAD_DIM
ATT_KV_HEADS = ATT_HEADS // 4
IDX_HEADS = 16
IDX_DIM = 128
TOPK_MAX = 256
Q_BLOCK = 128
ROPE_THETA = 10000.0
DSA_SPLITS = (ATT_HEADS * ATT_HEAD_DIM, ATT_KV_HEADS * ATT_HEAD_DIM, ATT_KV_HEADS * ATT_HEAD_DIM, IDX_HEADS * IDX_DIM, IDX_DIM, IDX_HEADS)
DSA_PROJ_DIM = sum(DSA_SPLITS)
IDX_SCALE = (IDX_DIM * IDX_HEADS) ** -0.5

LRU_WIDTH = D_MODEL
LRU_BLOCK = 256
LRU_BLOCKS = LRU_WIDTH // LRU_BLOCK
RG_C = 8.0

kernel_name = 'hybrid_gdn_dsa_rglru_step'


def rmsnorm(x, w):
    x32 = x.astype(jnp.float32)
    y = x32 * lax.rsqrt(jnp.mean(x32 * x32, axis=-1, keepdims=True) + NORM_EPS)
    return (y * w.astype(jnp.float32)).astype(x.dtype)


def l2norm(x):
    x32 = x.astype(jnp.float32)
    return x32 * lax.rsqrt(jnp.sum(x32 * x32, axis=-1, keepdims=True) + NORM_EPS)


def modulate(h, shift, scale):
    return h * (1 + scale) + shift


def swiglu(h, w_in, w_out):
    a, b = jnp.split(h @ w_in, 2, axis=-1)
    return (jax.nn.silu(a) * b) @ w_out


def causal_dwconv(full, w):
    t = full.shape[1] - (CONV_W - 1)
    return sum(full[:, j:j + t] * w[j] for j in range(CONV_W))


def rope(x, pos):
    half = x.shape[-1] // 2
    inv_freq = ROPE_THETA ** (-jnp.arange(half, dtype=jnp.float32) / half)
    ang = pos.astype(jnp.float32)[:, None] * inv_freq[None, :]
    cos = jnp.cos(ang)[None, :, None, :]
    sin = jnp.sin(ang)[None, :, None, :]
    x32 = x.astype(jnp.float32)
    x1, x2 = x32[..., :half], x32[..., half:]
    return jnp.concatenate([x1 * cos - x2 * sin, x2 * cos + x1 * sin], axis=-1).astype(x.dtype)


def macaron_layer(x, c, w_ada_l, b_ada_l, norm_l, ffn_in_l, ffn_out_l, mixer):
    mod = jax.nn.silu(c) @ w_ada_l + b_ada_l
    sh1, sc1, g1, sh2, sc2, g2, sh3, sc3, g3 = jnp.split(mod[:, None, :], N_ADA, axis=-1)
    x = x + 0.5 * g1 * swiglu(modulate(rmsnorm(x, norm_l[0]), sh1, sc1), ffn_in_l[0], ffn_out_l[0])
    mixed, state = mixer(modulate(rmsnorm(x, norm_l[1]), sh2, sc2))
    x = x + g2 * mixed
    x = x + 0.5 * g3 * swiglu(modulate(rmsnorm(x, norm_l[2]), sh3, sc3), ffn_in_l[1], ffn_out_l[1])
    return x, state


def final_norm(x, c, w_f, b_f, g):
    sh, sc = jnp.split((jax.nn.silu(c) @ w_f + b_f)[:, None, :], 2, axis=-1)
    return modulate(rmsnorm(x, g), sh, sc)


def chunk_gated_delta(q, k, v, g, beta, s0):
    bsz, t, h, dk = q.shape
    c = min(GDN_CHUNK, t)
    n = -(-t // c)
    pad = n * c - t

    def to_chunks(a):
        a = jnp.pad(a.astype(jnp.float32), [(0, 0), (0, pad)] + [(0, 0)] * (a.ndim - 2))
        a = a.reshape((bsz, n, c) + a.shape[2:])
        return jnp.moveaxis(a, (1, 3), (0, 2))

    q, k, v, g, beta = (to_chunks(a) for a in (q, k, v, g, beta))
    q = q * dk ** -0.5
    gc = jnp.cumsum(g, axis=-1)
    pos = jnp.arange(c)
    causal = pos[:, None] >= pos[None, :]
    strict = pos[:, None] > pos[None, :]
    decay = jnp.exp(jnp.where(causal, gc[..., :, None] - gc[..., None, :], -jnp.inf))
    kb = k * beta[..., None]
    lower = jnp.where(strict, jnp.einsum('nbhid,nbhjd->nbhij', kb, k) * decay, 0.0)
    tmat = lower + jnp.eye(c, dtype=jnp.float32)
    solve = functools.partial(lax.linalg.triangular_solve, left_side=True, lower=True, unit_diagonal=True)
    u_base = solve(tmat, v * beta[..., None])
    w_dec = solve(tmat, kb * jnp.exp(gc)[..., None])
    qk = jnp.einsum('nbhid,nbhjd->nbhij', q, k) * decay
    qg = q * jnp.exp(gc)[..., None]
    g_last = gc[..., -1:]
    kd = k * jnp.exp(g_last - gc)[..., None]
    g_tot = jnp.exp(g_last)[..., None]

    def step(s, xs):
        u_b, w_d, qk_c, qg_c, kd_c, gt_c = xs
        u = u_b - jnp.einsum('bhik,bhkv->bhiv', w_d, s)
        o = jnp.einsum('bhik,bhkv->bhiv', qg_c, s) + jnp.einsum('bhij,bhjv->bhiv', qk_c, u)
        s = s * gt_c + jnp.einsum('bhjk,bhjv->bhkv', kd_c, u)
        return s, o

    s, o = lax.scan(step, s0.astype(jnp.float32), (u_base, w_dec, qk, qg, kd, g_tot))
    o = jnp.moveaxis(o, (0, 2), (1, 3)).reshape(bsz, n * c, h, -1)[:, :t]
    return o, s


def gdn_mixer(h, conv_buf, s0, w_in, conv_w, a_log, dt_bias, norm_w, w_out):
    bsz, t, _ = h.shape
    cuts = [GDN_CONV_DIM, GDN_CONV_DIM + GDN_VAL_DIM, GDN_CONV_DIM + GDN_VAL_DIM + GDN_HV]
    qkv, z, b, a = jnp.split(h @ w_in, cuts, axis=-1)
    full = jnp.concatenate([conv_buf.astype(qkv.dtype), qkv], axis=1)
    new_buf = full[:, -(CONV_W - 1):]
    qkv = jax.nn.silu(causal_dwconv(full, conv_w))
    q, k, v = jnp.split(qkv, [GDN_KEY_DIM, 2 * GDN_KEY_DIM], axis=-1)
    rep = GDN_HV // GDN_HK
    q = jnp.repeat(l2norm(q.reshape(bsz, t, GDN_HK, GDN_DK)), rep, axis=2)
    k = jnp.repeat(l2norm(k.reshape(bsz, t, GDN_HK, GDN_DK)), rep, axis=2)
    v = v.reshape(bsz, t, GDN_HV, GDN_DV)
    beta = jax.nn.sigmoid(b.astype(jnp.float32))
    g = -jnp.exp(a_log.astype(jnp.float32)) * jax.nn.softplus(a.astype(jnp.float32) + dt_bias.astype(jnp.float32))
    o, s = chunk_gated_delta(q, k, v, g, beta, s0)
    o = rmsnorm(o, norm_w) * jax.nn.silu(z.astype(jnp.float32).reshape(bsz, t, GDN_HV, GDN_DV))
    out = o.reshape(bsz, t, GDN_VAL_DIM).astype(h.dtype) @ w_out
    return out, (new_buf, s.astype(s0.dtype))


def dsa_project(h, pos, w_in):
    bsz, t, _ = h.shape
    q, k, v, qi, ki, wi = jnp.split(h @ w_in, list(np.cumsum(DSA_SPLITS)[:-1]), axis=-1)
    q = rope(q.reshape(bsz, t, ATT_HEADS, ATT_HEAD_DIM), pos)
    k = rope(k.reshape(bsz, t, ATT_KV_HEADS, ATT_HEAD_DIM), pos)
    v = v.reshape(bsz, t, ATT_KV_HEADS, ATT_HEAD_DIM)
    qi = rope(qi.reshape(bsz, t, IDX_HEADS, IDX_DIM), pos)
    ki = rope(ki.reshape(bsz, t, 1, IDX_DIM), pos)[:, :, 0]
    return q, k, v, qi, ki, wi


def indexer_topk(qi, wi, ki, qpos, topk):
    logits = jnp.einsum('bthd,bsd->bths', qi.astype(jnp.float32), ki.astype(jnp.float32))
    score = jnp.einsum('bths,bth->bts', jax.nn.relu(logits), wi.astype(jnp.float32)) * IDX_SCALE
    visible = jnp.arange(ki.shape[1])[None, :] <= qpos[:, None]
    score = jnp.where(visible[None], score, NEG_INF)
    _, idx = lax.top_k(score, topk)
    return idx, idx <= qpos[None, :, None]


def sparse_attend(q, k_sel, v_sel, valid):
    bsz, tq = q.shape[:2]
    qg = q.reshape(bsz, tq, ATT_KV_HEADS, ATT_HEADS // ATT_KV_HEADS, ATT_HEAD_DIM).astype(jnp.float32)
    s = jnp.einsum('btkrd,btjkd->btkrj', qg, k_sel.astype(jnp.float32)) * ATT_HEAD_DIM ** -0.5
    s = jnp.where(valid[:, :, None, None, :], s, NEG_INF)
    p = jax.nn.softmax(s, axis=-1)
    o = jnp.einsum('btkrj,btjkd->btkrd', p, v_sel.astype(jnp.float32))
    return o.reshape(bsz, tq, ATT_HEADS * ATT_HEAD_DIM).astype(q.dtype)


def dsa_prompt(h, w_in, w_out):
    bsz, t, _ = h.shape
    pos = jnp.arange(t)
    q, k, v, qi, ki, wi = dsa_project(h, pos, w_in)
    topk = min(TOPK_MAX, t // 4)
    blk = min(Q_BLOCK, t)
    nb = t // blk
    gather = jax.vmap(lambda rows, ids: rows[ids])

    def to_blocks(a):
        return jnp.moveaxis(a.reshape((bsz, nb, blk) + a.shape[2:]), 1, 0)

    def attend_block(args):
        qb, qib, wib, posb = args
        idx, valid = indexer_topk(qib, wib, ki, posb, topk)
        return sparse_attend(qb, gather(k, idx), gather(v, idx), valid)

    o = lax.map(attend_block, (to_blocks(q), to_blocks(qi), to_blocks(wi), pos.reshape(nb, blk)))
    o = jnp.moveaxis(o, 0, 1).reshape(bsz, t, ATT_HEADS * ATT_HEAD_DIM)
    return o @ w_out, (k, v, ki)


def dsa_sample(h, cache_k, cache_v, cache_i, page_table, w_in, w_out):
    bsz, t, _ = h.shape
    past = page_table.shape[1] * PAGE_SIZE
    pos = past + jnp.arange(t)
    q, k, v, qi, ki, wi = dsa_project(h, pos, w_in)
    ki_past = cache_i[page_table].reshape(bsz, past, IDX_DIM)
    ki_all = jnp.concatenate([ki_past.astype(ki.dtype), ki], axis=1)
    topk = min(TOPK_MAX, (past + t) // 4)
    idx, valid = indexer_topk(qi, wi, ki_all, pos, topk)
    in_past = (idx < past)[..., None, None]
    pidx = jnp.minimum(idx, past - 1)
    phys = jax.vmap(lambda pt, ids: pt[ids])(page_table, pidx // PAGE_SIZE)
    off = pidx % PAGE_SIZE
    nidx = jnp.clip(idx - past, 0, t - 1)
    gather = jax.vmap(lambda rows, ids: rows[ids])
    k_sel = jnp.where(in_past, cache_k[phys, off].astype(k.dtype), gather(k, nidx))
    v_sel = jnp.where(in_past, cache_v[phys, off].astype(v.dtype), gather(v, nidx))
    o = sparse_attend(q, k_sel, v_sel, valid)
    return o @ w_out, (k, v, ki)


def _linear_combine(e1, e2):
    a1, b1 = e1
    a2, b2 = e2
    return a1 * a2, a2 * b1 + b2


def rglru_mixer(h, conv_buf, h0, w_in, conv_w, conv_b, w_ga, b_ga, w_gx, b_gx, lam, w_out):
    bsz, t, _ = h.shape
    gate, xb = jnp.split(h @ w_in, 2, axis=-1)
    full = jnp.concatenate([conv_buf.astype(xb.dtype), xb], axis=1)
    new_buf = full[:, -(CONV_W - 1):]
    xc = causal_dwconv(full, conv_w) + conv_b
    xblk = xc.reshape(bsz, t, LRU_BLOCKS, LRU_BLOCK)
    r = jax.nn.sigmoid(jnp.einsum('btni,nij->btnj', xblk, w_ga) + b_ga.reshape(LRU_BLOCKS, LRU_BLOCK)).reshape(bsz, t, LRU_WIDTH)
    i = jax.nn.sigmoid(jnp.einsum('btni,nij->btnj', xblk, w_gx) + b_gx.reshape(LRU_BLOCKS, LRU_BLOCK)).reshape(bsz, t, LRU_WIDTH)
    log_a = -RG_C * r.astype(jnp.float32) * jax.nn.softplus(-lam.astype(jnp.float32))
    a = jnp.exp(log_a)
    b = jnp.sqrt(-jnp.expm1(2.0 * log_a)) * (i * xc).astype(jnp.float32)
    b = b.at[:, 0].add(a[:, 0] * h0.astype(jnp.float32))
    _, hs = lax.associative_scan(_linear_combine, (a, b), axis=1)
    out = (hs.astype(h.dtype) * jax.nn.gelu(gate)) @ w_out
    return out, (new_buf, hs[:, -1].astype(h0.dtype))


def setup_inputs(seed: int = 0) -> dict:
    key = jax.random.key(seed)
    keys = iter(jax.random.split(key, 48))
    f32 = jnp.float32

    def normal(shape, scale=1.0):
        return jax.random.normal(next(keys), shape, f32) * scale

    def uniform(shape, lo, hi):
        return jax.random.uniform(next(keys), shape, f32, lo, hi)

    n_pages = PAST_LEN // PAGE_SIZE
    n_used = DEC_BATCH * n_pages
    n_pool = n_used + max(1, n_used // 4)
    x_prompt = normal((BATCH, SEQ, D_MODEL))
    x_sample = normal((DEC_BATCH, DEC_SEQ, D_MODEL))
    state_a_conv = normal((N_LAYERS_A, DEC_BATCH, CONV_W - 1, GDN_CONV_DIM))
    state_a_ssm = normal((N_LAYERS_A, DEC_BATCH, GDN_HV, GDN_DK, GDN_DV), 0.1)
    cache_b_k = normal((N_LAYERS_B, n_pool, PAGE_SIZE, ATT_KV_HEADS, ATT_HEAD_DIM))
    cache_b_v = normal((N_LAYERS_B, n_pool, PAGE_SIZE, ATT_KV_HEADS, ATT_HEAD_DIM))
    cache_b_idx = normal((N_LAYERS_B, n_pool, PAGE_SIZE, IDX_DIM))
    state_c_conv = normal((N_LAYERS_C, DEC_BATCH, CONV_W - 1, LRU_WIDTH))
    state_c_h = normal((N_LAYERS_C, DEC_BATCH, LRU_WIDTH), 0.5)
    page_table = jax.random.permutation(next(keys), n_pool)[:n_used].reshape(DEC_BATCH, n_pages).astype(jnp.int32)
    c_prompt = normal((BATCH, D_MODEL))
    c_sample = normal((DEC_BATCH, D_MODEL))
    w_ada = normal((DEPTH, D_MODEL, N_ADA * D_MODEL), D_MODEL ** -0.5)
    b_ada = normal((DEPTH, N_ADA * D_MODEL), 0.02)
    norm_w = 1.0 + normal((DEPTH, 3, D_MODEL), 0.05)
    ffn_w_in = normal((DEPTH, 2, D_MODEL, 2 * D_FF), D_MODEL ** -0.5)
    ffn_w_out = normal((DEPTH, 2, D_FF, D_MODEL), D_FF ** -0.5)
    gdn_w_in = normal((N_LAYERS_A, D_MODEL, GDN_PROJ_DIM), D_MODEL ** -0.5)
    gdn_conv_w = normal((N_LAYERS_A, CONV_W, GDN_CONV_DIM), 0.5)
    gdn_a_log = jnp.log(uniform((N_LAYERS_A, GDN_HV), 1.0, 16.0))
    dt0 = jnp.exp(uniform((N_LAYERS_A, GDN_HV), math.log(1e-3), math.log(1e-1)))
    gdn_dt_bias = dt0 + jnp.log(-jnp.expm1(-dt0))
    gdn_norm_w = 1.0 + normal((N_LAYERS_A, GDN_DV), 0.05)
    gdn_w_out = normal((N_LAYERS_A, GDN_VAL_DIM, D_MODEL), GDN_VAL_DIM ** -0.5)
    dsa_w_in = normal((N_LAYERS_B, D_MODEL, DSA_PROJ_DIM), D_MODEL ** -0.5)
    dsa_w_out = normal((N_LAYERS_B, ATT_HEADS * ATT_HEAD_DIM, D_MODEL), (ATT_HEADS * ATT_HEAD_DIM) ** -0.5)
    lru_w_in = normal((N_LAYERS_C, D_MODEL, 2 * LRU_WIDTH), D_MODEL ** -0.5)
    lru_conv_w = normal((N_LAYERS_C, CONV_W, LRU_WIDTH), 0.5)
    lru_conv_b = normal((N_LAYERS_C, LRU_WIDTH), 0.02)
    lru_w_gate_a = normal((N_LAYERS_C, LRU_BLOCKS, LRU_BLOCK, LRU_BLOCK), LRU_BLOCK ** -0.5)
    lru_b_gate_a = normal((N_LAYERS_C, LRU_WIDTH), 0.02)
    lru_w_gate_x = normal((N_LAYERS_C, LRU_BLOCKS, LRU_BLOCK, LRU_BLOCK), LRU_BLOCK ** -0.5)
    lru_b_gate_x = normal((N_LAYERS_C, LRU_WIDTH), 0.02)
    a0 = uniform((N_LAYERS_C, LRU_WIDTH), 0.9, 0.999) ** (1.0 / RG_C)
    lru_lambda = jnp.log(a0) - jnp.log1p(-a0)
    lru_w_out = normal((N_LAYERS_C, LRU_WIDTH, D_MODEL), LRU_WIDTH ** -0.5)
    w_ada_final = normal((D_MODEL, 2 * D_MODEL), D_MODEL ** -0.5)
    b_ada_final = normal((2 * D_MODEL,), 0.02)
    final_norm_w = 1.0 + normal((D_MODEL,), 0.05)
    return {'x_prompt': x_prompt, 'x_sample': x_sample, 'state_a_conv': state_a_conv, 'state_a_ssm': state_a_ssm, 'cache_b_k': cache_b_k, 'cache_b_v': cache_b_v, 'cache_b_idx': cache_b_idx, 'state_c_conv': state_c_conv, 'state_c_h': state_c_h, 'page_table': page_table, 'c_prompt': c_prompt, 'c_sample': c_sample, 'w_ada': w_ada, 'b_ada': b_ada, 'norm_w': norm_w, 'ffn_w_in': ffn_w_in, 'ffn_w_out': ffn_w_out, 'gdn_w_in': gdn_w_in, 'gdn_conv_w': gdn_conv_w, 'gdn_a_log': gdn_a_log, 'gdn_dt_bias': gdn_dt_bias, 'gdn_norm_w': gdn_norm_w, 'gdn_w_out': gdn_w_out, 'dsa_w_in': dsa_w_in, 'dsa_w_out': dsa_w_out, 'lru_w_in': lru_w_in, 'lru_conv_w': lru_conv_w, 'lru_conv_b': lru_conv_b, 'lru_w_gate_a': lru_w_gate_a, 'lru_b_gate_a': lru_b_gate_a, 'lru_w_gate_x': lru_w_gate_x, 'lru_b_gate_x': lru_b_gate_x, 'lru_lambda': lru_lambda, 'lru_w_out': lru_w_out, 'w_ada_final': w_ada_final, 'b_ada_final': b_ada_final, 'final_norm_w': final_norm_w}


def reference(x_prompt, x_sample, state_a_conv, state_a_ssm, cache_b_k, cache_b_v, cache_b_idx, state_c_conv, state_c_h, page_table, c_prompt, c_sample, w_ada, b_ada, norm_w, ffn_w_in, ffn_w_out, gdn_w_in, gdn_conv_w, gdn_a_log, gdn_dt_bias, gdn_norm_w, gdn_w_out, dsa_w_in, dsa_w_out, lru_w_in, lru_conv_w, lru_conv_b, lru_w_gate_a, lru_b_gate_a, lru_w_gate_x, lru_b_gate_x, lru_lambda, lru_w_out, w_ada_final, b_ada_final, final_norm_w):
    bp = x_prompt.shape[0]
    dt = x_prompt.dtype
    xp, xs = x_prompt, x_sample
    a_conv_p, a_conv_s, a_ssm_p, a_ssm_s = [], [], [], []
    b_k_p, b_k_s, b_v_p, b_v_s, b_i_p, b_i_s = [], [], [], [], [], []
    c_conv_p, c_conv_s, c_h_p, c_h_s = [], [], [], []
    for layer in range(DEPTH):
        kind, j = layer % N_MIXERS, layer // N_MIXERS
        shared = (w_ada[layer], b_ada[layer], norm_w[layer], ffn_w_in[layer], ffn_w_out[layer])
        if kind == 0:
            prm = (gdn_w_in[j], gdn_conv_w[j], gdn_a_log[j], gdn_dt_bias[j], gdn_norm_w[j], gdn_w_out[j])
            buf0 = jnp.zeros((bp, CONV_W - 1, GDN_CONV_DIM), dt)
            s0 = jnp.zeros((bp, GDN_HV, GDN_DK, GDN_DV), dt)
            xp, (buf, s) = macaron_layer(xp, c_prompt, *shared, lambda h: gdn_mixer(h, buf0, s0, *prm))
            a_conv_p.append(buf)
            a_ssm_p.append(s)
            xs, (buf, s) = macaron_layer(xs, c_sample, *shared, lambda h: gdn_mixer(h, state_a_conv[j], state_a_ssm[j], *prm))
            a_conv_s.append(buf)
            a_ssm_s.append(s)
        elif kind == 1:
            xp, (k, v, ki) = macaron_layer(xp, c_prompt, *shared, lambda h: dsa_prompt(h, dsa_w_in[j], dsa_w_out[j]))
            b_k_p.append(k)
            b_v_p.append(v)
            b_i_p.append(ki)
            xs, (k, v, ki) = macaron_layer(xs, c_sample, *shared, lambda h: dsa_sample(h, cache_b_k[j], cache_b_v[j], cache_b_idx[j], page_table, dsa_w_in[j], dsa_w_out[j]))
            b_k_s.append(k)
            b_v_s.append(v)
            b_i_s.append(ki)
        else:
            prm = (lru_w_in[j], lru_conv_w[j], lru_conv_b[j], lru_w_gate_a[j], lru_b_gate_a[j], lru_w_gate_x[j], lru_b_gate_x[j], lru_lambda[j], lru_w_out[j])
            buf0 = jnp.zeros((bp, CONV_W - 1, LRU_WIDTH), dt)
            h0 = jnp.zeros((bp, LRU_WIDTH), dt)
            xp, (buf, hl) = macaron_layer(xp, c_prompt, *shared, lambda h: rglru_mixer(h, buf0, h0, *prm))
            c_conv_p.append(buf)
            c_h_p.append(hl)
            xs, (buf, hl) = macaron_layer(xs, c_sample, *shared, lambda h: rglru_mixer(h, state_c_conv[j], state_c_h[j], *prm))
            c_conv_s.append(buf)
            c_h_s.append(hl)
    y_prompt = final_norm(xp, c_prompt, w_ada_final, b_ada_final, final_norm_w)
    y_sample = final_norm(xs, c_sample, w_ada_final, b_ada_final, final_norm_w)
    new_a_conv_p, new_a_conv_s = jnp.stack(a_conv_p), jnp.stack(a_conv_s)
    new_a_ssm_p, new_a_ssm_s = jnp.stack(a_ssm_p), jnp.stack(a_ssm_s)
    new_b_k_p, new_b_k_s = jnp.stack(b_k_p), jnp.stack(b_k_s)
    new_b_v_p, new_b_v_s = jnp.stack(b_v_p), jnp.stack(b_v_s)
    new_b_idx_p, new_b_idx_s = jnp.stack(b_i_p), jnp.stack(b_i_s)
    new_c_conv_p, new_c_conv_s = jnp.stack(c_conv_p), jnp.stack(c_conv_s)
    new_c_h_p, new_c_h_s = jnp.stack(c_h_p), jnp.stack(c_h_s)
    return (y_prompt, y_sample, new_a_conv_p, new_a_conv_s, new_a_ssm_p, new_a_ssm_s, new_b_k_p, new_b_k_s, new_b_v_p, new_b_v_s, new_b_idx_p, new_b_idx_s, new_c_conv_p, new_c_conv_s, new_c_h_p, new_c_h_s)
```

```python
import functools
import math

import jax
import jax.numpy as jnp
from jax import lax
from jax.experimental import pallas as pl
from jax.experimental.pallas import tpu as pltpu

F32 = jnp.float32
BF16 = jnp.bfloat16
I32 = jnp.int32

N_MIXERS = 3
N_ADA = 9
CONV_W = 4
NORM_EPS = 1e-6
NEG_INF = -1e30
GDN_CHUNK = 64
TOPK_MAX = 256
Q_BLOCK = 128
ROPE_THETA = 10000.0
RG_C = 8.0
LRU_BLOCK = 256

LANES = 128
SUBLANES = 8
VMEM_LIMIT_BYTES = 56 * 2**20
INT32_MIN = -2**31


def _cparams(*sem):
    return pltpu.CompilerParams(dimension_semantics=sem, vmem_limit_bytes=VMEM_LIMIT_BYTES)


def _tile(n, target, align=LANES):
    if n <= target:
        return n
    t = (target // align) * align
    while t >= align:
        if n % t == 0:
            return t
        t -= align
    raise ValueError(f"no {align}-aligned tile of {n} below {target}")


def _sigmoid(x):
    return jax.nn.sigmoid(x)


def _silu(x):
    return x * _sigmoid(x)


def _dot(a, b):
    return jnp.dot(a, b, preferred_element_type=F32)


def _dot_nt(a, b):
    return lax.dot_general(a, b, (((1,), (1,)), ((), ())), preferred_element_type=F32)


def _dot_tn(a, b):
    return lax.dot_general(a, b, (((0,), (0,)), ((), ())), preferred_element_type=F32)


def _dot_f32(a, b):
    return jnp.dot(a, b, preferred_element_type=F32, precision=lax.Precision.HIGHEST)


def _norm_mod(x, nw, sh, sc):
    ms = jnp.mean(x * x, axis=-1, keepdims=True)
    y = x * lax.rsqrt(ms + NORM_EPS) * nw
    return y * (1.0 + sc) + sh


def _ada_kernel(c_ref, w_ref, b_ref, o_ref):
    a = _silu(c_ref[...]).astype(BF16)
    o_ref[0] = _dot(a, w_ref[0].astype(BF16)) + b_ref[0]


def _ada(c_all, w, b):
    n_l, d, n = w.shape
    mp = c_all.shape[0]
    tn = _tile(n, 1024)
    return pl.pallas_call(
        _ada_kernel,
        grid=(n_l, n // tn),
        in_specs=[
            pl.BlockSpec((mp, d), lambda l, j: (0, 0)),
            pl.BlockSpec((1, d, tn), lambda l, j: (l, 0, j)),
            pl.BlockSpec((1, 1, tn), lambda l, j: (l, 0, j)),
        ],
        out_specs=pl.BlockSpec((1, mp, tn), lambda l, j: (l, 0, j)),
        out_shape=jax.ShapeDtypeStruct((n_l, mp, n), F32),
        name="ada_mod",
        compiler_params=_cparams("parallel", "parallel"),
    )(c_all, w, b.reshape(n_l, 1, n))


def _mod_spec(mod3, k, d, tm, rows_per_group, grid_rank):
    r = mod3.shape[1]
    if grid_rank == 2:
        return pl.BlockSpec((None, r, d), lambda i, j: ((i * tm) // rows_per_group, 0, k))
    raise ValueError(grid_rank)


def _ffn_kernel(x_ref, nw_ref, sh_ref, sc_ref, g_ref, wa_ref, wb_ref, wo_ref, o_ref, xn_ref, acc_ref):
    f = pl.program_id(1)

    @pl.when(f == 0)
    def _():
        xn_ref[...] = _norm_mod(x_ref[...], nw_ref[...], sh_ref[...], sc_ref[...]).astype(BF16)
        acc_ref[...] = jnp.zeros_like(acc_ref)

    xn = xn_ref[...]
    a = _dot(xn, wa_ref[...])
    b = _dot(xn, wb_ref[...])
    acc_ref[...] += _dot((_silu(a) * b).astype(BF16), wo_ref[...])

    @pl.when(f == pl.num_programs(1) - 1)
    def _():
        o_ref[...] = x_ref[...] + 0.5 * g_ref[...] * acc_ref[...]


def _ffn(x, mod3, k0, nw, w_in, w_out, *, tm, tf_target=512):
    m, d = x.shape
    f = w_out.shape[0]
    tf = _tile(f, tf_target)
    nf = f // tf
    rpg = m // mod3.shape[0]
    ms = lambda k: _mod_spec(mod3, k, d, tm, rpg, 2)
    return pl.pallas_call(
        _ffn_kernel,
        grid=(m // tm, nf),
        in_specs=[
            pl.BlockSpec((tm, d), lambda i, j: (i, 0)),
            pl.BlockSpec((1, d), lambda i, j: (0, 0)),
            ms(k0), ms(k0 + 1), ms(k0 + 2),
            pl.BlockSpec((d, tf), lambda i, j: (0, j)),
            pl.BlockSpec((d, tf), lambda i, j: (0, j + nf)),
            pl.BlockSpec((tf, d), lambda i, j: (j, 0)),
        ],
        out_specs=pl.BlockSpec((tm, d), lambda i, j: (i, 0)),
        out_shape=jax.ShapeDtypeStruct((m, d), F32),
        scratch_shapes=[pltpu.VMEM((tm, d), BF16), pltpu.VMEM((tm, d), F32)],
        name="ffn",
        compiler_params=_cparams("parallel", "arbitrary"),
    )(x, nw, mod3, mod3, mod3, w_in, w_in, w_out)


def _normproj_kernel(x_ref, nw_ref, sh_ref, sc_ref, w_ref, o_ref, xn_ref):
    @pl.when(pl.program_id(1) == 0)
    def _():
        xn_ref[...] = _norm_mod(x_ref[...], nw_ref[...], sh_ref[...], sc_ref[...]).astype(BF16)

    o_ref[...] = _dot(xn_ref[...], w_ref[...])


def _normproj(x, mod3, k0, nw, w, *, tm, tn_target=512):
    m, d = x.shape
    n = w.shape[1]
    tn = _tile(n, tn_target)
    rpg = m // mod3.shape[0]
    ms = lambda k: _mod_spec(mod3, k, d, tm, rpg, 2)
    return pl.pallas_call(
        _normproj_kernel,
        grid=(m // tm, n // tn),
        in_specs=[
            pl.BlockSpec((tm, d), lambda i, j: (i, 0)),
            pl.BlockSpec((1, d), lambda i, j: (0, 0)),
            ms(k0), ms(k0 + 1),
            pl.BlockSpec((d, tn), lambda i, j: (0, j)),
        ],
        out_specs=pl.BlockSpec((tm, tn), lambda i, j: (i, j)),
        out_shape=jax.ShapeDtypeStruct((m, n), F32),
        scratch_shapes=[pltpu.VMEM((tm, d), BF16)],
        name="normproj",
        compiler_params=_cparams("parallel", "arbitrary"),
    )(x, nw, mod3, mod3, w)


def _outproj_kernel(a_ref, w_ref, x_ref, g_ref, o_ref):
    o_ref[...] = x_ref[...] + g_ref[...] * _dot(a_ref[...], w_ref[...])


def _outproj(a, w, x, mod3, kg, *, tm, tn_target=512):
    m, kdim = a.shape
    d = w.shape[1]
    tn = _tile(d, tn_target)
    rpg = m // mod3.shape[0]
    r = mod3.shape[1]
    nd = d // tn
    return pl.pallas_call(
        _outproj_kernel,
        grid=(m // tm, nd),
        in_specs=[
            pl.BlockSpec((tm, kdim), lambda i, j: (i, 0)),
            pl.BlockSpec((kdim, tn), lambda i, j: (0, j)),
            pl.BlockSpec((tm, tn), lambda i, j: (i, j)),
            pl.BlockSpec((None, r, tn), lambda i, j: ((i * tm) // rpg, 0, kg * nd + j)),
        ],
        out_specs=pl.BlockSpec((tm, tn), lambda i, j: (i, j)),
        out_shape=jax.ShapeDtypeStruct((m, d), F32),
        name="outproj",
        compiler_params=_cparams("parallel", "arbitrary"),
    )(a, w, x, mod3)


def _final_kernel(x_ref, nw_ref, sh_ref, sc_ref, o_ref):
    o_ref[...] = _norm_mod(x_ref[...], nw_ref[...], sh_ref[...], sc_ref[...])


def _final_norm(x, mod3, nw, *, tm):
    m, d = x.shape
    rpg = m // mod3.shape[0]
    r = mod3.shape[1]
    ms = lambda k: pl.BlockSpec((None, r, d), lambda i: ((i * tm) // rpg, 0, k))
    return pl.pallas_call(
        _final_kernel,
        grid=(m // tm,),
        in_specs=[pl.BlockSpec((tm, d), lambda i: (i, 0)), pl.BlockSpec((1, d), lambda i: (0, 0)), ms(0), ms(1)],
        out_specs=pl.BlockSpec((tm, d), lambda i: (i, 0)),
        out_shape=jax.ShapeDtypeStruct((m, d), F32),
        name="final_norm",
        compiler_params=_cparams("parallel"),
    )(x, nw, mod3, mod3)


def _causal_conv(x, halo, w):
    tt = x.shape[0]
    row = lax.broadcasted_iota(I32, x.shape, 0)
    acc = x * w[CONV_W - 1:CONV_W]
    for j in range(1, CONV_W):
        xs = pltpu.roll(x, j, 0)
        hs = pltpu.roll(halo, j, 0)
        hs = jnp.concatenate([hs] * (tt // SUBLANES), axis=0) if tt > SUBLANES else hs
        acc = acc + jnp.where(row < j, hs, xs) * w[CONV_W - 1 - j:CONV_W - j]
    return acc


def _conv_specs(tt, tc, coff, boff):
    sub = tt // SUBLANES
    return [
        pl.BlockSpec((None, tt, tc), lambda b, t, c: (b, t, c + coff)),
        pl.BlockSpec((None, SUBLANES, tc), lambda b, t, c: (b, jnp.maximum(t * sub - 1, 0), c + coff)),
        pl.BlockSpec((None, SUBLANES, tc), lambda b, t, c: (b, 0, c + boff)),
    ]


def _pad_buf(buf):
    return jnp.pad(buf, ((0, 0), (SUBLANES - (CONV_W - 1), 0), (0, 0)))


def _gdn_prep_kernel(x_ref, halo_ref, buf_ref, w_ref, o_ref, *, norm):
    halo = jnp.where(pl.program_id(1) == 0, buf_ref[...], halo_ref[...])
    y = _silu(_causal_conv(x_ref[...], halo, w_ref[...]))
    if norm:
        for h in range(y.shape[1] // LANES):
            seg = y[:, h * LANES:(h + 1) * LANES]
            ss = jnp.sum(seg * seg, axis=-1, keepdims=True)
            o_ref[:, h * LANES:(h + 1) * LANES] = seg * lax.rsqrt(ss + NORM_EPS)
    else:
        o_ref[...] = y


def _gdn_prep(proj, buf8, conv_w, *, col0, ncols, norm, tt):
    bsz, t, _ = proj.shape
    tc = _tile(ncols, 512)
    return pl.pallas_call(
        functools.partial(_gdn_prep_kernel, norm=norm),
        grid=(bsz, t // tt, ncols // tc),
        in_specs=_conv_specs(tt, tc, col0 // tc, col0 // tc)
        + [pl.BlockSpec((CONV_W, tc), lambda b, i, c: (0, c + col0 // tc))],
        out_specs=pl.BlockSpec((None, tt, tc), lambda b, i, c: (b, i, c)),
        out_shape=jax.ShapeDtypeStruct((bsz, t, ncols), F32),
        name="gdn_prep",
        compiler_params=_cparams("parallel", "parallel", "parallel"),
    )(proj, proj, buf8, conv_w)


def _gdn_gate_kernel(x_ref, alog_ref, dtb_ref, beta_ref, gc_ref, *, hv, chunk):
    x = x_ref[...]
    tt = x.shape[0]
    beta_ref[...] = _sigmoid(x)
    z = x + dtb_ref[...]
    g = -jnp.exp(alog_ref[...]) * (jnp.maximum(z, 0.0) + jnp.log1p(jnp.exp(-jnp.abs(z))))
    row = lax.broadcasted_iota(I32, (tt, tt), 0)
    col = lax.broadcasted_iota(I32, (tt, tt), 1)
    tri = jnp.where((row >= col) & (row // chunk == col // chunk), 1.0, 0.0)
    gc_ref[...] = _dot_f32(tri, g)


def _gdn_gates(proj, a_log, dt_bias, *, col0, hv, chunk, tt):
    bsz, t, _ = proj.shape
    pad = lambda v: jnp.pad(v.astype(F32), (hv, LANES - 2 * hv)).reshape(1, LANES)
    blk = pl.BlockSpec((None, tt, LANES), lambda b, i: (b, i, col0 // LANES))
    out = pl.BlockSpec((None, tt, LANES), lambda b, i: (b, i, 0))
    par = pl.BlockSpec((1, LANES), lambda b, i: (0, 0))
    return pl.pallas_call(
        functools.partial(_gdn_gate_kernel, hv=hv, chunk=chunk),
        grid=(bsz, t // tt),
        in_specs=[blk, par, par],
        out_specs=[out, out],
        out_shape=[jax.ShapeDtypeStruct((bsz, t, LANES), F32)] * 2,
        name="gdn_gates",
        compiler_params=_cparams("parallel", "parallel"),
    )(proj, pad(a_log), pad(dt_bias))


def _tri_inv(lmat, c):
    row = lax.broadcasted_iota(I32, (c, c), 0)
    col = lax.broadcasted_iota(I32, (c, c), 1)
    eye = jnp.where(row == col, 1.0, 0.0)
    base = min(16, c)
    p = -jnp.where(row // base == col // base, lmat, 0.0)
    r = eye + p
    n = 2
    while n < base:
        p = _dot_f32(p, p)
        r = r + _dot_f32(r, p)
        n *= 2
    s = base
    while s < c:
        off = (row // (2 * s) == col // (2 * s)) & (row // s != col // s)
        r = r - _dot_f32(r, _dot_f32(jnp.where(off, lmat, 0.0), r))
        s *= 2
    return r


def _gdn_core_kernel(q_ref, k_ref, v_ref, z_ref, gcc_ref, gcr_ref, bc_ref, s0_ref, nw_ref, o_ref, so_ref, s_ref,
                     *, rep, c, dk):
    ci = pl.program_id(2)

    @pl.when(ci == 0)
    def _():
        s_ref[...] = s0_ref[...]

    q = q_ref[...] * dk ** -0.5
    k = k_ref[...]
    kb16 = k.astype(BF16)
    gram = _dot_nt(kb16, kb16)
    qk0 = _dot_nt(q.astype(BF16), kb16)
    row = lax.broadcasted_iota(I32, (c, c), 0)
    col = lax.broadcasted_iota(I32, (c, c), 1)
    causal = row >= col
    for r in range(rep):
        gcc = gcc_ref[:, r:r + 1]
        gcr = gcr_ref[r:r + 1, :]
        beta = bc_ref[:, r:r + 1]
        decay = jnp.where(causal, jnp.exp(jnp.where(causal, gcc - gcr, 0.0)), 0.0)
        lower = jnp.where(row > col, gram * beta * decay, 0.0)
        tinv = _tri_inv(lower, c)
        v = v_ref[:, r * LANES:(r + 1) * LANES]
        egc = jnp.exp(gcc)
        sol = _dot_f32(tinv, jnp.concatenate([v * beta, k * (beta * egc)], axis=1))
        u_base, w_dec = sol[:, :LANES], sol[:, LANES:]
        s = s_ref[r]
        s16 = s.astype(BF16)
        u = u_base - _dot(w_dec.astype(BF16), s16)
        o = _dot((q * egc).astype(BF16), s16) + _dot((qk0 * decay).astype(BF16), u.astype(BF16))
        g_last = gcc[c - 1:c, :]
        kd = k * jnp.exp(g_last - gcc)
        s_ref[r] = s * jnp.exp(g_last) + _dot_tn(kd.astype(BF16), u.astype(BF16))
        on = o * lax.rsqrt(jnp.mean(o * o, axis=-1, keepdims=True) + NORM_EPS) * nw_ref[...]
        o_ref[:, r * LANES:(r + 1) * LANES] = (on * _silu(z_ref[:, r * LANES:(r + 1) * LANES])).astype(BF16)

    @pl.when(ci == pl.num_programs(2) - 1)
    def _():
        so_ref[...] = s_ref[...]


def _gdn_core(qk, v, proj, zcol0, gc, beta, s0, norm_w, *, c):
    bsz, t, val = v.shape
    hv = s0.shape[1]
    dk, dv = s0.shape[2], s0.shape[3]
    hk = qk.shape[2] // (2 * dk)
    rep = hv // hk
    assert dk == LANES and dv == LANES and t % c == 0
    heads = lambda a, lo: a[:, :, lo:lo + hv].reshape(bsz, t, hk, rep).transpose(0, 2, 1, 3)
    gcc = heads(gc, hv)
    bcc = heads(beta, 0)
    gcr = gcc.transpose(0, 1, 3, 2)
    zb = zcol0 // (rep * dv)
    colspec = pl.BlockSpec((None, None, c, rep), lambda b, h, i: (b, h, i, 0))
    o, s_out = pl.pallas_call(
        functools.partial(_gdn_core_kernel, rep=rep, c=c, dk=dk),
        grid=(bsz, hk, t // c),
        in_specs=[
            pl.BlockSpec((None, c, dk), lambda b, h, i: (b, i, h)),
            pl.BlockSpec((None, c, dk), lambda b, h, i: (b, i, hk + h)),
            pl.BlockSpec((None, c, rep * dv), lambda b, h, i: (b, i, h)),
            pl.BlockSpec((None, c, rep * dv), lambda b, h, i: (b, i, zb + h)),
            colspec,
            pl.BlockSpec((None, None, rep, c), lambda b, h, i: (b, h, 0, i)),
            colspec,
            pl.BlockSpec((None, rep, dk, dv), lambda b, h, i: (b, h, 0, 0)),
            pl.BlockSpec((1, dv), lambda b, h, i: (0, 0)),
        ],
        out_specs=[
            pl.BlockSpec((None, c, rep * dv), lambda b, h, i: (b, i, h)),
            pl.BlockSpec((None, rep, dk, dv), lambda b, h, i: (b, h, 0, 0)),
        ],
        out_shape=[jax.ShapeDtypeStruct((bsz, t, val), BF16), jax.ShapeDtypeStruct(s0.shape, F32)],
        scratch_shapes=[pltpu.VMEM((rep, dk, dv), F32)],
        name="gdn_core",
        compiler_params=_cparams("parallel", "parallel", "arbitrary"),
    )(qk, qk, v, proj, gcc, gcr, bcc, s0, norm_w.reshape(1, dv))
    return o, s_out


def _gdn_mixer(x, mod3, nw, buf, s0, w_in, conv_w, a_log, dt_bias, norm_w, w_out, *, tm, tt, chunk):
    bsz, t, d = x.shape
    hv, dk, dv = s0.shape[1], s0.shape[2], s0.shape[3]
    val = hv * dv
    conv_dim = conv_w.shape[1]
    key = (conv_dim - val) // 2
    assert (conv_dim + val) % LANES == 0 and 2 * hv <= LANES and t >= CONV_W - 1
    x2 = x.reshape(bsz * t, d)
    proj = _normproj(x2, mod3, 3, nw, w_in, tm=tm).reshape(bsz, t, -1)
    buf8 = _pad_buf(buf)
    qk = _gdn_prep(proj, buf8, conv_w, col0=0, ncols=2 * key, norm=True, tt=tt)
    v = _gdn_prep(proj, buf8, conv_w, col0=2 * key, ncols=val, norm=False, tt=tt)
    tp = -(-t // chunk) * chunk
    beta, gc = _gdn_gates(proj, a_log, dt_bias, col0=conv_dim + val, hv=hv, chunk=min(chunk, tt), tt=tt)
    if tp != t:
        padt = lambda a: jnp.pad(a, ((0, 0), (0, tp - t), (0, 0)))
        gc = jnp.concatenate([gc, jnp.broadcast_to(gc[:, -1:], (bsz, tp - t, LANES))], axis=1)
        qk, v, beta, projz = padt(qk), padt(v), padt(beta), padt(proj)
    else:
        projz = proj
    o, s_new = _gdn_core(qk, v, projz, conv_dim, gc, beta, s0, norm_w, c=chunk)
    o2 = o[:, :t].reshape(bsz * t, val)
    xo = _outproj(o2, w_out, x2, mod3, 5, tm=tm).reshape(bsz, t, d)
    new_buf = proj[:, t - (CONV_W - 1):, :conv_dim]
    return xo, new_buf, s_new


def _rope_tables(pos, half):
    inv_freq = ROPE_THETA ** (-jnp.arange(half, dtype=F32) / half)
    ang = pos.astype(F32)[:, None] * inv_freq[None, :]
    cos, sin = jnp.cos(ang), jnp.sin(ang)
    return jnp.concatenate([cos, cos], axis=-1), jnp.concatenate([-sin, sin], axis=-1)


def _dsa_prep_kernel(x_ref, cos_ref, sin_ref, q_ref, k_ref, k16_ref, v16_ref, qi_ref, ki_ref, ki16_ref, *wt_ref,
                     nh, nkv, nih):
    cos, sin = cos_ref[...], sin_ref[...]

    def rope(col):
        seg = x_ref[:, col * LANES:(col + 1) * LANES]
        return seg * cos + pltpu.roll(seg, LANES // 2, 1) * sin

    for h in range(nh):
        q_ref[:, h * LANES:(h + 1) * LANES] = rope(h).astype(BF16)
    for h in range(nkv):
        kr = rope(nh + h)
        k_ref[:, h * LANES:(h + 1) * LANES] = kr
        k16_ref[:, h * LANES:(h + 1) * LANES] = kr.astype(BF16)
    v0 = (nh + nkv) * LANES
    v16_ref[...] = x_ref[:, v0:v0 + nkv * LANES].astype(BF16)
    c0 = nh + 2 * nkv
    for h in range(nih):
        qi_ref[:, h * LANES:(h + 1) * LANES] = rope(c0 + h).astype(BF16)
    kir = rope(c0 + nih)
    ki_ref[...] = kir
    ki16_ref[...] = kir.astype(BF16)
    if wt_ref:
        w0 = (c0 + nih + 1) * LANES
        wt_ref[0][...] = x_ref[:, w0:w0 + LANES].T[:wt_ref[0].shape[0], :]


def _dsa_prep(proj, pos, *, nh, nkv, nih, tt, with_wt):
    bsz, t, npj = proj.shape
    cos, sin = _rope_tables(pos, LANES // 2)
    row = lambda n, dt: jax.ShapeDtypeStruct((bsz, t, n * LANES), dt)
    ospec = lambda n: pl.BlockSpec((None, tt, n * LANES), lambda b, i: (b, i, 0))
    tab = pl.BlockSpec((tt, LANES), lambda b, i: (i, 0))
    nwt = -(-nih // SUBLANES) * SUBLANES
    return pl.pallas_call(
        functools.partial(_dsa_prep_kernel, nh=nh, nkv=nkv, nih=nih),
        grid=(bsz, t // tt),
        in_specs=[pl.BlockSpec((None, tt, npj), lambda b, i: (b, i, 0)), tab, tab],
        out_specs=[ospec(nh), ospec(nkv), ospec(nkv), ospec(nkv), ospec(nih), ospec(1), ospec(1)]
        + ([pl.BlockSpec((None, nwt, tt), lambda b, i: (b, 0, i))] if with_wt else []),
        out_shape=[row(nh, BF16), row(nkv, F32), row(nkv, BF16), row(nkv, BF16), row(nih, BF16), row(1, F32),
                   row(1, BF16)] + ([jax.ShapeDtypeStruct((bsz, nwt, t), F32)] if with_wt else []),
        name="dsa_prep",
        compiler_params=_cparams("parallel", "parallel"),
    )(proj, cos, sin)


def _sort_key(s):
    bits = pltpu.bitcast(jnp.where(s == 0.0, 0.0, s), I32)
    return jnp.where(bits < 0, bits ^ 0x7FFFFFFF, bits)


def _kth_largest_key(count_ge, shape, topk):
    def body(i, t):
        cand = t + lax.shift_left(jnp.int32(1), 31 - i)
        return jnp.where(count_ge(cand) >= topk, cand, t)
    return lax.fori_loop(0, 32, body, jnp.full(shape, INT32_MIN, I32))


def _dsa_prompt_kernel(qi_ref, wt_ref, q_ref, ki_ref, k_ref, v_ref, o_ref, key_ref, bias_ref,
                       *, nh, nkv, nih, topk, idx_scale):
    qb = pl.program_id(1)
    blk = Q_BLOCK
    nkt = qb + 1
    rowk = lax.broadcasted_iota(I32, (blk, blk), 0)
    colq = lax.broadcasted_iota(I32, (blk, blk), 1)
    qi = jnp.concatenate([qi_ref[:, h * LANES:(h + 1) * LANES] for h in range(nih)], axis=0)
    wt = wt_ref[...]

    def visible(kt):
        return kt * blk + rowk <= qb * blk + colq

    def score_body(kt, carry):
        lg = _dot_nt(ki_ref[pl.ds(pl.multiple_of(kt * blk, blk), blk), :], qi)
        acc = jnp.zeros((blk, blk), F32)
        for h in range(nih):
            acc = acc + jnp.maximum(lg[:, h * blk:(h + 1) * blk], 0.0) * wt[h:h + 1, :]
        key_ref[kt] = _sort_key(jnp.where(visible(kt), acc * idx_scale, NEG_INF))
        return carry

    lax.fori_loop(0, nkt, score_body, 0)

    def count_ge(cand):
        cnt = lax.fori_loop(0, nkt, lambda kt, c: c + jnp.where(key_ref[kt] >= cand, 1, 0),
                            jnp.zeros((blk, blk), I32))
        return jnp.sum(cnt, axis=0, keepdims=True)

    thr = _kth_largest_key(count_ge, (1, blk), topk)

    def bias_body(kt, carry):
        sel = (key_ref[kt] >= thr) & visible(kt)
        bias_ref[kt] = jnp.where(sel, 0.0, NEG_INF).T
        return carry

    lax.fori_loop(0, nkt, bias_body, 0)

    rep = nh // nkv
    scale = LANES ** -0.5
    for g in range(nkv):
        qg = jnp.concatenate([q_ref[:, (g * rep + r) * LANES:(g * rep + r + 1) * LANES] for r in range(rep)], axis=0)

        def att_body(kt, carry, g=g, qg=qg):
            m, l, acc = carry
            rows = pl.ds(pl.multiple_of(kt * blk, blk), blk)
            s = _dot_nt(qg, k_ref[rows, g * LANES:(g + 1) * LANES]) * scale
            s = s + jnp.concatenate([bias_ref[kt]] * rep, axis=0)
            m_new = jnp.maximum(m, jnp.max(s, axis=-1, keepdims=True))
            alpha = jnp.exp(m - m_new)
            p = jnp.exp(s - m_new)
            l = alpha * l + jnp.sum(p, axis=-1, keepdims=True)
            acc = alpha * acc + _dot(p.astype(BF16), v_ref[rows, g * LANES:(g + 1) * LANES])
            return m_new, l, acc

        m0 = jnp.full((rep * blk, 1), NEG_INF, F32)
        _, l, acc = lax.fori_loop(0, nkt, att_body, (m0, jnp.zeros((rep * blk, 1), F32),
                                                     jnp.zeros((rep * blk, LANES), F32)))
        og = acc / l
        for r in range(rep):
            h = g * rep + r
            o_ref[:, h * LANES:(h + 1) * LANES] = og[r * blk:(r + 1) * blk].astype(BF16)


def _dsa_prompt_attend(q16, qi16, wt, ki16, k16, v16, *, nh, nkv, nih, topk):
    bsz, t, _ = q16.shape
    nq = t // Q_BLOCK
    full = lambda n: pl.BlockSpec((None, t, n * LANES), lambda b, i: (b, 0, 0))
    blk = lambda n: pl.BlockSpec((None, Q_BLOCK, n * LANES), lambda b, i: (b, i, 0))
    return pl.pallas_call(
        functools.partial(_dsa_prompt_kernel, nh=nh, nkv=nkv, nih=nih, topk=topk,
                          idx_scale=(LANES * nih) ** -0.5),
        grid=(bsz, nq),
        in_specs=[blk(nih), pl.BlockSpec((None, wt.shape[1], Q_BLOCK), lambda b, i: (b, 0, i)), blk(nh),
                  full(1), full(nkv), full(nkv)],
        out_specs=blk(nh),
        out_shape=jax.ShapeDtypeStruct((bsz, t, nh * LANES), BF16),
        scratch_shapes=[pltpu.VMEM((nq, Q_BLOCK, Q_BLOCK), I32), pltpu.VMEM((nq, Q_BLOCK, Q_BLOCK), F32)],
        name="dsa_prompt_attend",
        compiler_params=_cparams("parallel", "arbitrary"),
    )(qi16, wt, q16, ki16, k16, v16)


def _dsa_sample_score_kernel(pt_ref, qi_ref, wc_ref, page_ref, new_ref, o_ref, *, n_pages, nih, tq, past, idx_scale):
    p = pl.program_id(1)
    keys = jnp.where(p == n_pages, new_ref[...], page_ref[...].astype(BF16))
    qi = jnp.concatenate([qi_ref[:, h * LANES:(h + 1) * LANES] for h in range(nih)], axis=0)
    w = jnp.maximum(_dot_nt(qi, keys), 0.0) * wc_ref[...]
    acc = w[0:tq]
    for h in range(1, nih):
        acc = acc + w[h * tq:(h + 1) * tq]
    s = p * LANES + lax.broadcasted_iota(I32, (tq, LANES), 1)
    qpos = past + lax.broadcasted_iota(I32, (tq, LANES), 0)
    o_ref[...] = jnp.where(s <= qpos, acc * idx_scale, NEG_INF)


def _dsa_sample_select_kernel(s_ref, o_ref, *, topk):
    key = _sort_key(s_ref[...])
    thr = _kth_largest_key(lambda cand: jnp.sum(jnp.where(key >= cand, 1, 0), axis=-1, keepdims=True),
                           (key.shape[0], 1), topk)
    o_ref[...] = jnp.where((key >= thr) & (s_ref[...] > 0.5 * NEG_INF), 0.0, NEG_INF)


def _dsa_sample_attn_kernel(pt_ref, q_ref, bias_ref, kp_ref, vp_ref, kn_ref, vn_ref, o_ref, m_ref, l_ref, acc_ref,
                            *, n_pages, nh, nkv, tq):
    p = pl.program_id(1)

    @pl.when(p == 0)
    def _():
        m_ref[...] = jnp.full_like(m_ref, NEG_INF)
        l_ref[...] = jnp.zeros_like(l_ref)
        acc_ref[...] = jnp.zeros_like(acc_ref)

    last = p == n_pages
    kpg = jnp.where(last, kn_ref[...], kp_ref[...].astype(BF16))
    vpg = jnp.where(last, vn_ref[...], vp_ref[...].astype(BF16))
    rep = nh // nkv
    q = jnp.concatenate([q_ref[:, h * LANES:(h + 1) * LANES] for h in range(nh)], axis=0)
    rows = rep * tq
    s = jnp.concatenate([_dot_nt(q[g * rows:(g + 1) * rows], kpg[:, g * LANES:(g + 1) * LANES])
                         for g in range(nkv)], axis=0) * LANES ** -0.5
    s = s + jnp.concatenate([bias_ref[...]] * nh, axis=0)
    m = m_ref[...]
    m_new = jnp.maximum(m, jnp.max(s, axis=-1, keepdims=True))
    alpha = jnp.exp(m - m_new)
    pr = jnp.exp(s - m_new)
    l_ref[...] = alpha * l_ref[...] + jnp.sum(pr, axis=-1, keepdims=True)
    pv = jnp.concatenate([_dot(pr[g * rows:(g + 1) * rows].astype(BF16), vpg[:, g * LANES:(g + 1) * LANES])
                          for g in range(nkv)], axis=0)
    acc_ref[...] = alpha * acc_ref[...] + pv
    m_ref[...] = m_new

    @pl.when(last)
    def _():
        og = acc_ref[...] / l_ref[...]
        for h in range(nh):
            o_ref[:, h * LANES:(h + 1) * LANES] = og[h * tq:(h + 1) * tq].astype(BF16)


def _dsa_sample_attend(q16, qi16, wi, ki16, k16, v16, cache_k, cache_v, cache_i, page_table, *, nh, nkv, nih, topk):
    bsz, tq, _ = q16.shape
    n_pages = page_table.shape[1]
    page = cache_i.shape[1]
    assert page == LANES and tq == SUBLANES
    past = n_pages * page
    ltot = past + page
    padk = lambda a: jnp.pad(a, ((0, 0), (0, page - tq), (0, 0)))
    wcol = wi.transpose(0, 2, 1).reshape(bsz, nih * tq, 1)
    pidx = lambda b, p, pt: (pt[b, jnp.minimum(p, n_pages - 1)], 0, 0)
    qspec = lambda n: pl.BlockSpec((None, tq, n * LANES), lambda b, p, pt: (b, 0, 0))
    newspec = lambda n: pl.BlockSpec((None, page, n * LANES), lambda b, p, pt: (b, 0, 0))
    idx_scale = (LANES * nih) ** -0.5
    scores = pl.pallas_call(
        functools.partial(_dsa_sample_score_kernel, n_pages=n_pages, nih=nih, tq=tq, past=past, idx_scale=idx_scale),
        grid_spec=pltpu.PrefetchScalarGridSpec(
            num_scalar_prefetch=1, grid=(bsz, n_pages + 1),
            in_specs=[qspec(nih), pl.BlockSpec((None, nih * tq, 1), lambda b, p, pt: (b, 0, 0)),
                      pl.BlockSpec((None, page, LANES), pidx), newspec(1)],
            out_specs=pl.BlockSpec((None, tq, LANES), lambda b, p, pt: (b, 0, p))),
        out_shape=jax.ShapeDtypeStruct((bsz, tq, ltot), F32),
        name="dsa_sample_scores",
        compiler_params=_cparams("parallel", "arbitrary"),
    )(page_table, qi16, wcol, cache_i, padk(ki16))
    bias = pl.pallas_call(
        functools.partial(_dsa_sample_select_kernel, topk=topk),
        grid=(bsz,),
        in_specs=[pl.BlockSpec((None, tq, ltot), lambda b: (b, 0, 0))],
        out_specs=pl.BlockSpec((None, tq, ltot), lambda b: (b, 0, 0)),
        out_shape=jax.ShapeDtypeStruct((bsz, tq, ltot), F32),
        name="dsa_sample_select",
        compiler_params=_cparams("parallel"),
    )(scores)
    n_pool = cache_k.shape[0]
    ck = cache_k.reshape(n_pool, page, nkv * LANES)
    cv = cache_v.reshape(n_pool, page, nkv * LANES)
    kvspec = pl.BlockSpec((None, page, nkv * LANES), pidx)
    return pl.pallas_call(
        functools.partial(_dsa_sample_attn_kernel, n_pages=n_pages, nh=nh, nkv=nkv, tq=tq),
        grid_spec=pltpu.PrefetchScalarGridSpec(
            num_scalar_prefetch=1, grid=(bsz, n_pages + 1),
            in_specs=[qspec(nh), pl.BlockSpec((None, tq, LANES), lambda b, p, pt: (b, 0, p)),
                      kvspec, kvspec, newspec(nkv), newspec(nkv)],
            out_specs=qspec(nh),
            scratch_shapes=[pltpu.VMEM((nh * tq, 1), F32), pltpu.VMEM((nh * tq, 1), F32),
                            pltpu.VMEM((nh * tq, LANES), F32)]),
        out_shape=jax.ShapeDtypeStruct((bsz, tq, nh * LANES), BF16),
        name="dsa_sample_attend",
        compiler_params=_cparams("parallel", "arbitrary"),
    )(page_table, q16, bias, ck, cv, padk(k16), padk(v16))


def _dsa_mixer(x, mod3, nw, w_in, w_out, n_proj, *, nkv, tm, tt, cache=None):
    bsz, t, d = x.shape
    nh = w_out.shape[0] // LANES
    nih = (n_proj - (nh + 2 * nkv + 1) * LANES) // (LANES + 1)
    assert (nh + 2 * nkv + nih + 1) * LANES + nih == n_proj and nih <= LANES
    x2 = x.reshape(bsz * t, d)
    proj = _normproj(x2, mod3, 3, nw, w_in, tm=tm).reshape(bsz, t, -1)
    v = proj[:, :, (nh + nkv) * LANES:(nh + 2 * nkv) * LANES]
    if cache is None:
        assert t % Q_BLOCK == 0
        q16, k, k16, v16, qi16, ki, ki16, wt = _dsa_prep(proj, jnp.arange(t), nh=nh, nkv=nkv, nih=nih, tt=tt,
                                                        with_wt=True)
        o = _dsa_prompt_attend(q16, qi16, wt, ki16, k16, v16, nh=nh, nkv=nkv, nih=nih, topk=min(TOPK_MAX, t // 4))
    else:
        cache_k, cache_v, cache_i, page_table = cache
        past = page_table.shape[1] * cache_i.shape[1]
        q16, k, k16, v16, qi16, ki, ki16 = _dsa_prep(proj, past + jnp.arange(t), nh=nh, nkv=nkv, nih=nih, tt=tt,
                                                    with_wt=False)
        w0 = (nh + 2 * nkv + nih + 1) * LANES
        o = _dsa_sample_attend(q16, qi16, proj[:, :, w0:w0 + nih], ki16, k16, v16, cache_k, cache_v, cache_i,
                               page_table, nh=nh, nkv=nkv, nih=nih, topk=min(TOPK_MAX, (past + t) // 4))
    xo = _outproj(o.reshape(bsz * t, nh * LANES), w_out, x2, mod3, 5, tm=tm).reshape(bsz, t, d)
    return xo, k.reshape(bsz, t, nkv, LANES), v.reshape(bsz, t, nkv, LANES), ki


def _gelu_tanh(x):
    return 0.5 * x * (1.0 + jnp.tanh(math.sqrt(2.0 / math.pi) * (x + 0.044715 * (x * x * x))))


def _lru_kernel(gate_ref, xb_ref, halo_ref, buf_ref, cw_ref, cb_ref, wa_ref, ba_ref, wx_ref, bx_ref, lam_ref,
                h0_ref, y_ref, hl_ref, h_ref, *, nblk):
    ti = pl.program_id(1)

    @pl.when(ti == 0)
    def _():
        h_ref[...] = h0_ref[...]

    halo = jnp.where(ti == 0, buf_ref[...], halo_ref[...])
    xc = _causal_conv(xb_ref[...], halo, cw_ref[...]) + cb_ref[...]
    tt = xc.shape[0]
    xc16 = xc.astype(BF16)
    rs, xs = [], []
    for n in range(nblk):
        blk = xc16[:, n * LRU_BLOCK:(n + 1) * LRU_BLOCK]
        rs.append(_dot(blk, wa_ref[n]))
        xs.append(_dot(blk, wx_ref[n]))
    r = _sigmoid(jnp.concatenate(rs, axis=1) + ba_ref[...])
    ig = _sigmoid(jnp.concatenate(xs, axis=1) + bx_ref[...])
    lam = lam_ref[...]
    softplus_neg = jnp.maximum(-lam, 0.0) + jnp.log1p(jnp.exp(-jnp.abs(lam)))
    log_a = -RG_C * r * softplus_neg
    a = jnp.exp(log_a)
    b = jnp.sqrt(-jnp.tanh(log_a) * (a * a + 1.0)) * (ig * xc)
    row = lax.broadcasted_iota(I32, a.shape, 0)
    d = 1
    while d < tt:
        keep = row >= d
        a_sh = jnp.where(keep, pltpu.roll(a, d, 0), 1.0)
        b_sh = jnp.where(keep, pltpu.roll(b, d, 0), 0.0)
        b = a * b_sh + b
        a = a * a_sh
        d *= 2
    hs = b + a * h_ref[...]
    h_ref[...] = hs[tt - 1:tt, :]
    y_ref[...] = (hs * _gelu_tanh(gate_ref[...])).astype(BF16)

    @pl.when(ti == pl.num_programs(1) - 1)
    def _():
        hl_ref[...] = hs[tt - 1:tt, :]


def _lru_core(proj, buf8, h0, conv_w, conv_b, w_ga, b_ga, w_gx, b_gx, lam, *, tt):
    bsz, t, w2 = proj.shape
    w = w2 // 2
    nblk = w // LRU_BLOCK
    sub = tt // SUBLANES
    vec = lambda: pl.BlockSpec((1, w), lambda b, i: (0, 0))
    wsp = lambda: pl.BlockSpec((nblk, LRU_BLOCK, LRU_BLOCK), lambda b, i: (0, 0, 0))
    r1 = lambda v: v.reshape(1, w).astype(F32)
    y, hl = pl.pallas_call(
        functools.partial(_lru_kernel, nblk=nblk),
        grid=(bsz, t // tt),
        in_specs=[
            pl.BlockSpec((None, tt, w), lambda b, i: (b, i, 0)),
            pl.BlockSpec((None, tt, w), lambda b, i: (b, i, 1)),
            pl.BlockSpec((None, SUBLANES, w), lambda b, i: (b, jnp.maximum(i * sub - 1, 0), 1)),
            pl.BlockSpec((None, SUBLANES, w), lambda b, i: (b, 0, 0)),
            pl.BlockSpec((CONV_W, w), lambda b, i: (0, 0)),
            vec(), wsp(), vec(), wsp(), vec(), vec(),
            pl.BlockSpec((None, 1, w), lambda b, i: (b, 0, 0)),
        ],
        out_specs=[pl.BlockSpec((None, tt, w), lambda b, i: (b, i, 0)),
                   pl.BlockSpec((None, 1, w), lambda b, i: (b, 0, 0))],
        out_shape=[jax.ShapeDtypeStruct((bsz, t, w), BF16), jax.ShapeDtypeStruct((bsz, 1, w), F32)],
        scratch_shapes=[pltpu.VMEM((1, w), F32)],
        name="lru_core",
        compiler_params=_cparams("parallel", "arbitrary"),
    )(proj, proj, proj, buf8, conv_w, r1(conv_b), w_ga, r1(b_ga), w_gx, r1(b_gx), r1(lam), h0.reshape(bsz, 1, w))
    return y, hl.reshape(bsz, w)


def _lru_mixer(x, mod3, nw, buf, h0, w_in, conv_w, conv_b, w_ga, b_ga, w_gx, b_gx, lam, w_out, *, tm, tt):
    bsz, t, d = x.shape
    assert t >= CONV_W - 1
    x2 = x.reshape(bsz * t, d)
    proj = _normproj(x2, mod3, 3, nw, w_in, tm=tm).reshape(bsz, t, -1)
    w = proj.shape[2] // 2
    y, hl = _lru_core(proj, _pad_buf(buf), h0, conv_w, conv_b, w_ga, b_ga, w_gx, b_gx, lam, tt=tt)
    xo = _outproj(y.reshape(bsz * t, w), w_out, x2, mod3, 5, tm=tm).reshape(bsz, t, d)
    return xo, proj[:, t - (CONV_W - 1):, w:], hl


def _bf16_padded(w, mult):
    n = w.shape[-1]
    return jnp.pad(w, ((0, 0), (0, -n % mult))).astype(BF16)


def kernel(x_prompt, x_sample, state_a_conv, state_a_ssm, cache_b_k, cache_b_v, cache_b_idx, state_c_conv, state_c_h, page_table, c_prompt, c_sample, w_ada, b_ada, norm_w, ffn_w_in, ffn_w_out, gdn_w_in, gdn_conv_w, gdn_a_log, gdn_dt_bias, gdn_norm_w, gdn_w_out, dsa_w_in, dsa_w_out, lru_w_in, lru_conv_w, lru_conv_b, lru_w_gate_a, lru_b_gate_a, lru_w_gate_x, lru_b_gate_x, lru_lambda, lru_w_out, w_ada_final, b_ada_final, final_norm_w):
    bp, seq, d = x_prompt.shape
    bs, ts, _ = x_sample.shape
    depth = w_ada.shape[0]
    nkv = cache_b_k.shape[3]
    tm_p = _tile(seq, 512, SUBLANES)
    tt_p = _tile(seq, 256, SUBLANES)
    tm_s = bs * ts

    c_all = jnp.concatenate([c_prompt, c_sample], axis=0)
    c_all = jnp.pad(c_all, ((0, -c_all.shape[0] % SUBLANES), (0, 0)))
    mod = _ada(c_all, w_ada, b_ada)
    mod_f = _ada(c_all, w_ada_final[None], b_ada_final[None])[0]

    def groups(m):
        return m[:bp, None, :], jnp.repeat(m[bp:bp + bs], ts, axis=0)[None]

    xp, xs = x_prompt, x_sample
    outs = {k: [] for k in ("a_conv_p", "a_conv_s", "a_ssm_p", "a_ssm_s", "b_k_p", "b_k_s", "b_v_p", "b_v_s",
                            "b_i_p", "b_i_s", "c_conv_p", "c_conv_s", "c_h_p", "c_h_s")}
    for layer in range(depth):
        kind, j = layer % N_MIXERS, layer // N_MIXERS
        mod_p, mod_s = groups(mod[layer])
        nw = norm_w[layer]

        def ffn(x, m3, tm, which):
            b, t, _ = x.shape
            return _ffn(x.reshape(b * t, d), m3, 6 * which, nw[2 * which:2 * which + 1],
                        ffn_w_in[layer, which].astype(BF16), ffn_w_out[layer, which].astype(BF16),
                        tm=tm).reshape(b, t, d)

        xp = ffn(xp, mod_p, tm_p, 0)
        xs = ffn(xs, mod_s, tm_s, 0)
        if kind == 0:
            prm = (_bf16_padded(gdn_w_in[j], 512), gdn_conv_w[j], gdn_a_log[j], gdn_dt_bias[j], gdn_norm_w[j],
                   gdn_w_out[j].astype(BF16))
            buf0 = jnp.zeros((bp,) + state_a_conv.shape[2:], F32)
            s0 = jnp.zeros((bp,) + state_a_ssm.shape[2:], F32)
            xp, buf, s = _gdn_mixer(xp, mod_p, nw[1:2], buf0, s0, *prm, tm=tm_p, tt=tt_p, chunk=2 * GDN_CHUNK)
            outs["a_conv_p"].append(buf)
            outs["a_ssm_p"].append(s)
            xs, buf, s = _gdn_mixer(xs, mod_s, nw[1:2], state_a_conv[j], state_a_ssm[j], *prm, tm=tm_s, tt=ts,
                                    chunk=GDN_CHUNK)
            outs["a_conv_s"].append(buf)
            outs["a_ssm_s"].append(s)
        elif kind == 1:
            w_in, w_out = _bf16_padded(dsa_w_in[j], 512), dsa_w_out[j].astype(BF16)
            n_proj = dsa_w_in.shape[2]
            xp, k, v, ki = _dsa_mixer(xp, mod_p, nw[1:2], w_in, w_out, n_proj, nkv=nkv, tm=tm_p, tt=tt_p)
            outs["b_k_p"].append(k)
            outs["b_v_p"].append(v)
            outs["b_i_p"].append(ki)
            xs, k, v, ki = _dsa_mixer(xs, mod_s, nw[1:2], w_in, w_out, n_proj, nkv=nkv, tm=tm_s, tt=ts,
                                      cache=(cache_b_k[j], cache_b_v[j], cache_b_idx[j], page_table))
            outs["b_k_s"].append(k)
            outs["b_v_s"].append(v)
            outs["b_i_s"].append(ki)
        else:
            prm = (lru_w_in[j].astype(BF16), lru_conv_w[j], lru_conv_b[j], lru_w_gate_a[j].astype(BF16),
                   lru_b_gate_a[j], lru_w_gate_x[j].astype(BF16), lru_b_gate_x[j], lru_lambda[j],
                   lru_w_out[j].astype(BF16))
            buf0 = jnp.zeros((bp,) + state_c_conv.shape[2:], F32)
            h0 = jnp.zeros((bp,) + state_c_h.shape[2:], F32)
            xp, buf, hl = _lru_mixer(xp, mod_p, nw[1:2], buf0, h0, *prm, tm=tm_p, tt=tt_p)
            outs["c_conv_p"].append(buf)
            outs["c_h_p"].append(hl)
            xs, buf, hl = _lru_mixer(xs, mod_s, nw[1:2], state_c_conv[j], state_c_h[j], *prm, tm=tm_s, tt=ts)
            outs["c_conv_s"].append(buf)
            outs["c_h_s"].append(hl)
        xp = ffn(xp, mod_p, tm_p, 1)
        xs = ffn(xs, mod_s, tm_s, 1)

    modf_p, modf_s = groups(mod_f)
    fnw = final_norm_w.reshape(1, d)
    y_p = _final_norm(xp.reshape(bp * seq, d), modf_p, fnw, tm=tm_p).reshape(bp, seq, d)
    y_s = _final_norm(xs.reshape(bs * ts, d), modf_s, fnw, tm=tm_s).reshape(bs, ts, d)
    st = {k: jnp.stack(v) for k, v in outs.items()}
    return (y_p, y_s, st["a_conv_p"], st["a_conv_s"], st["a_ssm_p"], st["a_ssm_s"], st["b_k_p"], st["b_k_s"],
            st["b_v_p"], st["b_v_s"], st["b_i_p"], st["b_i_s"], st["c_conv_p"], st["c_conv_s"], st["c_h_p"],
            st["c_h_s"])
```

```python
import functools
import math

import jax
import jax.numpy as jnp
from jax import lax
from jax.experimental import pallas as pl
from jax.experimental.pallas import tpu as pltpu

F32 = jnp.float32
BF16 = jnp.bfloat16
I32 = jnp.int32

N_MIXERS = 3
N_ADA = 9
CONV_W = 4
NORM_EPS = 1e-6
NEG_INF = -1e30
GDN_CHUNK = 64
TOPK_MAX = 256
Q_BLOCK = 128
ROPE_THETA = 10000.0
RG_C = 8.0
LRU_BLOCK = 256

LANES = 128
SUBLANES = 8
VMEM_LIMIT_BYTES = 56 * 2**20
INT32_MIN = -2**31


def _cparams(*sem):
    return pltpu.CompilerParams(dimension_semantics=sem, vmem_limit_bytes=VMEM_LIMIT_BYTES)


def _tile(n, target, align=LANES):
    if n <= target:
        return n
    t = (target // align) * align
    while t >= align:
        if n % t == 0:
            return t
        t -= align
    raise ValueError(f"no {align}-aligned tile of {n} below {target}")


def _sigmoid(x):
    return jax.nn.sigmoid(x)


def _silu(x):
    return x * _sigmoid(x)


def _dot(a, b):
    return jnp.dot(a, b, preferred_element_type=F32)


def _dot_nt(a, b):
    return lax.dot_general(a, b, (((1,), (1,)), ((), ())), preferred_element_type=F32)


def _dot_tn(a, b):
    return lax.dot_general(a, b, (((0,), (0,)), ((), ())), preferred_element_type=F32)


def _dot_f32(a, b):
    return jnp.dot(a, b, preferred_element_type=F32, precision=lax.Precision.HIGHEST)


def _norm_mod(x, nw, sh, sc):
    ms = jnp.mean(x * x, axis=-1, keepdims=True)
    y = x * lax.rsqrt(ms + NORM_EPS) * nw
    return y * (1.0 + sc) + sh


def _ada_kernel(c_ref, w_ref, b_ref, o_ref):
    a = _silu(c_ref[...]).astype(BF16)
    o_ref[0] = _dot(a, w_ref[0].astype(BF16)) + b_ref[0]


def _ada(c_all, w, b):
    n_l, d, n = w.shape
    mp = c_all.shape[0]
    tn = _tile(n, 1024)
    return pl.pallas_call(
        _ada_kernel,
        grid=(n_l, n // tn),
        in_specs=[
            pl.BlockSpec((mp, d), lambda l, j: (0, 0)),
            pl.BlockSpec((1, d, tn), lambda l, j: (l, 0, j)),
            pl.BlockSpec((1, 1, tn), lambda l, j: (l, 0, j)),
        ],
        out_specs=pl.BlockSpec((1, mp, tn), lambda l, j: (l, 0, j)),
        out_shape=jax.ShapeDtypeStruct((n_l, mp, n), F32),
        name="ada_mod",
        compiler_params=_cparams("parallel", "parallel"),
    )(c_all, w, b.reshape(n_l, 1, n))


def _mod_spec(mod3, k, d, tm, rows_per_group, grid_rank):
    r = mod3.shape[1]
    if grid_rank == 2:
        return pl.BlockSpec((None, r, d), lambda i, j: ((i * tm) // rows_per_group, 0, k))
    raise ValueError(grid_rank)


def _ffn_kernel(x_ref, nw_ref, sh_ref, sc_ref, g_ref, wa_ref, wb_ref, wo_ref, o_ref, xn_ref, acc_ref):
    f = pl.program_id(1)

    @pl.when(f == 0)
    def _():
        xn_ref[...] = _norm_mod(x_ref[...], nw_ref[...], sh_ref[...], sc_ref[...]).astype(BF16)
        acc_ref[...] = jnp.zeros_like(acc_ref)

    xn = xn_ref[...]
    a = _dot(xn, wa_ref[...])
    b = _dot(xn, wb_ref[...])
    acc_ref[...] += _dot((_silu(a) * b).astype(BF16), wo_ref[...])

    @pl.when(f == pl.num_programs(1) - 1)
    def _():
        o_ref[...] = x_ref[...] + 0.5 * g_ref[...] * acc_ref[...]


def _ffn(x, mod3, k0, nw, w_in, w_out, *, tm, tf_target=512):
    m, d = x.shape
    f = w_out.shape[0]
    tf = _tile(f, tf_target)
    nf = f // tf
    rpg = m // mod3.shape[0]
    ms = lambda k: _mod_spec(mod3, k, d, tm, rpg, 2)
    return pl.pallas_call(
        _ffn_kernel,
        grid=(m // tm, nf),
        in_specs=[
            pl.BlockSpec((tm, d), lambda i, j: (i, 0)),
            pl.BlockSpec((1, d), lambda i, j: (0, 0)),
            ms(k0), ms(k0 + 1), ms(k0 + 2),
            pl.BlockSpec((d, tf), lambda i, j: (0, j)),
            pl.BlockSpec((d, tf), lambda i, j: (0, j + nf)),
            pl.BlockSpec((tf, d), lambda i, j: (j, 0)),
        ],
        out_specs=pl.BlockSpec((tm, d), lambda i, j: (i, 0)),
        out_shape=jax.ShapeDtypeStruct((m, d), F32),
        scratch_shapes=[pltpu.VMEM((tm, d), BF16), pltpu.VMEM((tm, d), F32)],
        name="ffn",
        compiler_params=_cparams("parallel", "arbitrary"),
    )(x, nw, mod3, mod3, mod3, w_in, w_in, w_out)


def _normproj_kernel(x_ref, nw_ref, sh_ref, sc_ref, w_ref, o_ref, xn_ref):
    @pl.when(pl.program_id(1) == 0)
    def _():
        xn_ref[...] = _norm_mod(x_ref[...], nw_ref[...], sh_ref[...], sc_ref[...]).astype(BF16)

    o_ref[...] = _dot(xn_ref[...], w_ref[...])


def _normproj(x, mod3, k0, nw, w, *, tm, tn_target=512):
    m, d = x.shape
    n = w.shape[1]
    tn = _tile(n, tn_target)
    rpg = m // mod3.shape[0]
    ms = lambda k: _mod_spec(mod3, k, d, tm, rpg, 2)
    return pl.pallas_call(
        _normproj_kernel,
        grid=(m // tm, n // tn),
        in_specs=[
            pl.BlockSpec((tm, d), lambda i, j: (i, 0)),
            pl.BlockSpec((1, d), lambda i, j: (0, 0)),
            ms(k0), ms(k0 + 1),
            pl.BlockSpec((d, tn), lambda i, j: (0, j)),
        ],
        out_specs=pl.BlockSpec((tm, tn), lambda i, j: (i, j)),
        out_shape=jax.ShapeDtypeStruct((m, n), F32),
        scratch_shapes=[pltpu.VMEM((tm, d), BF16)],
        name="normproj",
        compiler_params=_cparams("parallel", "arbitrary"),
    )(x, nw, mod3, mod3, w)


def _outproj_kernel(a_ref, w_ref, x_ref, g_ref, o_ref):
    o_ref[...] = x_ref[...] + g_ref[...] * _dot(a_ref[...], w_ref[...])


def _outproj(a, w, x, mod3, kg, *, tm, tn_target=512):
    m, kdim = a.shape
    d = w.shape[1]
    tn = _tile(d, tn_target)
    rpg = m // mod3.shape[0]
    r = mod3.shape[1]
    nd = d // tn
    return pl.pallas_call(
        _outproj_kernel,
        grid=(m // tm, nd),
        in_specs=[
            pl.BlockSpec((tm, kdim), lambda i, j: (i, 0)),
            pl.BlockSpec((kdim, tn), lambda i, j: (0, j)),
            pl.BlockSpec((tm, tn), lambda i, j: (i, j)),
            pl.BlockSpec((None, r, tn), lambda i, j: ((i * tm) // rpg, 0, kg * nd + j)),
        ],
        out_specs=pl.BlockSpec((tm, tn), lambda i, j: (i, j)),
        out_shape=jax.ShapeDtypeStruct((m, d), F32),
        name="outproj",
        compiler_params=_cparams("parallel", "arbitrary"),
    )(a, w, x, mod3)


def _final_kernel(x_ref, nw_ref, sh_ref, sc_ref, o_ref):
    o_ref[...] = _norm_mod(x_ref[...], nw_ref[...], sh_ref[...], sc_ref[...])


def _final_norm(x, mod3, nw, *, tm):
    m, d = x.shape
    rpg = m // mod3.shape[0]
    r = mod3.shape[1]
    ms = lambda k: pl.BlockSpec((None, r, d), lambda i: ((i * tm) // rpg, 0, k))
    return pl.pallas_call(
        _final_kernel,
        grid=(m // tm,),
        in_specs=[pl.BlockSpec((tm, d), lambda i: (i, 0)), pl.BlockSpec((1, d), lambda i: (0, 0)), ms(0), ms(1)],
        out_specs=pl.BlockSpec((tm, d), lambda i: (i, 0)),
        out_shape=jax.ShapeDtypeStruct((m, d), F32),
        name="final_norm",
        compiler_params=_cparams("parallel"),
    )(x, nw, mod3, mod3)


def _causal_conv(x, halo, w):
    tt = x.shape[0]
    row = lax.broadcasted_iota(I32, x.shape, 0)
    acc = x * w[CONV_W - 1:CONV_W]
    for j in range(1, CONV_W):
        xs = pltpu.roll(x, j, 0)
        hs = pltpu.roll(halo, j, 0)
        hs = jnp.concatenate([hs] * (tt // SUBLANES), axis=0) if tt > SUBLANES else hs
        acc = acc + jnp.where(row < j, hs, xs) * w[CONV_W - 1 - j:CONV_W - j]
    return acc


def _conv_specs(tt, tc, coff, boff):
    sub = tt // SUBLANES
    return [
        pl.BlockSpec((None, tt, tc), lambda b, t, c: (b, t, c + coff)),
        pl.BlockSpec((None, SUBLANES, tc), lambda b, t, c: (b, jnp.maximum(t * sub - 1, 0), c + coff)),
        pl.BlockSpec((None, SUBLANES, tc), lambda b, t, c: (b, 0, c + boff)),
    ]


def _pad_buf(buf):
    return jnp.pad(buf, ((0, 0), (SUBLANES - (CONV_W - 1), 0), (0, 0)))


def _gdn_prep_kernel(x_ref, halo_ref, buf_ref, w_ref, o_ref, *, norm):
    halo = jnp.where(pl.program_id(1) == 0, buf_ref[...], halo_ref[...])
    y = _silu(_causal_conv(x_ref[...], halo, w_ref[...]))
    if norm:
        for h in range(y.shape[1] // LANES):
            seg = y[:, h * LANES:(h + 1) * LANES]
            ss = jnp.sum(seg * seg, axis=-1, keepdims=True)
            o_ref[:, h * LANES:(h + 1) * LANES] = seg * lax.rsqrt(ss + NORM_EPS)
    else:
        o_ref[...] = y


def _gdn_prep(proj, buf8, conv_w, *, col0, ncols, norm, tt):
    bsz, t, _ = proj.shape
    tc = _tile(ncols, 512)
    return pl.pallas_call(
        functools.partial(_gdn_prep_kernel, norm=norm),
        grid=(bsz, t // tt, ncols // tc),
        in_specs=_conv_specs(tt, tc, col0 // tc, col0 // tc)
        + [pl.BlockSpec((CONV_W, tc), lambda b, i, c: (0, c + col0 // tc))],
        out_specs=pl.BlockSpec((None, tt, tc), lambda b, i, c: (b, i, c)),
        out_shape=jax.ShapeDtypeStruct((bsz, t, ncols), F32),
        name="gdn_prep",
        compiler_params=_cparams("parallel", "parallel", "parallel"),
    )(proj, proj, buf8, conv_w)


def _gdn_gate_kernel(x_ref, alog_ref, dtb_ref, beta_ref, gc_ref, *, hv, chunk):
    x = x_ref[...]
    tt = x.shape[0]
    beta_ref[...] = _sigmoid(x)
    z = x + dtb_ref[...]
    g = -jnp.exp(alog_ref[...]) * (jnp.maximum(z, 0.0) + jnp.log1p(jnp.exp(-jnp.abs(z))))
    row = lax.broadcasted_iota(I32, (tt, tt), 0)
    col = lax.broadcasted_iota(I32, (tt, tt), 1)
    tri = jnp.where((row >= col) & (row // chunk == col // chunk), 1.0, 0.0)
    gc_ref[...] = _dot_f32(tri, g)


def _gdn_gates(proj, a_log, dt_bias, *, col0, hv, chunk, tt):
    bsz, t, _ = proj.shape
    pad = lambda v: jnp.pad(v.astype(F32), (hv, LANES - 2 * hv)).reshape(1, LANES)
    blk = pl.BlockSpec((None, tt, LANES), lambda b, i: (b, i, col0 // LANES))
    out = pl.BlockSpec((None, tt, LANES), lambda b, i: (b, i, 0))
    par = pl.BlockSpec((1, LANES), lambda b, i: (0, 0))
    return pl.pallas_call(
        functools.partial(_gdn_gate_kernel, hv=hv, chunk=chunk),
        grid=(bsz, t // tt),
        in_specs=[blk, par, par],
        out_specs=[out, out],
        out_shape=[jax.ShapeDtypeStruct((bsz, t, LANES), F32)] * 2,
        name="gdn_gates",
        compiler_params=_cparams("parallel", "parallel"),
    )(proj, pad(a_log), pad(dt_bias))


def _split_bf16(a):
    hi = a.astype(BF16)
    return hi, (a - hi.astype(F32)).astype(BF16)


def _dot_split(a, b):
    ah, al = _split_bf16(a)
    bh, bl = _split_bf16(b)
    return _dot(jnp.concatenate([ah, al, ah], axis=1), jnp.concatenate([bh, bh, bl], axis=0))


def _tri_inv_all(lmats, c, n_real):
    row = lax.broadcasted_iota(I32, (c, c), 0)
    col = lax.broadcasted_iota(I32, (c, c), 1)
    eye = jnp.where(row == col, 1.0, 0.0)
    base = min(16, c)
    ps = [-jnp.where(row // base == col // base, m, 0.0) for m in lmats]
    rs = [eye + p for p in ps]
    n = 2
    while n < base:
        ps = [_dot_split(p, p) for p in ps]
        rs = [r + _dot_split(r, p) for r, p in zip(rs, ps)]
        n *= 2
    s = base
    while s < min(c, n_real):
        off = (row // (2 * s) == col // (2 * s)) & (row // s != col // s)
        ts = [_dot_split(jnp.where(off, m, 0.0), r) for m, r in zip(lmats, rs)]
        rs = [r - _dot_split(r, t) for r, t in zip(rs, ts)]
        s *= 2
    return rs


def _gdn_core_kernel(q_ref, k_ref, v_ref, z_ref, gcc_ref, gcr_ref, bc_ref, s0_ref, nw_ref, o_ref, so_ref, s_ref,
                     *, rep, c, hpb, dk, n_real):
    ci = pl.program_id(2)

    @pl.when(ci == 0)
    def _():
        s_ref[...] = s0_ref[...]

    row = lax.broadcasted_iota(I32, (c, c), 0)
    col = lax.broadcasted_iota(I32, (c, c), 1)
    causal = row >= col
    heads = [(hh, r) for hh in range(hpb) for r in range(rep)]
    lanes = [slice((hh * rep + r) * LANES, (hh * rep + r + 1) * LANES) for hh, r in heads]
    qs = [q_ref[:, hh * LANES:(hh + 1) * LANES] * dk ** -0.5 for hh in range(hpb)]
    ks = [k_ref[:, hh * LANES:(hh + 1) * LANES] for hh in range(hpb)]
    k16 = [k.astype(BF16) for k in ks]
    grams = [_dot_nt(kb, kb) for kb in k16]
    qk0s = [_dot_nt(q.astype(BF16), kb) for q, kb in zip(qs, k16)]
    gccs = [gcc_ref[hh, :, r:r + 1] for hh, r in heads]
    gcrs = [gcr_ref[hh, r:r + 1, :] for hh, r in heads]
    betas = [bc_ref[hh, :, r:r + 1] for hh, r in heads]
    decays = [jnp.where(causal, jnp.exp(jnp.where(causal, gc - gr, 0.0)), 0.0) for gc, gr in zip(gccs, gcrs)]
    lowers = [jnp.where(row > col, grams[hh] * b * d, 0.0) for (hh, _), b, d in zip(heads, betas, decays)]
    tinvs = _tri_inv_all(lowers, c, n_real)
    egcs = [jnp.exp(gc) for gc in gccs]
    sols = [_dot_split(ti, jnp.concatenate([v_ref[:, ln] * b, ks[hh] * (b * e)], axis=1))
            for ti, ln, b, e, (hh, _) in zip(tinvs, lanes, betas, egcs, heads)]
    qg16 = [(qs[hh] * e).astype(BF16) for (hh, _), e in zip(heads, egcs)]
    qk16 = [(qk0s[hh] * d).astype(BF16) for (hh, _), d in zip(heads, decays)]
    g_last = [gc[c - 1:c, :] for gc in gccs]
    kd16 = [(ks[hh] * jnp.exp(gl - gc)).astype(BF16) for (hh, _), gl, gc in zip(heads, g_last, gccs)]
    ss = [s_ref[i] for i in range(len(heads))]
    s16 = [s.astype(BF16) for s in ss]
    us = [sol[:, :LANES] - _dot(sol[:, LANES:].astype(BF16), sb) for sol, sb in zip(sols, s16)]
    u16 = [u.astype(BF16) for u in us]
    for i in range(len(heads)):
        s_ref[i] = ss[i] * jnp.exp(g_last[i]) + _dot_tn(kd16[i], u16[i])
    os_ = [_dot(a, sb) + _dot(b, ub) for a, sb, b, ub in zip(qg16, s16, qk16, u16)]
    for o, ln in zip(os_, lanes):
        on = o * lax.rsqrt(jnp.mean(o * o, axis=-1, keepdims=True) + NORM_EPS) * nw_ref[...]
        o_ref[:, ln] = (on * _silu(z_ref[:, ln])).astype(BF16)

    @pl.when(ci == pl.num_programs(2) - 1)
    def _():
        so_ref[...] = s_ref[...]


def _gdn_core(qk, v, proj, zcol0, gc, beta, s0, norm_w, *, c, hpb, n_real):
    bsz, t, val = v.shape
    hv = s0.shape[1]
    dk, dv = s0.shape[2], s0.shape[3]
    hk = qk.shape[2] // (2 * dk)
    rep = hv // hk
    tb = c
    assert dk == LANES and dv == LANES and t % tb == 0 and hk % hpb == 0 and zcol0 % (hpb * rep * dv) == 0
    heads = lambda a, lo: a[:, :, lo:lo + hv].reshape(bsz, t, hk, rep).transpose(0, 2, 1, 3)
    gcc = heads(gc, hv)
    bcc = heads(beta, 0)
    gcr = gcc.transpose(0, 1, 3, 2)
    nhb = hk // hpb
    zb = zcol0 // (hpb * rep * dv)
    colspec = pl.BlockSpec((None, hpb, tb, rep), lambda b, h, i: (b, h, i, 0))
    o, s_out = pl.pallas_call(
        functools.partial(_gdn_core_kernel, rep=rep, c=c, hpb=hpb, dk=dk, n_real=n_real),
        grid=(bsz, nhb, t // tb),
        in_specs=[
            pl.BlockSpec((None, tb, hpb * dk), lambda b, h, i: (b, i, h)),
            pl.BlockSpec((None, tb, hpb * dk), lambda b, h, i: (b, i, nhb + h)),
            pl.BlockSpec((None, tb, hpb * rep * dv), lambda b, h, i: (b, i, h)),
            pl.BlockSpec((None, tb, hpb * rep * dv), lambda b, h, i: (b, i, zb + h)),
            colspec,
            pl.BlockSpec((None, hpb, rep, tb), lambda b, h, i: (b, h, 0, i)),
            colspec,
            pl.BlockSpec((None, hpb * rep, dk, dv), lambda b, h, i: (b, h, 0, 0)),
            pl.BlockSpec((1, dv), lambda b, h, i: (0, 0)),
        ],
        out_specs=[
            pl.BlockSpec((None, tb, hpb * rep * dv), lambda b, h, i: (b, i, h)),
            pl.BlockSpec((None, hpb * rep, dk, dv), lambda b, h, i: (b, h, 0, 0)),
        ],
        out_shape=[jax.ShapeDtypeStruct((bsz, t, val), BF16), jax.ShapeDtypeStruct(s0.shape, F32)],
        scratch_shapes=[pltpu.VMEM((hpb * rep, dk, dv), F32)],
        name="gdn_core",
        compiler_params=_cparams("parallel", "parallel", "arbitrary"),
    )(qk, qk, v, proj, gcc, gcr, bcc, s0, norm_w.reshape(1, dv))
    return o, s_out


def _gdn_mixer(x, mod3, nw, buf, s0, w_in, conv_w, a_log, dt_bias, norm_w, w_out, *, tm, tt, chunk):
    bsz, t, d = x.shape
    hv, dk, dv = s0.shape[1], s0.shape[2], s0.shape[3]
    val = hv * dv
    conv_dim = conv_w.shape[1]
    key = (conv_dim - val) // 2
    assert (conv_dim + val) % LANES == 0 and 2 * hv <= LANES and t >= CONV_W - 1
    x2 = x.reshape(bsz * t, d)
    proj = _normproj(x2, mod3, 3, nw, w_in, tm=tm).reshape(bsz, t, -1)
    buf8 = _pad_buf(buf)
    qk = _gdn_prep(proj, buf8, conv_w, col0=0, ncols=2 * key, norm=True, tt=tt)
    v = _gdn_prep(proj, buf8, conv_w, col0=2 * key, ncols=val, norm=False, tt=tt)
    tp = -(-t // chunk) * chunk
    beta, gc = _gdn_gates(proj, a_log, dt_bias, col0=conv_dim + val, hv=hv, chunk=min(chunk, tt), tt=tt)
    if tp != t:
        padt = lambda a: jnp.pad(a, ((0, 0), (0, tp - t), (0, 0)))
        gc = jnp.concatenate([gc, jnp.broadcast_to(gc[:, -1:], (bsz, tp - t, LANES))], axis=1)
        qk, v, beta, projz = padt(qk), padt(v), padt(beta), padt(proj)
    else:
        projz = proj
    hk = key // dk
    o, s_new = _gdn_core(qk, v, projz, conv_dim, gc, beta, s0, norm_w, c=chunk, hpb=min(4, hk),
                         n_real=min(t, chunk))
    o2 = o[:, :t].reshape(bsz * t, val)
    xo = _outproj(o2, w_out, x2, mod3, 5, tm=tm).reshape(bsz, t, d)
    new_buf = proj[:, t - (CONV_W - 1):, :conv_dim]
    return xo, new_buf, s_new


def _rope_tables(pos, half):
    inv_freq = ROPE_THETA ** (-jnp.arange(half, dtype=F32) / half)
    ang = pos.astype(F32)[:, None] * inv_freq[None, :]
    cos, sin = jnp.cos(ang), jnp.sin(ang)
    return jnp.concatenate([cos, cos], axis=-1), jnp.concatenate([-sin, sin], axis=-1)


def _dsa_prep_kernel(x_ref, cos_ref, sin_ref, q_ref, k_ref, k16_ref, v16_ref, qi_ref, ki_ref, ki16_ref, *tr_refs,
                     nh, nkv, nih):
    cos, sin = cos_ref[...], sin_ref[...]

    def rope(col):
        seg = x_ref[:, col * LANES:(col + 1) * LANES]
        return seg * cos + pltpu.roll(seg, LANES // 2, 1) * sin

    for h in range(nh):
        q_ref[:, h * LANES:(h + 1) * LANES] = rope(h).astype(BF16)
    for h in range(nkv):
        kr = rope(nh + h)
        k_ref[:, h * LANES:(h + 1) * LANES] = kr
        k16_ref[:, h * LANES:(h + 1) * LANES] = kr.astype(BF16)
    v0 = (nh + nkv) * LANES
    v16_ref[...] = x_ref[:, v0:v0 + nkv * LANES].astype(BF16)
    c0 = nh + 2 * nkv
    for h in range(nih):
        qi_ref[:, h * LANES:(h + 1) * LANES] = rope(c0 + h).astype(BF16)
    kir = rope(c0 + nih)
    ki_ref[...] = kir
    ki16_ref[...] = kir.astype(BF16)
    if tr_refs:
        wt_ref, vt_ref = tr_refs
        w0 = (c0 + nih + 1) * LANES
        wt_ref[...] = x_ref[:, w0:w0 + LANES].T[:wt_ref.shape[0], :]
        for h in range(nkv):
            vt_ref[h * LANES:(h + 1) * LANES, :] = x_ref[:, v0 + h * LANES:v0 + (h + 1) * LANES].T.astype(BF16)


def _dsa_prep(proj, pos, *, nh, nkv, nih, tt, with_wt):
    bsz, t, npj = proj.shape
    cos, sin = _rope_tables(pos, LANES // 2)
    row = lambda n, dt: jax.ShapeDtypeStruct((bsz, t, n * LANES), dt)
    ospec = lambda n: pl.BlockSpec((None, tt, n * LANES), lambda b, i: (b, i, 0))
    tab = pl.BlockSpec((tt, LANES), lambda b, i: (i, 0))
    nwt = -(-nih // SUBLANES) * SUBLANES
    return pl.pallas_call(
        functools.partial(_dsa_prep_kernel, nh=nh, nkv=nkv, nih=nih),
        grid=(bsz, t // tt),
        in_specs=[pl.BlockSpec((None, tt, npj), lambda b, i: (b, i, 0)), tab, tab],
        out_specs=[ospec(nh), ospec(nkv), ospec(nkv), ospec(nkv), ospec(nih), ospec(1), ospec(1)]
        + ([pl.BlockSpec((None, nwt, tt), lambda b, i: (b, 0, i)),
            pl.BlockSpec((None, None, nkv * LANES, tt), lambda b, i: (b, i, 0, 0))] if with_wt else []),
        out_shape=[row(nh, BF16), row(nkv, F32), row(nkv, BF16), row(nkv, BF16), row(nih, BF16), row(1, F32),
                   row(1, BF16)]
        + ([jax.ShapeDtypeStruct((bsz, nwt, t), F32),
            jax.ShapeDtypeStruct((bsz, t // tt, nkv * LANES, tt), BF16)] if with_wt else []),
        name="dsa_prep",
        compiler_params=_cparams("parallel", "parallel"),
    )(proj, cos, sin)


def _sort_key(s):
    bits = pltpu.bitcast(jnp.where(s == 0.0, 0.0, s), I32)
    return jnp.where(bits < 0, bits ^ 0x7FFFFFFF, bits)


def _kth_largest_key(count_ge, shape, topk):
    def body(i, t):
        cand = t + lax.shift_left(jnp.int32(1), 31 - i)
        return jnp.where(count_ge(cand) >= topk, cand, t)
    return lax.fori_loop(0, 32, body, jnp.full(shape, INT32_MIN, I32))


def _dsa_prompt_kernel(qi_ref, wt_ref, q_ref, ki_ref, k_ref, vt_ref, o_ref, key_ref, bias_ref, acc_ref,
                       *, nh, nkv, nih, topk, idx_scale):
    qb = pl.program_id(1)
    blk = Q_BLOCK
    nkt = qb + 1
    rowk = lax.broadcasted_iota(I32, (blk, blk), 0)
    colq = lax.broadcasted_iota(I32, (blk, blk), 1)
    qi = jnp.concatenate([qi_ref[:, h * LANES:(h + 1) * LANES] for h in range(nih)], axis=0)
    wt = wt_ref[...]

    def visible(kt):
        return kt * blk + rowk <= qb * blk + colq

    def score_body(kt, carry):
        lg = _dot_nt(ki_ref[pl.ds(pl.multiple_of(kt * blk, blk), blk), :], qi)
        acc = jnp.zeros((blk, blk), F32)
        for h in range(nih):
            acc = acc + jnp.maximum(lg[:, h * blk:(h + 1) * blk], 0.0) * wt[h:h + 1, :]
        key_ref[kt] = _sort_key(jnp.where(visible(kt), acc * idx_scale, NEG_INF))
        return carry

    lax.fori_loop(0, nkt, score_body, 0)

    def count_ge(cand):
        cnt = lax.fori_loop(0, nkt, lambda kt, c: c + jnp.where(key_ref[kt] >= cand, 1, 0),
                            jnp.zeros((blk, blk), I32))
        return jnp.sum(cnt, axis=0, keepdims=True)

    thr = _kth_largest_key(count_ge, (1, blk), topk)

    kb = vt_ref.shape[2]
    ktiles = kb // blk
    nkb = (qb + ktiles) // ktiles

    def bias_body(kt, carry):
        sel = (key_ref[kt] >= thr) & visible(kt)
        bias_ref[pl.ds(pl.multiple_of(kt * blk, blk), blk), :] = jnp.where(sel, 0.0, NEG_INF)
        return carry

    lax.fori_loop(0, nkt, bias_body, 0)

    def tail_body(kt, carry):
        bias_ref[pl.ds(pl.multiple_of(kt * blk, blk), blk), :] = jnp.full((blk, blk), NEG_INF, F32)
        return carry

    lax.fori_loop(nkt, nkb * ktiles, tail_body, 0)

    rep = nh // nkv
    scale = LANES ** -0.5
    qgs = [jnp.concatenate([q_ref[:, (g * rep + r) * LANES:(g * rep + r + 1) * LANES] for r in range(rep)], axis=0)
           for g in range(nkv)]
    acc_ref[...] = jnp.zeros_like(acc_ref)

    def att_body(j, carry):
        ms, ls = carry
        rows = pl.ds(pl.multiple_of(j * kb, kb), kb)
        bias = jnp.concatenate([bias_ref[rows, :]] * rep, axis=1)
        ss = [_dot_nt(k_ref[rows, g * LANES:(g + 1) * LANES], qgs[g]) * scale + bias for g in range(nkv)]
        m_new = [jnp.maximum(m, jnp.max(s, axis=0, keepdims=True)) for m, s in zip(ms, ss)]
        ps = [jnp.exp(s - m) for s, m in zip(ss, m_new)]
        pvs = [_dot(vt_ref[j, g * LANES:(g + 1) * LANES, :], p.astype(BF16)) for g, p in enumerate(ps)]
        alphas = [jnp.exp(m - mn) for m, mn in zip(ms, m_new)]
        for g in range(nkv):
            acc_ref[g] = alphas[g] * acc_ref[g] + pvs[g]
        l_new = [a * l + jnp.sum(p, axis=0, keepdims=True) for a, l, p in zip(alphas, ls, ps)]
        return tuple(m_new), tuple(l_new)

    row0 = lambda v: tuple(jnp.full((1, rep * blk), v, F32) for _ in range(nkv))
    _, ls = lax.fori_loop(0, nkb, att_body, (row0(NEG_INF), row0(0.0)))
    for g in range(nkv):
        og = acc_ref[g] / ls[g]
        for r in range(rep):
            h = g * rep + r
            o_ref[:, h * LANES:(h + 1) * LANES] = og[:, r * blk:(r + 1) * blk].T.astype(BF16)


def _dsa_prompt_attend(q16, qi16, wt, ki16, k16, vt16, *, nh, nkv, nih, topk):
    bsz, t, _ = q16.shape
    nq = t // Q_BLOCK
    rep = nh // nkv
    full = lambda n: pl.BlockSpec((None, t, n * LANES), lambda b, i: (b, 0, 0))
    blk = lambda n: pl.BlockSpec((None, Q_BLOCK, n * LANES), lambda b, i: (b, i, 0))
    return pl.pallas_call(
        functools.partial(_dsa_prompt_kernel, nh=nh, nkv=nkv, nih=nih, topk=topk,
                          idx_scale=(LANES * nih) ** -0.5),
        grid=(bsz, nq),
        in_specs=[blk(nih), pl.BlockSpec((None, wt.shape[1], Q_BLOCK), lambda b, i: (b, 0, i)), blk(nh),
                  full(1), full(nkv), pl.BlockSpec((None,) + vt16.shape[1:], lambda b, i: (b, 0, 0, 0))],
        out_specs=blk(nh),
        out_shape=jax.ShapeDtypeStruct((bsz, t, nh * LANES), BF16),
        scratch_shapes=[pltpu.VMEM((nq, Q_BLOCK, Q_BLOCK), I32), pltpu.VMEM((t, Q_BLOCK), F32),
                        pltpu.VMEM((nkv, LANES, rep * Q_BLOCK), F32)],
        name="dsa_prompt_attend",
        compiler_params=_cparams("parallel", "arbitrary"),
    )(qi16, wt, q16, ki16, k16, vt16)


def _dsa_sample_score_kernel(pt_ref, qi_ref, wc_ref, page_ref, new_ref, o_ref, *, n_pages, nih, tq, past, idx_scale):
    p = pl.program_id(1)
    keys = jnp.where(p == n_pages, new_ref[...], page_ref[...].astype(BF16))
    qi = jnp.concatenate([qi_ref[:, h * LANES:(h + 1) * LANES] for h in range(nih)], axis=0)
    w = jnp.maximum(_dot_nt(qi, keys), 0.0) * wc_ref[...]
    acc = w[0:tq]
    for h in range(1, nih):
        acc = acc + w[h * tq:(h + 1) * tq]
    s = p * LANES + lax.broadcasted_iota(I32, (tq, LANES), 1)
    qpos = past + lax.broadcasted_iota(I32, (tq, LANES), 0)
    o_ref[...] = jnp.where(s <= qpos, acc * idx_scale, NEG_INF)


def _dsa_sample_select_kernel(s_ref, o_ref, *, topk):
    key = _sort_key(s_ref[...])
    thr = _kth_largest_key(lambda cand: jnp.sum(jnp.where(key >= cand, 1, 0), axis=-1, keepdims=True),
                           (key.shape[0], 1), topk)
    o_ref[...] = jnp.where((key >= thr) & (s_ref[...] > 0.5 * NEG_INF), 0.0, NEG_INF)


def _dsa_sample_attn_kernel(pt_ref, q_ref, bias_ref, kp_ref, vp_ref, kn_ref, vn_ref, o_ref, m_ref, l_ref, acc_ref,
                            *, n_pages, nh, nkv, tq):
    p = pl.program_id(1)

    @pl.when(p == 0)
    def _():
        m_ref[...] = jnp.full_like(m_ref, NEG_INF)
        l_ref[...] = jnp.zeros_like(l_ref)
        acc_ref[...] = jnp.zeros_like(acc_ref)

    last = p == n_pages
    kpg = jnp.where(last, kn_ref[...], kp_ref[...].astype(BF16))
    vpg = jnp.where(last, vn_ref[...], vp_ref[...].astype(BF16))
    rep = nh // nkv
    q = jnp.concatenate([q_ref[:, h * LANES:(h + 1) * LANES] for h in range(nh)], axis=0)
    rows = rep * tq
    s = jnp.concatenate([_dot_nt(q[g * rows:(g + 1) * rows], kpg[:, g * LANES:(g + 1) * LANES])
                         for g in range(nkv)], axis=0) * LANES ** -0.5
    s = s + jnp.concatenate([bias_ref[...]] * nh, axis=0)
    m = m_ref[...]
    m_new = jnp.maximum(m, jnp.max(s, axis=-1, keepdims=True))
    alpha = jnp.exp(m - m_new)
    pr = jnp.exp(s - m_new)
    l_ref[...] = alpha * l_ref[...] + jnp.sum(pr, axis=-1, keepdims=True)
    pv = jnp.concatenate([_dot(pr[g * rows:(g + 1) * rows].astype(BF16), vpg[:, g * LANES:(g + 1) * LANES])
                          for g in range(nkv)], axis=0)
    acc_ref[...] = alpha * acc_ref[...] + pv
    m_ref[...] = m_new

    @pl.when(last)
    def _():
        og = acc_ref[...] / l_ref[...]
        for h in range(nh):
            o_ref[:, h * LANES:(h + 1) * LANES] = og[h * tq:(h + 1) * tq].astype(BF16)


def _dsa_sample_attend(q16, qi16, wi, ki16, k16, v16, cache_k, cache_v, cache_i, layer, page_table,
                       *, nh, nkv, nih, topk):
    bsz, tq, _ = q16.shape
    n_pages = page_table.shape[1]
    n_layers, n_pool, page = cache_i.shape[:3]
    assert page == LANES and tq == SUBLANES
    past = n_pages * page
    ltot = past + page
    padk = lambda a: jnp.pad(a, ((0, 0), (0, page - tq), (0, 0)))
    wcol = wi.transpose(0, 2, 1).reshape(bsz, nih * tq, 1)
    pidx = lambda b, p, pt: (layer * n_pool + pt[b, jnp.minimum(p, n_pages - 1)], 0, 0)
    cache_i = cache_i.reshape(n_layers * n_pool, page, LANES)
    qspec = lambda n: pl.BlockSpec((None, tq, n * LANES), lambda b, p, pt: (b, 0, 0))
    newspec = lambda n: pl.BlockSpec((None, page, n * LANES), lambda b, p, pt: (b, 0, 0))
    idx_scale = (LANES * nih) ** -0.5
    scores = pl.pallas_call(
        functools.partial(_dsa_sample_score_kernel, n_pages=n_pages, nih=nih, tq=tq, past=past, idx_scale=idx_scale),
        grid_spec=pltpu.PrefetchScalarGridSpec(
            num_scalar_prefetch=1, grid=(bsz, n_pages + 1),
            in_specs=[qspec(nih), pl.BlockSpec((None, nih * tq, 1), lambda b, p, pt: (b, 0, 0)),
                      pl.BlockSpec((None, page, LANES), pidx), newspec(1)],
            out_specs=pl.BlockSpec((None, tq, LANES), lambda b, p, pt: (b, 0, p))),
        out_shape=jax.ShapeDtypeStruct((bsz, tq, ltot), F32),
        name="dsa_sample_scores",
        compiler_params=_cparams("parallel", "arbitrary"),
    )(page_table, qi16, wcol, cache_i, padk(ki16))
    bias = pl.pallas_call(
        functools.partial(_dsa_sample_select_kernel, topk=topk),
        grid=(bsz,),
        in_specs=[pl.BlockSpec((None, tq, ltot), lambda b: (b, 0, 0))],
        out_specs=pl.BlockSpec((None, tq, ltot), lambda b: (b, 0, 0)),
        out_shape=jax.ShapeDtypeStruct((bsz, tq, ltot), F32),
        name="dsa_sample_select",
        compiler_params=_cparams("parallel"),
    )(scores)
    ck = cache_k.reshape(n_layers * n_pool, page, nkv * LANES)
    cv = cache_v.reshape(n_layers * n_pool, page, nkv * LANES)
    kvspec = pl.BlockSpec((None, page, nkv * LANES), pidx)
    return pl.pallas_call(
        functools.partial(_dsa_sample_attn_kernel, n_pages=n_pages, nh=nh, nkv=nkv, tq=tq),
        grid_spec=pltpu.PrefetchScalarGridSpec(
            num_scalar_prefetch=1, grid=(bsz, n_pages + 1),
            in_specs=[qspec(nh), pl.BlockSpec((None, tq, LANES), lambda b, p, pt: (b, 0, p)),
                      kvspec, kvspec, newspec(nkv), newspec(nkv)],
            out_specs=qspec(nh),
            scratch_shapes=[pltpu.VMEM((nh * tq, 1), F32), pltpu.VMEM((nh * tq, 1), F32),
                            pltpu.VMEM((nh * tq, LANES), F32)]),
        out_shape=jax.ShapeDtypeStruct((bsz, tq, nh * LANES), BF16),
        name="dsa_sample_attend",
        compiler_params=_cparams("parallel", "arbitrary"),
    )(page_table, q16, bias, ck, cv, padk(k16), padk(v16))


def _dsa_mixer(x, mod3, nw, w_in, w_out, n_proj, *, nkv, tm, tt, cache=None):
    bsz, t, d = x.shape
    nh = w_out.shape[0] // LANES
    nih = (n_proj - (nh + 2 * nkv + 1) * LANES) // (LANES + 1)
    assert (nh + 2 * nkv + nih + 1) * LANES + nih == n_proj and nih <= LANES
    x2 = x.reshape(bsz * t, d)
    proj = _normproj(x2, mod3, 3, nw, w_in, tm=tm).reshape(bsz, t, -1)
    v = proj[:, :, (nh + nkv) * LANES:(nh + 2 * nkv) * LANES]
    if cache is None:
        assert t % Q_BLOCK == 0
        q16, k, k16, _, qi16, ki, ki16, wt, vt16 = _dsa_prep(proj, jnp.arange(t), nh=nh, nkv=nkv, nih=nih,
                                                             tt=_tile(t, 4 * Q_BLOCK, Q_BLOCK), with_wt=True)
        o = _dsa_prompt_attend(q16, qi16, wt, ki16, k16, vt16, nh=nh, nkv=nkv, nih=nih, topk=min(TOPK_MAX, t // 4))
    else:
        cache_k, cache_v, cache_i, layer, page_table = cache
        past = page_table.shape[1] * cache_i.shape[2]
        q16, k, k16, v16, qi16, ki, ki16 = _dsa_prep(proj, past + jnp.arange(t), nh=nh, nkv=nkv, nih=nih, tt=tt,
                                                    with_wt=False)
        w0 = (nh + 2 * nkv + nih + 1) * LANES
        o = _dsa_sample_attend(q16, qi16, proj[:, :, w0:w0 + nih], ki16, k16, v16, cache_k, cache_v, cache_i,
                               layer, page_table, nh=nh, nkv=nkv, nih=nih, topk=min(TOPK_MAX, (past + t) // 4))
    xo = _outproj(o.reshape(bsz * t, nh * LANES), w_out, x2, mod3, 5, tm=tm).reshape(bsz, t, d)
    return xo, k.reshape(bsz, t, nkv, LANES), v.reshape(bsz, t, nkv, LANES), ki


def _gelu_tanh(x):
    return 0.5 * x * (1.0 + jnp.tanh(math.sqrt(2.0 / math.pi) * (x + 0.044715 * (x * x * x))))


def _lru_kernel(gate_ref, xb_ref, halo_ref, buf_ref, cw_ref, cb_ref, wa_ref, ba_ref, wx_ref, bx_ref, lam_ref,
                h0_ref, y_ref, hl_ref, h_ref, *, nblk):
    ti = pl.program_id(1)

    @pl.when(ti == 0)
    def _():
        h_ref[...] = h0_ref[...]

    halo = jnp.where(ti == 0, buf_ref[...], halo_ref[...])
    xc = _causal_conv(xb_ref[...], halo, cw_ref[...]) + cb_ref[...]
    tt = xc.shape[0]
    xc16 = xc.astype(BF16)
    rs, xs = [], []
    for n in range(nblk):
        blk = xc16[:, n * LRU_BLOCK:(n + 1) * LRU_BLOCK]
        rs.append(_dot(blk, wa_ref[n]))
        xs.append(_dot(blk, wx_ref[n]))
    r = _sigmoid(jnp.concatenate(rs, axis=1) + ba_ref[...])
    ig = _sigmoid(jnp.concatenate(xs, axis=1) + bx_ref[...])
    lam = lam_ref[...]
    softplus_neg = jnp.maximum(-lam, 0.0) + jnp.log1p(jnp.exp(-jnp.abs(lam)))
    log_a = -RG_C * r * softplus_neg
    a = jnp.exp(log_a)
    b = jnp.sqrt(-jnp.tanh(log_a) * (a * a + 1.0)) * (ig * xc)
    row = lax.broadcasted_iota(I32, a.shape, 0)
    d = 1
    while d < tt:
        keep = row >= d
        a_sh = jnp.where(keep, pltpu.roll(a, d, 0), 1.0)
        b_sh = jnp.where(keep, pltpu.roll(b, d, 0), 0.0)
        b = a * b_sh + b
        a = a * a_sh
        d *= 2
    hs = b + a * h_ref[...]
    h_ref[...] = hs[tt - 1:tt, :]
    y_ref[...] = (hs * _gelu_tanh(gate_ref[...])).astype(BF16)

    @pl.when(ti == pl.num_programs(1) - 1)
    def _():
        hl_ref[...] = hs[tt - 1:tt, :]


def _lru_core(proj, buf8, h0, conv_w, conv_b, w_ga, b_ga, w_gx, b_gx, lam, *, tt):
    bsz, t, w2 = proj.shape
    w = w2 // 2
    nblk = w // LRU_BLOCK
    sub = tt // SUBLANES
    vec = lambda: pl.BlockSpec((1, w), lambda b, i: (0, 0))
    wsp = lambda: pl.BlockSpec((nblk, LRU_BLOCK, LRU_BLOCK), lambda b, i: (0, 0, 0))
    r1 = lambda v: v.reshape(1, w).astype(F32)
    y, hl = pl.pallas_call(
        functools.partial(_lru_kernel, nblk=nblk),
        grid=(bsz, t // tt),
        in_specs=[
            pl.BlockSpec((None, tt, w), lambda b, i: (b, i, 0)),
            pl.BlockSpec((None, tt, w), lambda b, i: (b, i, 1)),
            pl.BlockSpec((None, SUBLANES, w), lambda b, i: (b, jnp.maximum(i * sub - 1, 0), 1)),
            pl.BlockSpec((None, SUBLANES, w), lambda b, i: (b, 0, 0)),
            pl.BlockSpec((CONV_W, w), lambda b, i: (0, 0)),
            vec(), wsp(), vec(), wsp(), vec(), vec(),
            pl.BlockSpec((None, 1, w), lambda b, i: (b, 0, 0)),
        ],
        out_specs=[pl.BlockSpec((None, tt, w), lambda b, i: (b, i, 0)),
                   pl.BlockSpec((None, 1, w), lambda b, i: (b, 0, 0))],
        out_shape=[jax.ShapeDtypeStruct((bsz, t, w), BF16), jax.ShapeDtypeStruct((bsz, 1, w), F32)],
        scratch_shapes=[pltpu.VMEM((1, w), F32)],
        name="lru_core",
        compiler_params=_cparams("parallel", "arbitrary"),
    )(proj, proj, proj, buf8, conv_w, r1(conv_b), w_ga, r1(b_ga), w_gx, r1(b_gx), r1(lam), h0.reshape(bsz, 1, w))
    return y, hl.reshape(bsz, w)


def _lru_mixer(x, mod3, nw, buf, h0, w_in, conv_w, conv_b, w_ga, b_ga, w_gx, b_gx, lam, w_out, *, tm, tt):
    bsz, t, d = x.shape
    assert t >= CONV_W - 1
    x2 = x.reshape(bsz * t, d)
    proj = _normproj(x2, mod3, 3, nw, w_in, tm=tm).reshape(bsz, t, -1)
    w = proj.shape[2] // 2
    y, hl = _lru_core(proj, _pad_buf(buf), h0, conv_w, conv_b, w_ga, b_ga, w_gx, b_gx, lam, tt=tt)
    xo = _outproj(y.reshape(bsz * t, w), w_out, x2, mod3, 5, tm=tm).reshape(bsz, t, d)
    return xo, proj[:, t - (CONV_W - 1):, w:], hl


def _bf16_padded(w, mult):
    n = w.shape[-1]
    return jnp.pad(w, ((0, 0), (0, -n % mult))).astype(BF16)


def kernel(x_prompt, x_sample, state_a_conv, state_a_ssm, cache_b_k, cache_b_v, cache_b_idx, state_c_conv, state_c_h, page_table, c_prompt, c_sample, w_ada, b_ada, norm_w, ffn_w_in, ffn_w_out, gdn_w_in, gdn_conv_w, gdn_a_log, gdn_dt_bias, gdn_norm_w, gdn_w_out, dsa_w_in, dsa_w_out, lru_w_in, lru_conv_w, lru_conv_b, lru_w_gate_a, lru_b_gate_a, lru_w_gate_x, lru_b_gate_x, lru_lambda, lru_w_out, w_ada_final, b_ada_final, final_norm_w):
    bp, seq, d = x_prompt.shape
    bs, ts, _ = x_sample.shape
    depth = w_ada.shape[0]
    nkv = cache_b_k.shape[3]
    tm_p = _tile(seq, 512, SUBLANES)
    tt_p = _tile(seq, 256, SUBLANES)
    tm_s = bs * ts

    c_all = jnp.concatenate([c_prompt, c_sample], axis=0)
    c_all = jnp.pad(c_all, ((0, -c_all.shape[0] % SUBLANES), (0, 0)))
    mod = _ada(c_all, w_ada, b_ada)
    mod_f = _ada(c_all, w_ada_final[None], b_ada_final[None])[0]

    def groups(m):
        return m[:bp, None, :], jnp.repeat(m[bp:bp + bs], ts, axis=0)[None]

    xp, xs = x_prompt, x_sample
    outs = {k: [] for k in ("a_conv_p", "a_conv_s", "a_ssm_p", "a_ssm_s", "b_k_p", "b_k_s", "b_v_p", "b_v_s",
                            "b_i_p", "b_i_s", "c_conv_p", "c_conv_s", "c_h_p", "c_h_s")}
    for layer in range(depth):
        kind, j = layer % N_MIXERS, layer // N_MIXERS
        mod_p, mod_s = groups(mod[layer])
        nw = norm_w[layer]

        def ffn(x, m3, tm, which):
            b, t, _ = x.shape
            return _ffn(x.reshape(b * t, d), m3, 6 * which, nw[2 * which:2 * which + 1],
                        ffn_w_in[layer, which].astype(BF16), ffn_w_out[layer, which].astype(BF16),
                        tm=tm).reshape(b, t, d)

        xp = ffn(xp, mod_p, tm_p, 0)
        xs = ffn(xs, mod_s, tm_s, 0)
        if kind == 0:
            prm = (_bf16_padded(gdn_w_in[j], 512), gdn_conv_w[j], gdn_a_log[j], gdn_dt_bias[j], gdn_norm_w[j],
                   gdn_w_out[j].astype(BF16))
            buf0 = jnp.zeros((bp,) + state_a_conv.shape[2:], F32)
            s0 = jnp.zeros((bp,) + state_a_ssm.shape[2:], F32)
            xp, buf, s = _gdn_mixer(xp, mod_p, nw[1:2], buf0, s0, *prm, tm=tm_p, tt=tt_p, chunk=2 * GDN_CHUNK)
            outs["a_conv_p"].append(buf)
            outs["a_ssm_p"].append(s)
            xs, buf, s = _gdn_mixer(xs, mod_s, nw[1:2], state_a_conv[j], state_a_ssm[j], *prm, tm=tm_s, tt=ts,
                                    chunk=GDN_CHUNK)
            outs["a_conv_s"].append(buf)
            outs["a_ssm_s"].append(s)
        elif kind == 1:
            w_in, w_out = _bf16_padded(dsa_w_in[j], 512), dsa_w_out[j].astype(BF16)
            n_proj = dsa_w_in.shape[2]
            xp, k, v, ki = _dsa_mixer(xp, mod_p, nw[1:2], w_in, w_out, n_proj, nkv=nkv, tm=tm_p, tt=tt_p)
            outs["b_k_p"].append(k)
            outs["b_v_p"].append(v)
            outs["b_i_p"].append(ki)
            xs, k, v, ki = _dsa_mixer(xs, mod_s, nw[1:2], w_in, w_out, n_proj, nkv=nkv, tm=tm_s, tt=ts,
                                      cache=(cache_b_k, cache_b_v, cache_b_idx, j, page_table))
            outs["b_k_s"].append(k)
            outs["b_v_s"].append(v)
            outs["b_i_s"].append(ki)
        else:
            prm = (lru_w_in[j].astype(BF16), lru_conv_w[j], lru_conv_b[j], lru_w_gate_a[j].astype(BF16),
                   lru_b_gate_a[j], lru_w_gate_x[j].astype(BF16), lru_b_gate_x[j], lru_lambda[j],
                   lru_w_out[j].astype(BF16))
            buf0 = jnp.zeros((bp,) + state_c_conv.shape[2:], F32)
            h0 = jnp.zeros((bp,) + state_c_h.shape[2:], F32)
            xp, buf, hl = _lru_mixer(xp, mod_p, nw[1:2], buf0, h0, *prm, tm=tm_p, tt=tt_p)
            outs["c_conv_p"].append(buf)
            outs["c_h_p"].append(hl)
            xs, buf, hl = _lru_mixer(xs, mod_s, nw[1:2], state_c_conv[j], state_c_h[j], *prm, tm=tm_s, tt=ts)
            outs["c_conv_s"].append(buf)
            outs["c_h_s"].append(hl)
        xp = ffn(xp, mod_p, tm_p, 1)
        xs = ffn(xs, mod_s, tm_s, 1)

    modf_p, modf_s = groups(mod_f)
    fnw = final_norm_w.reshape(1, d)
    y_p = _final_norm(xp.reshape(bp * seq, d), modf_p, fnw, tm=tm_p).reshape(bp, seq, d)
    y_s = _final_norm(xs.reshape(bs * ts, d), modf_s, fnw, tm=tm_s).reshape(bs, ts, d)
    st = {k: jnp.stack(v) for k, v in outs.items()}
    return (y_p, y_s, st["a_conv_p"], st["a_conv_s"], st["a_ssm_p"], st["a_ssm_s"], st["b_k_p"], st["b_k_s"],
            st["b_v_p"], st["b_v_s"], st["b_i_p"], st["b_i_s"], st["c_conv_p"], st["c_conv_s"], st["c_h_p"],
            st["c_h_s"])
```

```python
import functools
import math

import jax
import jax.numpy as jnp
from jax import lax
from jax.experimental import pallas as pl
from jax.experimental.pallas import tpu as pltpu

F32 = jnp.float32
BF16 = jnp.bfloat16
I32 = jnp.int32

N_MIXERS = 3
N_ADA = 9
CONV_W = 4
NORM_EPS = 1e-6
NEG_INF = -1e30
GDN_CHUNK = 64
TOPK_MAX = 256
Q_BLOCK = 128
ROPE_THETA = 10000.0
RG_C = 8.0
LRU_BLOCK = 256
SAMPLE_PAGES_PER_STEP = 4

LANES = 128
SUBLANES = 8
VMEM_LIMIT_BYTES = 56 * 2**20
INT32_MIN = -2**31


def _cparams(*sem):
    return pltpu.CompilerParams(dimension_semantics=sem, vmem_limit_bytes=VMEM_LIMIT_BYTES)


def _tile(n, target, align=LANES):
    if n <= target:
        return n
    t = (target // align) * align
    while t >= align:
        if n % t == 0:
            return t
        t -= align
    raise ValueError(f"no {align}-aligned tile of {n} below {target}")


def _sigmoid(x):
    return jax.nn.sigmoid(x)


def _silu(x):
    return x * _sigmoid(x)


def _dot(a, b):
    return jnp.dot(a, b, preferred_element_type=F32)


def _dot_nt(a, b):
    return lax.dot_general(a, b, (((1,), (1,)), ((), ())), preferred_element_type=F32)


def _dot_tn(a, b):
    return lax.dot_general(a, b, (((0,), (0,)), ((), ())), preferred_element_type=F32)


def _dot_f32(a, b):
    return jnp.dot(a, b, preferred_element_type=F32, precision=lax.Precision.HIGHEST)


def _norm_mod(x, nw, sh, sc):
    ms = jnp.mean(x * x, axis=-1, keepdims=True)
    y = x * lax.rsqrt(ms + NORM_EPS) * nw
    return y * (1.0 + sc) + sh


def _ada_kernel(c_ref, w_ref, b_ref, o_ref):
    a = _silu(c_ref[...]).astype(BF16)
    o_ref[0] = _dot(a, w_ref[0].astype(BF16)) + b_ref[0]


def _ada(c_all, w, b):
    n_l, d, n = w.shape
    mp = c_all.shape[0]
    tn = _tile(n, 1024)
    return pl.pallas_call(
        _ada_kernel,
        grid=(n_l, n // tn),
        in_specs=[
            pl.BlockSpec((mp, d), lambda l, j: (0, 0)),
            pl.BlockSpec((1, d, tn), lambda l, j: (l, 0, j)),
            pl.BlockSpec((1, 1, tn), lambda l, j: (l, 0, j)),
        ],
        out_specs=pl.BlockSpec((1, mp, tn), lambda l, j: (l, 0, j)),
        out_shape=jax.ShapeDtypeStruct((n_l, mp, n), F32),
        name="ada_mod",
        compiler_params=_cparams("parallel", "parallel"),
    )(c_all, w, b.reshape(n_l, 1, n))


def _mod_spec(mod3, k, d, tm, rows_per_group):
    return pl.BlockSpec((None, mod3.shape[1], d), lambda i, j: ((i * tm) // rows_per_group, 0, k))


def _wspec(w, block, imap):
    arr, idx = w
    return pl.BlockSpec((None,) * len(idx) + block, lambda i, j: idx + imap(i, j))


def _wshape(w):
    return w[0].shape[len(w[1]):]


def _ffn_kernel(x_ref, nw_ref, sh_ref, sc_ref, g_ref, wa_ref, wb_ref, wo_ref, o_ref, xn_ref, acc_ref):
    f = pl.program_id(1)

    @pl.when(f == 0)
    def _():
        xn_ref[...] = _norm_mod(x_ref[...], nw_ref[...], sh_ref[...], sc_ref[...]).astype(BF16)
        acc_ref[...] = jnp.zeros_like(acc_ref)

    xn = xn_ref[...]
    a = _dot(xn, wa_ref[...])
    b = _dot(xn, wb_ref[...])
    acc_ref[...] += _dot((_silu(a) * b).astype(BF16), wo_ref[...])

    @pl.when(f == pl.num_programs(1) - 1)
    def _():
        o_ref[...] = x_ref[...] + 0.5 * g_ref[...] * acc_ref[...]


def _ffn(x, mod3, k0, nw, w_in, w_out, *, tm, tf_target=512):
    m, d = x.shape
    f = _wshape(w_out)[0]
    tf = _tile(f, tf_target)
    nf = f // tf
    rpg = m // mod3.shape[0]
    ms = lambda k: _mod_spec(mod3, k, d, tm, rpg)
    return pl.pallas_call(
        _ffn_kernel,
        grid=(m // tm, nf),
        in_specs=[
            pl.BlockSpec((tm, d), lambda i, j: (i, 0)),
            pl.BlockSpec((1, d), lambda i, j: (0, 0)),
            ms(k0), ms(k0 + 1), ms(k0 + 2),
            _wspec(w_in, (d, tf), lambda i, j: (0, j)),
            _wspec(w_in, (d, tf), lambda i, j: (0, j + nf)),
            _wspec(w_out, (tf, d), lambda i, j: (j, 0)),
        ],
        out_specs=pl.BlockSpec((tm, d), lambda i, j: (i, 0)),
        out_shape=jax.ShapeDtypeStruct((m, d), F32),
        scratch_shapes=[pltpu.VMEM((tm, d), BF16), pltpu.VMEM((tm, d), F32)],
        name="ffn",
        compiler_params=_cparams("parallel", "arbitrary"),
    )(x, nw, mod3, mod3, mod3, w_in[0], w_in[0], w_out[0])


def _normproj_kernel(x_ref, nw_ref, sh_ref, sc_ref, w_ref, o_ref, xn_ref):
    @pl.when(pl.program_id(1) == 0)
    def _():
        xn_ref[...] = _norm_mod(x_ref[...], nw_ref[...], sh_ref[...], sc_ref[...]).astype(BF16)

    o_ref[...] = _dot(xn_ref[...], w_ref[...])


def _normproj(x, mod3, k0, nw, w, *, tm, tn_target=512):
    m, d = x.shape
    n = _wshape(w)[1]
    tn = _tile(n, tn_target)
    rpg = m // mod3.shape[0]
    ms = lambda k: _mod_spec(mod3, k, d, tm, rpg)
    return pl.pallas_call(
        _normproj_kernel,
        grid=(m // tm, n // tn),
        in_specs=[
            pl.BlockSpec((tm, d), lambda i, j: (i, 0)),
            pl.BlockSpec((1, d), lambda i, j: (0, 0)),
            ms(k0), ms(k0 + 1),
            _wspec(w, (d, tn), lambda i, j: (0, j)),
        ],
        out_specs=pl.BlockSpec((tm, tn), lambda i, j: (i, j)),
        out_shape=jax.ShapeDtypeStruct((m, n), F32),
        scratch_shapes=[pltpu.VMEM((tm, d), BF16)],
        name="normproj",
        compiler_params=_cparams("parallel", "arbitrary"),
    )(x, nw, mod3, mod3, w[0])


def _outproj_kernel(a_ref, w_ref, x_ref, g_ref, o_ref):
    o_ref[...] = x_ref[...] + g_ref[...] * _dot(a_ref[...], w_ref[...])


def _outproj(a, w, x, mod3, kg, *, tm, tn_target=512):
    m, kdim = a.shape
    d = _wshape(w)[1]
    tn = _tile(d, tn_target)
    rpg = m // mod3.shape[0]
    r = mod3.shape[1]
    nd = d // tn
    return pl.pallas_call(
        _outproj_kernel,
        grid=(m // tm, nd),
        in_specs=[
            pl.BlockSpec((tm, kdim), lambda i, j: (i, 0)),
            _wspec(w, (kdim, tn), lambda i, j: (0, j)),
            pl.BlockSpec((tm, tn), lambda i, j: (i, j)),
            pl.BlockSpec((None, r, tn), lambda i, j: ((i * tm) // rpg, 0, kg * nd + j)),
        ],
        out_specs=pl.BlockSpec((tm, tn), lambda i, j: (i, j)),
        out_shape=jax.ShapeDtypeStruct((m, d), F32),
        name="outproj",
        compiler_params=_cparams("parallel", "arbitrary"),
    )(a, w[0], x, mod3)


def _final_kernel(x_ref, nw_ref, sh_ref, sc_ref, o_ref):
    o_ref[...] = _norm_mod(x_ref[...], nw_ref[...], sh_ref[...], sc_ref[...])


def _final_norm(x, mod3, nw, *, tm):
    m, d = x.shape
    rpg = m // mod3.shape[0]
    r = mod3.shape[1]
    ms = lambda k: pl.BlockSpec((None, r, d), lambda i: ((i * tm) // rpg, 0, k))
    return pl.pallas_call(
        _final_kernel,
        grid=(m // tm,),
        in_specs=[pl.BlockSpec((tm, d), lambda i: (i, 0)), pl.BlockSpec((1, d), lambda i: (0, 0)), ms(0), ms(1)],
        out_specs=pl.BlockSpec((tm, d), lambda i: (i, 0)),
        out_shape=jax.ShapeDtypeStruct((m, d), F32),
        name="final_norm",
        compiler_params=_cparams("parallel"),
    )(x, nw, mod3, mod3)


def _causal_conv(x, halo, w):
    tt = x.shape[0]
    row = lax.broadcasted_iota(I32, x.shape, 0)
    acc = x * w[CONV_W - 1:CONV_W]
    for j in range(1, CONV_W):
        xs = pltpu.roll(x, j, 0)
        hs = pltpu.roll(halo, j, 0)
        hs = jnp.concatenate([hs] * (tt // SUBLANES), axis=0) if tt > SUBLANES else hs
        acc = acc + jnp.where(row < j, hs, xs) * w[CONV_W - 1 - j:CONV_W - j]
    return acc


def _conv_specs(tt, tc, coff, boff):
    sub = tt // SUBLANES
    return [
        pl.BlockSpec((None, tt, tc), lambda b, t, c: (b, t, c + coff)),
        pl.BlockSpec((None, SUBLANES, tc), lambda b, t, c: (b, jnp.maximum(t * sub - 1, 0), c + coff)),
        pl.BlockSpec((None, SUBLANES, tc), lambda b, t, c: (b, 0, c + boff)),
    ]


def _pad_buf(buf):
    return jnp.pad(buf, ((0, 0), (SUBLANES - (CONV_W - 1), 0), (0, 0)))


def _gdn_prep_kernel(x_ref, halo_ref, buf_ref, w_ref, o_ref, *, norm):
    halo = jnp.where(pl.program_id(1) == 0, buf_ref[...], halo_ref[...])
    y = _silu(_causal_conv(x_ref[...], halo, w_ref[...]))
    if norm:
        for h in range(y.shape[1] // LANES):
            seg = y[:, h * LANES:(h + 1) * LANES]
            ss = jnp.sum(seg * seg, axis=-1, keepdims=True)
            o_ref[:, h * LANES:(h + 1) * LANES] = seg * lax.rsqrt(ss + NORM_EPS)
    else:
        o_ref[...] = y


def _gdn_prep(proj, buf8, conv_w, *, col0, ncols, norm, tt):
    bsz, t, _ = proj.shape
    tc = _tile(ncols, 512)
    return pl.pallas_call(
        functools.partial(_gdn_prep_kernel, norm=norm),
        grid=(bsz, t // tt, ncols // tc),
        in_specs=_conv_specs(tt, tc, col0 // tc, col0 // tc)
        + [pl.BlockSpec((CONV_W, tc), lambda b, i, c: (0, c + col0 // tc))],
        out_specs=pl.BlockSpec((None, tt, tc), lambda b, i, c: (b, i, c)),
        out_shape=jax.ShapeDtypeStruct((bsz, t, ncols), F32),
        name="gdn_prep",
        compiler_params=_cparams("parallel", "parallel", "parallel"),
    )(proj, proj, buf8, conv_w)


def _gdn_gate_kernel(x_ref, alog_ref, dtb_ref, beta_ref, gc_ref, *, hv, chunk):
    x = x_ref[...]
    tt = x.shape[0]
    beta_ref[...] = _sigmoid(x)
    z = x + dtb_ref[...]
    g = -jnp.exp(alog_ref[...]) * (jnp.maximum(z, 0.0) + jnp.log1p(jnp.exp(-jnp.abs(z))))
    row = lax.broadcasted_iota(I32, (tt, tt), 0)
    col = lax.broadcasted_iota(I32, (tt, tt), 1)
    tri = jnp.where((row >= col) & (row // chunk == col // chunk), 1.0, 0.0)
    gc_ref[...] = _dot_f32(tri, g)


def _gdn_gates(proj, a_log, dt_bias, *, col0, hv, chunk, tt):
    bsz, t, _ = proj.shape
    pad = lambda v: jnp.pad(v.astype(F32), (hv, LANES - 2 * hv)).reshape(1, LANES)
    blk = pl.BlockSpec((None, tt, LANES), lambda b, i: (b, i, col0 // LANES))
    out = pl.BlockSpec((None, tt, LANES), lambda b, i: (b, i, 0))
    par = pl.BlockSpec((1, LANES), lambda b, i: (0, 0))
    return pl.pallas_call(
        functools.partial(_gdn_gate_kernel, hv=hv, chunk=chunk),
        grid=(bsz, t // tt),
        in_specs=[blk, par, par],
        out_specs=[out, out],
        out_shape=[jax.ShapeDtypeStruct((bsz, t, LANES), F32)] * 2,
        name="gdn_gates",
        compiler_params=_cparams("parallel", "parallel"),
    )(proj, pad(a_log), pad(dt_bias))


def _split_bf16(a):
    hi = a.astype(BF16)
    return hi, (a - hi.astype(F32)).astype(BF16)


def _dot_split(a, b):
    ah, al = _split_bf16(a)
    bh, bl = _split_bf16(b)
    return _dot(jnp.concatenate([ah, al, ah], axis=1), jnp.concatenate([bh, bh, bl], axis=0))


def _tri_inv_all(lmats, c, n_real):
    row = lax.broadcasted_iota(I32, (c, c), 0)
    col = lax.broadcasted_iota(I32, (c, c), 1)
    eye = jnp.where(row == col, 1.0, 0.0)
    base = min(16, c)
    ps = [-jnp.where(row // base == col // base, m, 0.0) for m in lmats]
    rs = [eye + p for p in ps]
    n = 2
    while n < base:
        ps = [_dot_split(p, p) for p in ps]
        rs = [r + _dot_split(r, p) for r, p in zip(rs, ps)]
        n *= 2
    s = base
    while s < min(c, n_real):
        off = (row // (2 * s) == col // (2 * s)) & (row // s != col // s)
        ts = [_dot_split(jnp.where(off, m, 0.0), r) for m, r in zip(lmats, rs)]
        rs = [r - _dot_split(r, t) for r, t in zip(rs, ts)]
        s *= 2
    return rs


def _gdn_core_kernel(q_ref, k_ref, v_ref, z_ref, gcc_ref, gcr_ref, bc_ref, s0_ref, nw_ref, o_ref, so_ref, s_ref,
                     *, rep, c, hpb, dk, n_real):
    ci = pl.program_id(2)

    @pl.when(ci == 0)
    def _():
        s_ref[...] = s0_ref[...]

    row = lax.broadcasted_iota(I32, (c, c), 0)
    col = lax.broadcasted_iota(I32, (c, c), 1)
    causal = row >= col
    heads = [(hh, r) for hh in range(hpb) for r in range(rep)]
    lanes = [slice((hh * rep + r) * LANES, (hh * rep + r + 1) * LANES) for hh, r in heads]
    qs = [q_ref[:, hh * LANES:(hh + 1) * LANES] * dk ** -0.5 for hh in range(hpb)]
    ks = [k_ref[:, hh * LANES:(hh + 1) * LANES] for hh in range(hpb)]
    k16 = [k.astype(BF16) for k in ks]
    grams = [_dot_nt(kb, kb) for kb in k16]
    qk0s = [_dot_nt(q.astype(BF16), kb) for q, kb in zip(qs, k16)]
    gccs = [gcc_ref[hh, :, r:r + 1] for hh, r in heads]
    gcrs = [gcr_ref[hh, r:r + 1, :] for hh, r in heads]
    betas = [bc_ref[hh, :, r:r + 1] for hh, r in heads]
    decays = [jnp.where(causal, jnp.exp(jnp.where(causal, gc - gr, 0.0)), 0.0) for gc, gr in zip(gccs, gcrs)]
    lowers = [jnp.where(row > col, grams[hh] * b * d, 0.0) for (hh, _), b, d in zip(heads, betas, decays)]
    tinvs = _tri_inv_all(lowers, c, n_real)
    egcs = [jnp.exp(gc) for gc in gccs]
    sols = [_dot_split(ti, jnp.concatenate([v_ref[:, ln] * b, ks[hh] * (b * e)], axis=1))
            for ti, ln, b, e, (hh, _) in zip(tinvs, lanes, betas, egcs, heads)]
    qg16 = [(qs[hh] * e).astype(BF16) for (hh, _), e in zip(heads, egcs)]
    qk16 = [(qk0s[hh] * d).astype(BF16) for (hh, _), d in zip(heads, decays)]
    g_last = [gc[c - 1:c, :] for gc in gccs]
    kd16 = [(ks[hh] * jnp.exp(gl - gc)).astype(BF16) for (hh, _), gl, gc in zip(heads, g_last, gccs)]
    ss = [s_ref[i] for i in range(len(heads))]
    s16 = [s.astype(BF16) for s in ss]
    us = [sol[:, :LANES] - _dot(sol[:, LANES:].astype(BF16), sb) for sol, sb in zip(sols, s16)]
    u16 = [u.astype(BF16) for u in us]
    for i in range(len(heads)):
        s_ref[i] = ss[i] * jnp.exp(g_last[i]) + _dot_tn(kd16[i], u16[i])
    os_ = [_dot(a, sb) + _dot(b, ub) for a, sb, b, ub in zip(qg16, s16, qk16, u16)]
    for o, ln in zip(os_, lanes):
        on = o * lax.rsqrt(jnp.mean(o * o, axis=-1, keepdims=True) + NORM_EPS) * nw_ref[...]
        o_ref[:, ln] = (on * _silu(z_ref[:, ln])).astype(BF16)

    @pl.when(ci == pl.num_programs(2) - 1)
    def _():
        so_ref[...] = s_ref[...]


def _gdn_core(qk, v, proj, zcol0, gc, beta, s0, norm_w, *, c, hpb, n_real):
    bsz, t, val = v.shape
    hv = s0.shape[1]
    dk, dv = s0.shape[2], s0.shape[3]
    hk = qk.shape[2] // (2 * dk)
    rep = hv // hk
    assert dk == LANES and dv == LANES and t % c == 0 and hk % hpb == 0 and zcol0 % (hpb * rep * dv) == 0
    heads = lambda a, lo: a[:, :, lo:lo + hv].reshape(bsz, t, hk, rep).transpose(0, 2, 1, 3)
    gcc = heads(gc, hv)
    bcc = heads(beta, 0)
    gcr = gcc.transpose(0, 1, 3, 2)
    nhb = hk // hpb
    zb = zcol0 // (hpb * rep * dv)
    colspec = pl.BlockSpec((None, hpb, c, rep), lambda b, h, i: (b, h, i, 0))
    o, s_out = pl.pallas_call(
        functools.partial(_gdn_core_kernel, rep=rep, c=c, hpb=hpb, dk=dk, n_real=n_real),
        grid=(bsz, nhb, t // c),
        in_specs=[
            pl.BlockSpec((None, c, hpb * dk), lambda b, h, i: (b, i, h)),
            pl.BlockSpec((None, c, hpb * dk), lambda b, h, i: (b, i, nhb + h)),
            pl.BlockSpec((None, c, hpb * rep * dv), lambda b, h, i: (b, i, h)),
            pl.BlockSpec((None, c, hpb * rep * dv), lambda b, h, i: (b, i, zb + h)),
            colspec,
            pl.BlockSpec((None, hpb, rep, c), lambda b, h, i: (b, h, 0, i)),
            colspec,
            pl.BlockSpec((None, hpb * rep, dk, dv), lambda b, h, i: (b, h, 0, 0)),
            pl.BlockSpec((1, dv), lambda b, h, i: (0, 0)),
        ],
        out_specs=[
            pl.BlockSpec((None, c, hpb * rep * dv), lambda b, h, i: (b, i, h)),
            pl.BlockSpec((None, hpb * rep, dk, dv), lambda b, h, i: (b, h, 0, 0)),
        ],
        out_shape=[jax.ShapeDtypeStruct((bsz, t, val), BF16), jax.ShapeDtypeStruct(s0.shape, F32)],
        scratch_shapes=[pltpu.VMEM((hpb * rep, dk, dv), F32)],
        name="gdn_core",
        compiler_params=_cparams("parallel", "parallel", "arbitrary"),
    )(qk, qk, v, proj, gcc, gcr, bcc, s0, norm_w.reshape(1, dv))
    return o, s_out


def _gdn_mixer(x, mod3, nw, buf, s0, w_in, conv_w, a_log, dt_bias, norm_w, w_out, *, tm, tm_in, tt, chunk):
    bsz, t, d = x.shape
    hv, dk, dv = s0.shape[1], s0.shape[2], s0.shape[3]
    val = hv * dv
    conv_dim = conv_w.shape[1]
    key = (conv_dim - val) // 2
    assert (conv_dim + val) % LANES == 0 and 2 * hv <= LANES and t >= CONV_W - 1
    x2 = x.reshape(bsz * t, d)
    proj = _normproj(x2, mod3, 3, nw, w_in, tm=tm_in).reshape(bsz, t, -1)
    buf8 = _pad_buf(buf)
    qk = _gdn_prep(proj, buf8, conv_w, col0=0, ncols=2 * key, norm=True, tt=tt)
    v = _gdn_prep(proj, buf8, conv_w, col0=2 * key, ncols=val, norm=False, tt=tt)
    tp = -(-t // chunk) * chunk
    beta, gc = _gdn_gates(proj, a_log, dt_bias, col0=conv_dim + val, hv=hv, chunk=min(chunk, tt), tt=tt)
    if tp != t:
        padt = lambda a: jnp.pad(a, ((0, 0), (0, tp - t), (0, 0)))
        gc = jnp.concatenate([gc, jnp.broadcast_to(gc[:, -1:], (bsz, tp - t, LANES))], axis=1)
        qk, v, beta, projz = padt(qk), padt(v), padt(beta), padt(proj)
    else:
        projz = proj
    o, s_new = _gdn_core(qk, v, projz, conv_dim, gc, beta, s0, norm_w, c=chunk, hpb=min(4, key // dk),
                         n_real=min(t, chunk))
    o2 = o[:, :t].reshape(bsz * t, val)
    xo = _outproj(o2, w_out, x2, mod3, 5, tm=tm).reshape(bsz, t, d)
    new_buf = proj[:, t - (CONV_W - 1):, :conv_dim]
    return xo, new_buf, s_new


def _rope_tables(pos, half):
    inv_freq = ROPE_THETA ** (-jnp.arange(half, dtype=F32) / half)
    ang = pos.astype(F32)[:, None] * inv_freq[None, :]
    cos, sin = jnp.cos(ang), jnp.sin(ang)
    return jnp.concatenate([cos, cos], axis=-1), jnp.concatenate([-sin, sin], axis=-1)


def _dsa_prep_kernel(x_ref, cos_ref, sin_ref, q_ref, k_ref, k16_ref, v16_ref, qi_ref, ki_ref, ki16_ref, *tr_refs,
                     nh, nkv, nih):
    cos, sin = cos_ref[...], sin_ref[...]

    def rope(col):
        seg = x_ref[:, col * LANES:(col + 1) * LANES]
        return seg * cos + pltpu.roll(seg, LANES // 2, 1) * sin

    for h in range(nh):
        q_ref[:, h * LANES:(h + 1) * LANES] = rope(h).astype(BF16)
    for h in range(nkv):
        kr = rope(nh + h)
        k_ref[:, h * LANES:(h + 1) * LANES] = kr
        k16_ref[:, h * LANES:(h + 1) * LANES] = kr.astype(BF16)
    v0 = (nh + nkv) * LANES
    v16_ref[...] = x_ref[:, v0:v0 + nkv * LANES].astype(BF16)
    c0 = nh + 2 * nkv
    for h in range(nih):
        qi_ref[:, h * LANES:(h + 1) * LANES] = rope(c0 + h).astype(BF16)
    kir = rope(c0 + nih)
    ki_ref[...] = kir
    ki16_ref[...] = kir.astype(BF16)
    if tr_refs:
        wt_ref, vt_ref = tr_refs
        w0 = (c0 + nih + 1) * LANES
        wt_ref[...] = x_ref[:, w0:w0 + LANES].T[:wt_ref.shape[0], :]
        for h in range(nkv):
            vt_ref[h * LANES:(h + 1) * LANES, :] = x_ref[:, v0 + h * LANES:v0 + (h + 1) * LANES].T.astype(BF16)


def _dsa_prep(proj, pos, *, nh, nkv, nih, tt, with_wt):
    bsz, t, npj = proj.shape
    cos, sin = _rope_tables(pos, LANES // 2)
    row = lambda n, dt: jax.ShapeDtypeStruct((bsz, t, n * LANES), dt)
    ospec = lambda n: pl.BlockSpec((None, tt, n * LANES), lambda b, i: (b, i, 0))
    tab = pl.BlockSpec((tt, LANES), lambda b, i: (i, 0))
    nwt = -(-nih // SUBLANES) * SUBLANES
    return pl.pallas_call(
        functools.partial(_dsa_prep_kernel, nh=nh, nkv=nkv, nih=nih),
        grid=(bsz, t // tt),
        in_specs=[pl.BlockSpec((None, tt, npj), lambda b, i: (b, i, 0)), tab, tab],
        out_specs=[ospec(nh), ospec(nkv), ospec(nkv), ospec(nkv), ospec(nih), ospec(1), ospec(1)]
        + ([pl.BlockSpec((None, nwt, tt), lambda b, i: (b, 0, i)),
            pl.BlockSpec((None, None, nkv * LANES, tt), lambda b, i: (b, i, 0, 0))] if with_wt else []),
        out_shape=[row(nh, BF16), row(nkv, F32), row(nkv, BF16), row(nkv, BF16), row(nih, BF16), row(1, F32),
                   row(1, BF16)]
        + ([jax.ShapeDtypeStruct((bsz, nwt, t), F32),
            jax.ShapeDtypeStruct((bsz, t // tt, nkv * LANES, tt), BF16)] if with_wt else []),
        name="dsa_prep",
        compiler_params=_cparams("parallel", "parallel"),
    )(proj, cos, sin)


def _sort_key(s):
    bits = pltpu.bitcast(jnp.where(s == 0.0, 0.0, s), I32)
    return jnp.where(bits < 0, bits ^ 0x7FFFFFFF, bits)


def _kth_largest_key(count_ge, shape, topk):
    def body(i, t):
        cand = t + lax.shift_left(jnp.int32(1), 31 - i)
        return jnp.where(count_ge(cand) >= topk, cand, t)
    return lax.fori_loop(0, 32, body, jnp.full(shape, INT32_MIN, I32))


def _dsa_prompt_kernel(qi_ref, wt_ref, q_ref, ki_ref, k_ref, vt_ref, o_ref, key_ref, bias_ref, acc_ref,
                       *, nh, nkv, nih, topk, idx_scale):
    qb = pl.program_id(1)
    blk = Q_BLOCK
    kb = vt_ref.shape[2]
    ktiles = kb // blk
    nkb = (qb + ktiles) // ktiles
    rowk = lax.broadcasted_iota(I32, (kb, blk), 0)
    colq = lax.broadcasted_iota(I32, (kb, blk), 1)
    wt = wt_ref[...]
    hq = max(1, nih // 4)
    qis = [jnp.concatenate([qi_ref[:, h * LANES:(h + 1) * LANES] for h in range(h0, min(h0 + hq, nih))], axis=0)
           for h0 in range(0, nih, hq)]

    def rows_of(j):
        return pl.ds(pl.multiple_of(j * kb, kb), kb)

    def visible(j):
        return j * kb + rowk <= qb * blk + colq

    def score_body(j, carry):
        keys = ki_ref[rows_of(j), :]
        lgs = [_dot_nt(keys, qi) for qi in qis]
        acc = jnp.zeros((kb, blk), F32)
        for i, lg in enumerate(lgs):
            for hh in range(lg.shape[1] // blk):
                h = i * hq + hh
                acc = acc + jnp.maximum(lg[:, hh * blk:(hh + 1) * blk], 0.0) * wt[h:h + 1, :]
        key_ref[rows_of(j), :] = _sort_key(jnp.where(visible(j), acc * idx_scale, NEG_INF))
        return carry

    lax.fori_loop(0, nkb, score_body, 0)

    def count_ge(cand):
        def body(j, c):
            hit = jnp.where(key_ref[rows_of(j), :] >= cand, 1, 0)
            for i in range(ktiles):
                c = c + hit[i * blk:(i + 1) * blk]
            return c
        cnt = lax.fori_loop(0, nkb, body, jnp.zeros((blk, blk), I32))
        return jnp.sum(cnt, axis=0, keepdims=True)

    thr = _kth_largest_key(count_ge, (1, blk), topk)

    def bias_body(j, carry):
        sel = (key_ref[rows_of(j), :] >= thr) & visible(j)
        bias_ref[rows_of(j), :] = jnp.where(sel, 0.0, NEG_INF)
        return carry

    lax.fori_loop(0, nkb, bias_body, 0)

    rep = nh // nkv
    scale = LANES ** -0.5
    qgs = [jnp.concatenate([q_ref[:, (g * rep + r) * LANES:(g * rep + r + 1) * LANES] for r in range(rep)], axis=0)
           for g in range(nkv)]
    acc_ref[...] = jnp.zeros_like(acc_ref)

    def att_body(j, carry):
        ms, ls = carry
        rows = rows_of(j)
        bias = jnp.concatenate([bias_ref[rows, :]] * rep, axis=1)
        ss = [_dot_nt(k_ref[rows, g * LANES:(g + 1) * LANES], qgs[g]) * scale + bias for g in range(nkv)]
        m_new = [jnp.maximum(m, jnp.max(s, axis=0, keepdims=True)) for m, s in zip(ms, ss)]
        ps = [jnp.exp(s - m) for s, m in zip(ss, m_new)]
        pvs = [_dot(vt_ref[j, g * LANES:(g + 1) * LANES, :], p.astype(BF16)) for g, p in enumerate(ps)]
        alphas = [jnp.exp(m - mn) for m, mn in zip(ms, m_new)]
        for g in range(nkv):
            acc_ref[g] = alphas[g] * acc_ref[g] + pvs[g]
        l_new = [a * l + jnp.sum(p, axis=0, keepdims=True) for a, l, p in zip(alphas, ls, ps)]
        return tuple(m_new), tuple(l_new)

    row0 = lambda v: tuple(jnp.full((1, rep * blk), v, F32) for _ in range(nkv))
    _, ls = lax.fori_loop(0, nkb, att_body, (row0(NEG_INF), row0(0.0)))
    for g in range(nkv):
        og = acc_ref[g] / ls[g]
        for r in range(rep):
            h = g * rep + r
            o_ref[:, h * LANES:(h + 1) * LANES] = og[:, r * blk:(r + 1) * blk].T.astype(BF16)


def _dsa_prompt_attend(q16, qi16, wt, ki16, k16, vt16, *, nh, nkv, nih, topk):
    bsz, t, _ = q16.shape
    rep = nh // nkv
    full = lambda n: pl.BlockSpec((None, t, n * LANES), lambda b, i: (b, 0, 0))
    blk = lambda n: pl.BlockSpec((None, Q_BLOCK, n * LANES), lambda b, i: (b, i, 0))
    return pl.pallas_call(
        functools.partial(_dsa_prompt_kernel, nh=nh, nkv=nkv, nih=nih, topk=topk,
                          idx_scale=(LANES * nih) ** -0.5),
        grid=(bsz, t // Q_BLOCK),
        in_specs=[blk(nih), pl.BlockSpec((None, wt.shape[1], Q_BLOCK), lambda b, i: (b, 0, i)), blk(nh),
                  full(1), full(nkv), pl.BlockSpec((None,) + vt16.shape[1:], lambda b, i: (b, 0, 0, 0))],
        out_specs=blk(nh),
        out_shape=jax.ShapeDtypeStruct((bsz, t, nh * LANES), BF16),
        scratch_shapes=[pltpu.VMEM((t, Q_BLOCK), I32), pltpu.VMEM((t, Q_BLOCK), F32),
                        pltpu.VMEM((nkv, LANES, rep * Q_BLOCK), F32)],
        name="dsa_prompt_attend",
        compiler_params=_cparams("parallel", "arbitrary"),
    )(qi16, wt, q16, ki16, k16, vt16)


def _dsa_sample_score_kernel(pt_ref, qi_ref, wc_ref, *refs, n_steps, pps, nih, tq, past, idx_scale):
    page_refs, new_ref, o_ref = refs[:pps], refs[pps], refs[pps + 1]
    p = pl.program_id(1)
    keys = jnp.concatenate([r[...] for r in page_refs], axis=0).astype(BF16)
    keys = jnp.where(p == n_steps - 1, new_ref[...], keys)
    qi = jnp.concatenate([qi_ref[:, h * LANES:(h + 1) * LANES] for h in range(nih)], axis=0)
    w = jnp.maximum(_dot_nt(qi, keys), 0.0) * wc_ref[...]
    acc = w[0:tq]
    for h in range(1, nih):
        acc = acc + w[h * tq:(h + 1) * tq]
    s = p * keys.shape[0] + lax.broadcasted_iota(I32, acc.shape, 1)
    qpos = past + lax.broadcasted_iota(I32, acc.shape, 0)
    o_ref[...] = jnp.where(s <= qpos, acc * idx_scale, NEG_INF)


def _dsa_sample_select_kernel(s_ref, o_ref, *, topk):
    key = _sort_key(s_ref[...])
    thr = _kth_largest_key(lambda cand: jnp.sum(jnp.where(key >= cand, 1, 0), axis=-1, keepdims=True),
                           (key.shape[0], 1), topk)
    o_ref[...] = jnp.where((key >= thr) & (s_ref[...] > 0.5 * NEG_INF), 1.0, 0.0)


def _dsa_sample_attn_kernel(pt_ref, q_ref, sel_ref, *refs, n_steps, pps, nh, nkv, tq):
    kp_refs, vp_refs = refs[:pps], refs[pps:2 * pps]
    kn_ref, vn_ref, o_ref, m_ref, l_ref, acc_ref = refs[2 * pps:]
    p = pl.program_id(1)

    @pl.when(p == 0)
    def _():
        m_ref[...] = jnp.full_like(m_ref, NEG_INF)
        l_ref[...] = jnp.zeros_like(l_ref)
        acc_ref[...] = jnp.zeros_like(acc_ref)

    last = p == n_steps - 1
    kcat = jnp.where(last, kn_ref[...], jnp.concatenate([r[...] for r in kp_refs], axis=0).astype(BF16))
    vcat = jnp.where(last, vn_ref[...], jnp.concatenate([r[...] for r in vp_refs], axis=0).astype(BF16))
    rep = nh // nkv
    q = jnp.concatenate([q_ref[:, h * LANES:(h + 1) * LANES] for h in range(nh)], axis=0)
    s = _dot_nt(q, kcat) * LANES ** -0.5
    page = sel_ref.shape[1] // pps
    expand = (lax.broadcasted_iota(I32, (page, page * nkv), 1) // nkv
              == lax.broadcasted_iota(I32, (page, page * nkv), 0)).astype(BF16)
    sel = sel_ref[...].astype(BF16)
    selx = jnp.concatenate([_dot(sel[:, i * page:(i + 1) * page], expand) for i in range(pps)], axis=1)
    selx = jnp.concatenate([selx] * nh, axis=0)
    head_ok = (lax.broadcasted_iota(I32, s.shape, 0) // (tq * rep) == lax.broadcasted_iota(I32, s.shape, 1) % nkv)
    s = jnp.where((selx > 0.5) & head_ok, s, NEG_INF)
    m = m_ref[...]
    m_new = jnp.maximum(m, jnp.max(s, axis=-1, keepdims=True))
    alpha = jnp.exp(m - m_new)
    pr = jnp.where(s > 0.5 * NEG_INF, jnp.exp(s - m_new), 0.0)
    l_ref[...] = alpha * l_ref[...] + jnp.sum(pr, axis=-1, keepdims=True)
    acc_ref[...] = alpha * acc_ref[...] + _dot(pr.astype(BF16), vcat)
    m_ref[...] = m_new

    @pl.when(last)
    def _():
        og = acc_ref[...] / l_ref[...]
        for h in range(nh):
            o_ref[:, h * LANES:(h + 1) * LANES] = og[h * tq:(h + 1) * tq].astype(BF16)


def _dsa_sample_attend(q16, qi16, wi, ki16, k16, v16, cache_k, cache_v, cache_i, layer, page_table,
                       *, nh, nkv, nih, topk):
    bsz, tq, _ = q16.shape
    n_pages = page_table.shape[1]
    n_layers, n_pool, page = cache_i.shape[:3]
    pps = math.gcd(n_pages, SAMPLE_PAGES_PER_STEP)
    assert page == LANES and tq == SUBLANES
    past = n_pages * page
    n_steps = n_pages // pps + 1
    ltot = n_steps * pps * page
    wcol = wi.transpose(0, 2, 1).reshape(bsz, nih * tq, 1)
    pidx = lambda i: (lambda b, p, pt: (layer * n_pool + pt[b, jnp.minimum(p * pps + i, n_pages - 1)], 0, 0))
    qspec = lambda n: pl.BlockSpec((None, tq, n * LANES), lambda b, p, pt: (b, 0, 0))
    idx_scale = (LANES * nih) ** -0.5
    cache_i = cache_i.reshape(n_layers * n_pool, page, LANES)
    ki_new = jnp.pad(ki16, ((0, 0), (0, pps * page - tq), (0, 0)))
    scores = pl.pallas_call(
        functools.partial(_dsa_sample_score_kernel, n_steps=n_steps, pps=pps, nih=nih, tq=tq, past=past,
                          idx_scale=idx_scale),
        grid_spec=pltpu.PrefetchScalarGridSpec(
            num_scalar_prefetch=1, grid=(bsz, n_steps),
            in_specs=[qspec(nih), pl.BlockSpec((None, nih * tq, 1), lambda b, p, pt: (b, 0, 0))]
            + [pl.BlockSpec((None, page, LANES), pidx(i)) for i in range(pps)]
            + [pl.BlockSpec((None, pps * page, LANES), lambda b, p, pt: (b, 0, 0))],
            out_specs=pl.BlockSpec((None, tq, pps * page), lambda b, p, pt: (b, 0, p))),
        out_shape=jax.ShapeDtypeStruct((bsz, tq, ltot), F32),
        name="dsa_sample_scores",
        compiler_params=_cparams("parallel", "arbitrary"),
    )(page_table, qi16, wcol, *([cache_i] * pps), ki_new)
    sel = pl.pallas_call(
        functools.partial(_dsa_sample_select_kernel, topk=topk),
        grid=(bsz,),
        in_specs=[pl.BlockSpec((None, tq, ltot), lambda b: (b, 0, 0))],
        out_specs=pl.BlockSpec((None, tq, ltot), lambda b: (b, 0, 0)),
        out_shape=jax.ShapeDtypeStruct((bsz, tq, ltot), F32),
        name="dsa_sample_select",
        compiler_params=_cparams("parallel"),
    )(scores)
    prow = page * nkv
    ck = cache_k.reshape(n_layers * n_pool, prow, LANES)
    cv = cache_v.reshape(n_layers * n_pool, prow, LANES)
    new_rows = lambda a: jnp.pad(a.reshape(bsz, tq * nkv, LANES), ((0, 0), (0, pps * prow - tq * nkv), (0, 0)))
    kvspecs = [pl.BlockSpec((None, prow, LANES), pidx(i)) for i in range(pps)]
    newspec = pl.BlockSpec((None, pps * prow, LANES), lambda b, p, pt: (b, 0, 0))
    return pl.pallas_call(
        functools.partial(_dsa_sample_attn_kernel, n_steps=n_steps, pps=pps, nh=nh, nkv=nkv, tq=tq),
        grid_spec=pltpu.PrefetchScalarGridSpec(
            num_scalar_prefetch=1, grid=(bsz, n_steps),
            in_specs=[qspec(nh), pl.BlockSpec((None, tq, pps * page), lambda b, p, pt: (b, 0, p))]
            + kvspecs + kvspecs + [newspec, newspec],
            out_specs=qspec(nh),
            scratch_shapes=[pltpu.VMEM((nh * tq, 1), F32), pltpu.VMEM((nh * tq, 1), F32),
                            pltpu.VMEM((nh * tq, LANES), F32)]),
        out_shape=jax.ShapeDtypeStruct((bsz, tq, nh * LANES), BF16),
        name="dsa_sample_attend",
        compiler_params=_cparams("parallel", "arbitrary"),
    )(page_table, q16, sel, *([ck] * pps), *([cv] * pps), new_rows(k16), new_rows(v16))


def _dsa_mixer(x, mod3, nw, w_in, w_out, n_proj, *, nkv, tm, tm_in, tt, cache=None):
    bsz, t, d = x.shape
    nh = _wshape(w_out)[0] // LANES
    nih = (n_proj - (nh + 2 * nkv + 1) * LANES) // (LANES + 1)
    assert (nh + 2 * nkv + nih + 1) * LANES + nih == n_proj and nih <= LANES
    x2 = x.reshape(bsz * t, d)
    proj = _normproj(x2, mod3, 3, nw, w_in, tm=tm_in).reshape(bsz, t, -1)
    v = proj[:, :, (nh + nkv) * LANES:(nh + 2 * nkv) * LANES]
    if cache is None:
        assert t % Q_BLOCK == 0
        q16, k, k16, _, qi16, ki, ki16, wt, vt16 = _dsa_prep(proj, jnp.arange(t), nh=nh, nkv=nkv, nih=nih,
                                                             tt=_tile(t, 4 * Q_BLOCK, Q_BLOCK), with_wt=True)
        o = _dsa_prompt_attend(q16, qi16, wt, ki16, k16, vt16, nh=nh, nkv=nkv, nih=nih, topk=min(TOPK_MAX, t // 4))
    else:
        cache_k, cache_v, cache_i, layer, page_table = cache
        past = page_table.shape[1] * cache_i.shape[2]
        q16, k, k16, v16, qi16, ki, ki16 = _dsa_prep(proj, past + jnp.arange(t), nh=nh, nkv=nkv, nih=nih, tt=tt,
                                                    with_wt=False)
        w0 = (nh + 2 * nkv + nih + 1) * LANES
        o = _dsa_sample_attend(q16, qi16, proj[:, :, w0:w0 + nih], ki16, k16, v16, cache_k, cache_v, cache_i,
                               layer, page_table, nh=nh, nkv=nkv, nih=nih, topk=min(TOPK_MAX, (past + t) // 4))
    xo = _outproj(o.reshape(bsz * t, nh * LANES), w_out, x2, mod3, 5, tm=tm).reshape(bsz, t, d)
    return xo, k.reshape(bsz, t, nkv, LANES), v.reshape(bsz, t, nkv, LANES), ki


def _gelu_tanh(x):
    return 0.5 * x * (1.0 + jnp.tanh(math.sqrt(2.0 / math.pi) * (x + 0.044715 * (x * x * x))))


def _lru_kernel(gate_ref, xb_ref, halo_ref, buf_ref, cw_ref, cb_ref, wa_ref, ba_ref, wx_ref, bx_ref, lam_ref,
                h0_ref, y_ref, hl_ref, h_ref, *, nblk):
    ti = pl.program_id(1)

    @pl.when(ti == 0)
    def _():
        h_ref[...] = h0_ref[...]

    halo = jnp.where(ti == 0, buf_ref[...], halo_ref[...])
    xc = _causal_conv(xb_ref[...], halo, cw_ref[...]) + cb_ref[...]
    tt = xc.shape[0]
    xc16 = xc.astype(BF16)
    rs, xs = [], []
    for n in range(nblk):
        blk = xc16[:, n * LRU_BLOCK:(n + 1) * LRU_BLOCK]
        rs.append(_dot(blk, wa_ref[n]))
        xs.append(_dot(blk, wx_ref[n]))
    r = _sigmoid(jnp.concatenate(rs, axis=1) + ba_ref[...])
    ig = _sigmoid(jnp.concatenate(xs, axis=1) + bx_ref[...])
    lam = lam_ref[...]
    softplus_neg = jnp.maximum(-lam, 0.0) + jnp.log1p(jnp.exp(-jnp.abs(lam)))
    log_a = -RG_C * r * softplus_neg
    a = jnp.exp(log_a)
    b = jnp.sqrt(-jnp.tanh(log_a) * (a * a + 1.0)) * (ig * xc)
    row = lax.broadcasted_iota(I32, a.shape, 0)
    d = 1
    while d < tt:
        keep = row >= d
        a_sh = jnp.where(keep, pltpu.roll(a, d, 0), 1.0)
        b_sh = jnp.where(keep, pltpu.roll(b, d, 0), 0.0)
        b = a * b_sh + b
        a = a * a_sh
        d *= 2
    hs = b + a * h_ref[...]
    h_ref[...] = hs[tt - 1:tt, :]
    y_ref[...] = (hs * _gelu_tanh(gate_ref[...])).astype(BF16)

    @pl.when(ti == pl.num_programs(1) - 1)
    def _():
        hl_ref[...] = hs[tt - 1:tt, :]


def _lru_core(proj, buf8, h0, conv_w, conv_b, w_ga, b_ga, w_gx, b_gx, lam, *, tt):
    bsz, t, w2 = proj.shape
    w = w2 // 2
    nblk = w // LRU_BLOCK
    sub = tt // SUBLANES
    vec = lambda: pl.BlockSpec((1, w), lambda b, i: (0, 0))
    wsp = lambda: pl.BlockSpec((nblk, LRU_BLOCK, LRU_BLOCK), lambda b, i: (0, 0, 0))
    r1 = lambda v: v.reshape(1, w).astype(F32)
    y, hl = pl.pallas_call(
        functools.partial(_lru_kernel, nblk=nblk),
        grid=(bsz, t // tt),
        in_specs=[
            pl.BlockSpec((None, tt, w), lambda b, i: (b, i, 0)),
            pl.BlockSpec((None, tt, w), lambda b, i: (b, i, 1)),
            pl.BlockSpec((None, SUBLANES, w), lambda b, i: (b, jnp.maximum(i * sub - 1, 0), 1)),
            pl.BlockSpec((None, SUBLANES, w), lambda b, i: (b, 0, 0)),
            pl.BlockSpec((CONV_W, w), lambda b, i: (0, 0)),
            vec(), wsp(), vec(), wsp(), vec(), vec(),
            pl.BlockSpec((None, 1, w), lambda b, i: (b, 0, 0)),
        ],
        out_specs=[pl.BlockSpec((None, tt, w), lambda b, i: (b, i, 0)),
                   pl.BlockSpec((None, 1, w), lambda b, i: (b, 0, 0))],
        out_shape=[jax.ShapeDtypeStruct((bsz, t, w), BF16), jax.ShapeDtypeStruct((bsz, 1, w), F32)],
        scratch_shapes=[pltpu.VMEM((1, w), F32)],
        name="lru_core",
        compiler_params=_cparams("parallel", "arbitrary"),
    )(proj, proj, proj, buf8, conv_w, r1(conv_b), w_ga, r1(b_ga), w_gx, r1(b_gx), r1(lam), h0.reshape(bsz, 1, w))
    return y, hl.reshape(bsz, w)


def _lru_mixer(x, mod3, nw, buf, h0, w_in, conv_w, conv_b, w_ga, b_ga, w_gx, b_gx, lam, w_out, *, tm, tm_in, tt):
    bsz, t, d = x.shape
    assert t >= CONV_W - 1
    x2 = x.reshape(bsz * t, d)
    proj = _normproj(x2, mod3, 3, nw, w_in, tm=tm_in).reshape(bsz, t, -1)
    w = proj.shape[2] // 2
    y, hl = _lru_core(proj, _pad_buf(buf), h0, conv_w, conv_b, w_ga, b_ga, w_gx, b_gx, lam, tt=tt)
    xo = _outproj(y.reshape(bsz * t, w), w_out, x2, mod3, 5, tm=tm).reshape(bsz, t, d)
    return xo, proj[:, t - (CONV_W - 1):, w:], hl


def _bf16_padded(w, mult):
    n = w.shape[-1]
    return jnp.pad(w, ((0, 0), (0, 0), (0, -n % mult))).astype(BF16)


def kernel(x_prompt, x_sample, state_a_conv, state_a_ssm, cache_b_k, cache_b_v, cache_b_idx, state_c_conv, state_c_h, page_table, c_prompt, c_sample, w_ada, b_ada, norm_w, ffn_w_in, ffn_w_out, gdn_w_in, gdn_conv_w, gdn_a_log, gdn_dt_bias, gdn_norm_w, gdn_w_out, dsa_w_in, dsa_w_out, lru_w_in, lru_conv_w, lru_conv_b, lru_w_gate_a, lru_b_gate_a, lru_w_gate_x, lru_b_gate_x, lru_lambda, lru_w_out, w_ada_final, b_ada_final, final_norm_w):
    bp, seq, d = x_prompt.shape
    bs, ts, _ = x_sample.shape
    depth = w_ada.shape[0]
    nkv = cache_b_k.shape[3]
    tm_p = _tile(seq, 512, SUBLANES)
    tm_in_p = _tile(seq, 1024, SUBLANES)
    tt_p = _tile(seq, 256, SUBLANES)
    tm_s = bs * ts
    tiles_p = dict(tm=tm_p, tm_in=tm_in_p, tt=tt_p)
    tiles_s = dict(tm=tm_s, tm_in=tm_s, tt=ts)

    c_all = jnp.concatenate([c_prompt, c_sample], axis=0)
    c_all = jnp.pad(c_all, ((0, -c_all.shape[0] % SUBLANES), (0, 0)))
    mod = _ada(c_all, w_ada, b_ada)
    mod_f = _ada(c_all, w_ada_final[None], b_ada_final[None])[0]

    def groups(m):
        return m[:bp, None, :], jnp.repeat(m[bp:bp + bs], ts, axis=0)[None]

    ffn_w_in16, ffn_w_out16 = ffn_w_in.astype(BF16), ffn_w_out.astype(BF16)
    gdn_w_in16, gdn_w_out16 = _bf16_padded(gdn_w_in, 512), gdn_w_out.astype(BF16)
    dsa_w_in16, dsa_w_out16 = _bf16_padded(dsa_w_in, 512), dsa_w_out.astype(BF16)
    lru_w_in16, lru_w_out16 = lru_w_in.astype(BF16), lru_w_out.astype(BF16)

    xp, xs = x_prompt, x_sample
    outs = {k: [] for k in ("a_conv_p", "a_conv_s", "a_ssm_p", "a_ssm_s", "b_k_p", "b_k_s", "b_v_p", "b_v_s",
                            "b_i_p", "b_i_s", "c_conv_p", "c_conv_s", "c_h_p", "c_h_s")}
    for layer in range(depth):
        kind, j = layer % N_MIXERS, layer // N_MIXERS
        mod_p, mod_s = groups(mod[layer])
        nw = norm_w[layer]

        def ffn(x, m3, tm, which):
            b, t, _ = x.shape
            return _ffn(x.reshape(b * t, d), m3, 6 * which, nw[2 * which:2 * which + 1],
                        (ffn_w_in16, (layer, which)), (ffn_w_out16, (layer, which)), tm=tm).reshape(b, t, d)

        xp = ffn(xp, mod_p, tm_p, 0)
        xs = ffn(xs, mod_s, tm_s, 0)
        if kind == 0:
            prm = ((gdn_w_in16, (j,)), gdn_conv_w[j], gdn_a_log[j], gdn_dt_bias[j], gdn_norm_w[j],
                   (gdn_w_out16, (j,)))
            buf0 = jnp.zeros((bp,) + state_a_conv.shape[2:], F32)
            s0 = jnp.zeros((bp,) + state_a_ssm.shape[2:], F32)
            xp, buf, s = _gdn_mixer(xp, mod_p, nw[1:2], buf0, s0, *prm, chunk=2 * GDN_CHUNK, **tiles_p)
            outs["a_conv_p"].append(buf)
            outs["a_ssm_p"].append(s)
            xs, buf, s = _gdn_mixer(xs, mod_s, nw[1:2], state_a_conv[j], state_a_ssm[j], *prm, chunk=GDN_CHUNK,
                                    **tiles_s)
            outs["a_conv_s"].append(buf)
            outs["a_ssm_s"].append(s)
        elif kind == 1:
            w_in, w_out = (dsa_w_in16, (j,)), (dsa_w_out16, (j,))
            n_proj = dsa_w_in.shape[2]
            xp, k, v, ki = _dsa_mixer(xp, mod_p, nw[1:2], w_in, w_out, n_proj, nkv=nkv, **tiles_p)
            outs["b_k_p"].append(k)
            outs["b_v_p"].append(v)
            outs["b_i_p"].append(ki)
            xs, k, v, ki = _dsa_mixer(xs, mod_s, nw[1:2], w_in, w_out, n_proj, nkv=nkv,
                                      cache=(cache_b_k, cache_b_v, cache_b_idx, j, page_table), **tiles_s)
            outs["b_k_s"].append(k)
            outs["b_v_s"].append(v)
            outs["b_i_s"].append(ki)
        else:
            prm = ((lru_w_in16, (j,)), lru_conv_w[j], lru_conv_b[j], lru_w_gate_a[j].astype(BF16),
                   lru_b_gate_a[j], lru_w_gate_x[j].astype(BF16), lru_b_gate_x[j], lru_lambda[j],
                   (lru_w_out16, (j,)))
            buf0 = jnp.zeros((bp,) + state_c_conv.shape[2:], F32)
            h0 = jnp.zeros((bp,) + state_c_h.shape[2:], F32)
            xp, buf, hl = _lru_mixer(xp, mod_p, nw[1:2], buf0, h0, *prm, **tiles_p)
            outs["c_conv_p"].append(buf)
            outs["c_h_p"].append(hl)
            xs, buf, hl = _lru_mixer(xs, mod_s, nw[1:2], state_c_conv[j], state_c_h[j], *prm, **tiles_s)
            outs["c_conv_s"].append(buf)
            outs["c_h_s"].append(hl)
        xp = ffn(xp, mod_p, tm_p, 1)
        xs = ffn(xs, mod_s, tm_s, 1)

    modf_p, modf_s = groups(mod_f)
    fnw = final_norm_w.reshape(1, d)
    y_p = _final_norm(xp.reshape(bp * seq, d), modf_p, fnw, tm=tm_p).reshape(bp, seq, d)
    y_s = _final_norm(xs.reshape(bs * ts, d), modf_s, fnw, tm=tm_s).reshape(bs, ts, d)
    st = {k: jnp.stack(v) for k, v in outs.items()}
    return (y_p, y_s, st["a_conv_p"], st["a_conv_s"], st["a_ssm_p"], st["a_ssm_s"], st["b_k_p"], st["b_k_s"],
            st["b_v_p"], st["b_v_s"], st["b_i_p"], st["b_i_s"], st["c_conv_p"], st["c_conv_s"], st["c_h_p"],
            st["c_h_s"])
```

```python
import functools
import math

import jax
import jax.numpy as jnp
from jax import lax
from jax.experimental import pallas as pl
from jax.experimental.pallas import tpu as pltpu

F32 = jnp.float32
BF16 = jnp.bfloat16
I32 = jnp.int32

N_MIXERS = 3
N_ADA = 9
CONV_W = 4
NORM_EPS = 1e-6
NEG_INF = -1e30
GDN_CHUNK = 64
TOPK_MAX = 256
Q_BLOCK = 128
ROPE_THETA = 10000.0
RG_C = 8.0
LRU_BLOCK = 256
SAMPLE_PAGES_PER_STEP = 4

LANES = 128
SUBLANES = 8
VMEM_LIMIT_BYTES = 56 * 2**20
INT32_MIN = -2**31


def _cparams(*sem):
    return pltpu.CompilerParams(dimension_semantics=sem, vmem_limit_bytes=VMEM_LIMIT_BYTES)


def _tile(n, target, align=LANES):
    if n <= target:
        return n
    t = (target // align) * align
    while t >= align:
        if n % t == 0:
            return t
        t -= align
    raise ValueError(f"no {align}-aligned tile of {n} below {target}")


def _sigmoid(x):
    return jax.nn.sigmoid(x)


def _silu(x):
    return x * _sigmoid(x)


def _dot(a, b):
    return jnp.dot(a, b, preferred_element_type=F32)


def _dot_nt(a, b):
    return lax.dot_general(a, b, (((1,), (1,)), ((), ())), preferred_element_type=F32)


def _dot_tn(a, b):
    return lax.dot_general(a, b, (((0,), (0,)), ((), ())), preferred_element_type=F32)


def _dot_f32(a, b):
    return jnp.dot(a, b, preferred_element_type=F32, precision=lax.Precision.HIGHEST)


def _norm_mod(x, nw, sh, sc):
    ms = jnp.mean(x * x, axis=-1, keepdims=True)
    y = x * lax.rsqrt(ms + NORM_EPS) * nw
    return y * (1.0 + sc) + sh


def _ada_kernel(c_ref, w_ref, b_ref, o_ref):
    a = _silu(c_ref[...]).astype(BF16)
    o_ref[0] = _dot(a, w_ref[0].astype(BF16)) + b_ref[0]


def _ada(c_all, w, b):
    n_l, d, n = w.shape
    mp = c_all.shape[0]
    tn = _tile(n, 1024)
    return pl.pallas_call(
        _ada_kernel,
        grid=(n_l, n // tn),
        in_specs=[
            pl.BlockSpec((mp, d), lambda l, j: (0, 0)),
            pl.BlockSpec((1, d, tn), lambda l, j: (l, 0, j)),
            pl.BlockSpec((1, 1, tn), lambda l, j: (l, 0, j)),
        ],
        out_specs=pl.BlockSpec((1, mp, tn), lambda l, j: (l, 0, j)),
        out_shape=jax.ShapeDtypeStruct((n_l, mp, n), F32),
        name="ada_mod",
        compiler_params=_cparams("parallel", "parallel"),
    )(c_all, w, b.reshape(n_l, 1, n))


def _mod_spec(mod3, k, d, tm, rows_per_group):
    return pl.BlockSpec((None, mod3.shape[1], d), lambda i, j: ((i * tm) // rows_per_group, 0, k))


def _wspec(w, block, imap):
    arr, idx = w
    return pl.BlockSpec((None,) * len(idx) + block, lambda i, j: idx + imap(i, j))


def _wshape(w):
    return w[0].shape[len(w[1]):]


def _ffn_kernel(x_ref, nw_ref, sh_ref, sc_ref, g_ref, wa_ref, wb_ref, wo_ref, o_ref, xn_ref):
    f = pl.program_id(1)

    @pl.when(f == 0)
    def _():
        xn_ref[...] = _norm_mod(x_ref[...], nw_ref[...], sh_ref[...], sc_ref[...]).astype(BF16)
        o_ref[...] = jnp.zeros_like(o_ref)

    xn = xn_ref[...]
    a = _dot(xn, wa_ref[...])
    b = _dot(xn, wb_ref[...])
    o_ref[...] += _dot((_silu(a) * b).astype(BF16), wo_ref[...])

    @pl.when(f == pl.num_programs(1) - 1)
    def _():
        o_ref[...] = x_ref[...] + 0.5 * g_ref[...] * o_ref[...]


def _ffn(x, mod3, k0, nw, w_in, w_out, *, tm, tf_target):
    m, d = x.shape
    f = _wshape(w_out)[0]
    tf = _tile(f, tf_target)
    nf = f // tf
    rpg = m // mod3.shape[0]
    ms = lambda k: _mod_spec(mod3, k, d, tm, rpg)
    return pl.pallas_call(
        _ffn_kernel,
        grid=(m // tm, nf),
        in_specs=[
            pl.BlockSpec((tm, d), lambda i, j: (i, 0)),
            pl.BlockSpec((1, d), lambda i, j: (0, 0)),
            ms(k0), ms(k0 + 1), ms(k0 + 2),
            _wspec(w_in, (d, tf), lambda i, j: (0, j)),
            _wspec(w_in, (d, tf), lambda i, j: (0, j + nf)),
            _wspec(w_out, (tf, d), lambda i, j: (j, 0)),
        ],
        out_specs=pl.BlockSpec((tm, d), lambda i, j: (i, 0)),
        out_shape=jax.ShapeDtypeStruct((m, d), F32),
        scratch_shapes=[pltpu.VMEM((tm, d), BF16)],
        name="ffn",
        compiler_params=_cparams("parallel", "arbitrary"),
    )(x, nw, mod3, mod3, mod3, w_in[0], w_in[0], w_out[0])


def _normproj_kernel(x_ref, nw_ref, sh_ref, sc_ref, w_ref, o_ref, xn_ref):
    @pl.when(pl.program_id(1) == 0)
    def _():
        xn_ref[...] = _norm_mod(x_ref[...], nw_ref[...], sh_ref[...], sc_ref[...]).astype(BF16)

    o_ref[...] = _dot(xn_ref[...], w_ref[...])


def _normproj(x, mod3, k0, nw, w, *, tm, tn_target=1024):
    m, d = x.shape
    n = _wshape(w)[1]
    tn = _tile(n, tn_target)
    rpg = m // mod3.shape[0]
    ms = lambda k: _mod_spec(mod3, k, d, tm, rpg)
    return pl.pallas_call(
        _normproj_kernel,
        grid=(m // tm, n // tn),
        in_specs=[
            pl.BlockSpec((tm, d), lambda i, j: (i, 0)),
            pl.BlockSpec((1, d), lambda i, j: (0, 0)),
            ms(k0), ms(k0 + 1),
            _wspec(w, (d, tn), lambda i, j: (0, j)),
        ],
        out_specs=pl.BlockSpec((tm, tn), lambda i, j: (i, j)),
        out_shape=jax.ShapeDtypeStruct((m, n), F32),
        scratch_shapes=[pltpu.VMEM((tm, d), BF16)],
        name="normproj",
        compiler_params=_cparams("parallel", "arbitrary"),
    )(x, nw, mod3, mod3, w[0])


def _outproj_kernel(a_ref, w_ref, x_ref, g_ref, o_ref):
    o_ref[...] = x_ref[...] + g_ref[...] * _dot(a_ref[...], w_ref[...])


def _outproj(a, w, x, mod3, kg, *, tm, tn_target=1024):
    m, kdim = a.shape
    d = _wshape(w)[1]
    tn = _tile(d, tn_target)
    rpg = m // mod3.shape[0]
    r = mod3.shape[1]
    nd = d // tn
    return pl.pallas_call(
        _outproj_kernel,
        grid=(m // tm, nd),
        in_specs=[
            pl.BlockSpec((tm, kdim), lambda i, j: (i, 0)),
            _wspec(w, (kdim, tn), lambda i, j: (0, j)),
            pl.BlockSpec((tm, tn), lambda i, j: (i, j)),
            pl.BlockSpec((None, r, tn), lambda i, j: ((i * tm) // rpg, 0, kg * nd + j)),
        ],
        out_specs=pl.BlockSpec((tm, tn), lambda i, j: (i, j)),
        out_shape=jax.ShapeDtypeStruct((m, d), F32),
        name="outproj",
        compiler_params=_cparams("parallel", "arbitrary"),
    )(a, w[0], x, mod3)


def _final_kernel(x_ref, nw_ref, sh_ref, sc_ref, o_ref):
    o_ref[...] = _norm_mod(x_ref[...], nw_ref[...], sh_ref[...], sc_ref[...])


def _final_norm(x, mod3, nw, *, tm):
    m, d = x.shape
    rpg = m // mod3.shape[0]
    r = mod3.shape[1]
    ms = lambda k: pl.BlockSpec((None, r, d), lambda i: ((i * tm) // rpg, 0, k))
    return pl.pallas_call(
        _final_kernel,
        grid=(m // tm,),
        in_specs=[pl.BlockSpec((tm, d), lambda i: (i, 0)), pl.BlockSpec((1, d), lambda i: (0, 0)), ms(0), ms(1)],
        out_specs=pl.BlockSpec((tm, d), lambda i: (i, 0)),
        out_shape=jax.ShapeDtypeStruct((m, d), F32),
        name="final_norm",
        compiler_params=_cparams("parallel"),
    )(x, nw, mod3, mod3)


def _causal_conv(x, halo, w):
    def taps(rows, fix):
        acc = rows * w[CONV_W - 1:CONV_W]
        for j in range(1, CONV_W):
            acc = acc + fix(pltpu.roll(rows, j, 0), j) * w[CONV_W - 1 - j:CONV_W - j]
        return acc

    row = lax.broadcasted_iota(I32, halo.shape, 0)
    head = taps(x[:SUBLANES], lambda r, j: jnp.where(row < j, pltpu.roll(halo, j, 0), r))
    if x.shape[0] == SUBLANES:
        return head
    return jnp.concatenate([head, taps(x, lambda r, j: r)[SUBLANES:]], axis=0)


def _conv_specs(tt, tc, coff, boff):
    sub = tt // SUBLANES
    return [
        pl.BlockSpec((None, tt, tc), lambda b, t, c: (b, t, c + coff)),
        pl.BlockSpec((None, SUBLANES, tc), lambda b, t, c: (b, jnp.maximum(t * sub - 1, 0), c + coff)),
        pl.BlockSpec((None, SUBLANES, tc), lambda b, t, c: (b, 0, c + boff)),
    ]


def _pad_buf(buf):
    return jnp.pad(buf, ((0, 0), (SUBLANES - (CONV_W - 1), 0), (0, 0)))


def _gdn_prep_kernel(x_ref, halo_ref, buf_ref, w_ref, o_ref, *, norm):
    halo = jnp.where(pl.program_id(1) == 0, buf_ref[...], halo_ref[...])
    y = _silu(_causal_conv(x_ref[...], halo, w_ref[...]))
    if norm:
        for h in range(y.shape[1] // LANES):
            seg = y[:, h * LANES:(h + 1) * LANES]
            ss = jnp.sum(seg * seg, axis=-1, keepdims=True)
            o_ref[:, h * LANES:(h + 1) * LANES] = seg * lax.rsqrt(ss + NORM_EPS)
    else:
        o_ref[...] = y


def _gdn_prep(proj, buf8, conv_w, *, col0, ncols, norm, tt):
    bsz, t, _ = proj.shape
    tc = _tile(ncols, 512)
    return pl.pallas_call(
        functools.partial(_gdn_prep_kernel, norm=norm),
        grid=(bsz, t // tt, ncols // tc),
        in_specs=_conv_specs(tt, tc, col0 // tc, col0 // tc)
        + [pl.BlockSpec((CONV_W, tc), lambda b, i, c: (0, c + col0 // tc))],
        out_specs=pl.BlockSpec((None, tt, tc), lambda b, i, c: (b, i, c)),
        out_shape=jax.ShapeDtypeStruct((bsz, t, ncols), F32),
        name="gdn_prep",
        compiler_params=_cparams("parallel", "parallel", "parallel"),
    )(proj, proj, buf8, conv_w)


def _gdn_gate_kernel(x_ref, alog_ref, dtb_ref, beta_ref, gc_ref, *, hv, chunk):
    x = x_ref[...]
    tt = x.shape[0]
    beta_ref[...] = _sigmoid(x)
    z = x + dtb_ref[...]
    g = -jnp.exp(alog_ref[...]) * (jnp.maximum(z, 0.0) + jnp.log1p(jnp.exp(-jnp.abs(z))))
    row = lax.broadcasted_iota(I32, (tt, tt), 0)
    col = lax.broadcasted_iota(I32, (tt, tt), 1)
    tri = jnp.where((row >= col) & (row // chunk == col // chunk), 1.0, 0.0)
    gc_ref[...] = _dot_f32(tri, g)


def _gdn_gates(proj, a_log, dt_bias, *, col0, hv, chunk, tt):
    bsz, t, _ = proj.shape
    pad = lambda v: jnp.pad(v.astype(F32), (hv, LANES - 2 * hv)).reshape(1, LANES)
    blk = pl.BlockSpec((None, tt, LANES), lambda b, i: (b, i, col0 // LANES))
    out = pl.BlockSpec((None, tt, LANES), lambda b, i: (b, i, 0))
    par = pl.BlockSpec((1, LANES), lambda b, i: (0, 0))
    return pl.pallas_call(
        functools.partial(_gdn_gate_kernel, hv=hv, chunk=chunk),
        grid=(bsz, t // tt),
        in_specs=[blk, par, par],
        out_specs=[out, out],
        out_shape=[jax.ShapeDtypeStruct((bsz, t, LANES), F32)] * 2,
        name="gdn_gates",
        compiler_params=_cparams("parallel", "parallel"),
    )(proj, pad(a_log), pad(dt_bias))


def _split_bf16(a):
    hi = a.astype(BF16)
    return hi, (a - hi.astype(F32)).astype(BF16)


def _dot_split(a, b):
    ah, al = _split_bf16(a)
    bh, bl = _split_bf16(b)
    return _dot(jnp.concatenate([ah, al, ah], axis=1), jnp.concatenate([bh, bh, bl], axis=0))


def _dot_bf16(a, b):
    return _dot(a.astype(BF16), b.astype(BF16))


def _tri_inv_all(lmats, c, n_real):
    row = lax.broadcasted_iota(I32, (c, c), 0)
    col = lax.broadcasted_iota(I32, (c, c), 1)
    eye = jnp.where(row == col, 1.0, 0.0)
    base = min(16, c)
    ps = [-jnp.where(row // base == col // base, m, 0.0) for m in lmats]
    rs = [eye + p for p in ps]
    n = 2
    while n < base:
        ps = [_dot_bf16(p, p) for p in ps]
        rs = [r + _dot_bf16(r, p) for r, p in zip(rs, ps)]
        n *= 2
    s = base
    while s < min(c, n_real):
        off = (row // (2 * s) == col // (2 * s)) & (row // s != col // s)
        ts = [_dot_bf16(jnp.where(off, m, 0.0), r) for m, r in zip(lmats, rs)]
        rs = [r - _dot_bf16(r, t) for r, t in zip(rs, ts)]
        s *= 2
    res = [eye - r - _dot_split(m, r) for m, r in zip(lmats, rs)]
    return [r + _dot_bf16(r, e) for r, e in zip(rs, res)]


def _gdn_core_kernel(q_ref, k_ref, v_ref, z_ref, gcc_ref, gcr_ref, bc_ref, s0_ref, nw_ref, o_ref, so_ref, s_ref,
                     *, rep, c, hpb, dk, n_real):
    ci = pl.program_id(2)

    @pl.when(ci == 0)
    def _():
        s_ref[...] = s0_ref[...]

    row = lax.broadcasted_iota(I32, (c, c), 0)
    col = lax.broadcasted_iota(I32, (c, c), 1)
    causal = row >= col
    heads = [(hh, r) for hh in range(hpb) for r in range(rep)]
    lanes = [slice((hh * rep + r) * LANES, (hh * rep + r + 1) * LANES) for hh, r in heads]
    qs = [q_ref[:, hh * LANES:(hh + 1) * LANES] * dk ** -0.5 for hh in range(hpb)]
    ks = [k_ref[:, hh * LANES:(hh + 1) * LANES] for hh in range(hpb)]
    k16 = [k.astype(BF16) for k in ks]
    grams = [_dot_nt(kb, kb) for kb in k16]
    qk0s = [_dot_nt(q.astype(BF16), kb) for q, kb in zip(qs, k16)]
    gccs = [gcc_ref[hh, :, r:r + 1] for hh, r in heads]
    gcrs = [gcr_ref[hh, r:r + 1, :] for hh, r in heads]
    betas = [bc_ref[hh, :, r:r + 1] for hh, r in heads]
    decays = [jnp.where(causal, jnp.exp(jnp.where(causal, gc - gr, 0.0)), 0.0) for gc, gr in zip(gccs, gcrs)]
    lowers = [jnp.where(row > col, grams[hh] * b * d, 0.0) for (hh, _), b, d in zip(heads, betas, decays)]
    tinvs = _tri_inv_all(lowers, c, n_real)
    egcs = [jnp.exp(gc) for gc in gccs]
    sols = [_dot_split(ti, jnp.concatenate([v_ref[:, ln] * b, ks[hh] * (b * e)], axis=1))
            for ti, ln, b, e, (hh, _) in zip(tinvs, lanes, betas, egcs, heads)]
    qg16 = [(qs[hh] * e).astype(BF16) for (hh, _), e in zip(heads, egcs)]
    qk16 = [(qk0s[hh] * d).astype(BF16) for (hh, _), d in zip(heads, decays)]
    g_last = [gc[c - 1:c, :] for gc in gccs]
    kd16 = [(ks[hh] * jnp.exp(gl - gc)).astype(BF16) for (hh, _), gl, gc in zip(heads, g_last, gccs)]
    ss = [s_ref[i] for i in range(len(heads))]
    s16 = [s.astype(BF16) for s in ss]
    us = [sol[:, :LANES] - _dot(sol[:, LANES:].astype(BF16), sb) for sol, sb in zip(sols, s16)]
    u16 = [u.astype(BF16) for u in us]
    for i in range(len(heads)):
        s_ref[i] = ss[i] * jnp.exp(g_last[i]) + _dot_tn(kd16[i], u16[i])
    os_ = [_dot(a, sb) + _dot(b, ub) for a, sb, b, ub in zip(qg16, s16, qk16, u16)]
    for o, ln in zip(os_, lanes):
        on = o * lax.rsqrt(jnp.mean(o * o, axis=-1, keepdims=True) + NORM_EPS) * nw_ref[...]
        o_ref[:, ln] = (on * _silu(z_ref[:, ln])).astype(BF16)

    @pl.when(ci == pl.num_programs(2) - 1)
    def _():
        so_ref[...] = s_ref[...]


def _gdn_core(qk, v, proj, zcol0, gc, beta, s0, norm_w, *, c, hpb, n_real):
    bsz, t, val = v.shape
    hv = s0.shape[1]
    dk, dv = s0.shape[2], s0.shape[3]
    hk = qk.shape[2] // (2 * dk)
    rep = hv // hk
    assert dk == LANES and dv == LANES and t % c == 0 and hk % hpb == 0 and zcol0 % (hpb * rep * dv) == 0
    heads = lambda a, lo: a[:, :, lo:lo + hv].reshape(bsz, t, hk, rep).transpose(0, 2, 1, 3)
    gcc = heads(gc, hv)
    bcc = heads(beta, 0)
    gcr = gcc.transpose(0, 1, 3, 2)
    nhb = hk // hpb
    zb = zcol0 // (hpb * rep * dv)
    colspec = pl.BlockSpec((None, hpb, c, rep), lambda b, h, i: (b, h, i, 0))
    o, s_out = pl.pallas_call(
        functools.partial(_gdn_core_kernel, rep=rep, c=c, hpb=hpb, dk=dk, n_real=n_real),
        grid=(bsz, nhb, t // c),
        in_specs=[
            pl.BlockSpec((None, c, hpb * dk), lambda b, h, i: (b, i, h)),
            pl.BlockSpec((None, c, hpb * dk), lambda b, h, i: (b, i, nhb + h)),
            pl.BlockSpec((None, c, hpb * rep * dv), lambda b, h, i: (b, i, h)),
            pl.BlockSpec((None, c, hpb * rep * dv), lambda b, h, i: (b, i, zb + h)),
            colspec,
            pl.BlockSpec((None, hpb, rep, c), lambda b, h, i: (b, h, 0, i)),
            colspec,
            pl.BlockSpec((None, hpb * rep, dk, dv), lambda b, h, i: (b, h, 0, 0)),
            pl.BlockSpec((1, dv), lambda b, h, i: (0, 0)),
        ],
        out_specs=[
            pl.BlockSpec((None, c, hpb * rep * dv), lambda b, h, i: (b, i, h)),
            pl.BlockSpec((None, hpb * rep, dk, dv), lambda b, h, i: (b, h, 0, 0)),
        ],
        out_shape=[jax.ShapeDtypeStruct((bsz, t, val), BF16), jax.ShapeDtypeStruct(s0.shape, F32)],
        scratch_shapes=[pltpu.VMEM((hpb * rep, dk, dv), F32)],
        name="gdn_core",
        compiler_params=_cparams("parallel", "parallel", "arbitrary"),
    )(qk, qk, v, proj, gcc, gcr, bcc, s0, norm_w.reshape(1, dv))
    return o, s_out


def _gdn_mixer(x, mod3, nw, buf, s0, w_in, conv_w, a_log, dt_bias, norm_w, w_out, *, tm, tm_in, tt, chunk):
    bsz, t, d = x.shape
    hv, dk, dv = s0.shape[1], s0.shape[2], s0.shape[3]
    val = hv * dv
    conv_dim = conv_w.shape[1]
    key = (conv_dim - val) // 2
    assert (conv_dim + val) % LANES == 0 and 2 * hv <= LANES and t >= CONV_W - 1
    x2 = x.reshape(bsz * t, d)
    proj = _normproj(x2, mod3, 3, nw, w_in, tm=tm_in).reshape(bsz, t, -1)
    buf8 = _pad_buf(buf)
    qk = _gdn_prep(proj, buf8, conv_w, col0=0, ncols=2 * key, norm=True, tt=tt)
    v = _gdn_prep(proj, buf8, conv_w, col0=2 * key, ncols=val, norm=False, tt=tt)
    tp = -(-t // chunk) * chunk
    beta, gc = _gdn_gates(proj, a_log, dt_bias, col0=conv_dim + val, hv=hv, chunk=min(chunk, tt), tt=tt)
    if tp != t:
        padt = lambda a: jnp.pad(a, ((0, 0), (0, tp - t), (0, 0)))
        gc = jnp.concatenate([gc, jnp.broadcast_to(gc[:, -1:], (bsz, tp - t, LANES))], axis=1)
        qk, v, beta, projz = padt(qk), padt(v), padt(beta), padt(proj)
    else:
        projz = proj
    o, s_new = _gdn_core(qk, v, projz, conv_dim, gc, beta, s0, norm_w, c=chunk, hpb=min(4, key // dk),
                         n_real=min(t, chunk))
    o2 = o[:, :t].reshape(bsz * t, val)
    xo = _outproj(o2, w_out, x2, mod3, 5, tm=tm).reshape(bsz, t, d)
    new_buf = proj[:, t - (CONV_W - 1):, :conv_dim]
    return xo, new_buf, s_new


def _rope_tables(pos, half):
    inv_freq = ROPE_THETA ** (-jnp.arange(half, dtype=F32) / half)
    ang = pos.astype(F32)[:, None] * inv_freq[None, :]
    cos, sin = jnp.cos(ang), jnp.sin(ang)
    return jnp.concatenate([cos, cos], axis=-1), jnp.concatenate([-sin, sin], axis=-1)


def _dsa_prep_kernel(x_ref, cos_ref, sin_ref, q_ref, k_ref, k16_ref, v16_ref, qi_ref, ki_ref, ki16_ref, *tr_refs,
                     nh, nkv, nih):
    cos, sin = cos_ref[...], sin_ref[...]

    def rope(col):
        seg = x_ref[:, col * LANES:(col + 1) * LANES]
        return seg * cos + pltpu.roll(seg, LANES // 2, 1) * sin

    for h in range(nh):
        q_ref[:, h * LANES:(h + 1) * LANES] = (rope(h) * LANES ** -0.5).astype(BF16)
    for h in range(nkv):
        kr = rope(nh + h)
        k_ref[:, h * LANES:(h + 1) * LANES] = kr
        k16_ref[:, h * LANES:(h + 1) * LANES] = kr.astype(BF16)
    v0 = (nh + nkv) * LANES
    v16_ref[...] = x_ref[:, v0:v0 + nkv * LANES].astype(BF16)
    c0 = nh + 2 * nkv
    for h in range(nih):
        qi_ref[:, h * LANES:(h + 1) * LANES] = rope(c0 + h).astype(BF16)
    kir = rope(c0 + nih)
    ki_ref[...] = kir
    ki16_ref[...] = kir.astype(BF16)
    if tr_refs:
        wt_ref, vt_ref = tr_refs
        w0 = (c0 + nih + 1) * LANES
        wt_ref[...] = x_ref[:, w0:w0 + LANES].T[:wt_ref.shape[0], :]
        for h in range(nkv):
            vt_ref[h * LANES:(h + 1) * LANES, :] = x_ref[:, v0 + h * LANES:v0 + (h + 1) * LANES].T.astype(BF16)


def _dsa_prep(proj, pos, *, nh, nkv, nih, tt, with_wt):
    bsz, t, npj = proj.shape
    cos, sin = _rope_tables(pos, LANES // 2)
    row = lambda n, dt: jax.ShapeDtypeStruct((bsz, t, n * LANES), dt)
    ospec = lambda n: pl.BlockSpec((None, tt, n * LANES), lambda b, i: (b, i, 0))
    tab = pl.BlockSpec((tt, LANES), lambda b, i: (i, 0))
    nwt = -(-nih // SUBLANES) * SUBLANES
    return pl.pallas_call(
        functools.partial(_dsa_prep_kernel, nh=nh, nkv=nkv, nih=nih),
        grid=(bsz, t // tt),
        in_specs=[pl.BlockSpec((None, tt, npj), lambda b, i: (b, i, 0)), tab, tab],
        out_specs=[ospec(nh), ospec(nkv), ospec(nkv), ospec(nkv), ospec(nih), ospec(1), ospec(1)]
        + ([pl.BlockSpec((None, nwt, tt), lambda b, i: (b, 0, i)),
            pl.BlockSpec((None, None, nkv * LANES, tt), lambda b, i: (b, i, 0, 0))] if with_wt else []),
        out_shape=[row(nh, BF16), row(nkv, F32), row(nkv, BF16), row(nkv, BF16), row(nih, BF16), row(1, F32),
                   row(1, BF16)]
        + ([jax.ShapeDtypeStruct((bsz, nwt, t), F32),
            jax.ShapeDtypeStruct((bsz, t // tt, nkv * LANES, tt), BF16)] if with_wt else []),
        name="dsa_prep",
        compiler_params=_cparams("parallel", "parallel"),
    )(proj, cos, sin)


def _sort_key(s):
    bits = pltpu.bitcast(jnp.where(s == 0.0, 0.0, s), I32)
    return jnp.where(bits < 0, bits ^ 0x7FFFFFFF, bits)


def _kth_largest_key(count_ge, shape, topk):
    def body(i, t):
        cand = t + lax.shift_left(jnp.int32(1), 31 - i)
        return jnp.where(count_ge(cand) >= topk, cand, t)
    return lax.fori_loop(0, 32, body, jnp.full(shape, INT32_MIN, I32))


def _dsa_prompt_kernel(qi_ref, wt_ref, q_ref, ki_ref, k_ref, vt_ref, o_ref, key_ref, bias_ref, acc_ref,
                       *, nh, nkv, nih, topk, idx_scale):
    qb = pl.program_id(1)
    blk = Q_BLOCK
    kb = vt_ref.shape[2]
    ktiles = kb // blk
    nkb = (qb + ktiles) // ktiles
    rowk = lax.broadcasted_iota(I32, (kb, blk), 0)
    colq = lax.broadcasted_iota(I32, (kb, blk), 1)
    wt = wt_ref[...]
    hq = max(1, nih // 4)
    qis = [jnp.concatenate([qi_ref[:, h * LANES:(h + 1) * LANES] for h in range(h0, min(h0 + hq, nih))], axis=0)
           for h0 in range(0, nih, hq)]

    def rows_of(j):
        return pl.ds(pl.multiple_of(j * kb, kb), kb)

    def visible(j):
        return j * kb + rowk <= qb * blk + colq

    def score_body(j, carry):
        keys = ki_ref[rows_of(j), :]
        lgs = [_dot_nt(keys, qi) for qi in qis]
        acc = jnp.zeros((kb, blk), F32)
        for i, lg in enumerate(lgs):
            for hh in range(lg.shape[1] // blk):
                h = i * hq + hh
                acc = acc + jnp.maximum(lg[:, hh * blk:(hh + 1) * blk], 0.0) * wt[h:h + 1, :]
        key_ref[rows_of(j), :] = _sort_key(jnp.where(visible(j), acc * idx_scale, NEG_INF))
        return carry

    lax.fori_loop(0, nkb, score_body, 0)

    def count_ge(cand):
        def body(j, c):
            hit = jnp.where(key_ref[rows_of(j), :] >= cand, 1, 0)
            for i in range(ktiles):
                c = c + hit[i * blk:(i + 1) * blk]
            return c
        cnt = lax.fori_loop(0, nkb, body, jnp.zeros((blk, blk), I32))
        return jnp.sum(cnt, axis=0, keepdims=True)

    thr = _kth_largest_key(count_ge, (1, blk), topk)

    def bias_body(j, carry):
        sel = (key_ref[rows_of(j), :] >= thr) & visible(j)
        bias_ref[rows_of(j), :] = jnp.where(sel, 0.0, NEG_INF)
        return carry

    lax.fori_loop(0, nkb, bias_body, 0)

    rep = nh // nkv
    qgs =[jnp.concatenate([q_ref[:, (g * rep + r) * LANES:(g * rep + r + 1) * LANES] for r in range(rep)], axis=0)
           for g in range(nkv)]
    acc_ref[...] = jnp.zeros_like(acc_ref)

    def att_body(j, carry):
        ms, ls = carry
        rows = rows_of(j)
        bias = jnp.concatenate([bias_ref[rows, :]] * rep, axis=1)
        ss = [_dot_nt(k_ref[rows, g * LANES:(g + 1) * LANES], qgs[g]) + bias for g in range(nkv)]
        m_new = [jnp.maximum(m, jnp.max(s, axis=0, keepdims=True)) for m, s in zip(ms, ss)]
        ps = [jnp.exp(s - m) for s, m in zip(ss, m_new)]
        pvs = [_dot(vt_ref[j, g * LANES:(g + 1) * LANES, :], p.astype(BF16)) for g, p in enumerate(ps)]
        alphas = [jnp.exp(m - mn) for m, mn in zip(ms, m_new)]
        for g in range(nkv):
            acc_ref[g] = alphas[g] * acc_ref[g] + pvs[g]
        l_new = [a * l + jnp.sum(p, axis=0, keepdims=True) for a, l, p in zip(alphas, ls, ps)]
        return tuple(m_new), tuple(l_new)

    row0 = lambda v: tuple(jnp.full((1, rep * blk), v, F32) for _ in range(nkv))
    _, ls = lax.fori_loop(0, nkb, att_body, (row0(NEG_INF), row0(0.0)))
    for g in range(nkv):
        og = acc_ref[g] / ls[g]
        for r in range(rep):
            h = g * rep + r
            o_ref[:, h * LANES:(h + 1) * LANES] = og[:, r * blk:(r + 1) * blk].T.astype(BF16)


def _dsa_prompt_attend(q16, qi16, wt, ki16, k16, vt16, *, nh, nkv, nih, topk):
    bsz, t, _ = q16.shape
    rep = nh // nkv
    full = lambda n: pl.BlockSpec((None, t, n * LANES), lambda b, i: (b, 0, 0))
    blk = lambda n: pl.BlockSpec((None, Q_BLOCK, n * LANES), lambda b, i: (b, i, 0))
    return pl.pallas_call(
        functools.partial(_dsa_prompt_kernel, nh=nh, nkv=nkv, nih=nih, topk=topk,
                          idx_scale=(LANES * nih) ** -0.5),
        grid=(bsz, t // Q_BLOCK),
        in_specs=[blk(nih), pl.BlockSpec((None, wt.shape[1], Q_BLOCK), lambda b, i: (b, 0, i)), blk(nh),
                  full(1), full(nkv), pl.BlockSpec((None,) + vt16.shape[1:], lambda b, i: (b, 0, 0, 0))],
        out_specs=blk(nh),
        out_shape=jax.ShapeDtypeStruct((bsz, t, nh * LANES), BF16),
        scratch_shapes=[pltpu.VMEM((t, Q_BLOCK), I32), pltpu.VMEM((t, Q_BLOCK), F32),
                        pltpu.VMEM((nkv, LANES, rep * Q_BLOCK), F32)],
        name="dsa_prompt_attend",
        compiler_params=_cparams("parallel", "arbitrary"),
    )(qi16, wt, q16, ki16, k16, vt16)


def _dsa_sample_score_kernel(pt_ref, qi_ref, wc_ref, *refs, n_steps, pps, nih, tq, past, idx_scale):
    page_refs, new_ref, o_ref = refs[:pps], refs[pps], refs[pps + 1]
    p = pl.program_id(1)
    keys = jnp.concatenate([r[...] for r in page_refs], axis=0).astype(BF16)
    keys = jnp.where(p == n_steps - 1, new_ref[...], keys)
    qi = jnp.concatenate([qi_ref[:, h * LANES:(h + 1) * LANES] for h in range(nih)], axis=0)
    w = jnp.maximum(_dot_nt(qi, keys), 0.0) * wc_ref[...]
    acc = w[0:tq]
    for h in range(1, nih):
        acc = acc + w[h * tq:(h + 1) * tq]
    s = p * keys.shape[0] + lax.broadcasted_iota(I32, acc.shape, 1)
    qpos = past + lax.broadcasted_iota(I32, acc.shape, 0)
    o_ref[...] = jnp.where(s <= qpos, acc * idx_scale, NEG_INF)


def _dsa_sample_select_kernel(s_ref, o_ref, *, topk):
    key = _sort_key(s_ref[...])
    thr = _kth_largest_key(lambda cand: jnp.sum(jnp.where(key >= cand, 1, 0), axis=-1, keepdims=True),
                           (key.shape[0], 1), topk)
    o_ref[...] = jnp.where((key >= thr) & (s_ref[...] > 0.5 * NEG_INF), 1.0, 0.0)


def _dsa_sample_attn_kernel(pt_ref, q_ref, sel_ref, *refs, n_steps, pps, nh, nkv, tq):
    kp_refs, vp_refs = refs[:pps], refs[pps:2 * pps]
    kn_ref, vn_ref, o_ref, m_ref, l_ref, acc_ref = refs[2 * pps:]
    p = pl.program_id(1)

    @pl.when(p == 0)
    def _():
        m_ref[...] = jnp.full_like(m_ref, NEG_INF)
        l_ref[...] = jnp.zeros_like(l_ref)
        acc_ref[...] = jnp.zeros_like(acc_ref)

    last = p == n_steps - 1
    kcat = jnp.where(last, kn_ref[...], jnp.concatenate([r[...] for r in kp_refs], axis=0).astype(BF16))
    vcat = jnp.where(last, vn_ref[...], jnp.concatenate([r[...] for r in vp_refs], axis=0).astype(BF16))
    rep = nh // nkv
    q = jnp.concatenate([q_ref[:, h * LANES:(h + 1) * LANES] for h in range(nh)], axis=0)
    s = _dot_nt(q, kcat)
    page = sel_ref.shape[1] // pps
    expand = (lax.broadcasted_iota(I32, (page, page * nkv), 1) // nkv
              == lax.broadcasted_iota(I32, (page, page * nkv), 0)).astype(BF16)
    sel = sel_ref[...].astype(BF16)
    selx = jnp.concatenate([_dot(sel[:, i * page:(i + 1) * page], expand) for i in range(pps)], axis=1)
    selx = jnp.concatenate([selx] * nh, axis=0)
    head_ok = (lax.broadcasted_iota(I32, s.shape, 0) // (tq * rep) == lax.broadcasted_iota(I32, s.shape, 1) % nkv)
    s = jnp.where((selx > 0.5) & head_ok, s, NEG_INF)
    m = m_ref[...]
    m_new = jnp.maximum(m, jnp.max(s, axis=-1, keepdims=True))
    alpha = jnp.exp(m - m_new)
    pr = jnp.where(s > 0.5 * NEG_INF, jnp.exp(s - m_new), 0.0)
    l_ref[...] = alpha * l_ref[...] + jnp.sum(pr, axis=-1, keepdims=True)
    acc_ref[...] = alpha * acc_ref[...] + _dot(pr.astype(BF16), vcat)
    m_ref[...] = m_new

    @pl.when(last)
    def _():
        og = acc_ref[...] / l_ref[...]
        for h in range(nh):
            o_ref[:, h * LANES:(h + 1) * LANES] = og[h * tq:(h + 1) * tq].astype(BF16)


def _dsa_sample_attend(q16, qi16, wi, ki16, k16, v16, cache_k, cache_v, cache_i, layer, page_table,
                       *, nh, nkv, nih, topk):
    bsz, tq, _ = q16.shape
    n_pages = page_table.shape[1]
    n_layers, n_pool, page = cache_i.shape[:3]
    pps = math.gcd(n_pages, SAMPLE_PAGES_PER_STEP)
    assert page == LANES and tq == SUBLANES
    past = n_pages * page
    n_steps = n_pages // pps + 1
    ltot = n_steps * pps * page
    wcol = wi.transpose(0, 2, 1).reshape(bsz, nih * tq, 1)
    pidx = lambda i: (lambda b, p, pt: (layer * n_pool + pt[b, jnp.minimum(p * pps + i, n_pages - 1)], 0, 0))
    qspec = lambda n: pl.BlockSpec((None, tq, n * LANES), lambda b, p, pt: (b, 0, 0))
    idx_scale = (LANES * nih) ** -0.5
    cache_i = cache_i.reshape(n_layers * n_pool, page, LANES)
    ki_new = jnp.pad(ki16, ((0, 0), (0, pps * page - tq), (0, 0)))
    scores = pl.pallas_call(
        functools.partial(_dsa_sample_score_kernel, n_steps=n_steps, pps=pps, nih=nih, tq=tq, past=past,
                          idx_scale=idx_scale),
        grid_spec=pltpu.PrefetchScalarGridSpec(
            num_scalar_prefetch=1, grid=(bsz, n_steps),
            in_specs=[qspec(nih), pl.BlockSpec((None, nih * tq, 1), lambda b, p, pt: (b, 0, 0))]
            + [pl.BlockSpec((None, page, LANES), pidx(i)) for i in range(pps)]
            + [pl.BlockSpec((None, pps * page, LANES), lambda b, p, pt: (b, 0, 0))],
            out_specs=pl.BlockSpec((None, tq, pps * page), lambda b, p, pt: (b, 0, p))),
        out_shape=jax.ShapeDtypeStruct((bsz, tq, ltot), F32),
        name="dsa_sample_scores",
        compiler_params=_cparams("parallel", "arbitrary"),
    )(page_table, qi16, wcol, *([cache_i] * pps), ki_new)
    sel = pl.pallas_call(
        functools.partial(_dsa_sample_select_kernel, topk=topk),
        grid=(bsz,),
        in_specs=[pl.BlockSpec((None, tq, ltot), lambda b: (b, 0, 0))],
        out_specs=pl.BlockSpec((None, tq, ltot), lambda b: (b, 0, 0)),
        out_shape=jax.ShapeDtypeStruct((bsz, tq, ltot), F32),
        name="dsa_sample_select",
        compiler_params=_cparams("parallel"),
    )(scores)
    prow = page * nkv
    ck = cache_k.reshape(n_layers * n_pool, prow, LANES)
    cv = cache_v.reshape(n_layers * n_pool, prow, LANES)
    new_rows = lambda a: jnp.pad(a.reshape(bsz, tq * nkv, LANES), ((0, 0), (0, pps * prow - tq * nkv), (0, 0)))
    kvspecs = [pl.BlockSpec((None, prow, LANES), pidx(i)) for i in range(pps)]
    newspec = pl.BlockSpec((None, pps * prow, LANES), lambda b, p, pt: (b, 0, 0))
    return pl.pallas_call(
        functools.partial(_dsa_sample_attn_kernel, n_steps=n_steps, pps=pps, nh=nh, nkv=nkv, tq=tq),
        grid_spec=pltpu.PrefetchScalarGridSpec(
            num_scalar_prefetch=1, grid=(bsz, n_steps),
            in_specs=[qspec(nh), pl.BlockSpec((None, tq, pps * page), lambda b, p, pt: (b, 0, p))]
            + kvspecs + kvspecs + [newspec, newspec],
            out_specs=qspec(nh),
            scratch_shapes=[pltpu.VMEM((nh * tq, 1), F32), pltpu.VMEM((nh * tq, 1), F32),
                            pltpu.VMEM((nh * tq, LANES), F32)]),
        out_shape=jax.ShapeDtypeStruct((bsz, tq, nh * LANES), BF16),
        name="dsa_sample_attend",
        compiler_params=_cparams("parallel", "arbitrary"),
    )(page_table, q16, sel, *([ck] * pps), *([cv] * pps), new_rows(k16), new_rows(v16))


def _dsa_mixer(x, mod3, nw, w_in, w_out, n_proj, *, nkv, tm, tm_in, tt, cache=None):
    bsz, t, d = x.shape
    nh = _wshape(w_out)[0] // LANES
    nih = (n_proj - (nh + 2 * nkv + 1) * LANES) // (LANES + 1)
    assert (nh + 2 * nkv + nih + 1) * LANES + nih == n_proj and nih <= LANES
    x2 = x.reshape(bsz * t, d)
    proj = _normproj(x2, mod3, 3, nw, w_in, tm=tm_in).reshape(bsz, t, -1)
    v = proj[:, :, (nh + nkv) * LANES:(nh + 2 * nkv) * LANES]
    if cache is None:
        assert t % Q_BLOCK == 0
        q16, k, k16, _, qi16, ki, ki16, wt, vt16 = _dsa_prep(proj, jnp.arange(t), nh=nh, nkv=nkv, nih=nih,
                                                             tt=_tile(t, 4 * Q_BLOCK, Q_BLOCK), with_wt=True)
        o = _dsa_prompt_attend(q16, qi16, wt, ki16, k16, vt16, nh=nh, nkv=nkv, nih=nih, topk=min(TOPK_MAX, t // 4))
    else:
        cache_k, cache_v, cache_i, layer, page_table = cache
        past = page_table.shape[1] * cache_i.shape[2]
        q16, k, k16, v16, qi16, ki, ki16 = _dsa_prep(proj, past + jnp.arange(t), nh=nh, nkv=nkv, nih=nih, tt=tt,
                                                    with_wt=False)
        w0 = (nh + 2 * nkv + nih + 1) * LANES
        o = _dsa_sample_attend(q16, qi16, proj[:, :, w0:w0 + nih], ki16, k16, v16, cache_k, cache_v, cache_i,
                               layer, page_table, nh=nh, nkv=nkv, nih=nih, topk=min(TOPK_MAX, (past + t) // 4))
    xo = _outproj(o.reshape(bsz * t, nh * LANES), w_out, x2, mod3, 5, tm=tm).reshape(bsz, t, d)
    return xo, k.reshape(bsz, t, nkv, LANES), v.reshape(bsz, t, nkv, LANES), ki


def _gelu_tanh(x):
    return 0.5 * x * (1.0 + jnp.tanh(math.sqrt(2.0 / math.pi) * (x + 0.044715 * (x * x * x))))


def _lru_kernel(gate_ref, xb_ref, halo_ref, buf_ref, cw_ref, cb_ref, wa_ref, ba_ref, wx_ref, bx_ref, lam_ref,
                h0_ref, y_ref, hl_ref, h_ref, *, nblk):
    ti = pl.program_id(1)

    @pl.when(ti == 0)
    def _():
        h_ref[...] = h0_ref[...]

    halo = jnp.where(ti == 0, buf_ref[...], halo_ref[...])
    xc = _causal_conv(xb_ref[...], halo, cw_ref[...]) + cb_ref[...]
    tt = xc.shape[0]
    xc16 = xc.astype(BF16)
    rs, xs = [], []
    for n in range(nblk):
        blk = xc16[:, n * LRU_BLOCK:(n + 1) * LRU_BLOCK]
        rs.append(_dot(blk, wa_ref[n]))
        xs.append(_dot(blk, wx_ref[n]))
    r = _sigmoid(jnp.concatenate(rs, axis=1) + ba_ref[...])
    ig = _sigmoid(jnp.concatenate(xs, axis=1) + bx_ref[...])
    lam = lam_ref[...]
    softplus_neg = jnp.maximum(-lam, 0.0) + jnp.log1p(jnp.exp(-jnp.abs(lam)))
    log_a = -RG_C * r * softplus_neg
    a = jnp.exp(log_a)
    b = jnp.sqrt(-jnp.tanh(log_a) * (a * a + 1.0)) * (ig * xc)
    row = lax.broadcasted_iota(I32, a.shape, 0)
    d = 1
    while d < tt:
        keep = row >= d
        a_sh = jnp.where(keep, pltpu.roll(a, d, 0), 1.0)
        b_sh = jnp.where(keep, pltpu.roll(b, d, 0), 0.0)
        b = a * b_sh + b
        a = a * a_sh
        d *= 2
    hs = b + a * h_ref[...]
    h_ref[...] = hs[tt - 1:tt, :]
    y_ref[...] = (hs * _gelu_tanh(gate_ref[...])).astype(BF16)

    @pl.when(ti == pl.num_programs(1) - 1)
    def _():
        hl_ref[...] = hs[tt - 1:tt, :]


def _lru_core(proj, buf8, h0, conv_w, conv_b, w_ga, b_ga, w_gx, b_gx, lam, *, tt):
    bsz, t, w2 = proj.shape
    w = w2 // 2
    nblk = w // LRU_BLOCK
    sub = tt // SUBLANES
    vec = lambda: pl.BlockSpec((1, w), lambda b, i: (0, 0))
    wsp = lambda: pl.BlockSpec((nblk, LRU_BLOCK, LRU_BLOCK), lambda b, i: (0, 0, 0))
    r1 = lambda v: v.reshape(1, w).astype(F32)
    y, hl = pl.pallas_call(
        functools.partial(_lru_kernel, nblk=nblk),
        grid=(bsz, t // tt),
        in_specs=[
            pl.BlockSpec((None, tt, w), lambda b, i: (b, i, 0)),
            pl.BlockSpec((None, tt, w), lambda b, i: (b, i, 1)),
            pl.BlockSpec((None, SUBLANES, w), lambda b, i: (b, jnp.maximum(i * sub - 1, 0), 1)),
            pl.BlockSpec((None, SUBLANES, w), lambda b, i: (b, 0, 0)),
            pl.BlockSpec((CONV_W, w), lambda b, i: (0, 0)),
            vec(), wsp(), vec(), wsp(), vec(), vec(),
            pl.BlockSpec((None, 1, w), lambda b, i: (b, 0, 0)),
        ],
        out_specs=[pl.BlockSpec((None, tt, w), lambda b, i: (b, i, 0)),
                   pl.BlockSpec((None, 1, w), lambda b, i: (b, 0, 0))],
        out_shape=[jax.ShapeDtypeStruct((bsz, t, w), BF16), jax.ShapeDtypeStruct((bsz, 1, w), F32)],
        scratch_shapes=[pltpu.VMEM((1, w), F32)],
        name="lru_core",
        compiler_params=_cparams("parallel", "arbitrary"),
    )(proj, proj, proj, buf8, conv_w, r1(conv_b), w_ga, r1(b_ga), w_gx, r1(b_gx), r1(lam), h0.reshape(bsz, 1, w))
    return y, hl.reshape(bsz, w)


def _lru_mixer(x, mod3, nw, buf, h0, w_in, conv_w, conv_b, w_ga, b_ga, w_gx, b_gx, lam, w_out, *, tm, tm_in, tt):
    bsz, t, d = x.shape
    assert t >= CONV_W - 1
    x2 = x.reshape(bsz * t, d)
    proj = _normproj(x2, mod3, 3, nw, w_in, tm=tm_in).reshape(bsz, t, -1)
    w = proj.shape[2] // 2
    y, hl = _lru_core(proj, _pad_buf(buf), h0, conv_w, conv_b, w_ga, b_ga, w_gx, b_gx, lam, tt=tt)
    xo = _outproj(y.reshape(bsz * t, w), w_out, x2, mod3, 5, tm=tm).reshape(bsz, t, d)
    return xo, proj[:, t - (CONV_W - 1):, w:], hl


def _bf16_padded(w, mult):
    n = w.shape[-1]
    return jnp.pad(w, ((0, 0), (0, 0), (0, -n % mult))).astype(BF16)


def kernel(x_prompt, x_sample, state_a_conv, state_a_ssm, cache_b_k, cache_b_v, cache_b_idx, state_c_conv, state_c_h, page_table, c_prompt, c_sample, w_ada, b_ada, norm_w, ffn_w_in, ffn_w_out, gdn_w_in, gdn_conv_w, gdn_a_log, gdn_dt_bias, gdn_norm_w, gdn_w_out, dsa_w_in, dsa_w_out, lru_w_in, lru_conv_w, lru_conv_b, lru_w_gate_a, lru_b_gate_a, lru_w_gate_x, lru_b_gate_x, lru_lambda, lru_w_out, w_ada_final, b_ada_final, final_norm_w):
    bp, seq, d = x_prompt.shape
    bs, ts, _ = x_sample.shape
    depth = w_ada.shape[0]
    nkv = cache_b_k.shape[3]
    tm_p = _tile(seq, 512, SUBLANES)
    tm_in_p = _tile(seq, 1024, SUBLANES)
    tt_p = _tile(seq, 256, SUBLANES)
    tm_s = bs * ts
    tiles_p = dict(tm=tm_p, tm_in=tm_in_p, tt=tt_p)
    tiles_s = dict(tm=tm_s, tm_in=tm_s, tt=ts)

    c_all = jnp.concatenate([c_prompt, c_sample], axis=0)
    c_all = jnp.pad(c_all, ((0, -c_all.shape[0] % SUBLANES), (0, 0)))
    mod = _ada(c_all, w_ada, b_ada)
    mod_f = _ada(c_all, w_ada_final[None], b_ada_final[None])[0]

    def groups(m):
        return m[:bp, None, :], jnp.repeat(m[bp:bp + bs], ts, axis=0)[None]

    ffn_w_in16, ffn_w_out16 = ffn_w_in.astype(BF16), ffn_w_out.astype(BF16)
    gdn_w_in16, gdn_w_out16 = _bf16_padded(gdn_w_in, 512), gdn_w_out.astype(BF16)
    dsa_w_in16, dsa_w_out16 = _bf16_padded(dsa_w_in, 512), dsa_w_out.astype(BF16)
    lru_w_in16, lru_w_out16 = lru_w_in.astype(BF16), lru_w_out.astype(BF16)

    xp, xs = x_prompt, x_sample
    outs = {k: [] for k in ("a_conv_p", "a_conv_s", "a_ssm_p", "a_ssm_s", "b_k_p", "b_k_s", "b_v_p", "b_v_s",
                            "b_i_p", "b_i_s", "c_conv_p", "c_conv_s", "c_h_p", "c_h_s")}
    for layer in range(depth):
        kind, j = layer % N_MIXERS, layer // N_MIXERS
        mod_p, mod_s = groups(mod[layer])
        nw = norm_w[layer]

        def ffn(x, m3, tm, tf, which):
            b, t, _ = x.shape
            return _ffn(x.reshape(b * t, d), m3, 6 * which, nw[2 * which:2 * which + 1],
                        (ffn_w_in16, (layer, which)), (ffn_w_out16, (layer, which)), tm=tm,
                        tf_target=tf).reshape(b, t, d)

        xp = ffn(xp, mod_p, tm_in_p, 256, 0)
        xs = ffn(xs, mod_s, tm_s, 512, 0)
        if kind == 0:
            prm = ((gdn_w_in16, (j,)), gdn_conv_w[j], gdn_a_log[j], gdn_dt_bias[j], gdn_norm_w[j],
                   (gdn_w_out16, (j,)))
            buf0 = jnp.zeros((bp,) + state_a_conv.shape[2:], F32)
            s0 = jnp.zeros((bp,) + state_a_ssm.shape[2:], F32)
            xp, buf, s = _gdn_mixer(xp, mod_p, nw[1:2], buf0, s0, *prm, chunk=2 * GDN_CHUNK, **tiles_p)
            outs["a_conv_p"].append(buf)
            outs["a_ssm_p"].append(s)
            xs, buf, s = _gdn_mixer(xs, mod_s, nw[1:2], state_a_conv[j], state_a_ssm[j], *prm, chunk=GDN_CHUNK,
                                    **tiles_s)
            outs["a_conv_s"].append(buf)
            outs["a_ssm_s"].append(s)
        elif kind == 1:
            w_in, w_out = (dsa_w_in16, (j,)), (dsa_w_out16, (j,))
            n_proj = dsa_w_in.shape[2]
            xp, k, v, ki = _dsa_mixer(xp, mod_p, nw[1:2], w_in, w_out, n_proj, nkv=nkv, **tiles_p)
            outs["b_k_p"].append(k)
            outs["b_v_p"].append(v)
            outs["b_i_p"].append(ki)
            xs, k, v, ki = _dsa_mixer(xs, mod_s, nw[1:2], w_in, w_out, n_proj, nkv=nkv,
                                      cache=(cache_b_k, cache_b_v, cache_b_idx, j, page_table), **tiles_s)
            outs["b_k_s"].append(k)
            outs["b_v_s"].append(v)
            outs["b_i_s"].append(ki)
        else:
            prm = ((lru_w_in16, (j,)), lru_conv_w[j], lru_conv_b[j], lru_w_gate_a[j].astype(BF16),
                   lru_b_gate_a[j], lru_w_gate_x[j].astype(BF16), lru_b_gate_x[j], lru_lambda[j],
                   (lru_w_out16, (j,)))
            buf0 = jnp.zeros((bp,) + state_c_conv.shape[2:], F32)
            h0 = jnp.zeros((bp,) + state_c_h.shape[2:], F32)
            xp, buf, hl = _lru_mixer(xp, mod_p, nw[1:2], buf0, h0, *prm, **tiles_p)
            outs["c_conv_p"].append(buf)
            outs["c_h_p"].append(hl)
            xs, buf, hl = _lru_mixer(xs, mod_s, nw[1:2], state_c_conv[j], state_c_h[j], *prm, **tiles_s)
            outs["c_conv_s"].append(buf)
            outs["c_h_s"].append(hl)
        xp = ffn(xp, mod_p, tm_in_p, 256, 1)
        xs = ffn(xs, mod_s, tm_s, 512, 1)

    modf_p, modf_s = groups(mod_f)
    fnw = final_norm_w.reshape(1, d)
    y_p = _final_norm(xp.reshape(bp * seq, d), modf_p, fnw, tm=tm_p).reshape(bp, seq, d)
    y_s = _final_norm(xs.reshape(bs * ts, d), modf_s, fnw, tm=tm_s).reshape(bs, ts, d)
    st = {k: jnp.stack(v) for k, v in outs.items()}
    return (y_p, y_s, st["a_conv_p"], st["a_conv_s"], st["a_ssm_p"], st["a_ssm_s"], st["b_k_p"], st["b_k_s"],
            st["b_v_p"], st["b_v_s"], st["b_i_p"], st["b_i_s"], st["c_conv_p"], st["c_conv_s"], st["c_h_p"],
            st["c_h_s"])
```

```python
import functools
import math

import jax
import jax.numpy as jnp
import numpy as np
from jax import lax
from jax.experimental import pallas as pl
from jax.experimental.pallas import tpu as pltpu

F32 = jnp.float32
BF16 = jnp.bfloat16
I32 = jnp.int32

N_MIXERS = 3
N_ADA = 9
CONV_W = 4
NORM_EPS = 1e-6
NEG_INF = -1e30
GDN_CHUNK = 64
TOPK_MAX = 256
Q_BLOCK = 128
ROPE_THETA = 10000.0
RG_C = 8.0
LRU_BLOCK = 256
SAMPLE_PAGES_PER_STEP = 4

LANES = 128
SUBLANES = 8
VMEM_LIMIT_BYTES = 56 * 2**20
INT32_MIN = -2**31
INT32_MAX = 2**31 - 1
KEY_OF_NEG_INF = int(np.float32(NEG_INF).view(np.int32)) ^ 0x7FFFFFFF


def _cparams(*sem):
    return pltpu.CompilerParams(dimension_semantics=sem, vmem_limit_bytes=VMEM_LIMIT_BYTES)


def _tile(n, target, align=LANES):
    if n <= target:
        return n
    t = (target // align) * align
    while t >= align:
        if n % t == 0:
            return t
        t -= align
    raise ValueError(f"no {align}-aligned tile of {n} below {target}")


def _sigmoid(x):
    return jax.nn.sigmoid(x)


def _silu(x):
    return x * _sigmoid(x)


def _dot(a, b):
    return jnp.dot(a, b, preferred_element_type=F32)


def _dot_nt(a, b):
    return lax.dot_general(a, b, (((1,), (1,)), ((), ())), preferred_element_type=F32)


def _dot_tn(a, b):
    return lax.dot_general(a, b, (((0,), (0,)), ((), ())), preferred_element_type=F32)


def _dot_f32(a, b):
    return jnp.dot(a, b, preferred_element_type=F32, precision=lax.Precision.HIGHEST)


def _norm_mod(x, nw, sh, sc):
    ms = jnp.mean(x * x, axis=-1, keepdims=True)
    y = x * lax.rsqrt(ms + NORM_EPS) * nw
    return y * (1.0 + sc) + sh


def _ada_kernel(c_ref, w_ref, b_ref, o_ref):
    a = _silu(c_ref[...]).astype(BF16)
    o_ref[0] = _dot(a, w_ref[0].astype(BF16)) + b_ref[0]


def _ada(c_all, w, b):
    n_l, d, n = w.shape
    mp = c_all.shape[0]
    tn = _tile(n, 1024)
    return pl.pallas_call(
        _ada_kernel,
        grid=(n_l, n // tn),
        in_specs=[
            pl.BlockSpec((mp, d), lambda l, j: (0, 0)),
            pl.BlockSpec((1, d, tn), lambda l, j: (l, 0, j)),
            pl.BlockSpec((1, 1, tn), lambda l, j: (l, 0, j)),
        ],
        out_specs=pl.BlockSpec((1, mp, tn), lambda l, j: (l, 0, j)),
        out_shape=jax.ShapeDtypeStruct((n_l, mp, n), F32),
        name="ada_mod",
        compiler_params=_cparams("parallel", "parallel"),
    )(c_all, w, b.reshape(n_l, 1, n))


def _mod_spec(mod3, k, d, tm, rows_per_group):
    return pl.BlockSpec((None, mod3.shape[1], d), lambda i, j: ((i * tm) // rows_per_group, 0, k))


def _wspec(w, block, imap):
    arr, idx = w
    return pl.BlockSpec((None,) * len(idx) + block, lambda i, j: idx + imap(i, j))


def _wshape(w):
    return w[0].shape[len(w[1]):]


def _ffn_kernel(x_ref, nw_ref, sh_ref, sc_ref, g_ref, wa_ref, wb_ref, wo_ref, o_ref, xn_ref):
    f = pl.program_id(1)

    @pl.when(f == 0)
    def _():
        xn_ref[...] = _norm_mod(x_ref[...], nw_ref[...], sh_ref[...], sc_ref[...]).astype(BF16)
        o_ref[...] = jnp.zeros_like(o_ref)

    xn = xn_ref[...]
    a = _dot(xn, wa_ref[...])
    b = _dot(xn, wb_ref[...])
    o_ref[...] += _dot((_silu(a) * b).astype(BF16), wo_ref[...])

    @pl.when(f == pl.num_programs(1) - 1)
    def _():
        o_ref[...] = x_ref[...] + 0.5 * g_ref[...] * o_ref[...]


def _ffn(x, mod3, k0, nw, w_in, w_out, *, tm, tf_target):
    m, d = x.shape
    f = _wshape(w_out)[0]
    tf = _tile(f, tf_target)
    nf = f // tf
    rpg = m // mod3.shape[0]
    ms = lambda k: _mod_spec(mod3, k, d, tm, rpg)
    return pl.pallas_call(
        _ffn_kernel,
        grid=(m // tm, nf),
        in_specs=[
            pl.BlockSpec((tm, d), lambda i, j: (i, 0)),
            pl.BlockSpec((1, d), lambda i, j: (0, 0)),
            ms(k0), ms(k0 + 1), ms(k0 + 2),
            _wspec(w_in, (d, tf), lambda i, j: (0, j)),
            _wspec(w_in, (d, tf), lambda i, j: (0, j + nf)),
            _wspec(w_out, (tf, d), lambda i, j: (j, 0)),
        ],
        out_specs=pl.BlockSpec((tm, d), lambda i, j: (i, 0)),
        out_shape=jax.ShapeDtypeStruct((m, d), F32),
        scratch_shapes=[pltpu.VMEM((tm, d), BF16)],
        name="ffn",
        compiler_params=_cparams("parallel", "arbitrary"),
    )(x, nw, mod3, mod3, mod3, w_in[0], w_in[0], w_out[0])


def _normproj_kernel(x_ref, nw_ref, sh_ref, sc_ref, w_ref, o_ref, xn_ref):
    @pl.when(pl.program_id(1) == 0)
    def _():
        xn_ref[...] = _norm_mod(x_ref[...], nw_ref[...], sh_ref[...], sc_ref[...]).astype(BF16)

    o_ref[...] = _dot(xn_ref[...], w_ref[...])


def _normproj(x, mod3, k0, nw, w, *, tm, tn_target=1024):
    m, d = x.shape
    n = _wshape(w)[1]
    tn = _tile(n, tn_target)
    rpg = m // mod3.shape[0]
    ms = lambda k: _mod_spec(mod3, k, d, tm, rpg)
    return pl.pallas_call(
        _normproj_kernel,
        grid=(m // tm, n // tn),
        in_specs=[
            pl.BlockSpec((tm, d), lambda i, j: (i, 0)),
            pl.BlockSpec((1, d), lambda i, j: (0, 0)),
            ms(k0), ms(k0 + 1),
            _wspec(w, (d, tn), lambda i, j: (0, j)),
        ],
        out_specs=pl.BlockSpec((tm, tn), lambda i, j: (i, j)),
        out_shape=jax.ShapeDtypeStruct((m, n), F32),
        scratch_shapes=[pltpu.VMEM((tm, d), BF16)],
        name="normproj",
        compiler_params=_cparams("parallel", "arbitrary"),
    )(x, nw, mod3, mod3, w[0])


def _outproj_kernel(a_ref, w_ref, x_ref, g_ref, o_ref):
    o_ref[...] = x_ref[...] + g_ref[...] * _dot(a_ref[...], w_ref[...])


def _outproj(a, w, x, mod3, kg, *, tm, tn_target=1024):
    m, kdim = a.shape
    d = _wshape(w)[1]
    tn = _tile(d, tn_target)
    rpg = m // mod3.shape[0]
    r = mod3.shape[1]
    nd = d // tn
    return pl.pallas_call(
        _outproj_kernel,
        grid=(m // tm, nd),
        in_specs=[
            pl.BlockSpec((tm, kdim), lambda i, j: (i, 0)),
            _wspec(w, (kdim, tn), lambda i, j: (0, j)),
            pl.BlockSpec((tm, tn), lambda i, j: (i, j)),
            pl.BlockSpec((None, r, tn), lambda i, j: ((i * tm) // rpg, 0, kg * nd + j)),
        ],
        out_specs=pl.BlockSpec((tm, tn), lambda i, j: (i, j)),
        out_shape=jax.ShapeDtypeStruct((m, d), F32),
        name="outproj",
        compiler_params=_cparams("parallel", "arbitrary"),
    )(a, w[0], x, mod3)


def _final_kernel(x_ref, nw_ref, sh_ref, sc_ref, o_ref):
    o_ref[...] = _norm_mod(x_ref[...], nw_ref[...], sh_ref[...], sc_ref[...])


def _final_norm(x, mod3, nw, *, tm):
    m, d = x.shape
    rpg = m // mod3.shape[0]
    r = mod3.shape[1]
    ms = lambda k: pl.BlockSpec((None, r, d), lambda i: ((i * tm) // rpg, 0, k))
    return pl.pallas_call(
        _final_kernel,
        grid=(m // tm,),
        in_specs=[pl.BlockSpec((tm, d), lambda i: (i, 0)), pl.BlockSpec((1, d), lambda i: (0, 0)), ms(0), ms(1)],
        out_specs=pl.BlockSpec((tm, d), lambda i: (i, 0)),
        out_shape=jax.ShapeDtypeStruct((m, d), F32),
        name="final_norm",
        compiler_params=_cparams("parallel"),
    )(x, nw, mod3, mod3)


def _causal_conv(x, halo, w):
    def taps(rows, fix):
        acc = rows * w[CONV_W - 1:CONV_W]
        for j in range(1, CONV_W):
            acc = acc + fix(pltpu.roll(rows, j, 0), j) * w[CONV_W - 1 - j:CONV_W - j]
        return acc

    row = lax.broadcasted_iota(I32, halo.shape, 0)
    head = taps(x[:SUBLANES], lambda r, j: jnp.where(row < j, pltpu.roll(halo, j, 0), r))
    if x.shape[0] == SUBLANES:
        return head
    return jnp.concatenate([head, taps(x, lambda r, j: r)[SUBLANES:]], axis=0)


def _conv_specs(tt, tc, coff, boff):
    sub = tt // SUBLANES
    return [
        pl.BlockSpec((None, tt, tc), lambda b, t, c: (b, t, c + coff)),
        pl.BlockSpec((None, SUBLANES, tc), lambda b, t, c: (b, jnp.maximum(t * sub - 1, 0), c + coff)),
        pl.BlockSpec((None, SUBLANES, tc), lambda b, t, c: (b, 0, c + boff)),
    ]


def _pad_buf(buf):
    return jnp.pad(buf, ((0, 0), (SUBLANES - (CONV_W - 1), 0), (0, 0)))


def _gdn_prep_kernel(x_ref, halo_ref, buf_ref, w_ref, o_ref, *, norm):
    halo = jnp.where(pl.program_id(1) == 0, buf_ref[...], halo_ref[...])
    y = _silu(_causal_conv(x_ref[...], halo, w_ref[...]))
    if norm:
        for h in range(y.shape[1] // LANES):
            seg = y[:, h * LANES:(h + 1) * LANES]
            ss = jnp.sum(seg * seg, axis=-1, keepdims=True)
            o_ref[:, h * LANES:(h + 1) * LANES] = seg * lax.rsqrt(ss + NORM_EPS)
    else:
        o_ref[...] = y


def _gdn_prep(proj, buf8, conv_w, *, col0, ncols, norm, tt):
    bsz, t, _ = proj.shape
    tc = _tile(ncols, 512)
    return pl.pallas_call(
        functools.partial(_gdn_prep_kernel, norm=norm),
        grid=(bsz, t // tt, ncols // tc),
        in_specs=_conv_specs(tt, tc, col0 // tc, col0 // tc)
        + [pl.BlockSpec((CONV_W, tc), lambda b, i, c: (0, c + col0 // tc))],
        out_specs=pl.BlockSpec((None, tt, tc), lambda b, i, c: (b, i, c)),
        out_shape=jax.ShapeDtypeStruct((bsz, t, ncols), F32),
        name="gdn_prep",
        compiler_params=_cparams("parallel", "parallel", "parallel"),
    )(proj, proj, buf8, conv_w)


def _gdn_gate_kernel(x_ref, alog_ref, dtb_ref, beta_ref, gc_ref, *, hv, chunk):
    x = x_ref[...]
    tt = x.shape[0]
    beta_ref[...] = _sigmoid(x)
    z = x + dtb_ref[...]
    g = -jnp.exp(alog_ref[...]) * (jnp.maximum(z, 0.0) + jnp.log1p(jnp.exp(-jnp.abs(z))))
    row = lax.broadcasted_iota(I32, (tt, tt), 0)
    col = lax.broadcasted_iota(I32, (tt, tt), 1)
    tri = jnp.where((row >= col) & (row // chunk == col // chunk), 1.0, 0.0)
    gc_ref[...] = _dot_f32(tri, g)


def _gdn_gates(proj, a_log, dt_bias, *, col0, hv, chunk, tt):
    bsz, t, _ = proj.shape
    pad = lambda v: jnp.pad(v.astype(F32), (hv, LANES - 2 * hv)).reshape(1, LANES)
    blk = pl.BlockSpec((None, tt, LANES), lambda b, i: (b, i, col0 // LANES))
    out = pl.BlockSpec((None, tt, LANES), lambda b, i: (b, i, 0))
    par = pl.BlockSpec((1, LANES), lambda b, i: (0, 0))
    return pl.pallas_call(
        functools.partial(_gdn_gate_kernel, hv=hv, chunk=chunk),
        grid=(bsz, t // tt),
        in_specs=[blk, par, par],
        out_specs=[out, out],
        out_shape=[jax.ShapeDtypeStruct((bsz, t, LANES), F32)] * 2,
        name="gdn_gates",
        compiler_params=_cparams("parallel", "parallel"),
    )(proj, pad(a_log), pad(dt_bias))


def _split_bf16(a):
    hi = a.astype(BF16)
    return hi, (a - hi.astype(F32)).astype(BF16)


def _dot_split(a, b):
    ah, al = _split_bf16(a)
    bh, bl = _split_bf16(b)
    return _dot(jnp.concatenate([ah, al, ah], axis=1), jnp.concatenate([bh, bh, bl], axis=0))


def _dot_bf16(a, b):
    return _dot(a.astype(BF16), b.astype(BF16))


def _tri_inv_all(lmats, c, n_real):
    row = lax.broadcasted_iota(I32, (c, c), 0)
    col = lax.broadcasted_iota(I32, (c, c), 1)
    eye = jnp.where(row == col, 1.0, 0.0)
    base = min(16, c)
    ps = [-jnp.where(row // base == col // base, m, 0.0) for m in lmats]
    rs = [eye + p for p in ps]
    n = 2
    while n < base:
        ps = [_dot_bf16(p, p) for p in ps]
        rs = [r + _dot_bf16(r, p) for r, p in zip(rs, ps)]
        n *= 2
    s = base
    while s < min(c, n_real):
        off = (row // (2 * s) == col // (2 * s)) & (row // s != col // s)
        ts = [_dot_bf16(jnp.where(off, m, 0.0), r) for m, r in zip(lmats, rs)]
        rs = [r - _dot_bf16(r, t) for r, t in zip(rs, ts)]
        s *= 2
    res = [eye - r - _dot_split(m, r) for m, r in zip(lmats, rs)]
    return [r + _dot_bf16(r, e) for r, e in zip(rs, res)]


def _gdn_core_kernel(q_ref, k_ref, v_ref, z_ref, gcc_ref, gcr_ref, bc_ref, s0_ref, nw_ref, o_ref, so_ref, s_ref,
                     *, rep, c, hpb, dk, n_real):
    ci = pl.program_id(2)

    @pl.when(ci == 0)
    def _():
        s_ref[...] = s0_ref[...]

    row = lax.broadcasted_iota(I32, (c, c), 0)
    col = lax.broadcasted_iota(I32, (c, c), 1)
    causal = row >= col
    heads = [(hh, r) for hh in range(hpb) for r in range(rep)]
    lanes = [slice((hh * rep + r) * LANES, (hh * rep + r + 1) * LANES) for hh, r in heads]
    qs = [q_ref[:, hh * LANES:(hh + 1) * LANES] * dk ** -0.5 for hh in range(hpb)]
    ks = [k_ref[:, hh * LANES:(hh + 1) * LANES] for hh in range(hpb)]
    k16 = [k.astype(BF16) for k in ks]
    grams = [_dot_nt(kb, kb) for kb in k16]
    qk0s = [_dot_nt(q.astype(BF16), kb) for q, kb in zip(qs, k16)]
    gccs = [gcc_ref[hh, :, r:r + 1] for hh, r in heads]
    gcrs = [gcr_ref[hh, r:r + 1, :] for hh, r in heads]
    betas = [bc_ref[hh, :, r:r + 1] for hh, r in heads]
    decays = [jnp.where(causal, jnp.exp(jnp.where(causal, gc - gr, 0.0)), 0.0) for gc, gr in zip(gccs, gcrs)]
    lowers = [jnp.where(row > col, grams[hh] * b * d, 0.0) for (hh, _), b, d in zip(heads, betas, decays)]
    tinvs = _tri_inv_all(lowers, c, n_real)
    egcs = [jnp.exp(gc) for gc in gccs]
    sols = [_dot_split(ti, jnp.concatenate([v_ref[:, ln] * b, ks[hh] * (b * e)], axis=1))
            for ti, ln, b, e, (hh, _) in zip(tinvs, lanes, betas, egcs, heads)]
    qg16 = [(qs[hh] * e).astype(BF16) for (hh, _), e in zip(heads, egcs)]
    qk16 = [(qk0s[hh] * d).astype(BF16) for (hh, _), d in zip(heads, decays)]
    g_last = [gc[c - 1:c, :] for gc in gccs]
    kd16 = [(ks[hh] * jnp.exp(gl - gc)).astype(BF16) for (hh, _), gl, gc in zip(heads, g_last, gccs)]
    ss = [s_ref[i] for i in range(len(heads))]
    s16 = [s.astype(BF16) for s in ss]
    us = [sol[:, :LANES] - _dot(sol[:, LANES:].astype(BF16), sb) for sol, sb in zip(sols, s16)]
    u16 = [u.astype(BF16) for u in us]
    for i in range(len(heads)):
        s_ref[i] = ss[i] * jnp.exp(g_last[i]) + _dot_tn(kd16[i], u16[i])
    os_ = [_dot(a, sb) + _dot(b, ub) for a, sb, b, ub in zip(qg16, s16, qk16, u16)]
    for o, ln in zip(os_, lanes):
        on = o * lax.rsqrt(jnp.mean(o * o, axis=-1, keepdims=True) + NORM_EPS) * nw_ref[...]
        o_ref[:, ln] = (on * _silu(z_ref[:, ln])).astype(BF16)

    @pl.when(ci == pl.num_programs(2) - 1)
    def _():
        so_ref[...] = s_ref[...]


def _gdn_core(qk, v, proj, zcol0, gc, beta, s0, norm_w, *, c, hpb, n_real):
    bsz, t, val = v.shape
    hv = s0.shape[1]
    dk, dv = s0.shape[2], s0.shape[3]
    hk = qk.shape[2] // (2 * dk)
    rep = hv // hk
    assert dk == LANES and dv == LANES and t % c == 0 and hk % hpb == 0 and zcol0 % (hpb * rep * dv) == 0
    heads = lambda a, lo: a[:, :, lo:lo + hv].reshape(bsz, t, hk, rep).transpose(0, 2, 1, 3)
    gcc = heads(gc, hv)
    bcc = heads(beta, 0)
    gcr = gcc.transpose(0, 1, 3, 2)
    nhb = hk // hpb
    zb = zcol0 // (hpb * rep * dv)
    colspec = pl.BlockSpec((None, hpb, c, rep), lambda b, h, i: (b, h, i, 0))
    o, s_out = pl.pallas_call(
        functools.partial(_gdn_core_kernel, rep=rep, c=c, hpb=hpb, dk=dk, n_real=n_real),
        grid=(bsz, nhb, t // c),
        in_specs=[
            pl.BlockSpec((None, c, hpb * dk), lambda b, h, i: (b, i, h)),
            pl.BlockSpec((None, c, hpb * dk), lambda b, h, i: (b, i, nhb + h)),
            pl.BlockSpec((None, c, hpb * rep * dv), lambda b, h, i: (b, i, h)),
            pl.BlockSpec((None, c, hpb * rep * dv), lambda b, h, i: (b, i, zb + h)),
            colspec,
            pl.BlockSpec((None, hpb, rep, c), lambda b, h, i: (b, h, 0, i)),
            colspec,
            pl.BlockSpec((None, hpb * rep, dk, dv), lambda b, h, i: (b, h, 0, 0)),
            pl.BlockSpec((1, dv), lambda b, h, i: (0, 0)),
        ],
        out_specs=[
            pl.BlockSpec((None, c, hpb * rep * dv), lambda b, h, i: (b, i, h)),
            pl.BlockSpec((None, hpb * rep, dk, dv), lambda b, h, i: (b, h, 0, 0)),
        ],
        out_shape=[jax.ShapeDtypeStruct((bsz, t, val), BF16), jax.ShapeDtypeStruct(s0.shape, F32)],
        scratch_shapes=[pltpu.VMEM((hpb * rep, dk, dv), F32)],
        name="gdn_core",
        compiler_params=_cparams("parallel", "parallel", "arbitrary"),
    )(qk, qk, v, proj, gcc, gcr, bcc, s0, norm_w.reshape(1, dv))
    return o, s_out


def _gdn_mixer(x, mod3, nw, buf, s0, w_in, conv_w, a_log, dt_bias, norm_w, w_out, *, tm, tm_in, tt, chunk):
    bsz, t, d = x.shape
    hv, dk, dv = s0.shape[1], s0.shape[2], s0.shape[3]
    val = hv * dv
    conv_dim = conv_w.shape[1]
    key = (conv_dim - val) // 2
    assert (conv_dim + val) % LANES == 0 and 2 * hv <= LANES and t >= CONV_W - 1
    x2 = x.reshape(bsz * t, d)
    proj = _normproj(x2, mod3, 3, nw, w_in, tm=tm_in, tn_target=512).reshape(bsz, t, -1)
    buf8 = _pad_buf(buf)
    qk = _gdn_prep(proj, buf8, conv_w, col0=0, ncols=2 * key, norm=True, tt=tt)
    v = _gdn_prep(proj, buf8, conv_w, col0=2 * key, ncols=val, norm=False, tt=tt)
    tp = -(-t // chunk) * chunk
    beta, gc = _gdn_gates(proj, a_log, dt_bias, col0=conv_dim + val, hv=hv, chunk=min(chunk, tt), tt=tt)
    if tp != t:
        padt = lambda a: jnp.pad(a, ((0, 0), (0, tp - t), (0, 0)))
        gc = jnp.concatenate([gc, jnp.broadcast_to(gc[:, -1:], (bsz, tp - t, LANES))], axis=1)
        qk, v, beta, projz = padt(qk), padt(v), padt(beta), padt(proj)
    else:
        projz = proj
    o, s_new = _gdn_core(qk, v, projz, conv_dim, gc, beta, s0, norm_w, c=chunk, hpb=min(4, key // dk),
                         n_real=min(t, chunk))
    o2 = o[:, :t].reshape(bsz * t, val)
    xo = _outproj(o2, w_out, x2, mod3, 5, tm=tm).reshape(bsz, t, d)
    new_buf = proj[:, t - (CONV_W - 1):, :conv_dim]
    return xo, new_buf, s_new


def _rope_tables(pos, half):
    inv_freq = ROPE_THETA ** (-jnp.arange(half, dtype=F32) / half)
    ang = pos.astype(F32)[:, None] * inv_freq[None, :]
    cos, sin = jnp.cos(ang), jnp.sin(ang)
    return jnp.concatenate([cos, cos], axis=-1), jnp.concatenate([-sin, sin], axis=-1)


def _dsa_prep_kernel(x_ref, cos_ref, sin_ref, q_ref, k_ref, k16_ref, v16_ref, qi_ref, ki_ref, ki16_ref, *tr_refs,
                     nh, nkv, nih):
    cos, sin = cos_ref[...], sin_ref[...]

    def rope(col):
        seg = x_ref[:, col * LANES:(col + 1) * LANES]
        return seg * cos + pltpu.roll(seg, LANES // 2, 1) * sin

    for h in range(nh):
        q_ref[:, h * LANES:(h + 1) * LANES] = (rope(h) * LANES ** -0.5).astype(BF16)
    for h in range(nkv):
        kr = rope(nh + h)
        k_ref[:, h * LANES:(h + 1) * LANES] = kr
        k16_ref[:, h * LANES:(h + 1) * LANES] = kr.astype(BF16)
    v0 = (nh + nkv) * LANES
    v16_ref[...] = x_ref[:, v0:v0 + nkv * LANES].astype(BF16)
    c0 = nh + 2 * nkv
    for h in range(nih):
        qi_ref[:, h * LANES:(h + 1) * LANES] = rope(c0 + h).astype(BF16)
    kir = rope(c0 + nih)
    ki_ref[...] = kir
    ki16_ref[...] = kir.astype(BF16)
    if tr_refs:
        wt_ref, vt_ref = tr_refs
        w0 = (c0 + nih + 1) * LANES
        wt_ref[...] = x_ref[:, w0:w0 + LANES].T[:wt_ref.shape[0], :]
        for h in range(nkv):
            vt_ref[h * LANES:(h + 1) * LANES, :] = x_ref[:, v0 + h * LANES:v0 + (h + 1) * LANES].T.astype(BF16)


def _dsa_prep(proj, pos, *, nh, nkv, nih, tt, with_wt):
    bsz, t, npj = proj.shape
    cos, sin = _rope_tables(pos, LANES // 2)
    row = lambda n, dt: jax.ShapeDtypeStruct((bsz, t, n * LANES), dt)
    ospec = lambda n: pl.BlockSpec((None, tt, n * LANES), lambda b, i: (b, i, 0))
    tab = pl.BlockSpec((tt, LANES), lambda b, i: (i, 0))
    nwt = -(-nih // SUBLANES) * SUBLANES
    return pl.pallas_call(
        functools.partial(_dsa_prep_kernel, nh=nh, nkv=nkv, nih=nih),
        grid=(bsz, t // tt),
        in_specs=[pl.BlockSpec((None, tt, npj), lambda b, i: (b, i, 0)), tab, tab],
        out_specs=[ospec(nh), ospec(nkv), ospec(nkv), ospec(nkv), ospec(nih), ospec(1), ospec(1)]
        + ([pl.BlockSpec((None, nwt, tt), lambda b, i: (b, 0, i)),
            pl.BlockSpec((None, None, nkv * LANES, tt), lambda b, i: (b, i, 0, 0))] if with_wt else []),
        out_shape=[row(nh, BF16), row(nkv, F32), row(nkv, BF16), row(nkv, BF16), row(nih, BF16), row(1, F32),
                   row(1, BF16)]
        + ([jax.ShapeDtypeStruct((bsz, nwt, t), F32),
            jax.ShapeDtypeStruct((bsz, t // tt, nkv * LANES, tt), BF16)] if with_wt else []),
        name="dsa_prep",
        compiler_params=_cparams("parallel", "parallel"),
    )(proj, cos, sin)


def _sort_key(s):
    bits = pltpu.bitcast(jnp.where(s == 0.0, 0.0, s), I32)
    return jnp.where(bits < 0, bits ^ 0x7FFFFFFF, bits)


def _topk_cut(count_where, shape, topk, idx_bits):
    def body(i, carry):
        t, n_t = carry
        cand = t + lax.shift_left(jnp.int32(1), 31 - i)
        n = count_where(lambda key, idx: key >= cand)
        ok = n >= topk
        return jnp.where(ok, cand, t), jnp.where(ok, n, n_t)

    thr, n_ge = lax.fori_loop(0, 32, body, (jnp.full(shape, INT32_MIN, I32), jnp.full(shape, INT32_MAX, I32)))
    tie = (n_ge > topk) & (thr > KEY_OF_NEG_INF)

    def cut():
        need = topk - count_where(lambda key, idx: key > thr)

        def jbody(i, j):
            cand = j + lax.shift_left(jnp.int32(1), idx_bits - 1 - i)
            below = count_where(lambda key, idx: (key == thr) & (idx < cand))
            return jnp.where(below < need, cand, j)

        return jnp.where(tie, lax.fori_loop(0, idx_bits, jbody, jnp.zeros(shape, I32)), INT32_MAX)

    jcut = lax.cond(jnp.any(tie), cut, lambda: jnp.full(shape, INT32_MAX, I32))
    return thr, jcut


def _topk_chosen(key, idx, thr, jcut):
    return (key > thr) | ((key == thr) & (idx <= jcut))


def _dsa_prompt_kernel(qi_ref, wt_ref, q_ref, ki_ref, k_ref, vt_ref, o_ref, key_ref, bias_ref, acc_ref,
                       *, nh, nkv, nih, topk, idx_scale):
    qb = pl.program_id(1)
    blk = Q_BLOCK
    kb = vt_ref.shape[2]
    ktiles = kb // blk
    nkb = (qb + ktiles) // ktiles
    rowk = lax.broadcasted_iota(I32, (kb, blk), 0)
    colq = lax.broadcasted_iota(I32, (kb, blk), 1)
    wt = wt_ref[...]
    hq = max(1, nih // 4)
    qis = [jnp.concatenate([qi_ref[:, h * LANES:(h + 1) * LANES] for h in range(h0, min(h0 + hq, nih))], axis=0)
           for h0 in range(0, nih, hq)]

    def rows_of(j):
        return pl.ds(pl.multiple_of(j * kb, kb), kb)

    def visible(j):
        return j * kb + rowk <= qb * blk + colq

    def score_body(j, carry):
        keys = ki_ref[rows_of(j), :]
        lgs = [_dot_nt(keys, qi) for qi in qis]
        acc = jnp.zeros((kb, blk), F32)
        for i, lg in enumerate(lgs):
            for hh in range(lg.shape[1] // blk):
                h = i * hq + hh
                acc = acc + jnp.maximum(lg[:, hh * blk:(hh + 1) * blk], 0.0) * wt[h:h + 1, :]
        key_ref[rows_of(j), :] = _sort_key(jnp.where(visible(j), acc * idx_scale, NEG_INF))
        return carry

    lax.fori_loop(0, nkb, score_body, 0)

    def count_where(pred):
        def body(j, c):
            hit = jnp.where(pred(key_ref[rows_of(j), :], j * kb + rowk), 1, 0)
            for i in range(ktiles):
                c = c + hit[i * blk:(i + 1) * blk]
            return c
        cnt = lax.fori_loop(0, nkb, body, jnp.zeros((blk, blk), I32))
        return jnp.sum(cnt, axis=0, keepdims=True)

    thr, jcut = _topk_cut(count_where, (1, blk), topk, (key_ref.shape[0] - 1).bit_length())

    def bias_body(j, carry):
        sel = _topk_chosen(key_ref[rows_of(j), :], j * kb + rowk, thr, jcut) & visible(j)
        bias_ref[rows_of(j), :] = jnp.where(sel, 0.0, NEG_INF)
        return carry

    lax.fori_loop(0, nkb, bias_body, 0)

    rep = nh // nkv
    qgs =[jnp.concatenate([q_ref[:, (g * rep + r) * LANES:(g * rep + r + 1) * LANES] for r in range(rep)], axis=0)
           for g in range(nkv)]
    acc_ref[...] = jnp.zeros_like(acc_ref)

    def att_body(j, carry):
        ms, ls = carry
        rows = rows_of(j)
        bias = jnp.concatenate([bias_ref[rows, :]] * rep, axis=1)
        ss = [_dot_nt(k_ref[rows, g * LANES:(g + 1) * LANES], qgs[g]) + bias for g in range(nkv)]
        m_new = [jnp.maximum(m, jnp.max(s, axis=0, keepdims=True)) for m, s in zip(ms, ss)]
        ps = [jnp.exp(s - m) for s, m in zip(ss, m_new)]
        pvs = [_dot(vt_ref[j, g * LANES:(g + 1) * LANES, :], p.astype(BF16)) for g, p in enumerate(ps)]
        alphas = [jnp.exp(m - mn) for m, mn in zip(ms, m_new)]
        for g in range(nkv):
            acc_ref[g] = alphas[g] * acc_ref[g] + pvs[g]
        l_new = [a * l + jnp.sum(p, axis=0, keepdims=True) for a, l, p in zip(alphas, ls, ps)]
        return tuple(m_new), tuple(l_new)

    row0 = lambda v: tuple(jnp.full((1, rep * blk), v, F32) for _ in range(nkv))
    _, ls = lax.fori_loop(0, nkb, att_body, (row0(NEG_INF), row0(0.0)))
    for g in range(nkv):
        og = acc_ref[g] / ls[g]
        for r in range(rep):
            h = g * rep + r
            o_ref[:, h * LANES:(h + 1) * LANES] = og[:, r * blk:(r + 1) * blk].T.astype(BF16)


def _dsa_prompt_attend(q16, qi16, wt, ki16, k16, vt16, *, nh, nkv, nih, topk):
    bsz, t, _ = q16.shape
    rep = nh // nkv
    full = lambda n: pl.BlockSpec((None, t, n * LANES), lambda b, i: (b, 0, 0))
    blk = lambda n: pl.BlockSpec((None, Q_BLOCK, n * LANES), lambda b, i: (b, i, 0))
    return pl.pallas_call(
        functools.partial(_dsa_prompt_kernel, nh=nh, nkv=nkv, nih=nih, topk=topk,
                          idx_scale=(LANES * nih) ** -0.5),
        grid=(bsz, t // Q_BLOCK),
        in_specs=[blk(nih), pl.BlockSpec((None, wt.shape[1], Q_BLOCK), lambda b, i: (b, 0, i)), blk(nh),
                  full(1), full(nkv), pl.BlockSpec((None,) + vt16.shape[1:], lambda b, i: (b, 0, 0, 0))],
        out_specs=blk(nh),
        out_shape=jax.ShapeDtypeStruct((bsz, t, nh * LANES), BF16),
        scratch_shapes=[pltpu.VMEM((t, Q_BLOCK), I32), pltpu.VMEM((t, Q_BLOCK), F32),
                        pltpu.VMEM((nkv, LANES, rep * Q_BLOCK), F32)],
        name="dsa_prompt_attend",
        compiler_params=_cparams("parallel", "arbitrary"),
    )(qi16, wt, q16, ki16, k16, vt16)


def _dsa_sample_score_kernel(pt_ref, qi_ref, wc_ref, *refs, n_steps, pps, nih, tq, past, idx_scale):
    page_refs, new_ref, o_ref = refs[:pps], refs[pps], refs[pps + 1]
    p = pl.program_id(1)
    keys = jnp.concatenate([r[...] for r in page_refs], axis=0).astype(BF16)
    keys = jnp.where(p == n_steps - 1, new_ref[...], keys)
    qi = jnp.concatenate([qi_ref[:, h * LANES:(h + 1) * LANES] for h in range(nih)], axis=0)
    w = jnp.maximum(_dot_nt(qi, keys), 0.0) * wc_ref[...]
    acc = w[0:tq]
    for h in range(1, nih):
        acc = acc + w[h * tq:(h + 1) * tq]
    s = p * keys.shape[0] + lax.broadcasted_iota(I32, acc.shape, 1)
    qpos = past + lax.broadcasted_iota(I32, acc.shape, 0)
    o_ref[...] = jnp.where(s <= qpos, acc * idx_scale, NEG_INF)


def _dsa_sample_select_kernel(s_ref, o_ref, *, topk):
    key = _sort_key(s_ref[...])
    idx = lax.broadcasted_iota(I32, key.shape, 1)
    count_where = lambda pred: jnp.sum(jnp.where(pred(key, idx), 1, 0), axis=-1, keepdims=True)
    thr, jcut = _topk_cut(count_where, (key.shape[0], 1), topk, (key.shape[1] - 1).bit_length())
    o_ref[...] = jnp.where(_topk_chosen(key, idx, thr, jcut) & (s_ref[...] > 0.5 * NEG_INF), 1.0, 0.0)


def _dsa_sample_attn_kernel(pt_ref, q_ref, sel_ref, *refs, n_steps, pps, nh, nkv, tq):
    kp_refs, vp_refs = refs[:pps], refs[pps:2 * pps]
    kn_ref, vn_ref, o_ref, m_ref, l_ref, acc_ref = refs[2 * pps:]
    p = pl.program_id(1)

    @pl.when(p == 0)
    def _():
        m_ref[...] = jnp.full_like(m_ref, NEG_INF)
        l_ref[...] = jnp.zeros_like(l_ref)
        acc_ref[...] = jnp.zeros_like(acc_ref)

    last = p == n_steps - 1
    kcat = jnp.where(last, kn_ref[...], jnp.concatenate([r[...] for r in kp_refs], axis=0).astype(BF16))
    vcat = jnp.where(last, vn_ref[...], jnp.concatenate([r[...] for r in vp_refs], axis=0).astype(BF16))
    rep = nh // nkv
    q = jnp.concatenate([q_ref[:, h * LANES:(h + 1) * LANES] for h in range(nh)], axis=0)
    s = _dot_nt(q, kcat)
    page = sel_ref.shape[1] // pps
    expand = (lax.broadcasted_iota(I32, (page, page * nkv), 1) // nkv
              == lax.broadcasted_iota(I32, (page, page * nkv), 0)).astype(BF16)
    sel = sel_ref[...].astype(BF16)
    selx = jnp.concatenate([_dot(sel[:, i * page:(i + 1) * page], expand) for i in range(pps)], axis=1)
    selx = jnp.concatenate([selx] * nh, axis=0)
    head_ok = (lax.broadcasted_iota(I32, s.shape, 0) // (tq * rep) == lax.broadcasted_iota(I32, s.shape, 1) % nkv)
    s = jnp.where((selx > 0.5) & head_ok, s, NEG_INF)
    m = m_ref[...]
    m_new = jnp.maximum(m, jnp.max(s, axis=-1, keepdims=True))
    alpha = jnp.exp(m - m_new)
    pr = jnp.where(s > 0.5 * NEG_INF, jnp.exp(s - m_new), 0.0)
    l_ref[...] = alpha * l_ref[...] + jnp.sum(pr, axis=-1, keepdims=True)
    acc_ref[...] = alpha * acc_ref[...] + _dot(pr.astype(BF16), vcat)
    m_ref[...] = m_new

    @pl.when(last)
    def _():
        og = acc_ref[...] / l_ref[...]
        for h in range(nh):
            o_ref[:, h * LANES:(h + 1) * LANES] = og[h * tq:(h + 1) * tq].astype(BF16)


def _dsa_sample_attend(q16, qi16, wi, ki16, k16, v16, cache_k, cache_v, cache_i, layer, page_table,
                       *, nh, nkv, nih, topk):
    bsz, tq, _ = q16.shape
    n_pages = page_table.shape[1]
    n_layers, n_pool, page = cache_i.shape[:3]
    pps = math.gcd(n_pages, SAMPLE_PAGES_PER_STEP)
    assert page == LANES and tq == SUBLANES
    past = n_pages * page
    n_steps = n_pages // pps + 1
    ltot = n_steps * pps * page
    wcol = wi.transpose(0, 2, 1).reshape(bsz, nih * tq, 1)
    pidx = lambda i: (lambda b, p, pt: (layer * n_pool + pt[b, jnp.minimum(p * pps + i, n_pages - 1)], 0, 0))
    qspec = lambda n: pl.BlockSpec((None, tq, n * LANES), lambda b, p, pt: (b, 0, 0))
    idx_scale = (LANES * nih) ** -0.5
    cache_i = cache_i.reshape(n_layers * n_pool, page, LANES)
    ki_new = jnp.pad(ki16, ((0, 0), (0, pps * page - tq), (0, 0)))
    scores = pl.pallas_call(
        functools.partial(_dsa_sample_score_kernel, n_steps=n_steps, pps=pps, nih=nih, tq=tq, past=past,
                          idx_scale=idx_scale),
        grid_spec=pltpu.PrefetchScalarGridSpec(
            num_scalar_prefetch=1, grid=(bsz, n_steps),
            in_specs=[qspec(nih), pl.BlockSpec((None, nih * tq, 1), lambda b, p, pt: (b, 0, 0))]
            + [pl.BlockSpec((None, page, LANES), pidx(i)) for i in range(pps)]
            + [pl.BlockSpec((None, pps * page, LANES), lambda b, p, pt: (b, 0, 0))],
            out_specs=pl.BlockSpec((None, tq, pps * page), lambda b, p, pt: (b, 0, p))),
        out_shape=jax.ShapeDtypeStruct((bsz, tq, ltot), F32),
        name="dsa_sample_scores",
        compiler_params=_cparams("parallel", "arbitrary"),
    )(page_table, qi16, wcol, *([cache_i] * pps), ki_new)
    sel = pl.pallas_call(
        functools.partial(_dsa_sample_select_kernel, topk=topk),
        grid=(bsz,),
        in_specs=[pl.BlockSpec((None, tq, ltot), lambda b: (b, 0, 0))],
        out_specs=pl.BlockSpec((None, tq, ltot), lambda b: (b, 0, 0)),
        out_shape=jax.ShapeDtypeStruct((bsz, tq, ltot), F32),
        name="dsa_sample_select",
        compiler_params=_cparams("parallel"),
    )(scores)
    prow = page * nkv
    ck = cache_k.reshape(n_layers * n_pool, prow, LANES)
    cv = cache_v.reshape(n_layers * n_pool, prow, LANES)
    new_rows = lambda a: jnp.pad(a.reshape(bsz, tq * nkv, LANES), ((0, 0), (0, pps * prow - tq * nkv), (0, 0)))
    kvspecs = [pl.BlockSpec((None, prow, LANES), pidx(i)) for i in range(pps)]
    newspec = pl.BlockSpec((None, pps * prow, LANES), lambda b, p, pt: (b, 0, 0))
    return pl.pallas_call(
        functools.partial(_dsa_sample_attn_kernel, n_steps=n_steps, pps=pps, nh=nh, nkv=nkv, tq=tq),
        grid_spec=pltpu.PrefetchScalarGridSpec(
            num_scalar_prefetch=1, grid=(bsz, n_steps),
            in_specs=[qspec(nh), pl.BlockSpec((None, tq, pps * page), lambda b, p, pt: (b, 0, p))]
            + kvspecs + kvspecs + [newspec, newspec],
            out_specs=qspec(nh),
            scratch_shapes=[pltpu.VMEM((nh * tq, 1), F32), pltpu.VMEM((nh * tq, 1), F32),
                            pltpu.VMEM((nh * tq, LANES), F32)]),
        out_shape=jax.ShapeDtypeStruct((bsz, tq, nh * LANES), BF16),
        name="dsa_sample_attend",
        compiler_params=_cparams("parallel", "arbitrary"),
    )(page_table, q16, sel, *([ck] * pps), *([cv] * pps), new_rows(k16), new_rows(v16))


def _dsa_mixer(x, mod3, nw, w_in, w_out, n_proj, *, nkv, tm, tm_in, tt, cache=None):
    bsz, t, d = x.shape
    nh = _wshape(w_out)[0] // LANES
    nih = (n_proj - (nh + 2 * nkv + 1) * LANES) // (LANES + 1)
    assert (nh + 2 * nkv + nih + 1) * LANES + nih == n_proj and nih <= LANES
    x2 = x.reshape(bsz * t, d)
    proj = _normproj(x2, mod3, 3, nw, w_in, tm=tm_in).reshape(bsz, t, -1)
    v = proj[:, :, (nh + nkv) * LANES:(nh + 2 * nkv) * LANES]
    if cache is None:
        assert t % Q_BLOCK == 0
        q16, k, k16, _, qi16, ki, ki16, wt, vt16 = _dsa_prep(proj, jnp.arange(t), nh=nh, nkv=nkv, nih=nih,
                                                             tt=_tile(t, 4 * Q_BLOCK, Q_BLOCK), with_wt=True)
        o = _dsa_prompt_attend(q16, qi16, wt, ki16, k16, vt16, nh=nh, nkv=nkv, nih=nih, topk=min(TOPK_MAX, t // 4))
    else:
        cache_k, cache_v, cache_i, layer, page_table = cache
        past = page_table.shape[1] * cache_i.shape[2]
        q16, k, k16, v16, qi16, ki, ki16 = _dsa_prep(proj, past + jnp.arange(t), nh=nh, nkv=nkv, nih=nih, tt=tt,
                                                    with_wt=False)
        w0 = (nh + 2 * nkv + nih + 1) * LANES
        o = _dsa_sample_attend(q16, qi16, proj[:, :, w0:w0 + nih], ki16, k16, v16, cache_k, cache_v, cache_i,
                               layer, page_table, nh=nh, nkv=nkv, nih=nih, topk=min(TOPK_MAX, (past + t) // 4))
    xo = _outproj(o.reshape(bsz * t, nh * LANES), w_out, x2, mod3, 5, tm=tm).reshape(bsz, t, d)
    return xo, k.reshape(bsz, t, nkv, LANES), v.reshape(bsz, t, nkv, LANES), ki


def _gelu_tanh(x):
    return 0.5 * x * (1.0 + jnp.tanh(math.sqrt(2.0 / math.pi) * (x + 0.044715 * (x * x * x))))


def _lru_kernel(gate_ref, xb_ref, halo_ref, buf_ref, cw_ref, cb_ref, wa_ref, ba_ref, wx_ref, bx_ref, lam_ref,
                h0_ref, y_ref, hl_ref, h_ref, *, nblk):
    ti = pl.program_id(1)

    @pl.when(ti == 0)
    def _():
        h_ref[...] = h0_ref[...]

    halo = jnp.where(ti == 0, buf_ref[...], halo_ref[...])
    xc = _causal_conv(xb_ref[...], halo, cw_ref[...]) + cb_ref[...]
    tt = xc.shape[0]
    xc16 = xc.astype(BF16)
    rs, xs = [], []
    for n in range(nblk):
        blk = xc16[:, n * LRU_BLOCK:(n + 1) * LRU_BLOCK]
        rs.append(_dot(blk, wa_ref[n]))
        xs.append(_dot(blk, wx_ref[n]))
    r = _sigmoid(jnp.concatenate(rs, axis=1) + ba_ref[...])
    ig = _sigmoid(jnp.concatenate(xs, axis=1) + bx_ref[...])
    lam = lam_ref[...]
    softplus_neg = jnp.maximum(-lam, 0.0) + jnp.log1p(jnp.exp(-jnp.abs(lam)))
    log_a = -RG_C * r * softplus_neg
    a = jnp.exp(log_a)
    b = jnp.sqrt(-jnp.tanh(log_a) * (a * a + 1.0)) * (ig * xc)
    row = lax.broadcasted_iota(I32, a.shape, 0)
    d = 1
    while d < tt:
        keep = row >= d
        a_sh = jnp.where(keep, pltpu.roll(a, d, 0), 1.0)
        b_sh = jnp.where(keep, pltpu.roll(b, d, 0), 0.0)
        b = a * b_sh + b
        a = a * a_sh
        d *= 2
    hs = b + a * h_ref[...]
    h_ref[...] = hs[tt - 1:tt, :]
    y_ref[...] = (hs * _gelu_tanh(gate_ref[...])).astype(BF16)

    @pl.when(ti == pl.num_programs(1) - 1)
    def _():
        hl_ref[...] = hs[tt - 1:tt, :]


def _lru_core(proj, buf8, h0, conv_w, conv_b, w_ga, b_ga, w_gx, b_gx, lam, *, tt):
    bsz, t, w2 = proj.shape
    w = w2 // 2
    nblk = w // LRU_BLOCK
    sub = tt // SUBLANES
    vec = lambda: pl.BlockSpec((1, w), lambda b, i: (0, 0))
    wsp = lambda: pl.BlockSpec((nblk, LRU_BLOCK, LRU_BLOCK), lambda b, i: (0, 0, 0))
    r1 = lambda v: v.reshape(1, w).astype(F32)
    y, hl = pl.pallas_call(
        functools.partial(_lru_kernel, nblk=nblk),
        grid=(bsz, t // tt),
        in_specs=[
            pl.BlockSpec((None, tt, w), lambda b, i: (b, i, 0)),
            pl.BlockSpec((None, tt, w), lambda b, i: (b, i, 1)),
            pl.BlockSpec((None, SUBLANES, w), lambda b, i: (b, jnp.maximum(i * sub - 1, 0), 1)),
            pl.BlockSpec((None, SUBLANES, w), lambda b, i: (b, 0, 0)),
            pl.BlockSpec((CONV_W, w), lambda b, i: (0, 0)),
            vec(), wsp(), vec(), wsp(), vec(), vec(),
            pl.BlockSpec((None, 1, w), lambda b, i: (b, 0, 0)),
        ],
        out_specs=[pl.BlockSpec((None, tt, w), lambda b, i: (b, i, 0)),
                   pl.BlockSpec((None, 1, w), lambda b, i: (b, 0, 0))],
        out_shape=[jax.ShapeDtypeStruct((bsz, t, w), BF16), jax.ShapeDtypeStruct((bsz, 1, w), F32)],
        scratch_shapes=[pltpu.VMEM((1, w), F32)],
        name="lru_core",
        compiler_params=_cparams("parallel", "arbitrary"),
    )(proj, proj, proj, buf8, conv_w, r1(conv_b), w_ga, r1(b_ga), w_gx, r1(b_gx), r1(lam), h0.reshape(bsz, 1, w))
    return y, hl.reshape(bsz, w)


def _lru_mixer(x, mod3, nw, buf, h0, w_in, conv_w, conv_b, w_ga, b_ga, w_gx, b_gx, lam, w_out, *, tm, tm_in, tt):
    bsz, t, d = x.shape
    assert t >= CONV_W - 1
    x2 = x.reshape(bsz * t, d)
    proj = _normproj(x2, mod3, 3, nw, w_in, tm=tm_in).reshape(bsz, t, -1)
    w = proj.shape[2] // 2
    y, hl = _lru_core(proj, _pad_buf(buf), h0, conv_w, conv_b, w_ga, b_ga, w_gx, b_gx, lam, tt=tt)
    xo = _outproj(y.reshape(bsz * t, w), w_out, x2, mod3, 5, tm=tm).reshape(bsz, t, d)
    return xo, proj[:, t - (CONV_W - 1):, w:], hl


def _bf16_padded(w, mult):
    n = w.shape[-1]
    return jnp.pad(w, ((0, 0), (0, 0), (0, -n % mult))).astype(BF16)


def kernel(x_prompt, x_sample, state_a_conv, state_a_ssm, cache_b_k, cache_b_v, cache_b_idx, state_c_conv, state_c_h, page_table, c_prompt, c_sample, w_ada, b_ada, norm_w, ffn_w_in, ffn_w_out, gdn_w_in, gdn_conv_w, gdn_a_log, gdn_dt_bias, gdn_norm_w, gdn_w_out, dsa_w_in, dsa_w_out, lru_w_in, lru_conv_w, lru_conv_b, lru_w_gate_a, lru_b_gate_a, lru_w_gate_x, lru_b_gate_x, lru_lambda, lru_w_out, w_ada_final, b_ada_final, final_norm_w):
    bp, seq, d = x_prompt.shape
    bs, ts, _ = x_sample.shape
    depth = w_ada.shape[0]
    nkv = cache_b_k.shape[3]
    tm_p = _tile(seq, 512, SUBLANES)
    tm_in_p = _tile(seq, 1024, SUBLANES)
    tt_p = _tile(seq, 256, SUBLANES)
    tm_s = bs * ts
    tiles_p = dict(tm=tm_p, tm_in=tm_in_p, tt=tt_p)
    tiles_s = dict(tm=tm_s, tm_in=tm_s, tt=ts)

    c_all = jnp.concatenate([c_prompt, c_sample], axis=0)
    c_all = jnp.pad(c_all, ((0, -c_all.shape[0] % SUBLANES), (0, 0)))
    mod = _ada(c_all, w_ada, b_ada)
    mod_f = _ada(c_all, w_ada_final[None], b_ada_final[None])[0]

    def groups(m):
        return m[:bp, None, :], jnp.repeat(m[bp:bp + bs], ts, axis=0)[None]

    ffn_w_in16, ffn_w_out16 = ffn_w_in.astype(BF16), ffn_w_out.astype(BF16)
    gdn_w_in16, gdn_w_out16 = _bf16_padded(gdn_w_in, 512), gdn_w_out.astype(BF16)
    dsa_w_in16, dsa_w_out16 = _bf16_padded(dsa_w_in, 512), dsa_w_out.astype(BF16)
    lru_w_in16, lru_w_out16 = lru_w_in.astype(BF16), lru_w_out.astype(BF16)

    xp, xs = x_prompt, x_sample
    outs = {k: [] for k in ("a_conv_p", "a_conv_s", "a_ssm_p", "a_ssm_s", "b_k_p", "b_k_s", "b_v_p", "b_v_s",
                            "b_i_p", "b_i_s", "c_conv_p", "c_conv_s", "c_h_p", "c_h_s")}
    for layer in range(depth):
        kind, j = layer % N_MIXERS, layer // N_MIXERS
        mod_p, mod_s = groups(mod[layer])
        nw = norm_w[layer]

        def ffn(x, m3, tm, tf, which):
            b, t, _ = x.shape
            return _ffn(x.reshape(b * t, d), m3, 6 * which, nw[2 * which:2 * which + 1],
                        (ffn_w_in16, (layer, which)), (ffn_w_out16, (layer, which)), tm=tm,
                        tf_target=tf).reshape(b, t, d)

        xp = ffn(xp, mod_p, tm_in_p, 256, 0)
        xs = ffn(xs, mod_s, tm_s, 512, 0)
        if kind == 0:
            prm = ((gdn_w_in16, (j,)), gdn_conv_w[j], gdn_a_log[j], gdn_dt_bias[j], gdn_norm_w[j],
                   (gdn_w_out16, (j,)))
            buf0 = jnp.zeros((bp,) + state_a_conv.shape[2:], F32)
            s0 = jnp.zeros((bp,) + state_a_ssm.shape[2:], F32)
            xp, buf, s = _gdn_mixer(xp, mod_p, nw[1:2], buf0, s0, *prm, chunk=2 * GDN_CHUNK, **tiles_p)
            outs["a_conv_p"].append(buf)
            outs["a_ssm_p"].append(s)
            xs, buf, s = _gdn_mixer(xs, mod_s, nw[1:2], state_a_conv[j], state_a_ssm[j], *prm, chunk=GDN_CHUNK,
                                    **tiles_s)
            outs["a_conv_s"].append(buf)
            outs["a_ssm_s"].append(s)
        elif kind == 1:
            w_in, w_out = (dsa_w_in16, (j,)), (dsa_w_out16, (j,))
            n_proj = dsa_w_in.shape[2]
            xp, k, v, ki = _dsa_mixer(xp, mod_p, nw[1:2], w_in, w_out, n_proj, nkv=nkv, **tiles_p)
            outs["b_k_p"].append(k)
            outs["b_v_p"].append(v)
            outs["b_i_p"].append(ki)
            xs, k, v, ki = _dsa_mixer(xs, mod_s, nw[1:2], w_in, w_out, n_proj, nkv=nkv,
                                      cache=(cache_b_k, cache_b_v, cache_b_idx, j, page_table), **tiles_s)
            outs["b_k_s"].append(k)
            outs["b_v_s"].append(v)
            outs["b_i_s"].append(ki)
        else:
            prm = ((lru_w_in16, (j,)), lru_conv_w[j], lru_conv_b[j], lru_w_gate_a[j].astype(BF16),
                   lru_b_gate_a[j], lru_w_gate_x[j].astype(BF16), lru_b_gate_x[j], lru_lambda[j],
                   (lru_w_out16, (j,)))
            buf0 = jnp.zeros((bp,) + state_c_conv.shape[2:], F32)
            h0 = jnp.zeros((bp,) + state_c_h.shape[2:], F32)
            xp, buf, hl = _lru_mixer(xp, mod_p, nw[1:2], buf0, h0, *prm, **tiles_p)
            outs["c_conv_p"].append(buf)
            outs["c_h_p"].append(hl)
            xs, buf, hl = _lru_mixer(xs, mod_s, nw[1:2], state_c_conv[j], state_c_h[j], *prm, **tiles_s)
            outs["c_conv_s"].append(buf)
            outs["c_h_s"].append(hl)
        xp = ffn(xp, mod_p, tm_in_p, 256, 1)
        xs = ffn(xs, mod_s, tm_s, 512, 1)

    modf_p, modf_s = groups(mod_f)
    fnw = final_norm_w.reshape(1, d)
    y_p = _final_norm(xp.reshape(bp * seq, d), modf_p, fnw, tm=tm_p).reshape(bp, seq, d)
    y_s = _final_norm(xs.reshape(bs * ts, d), modf_s, fnw, tm=tm_s).reshape(bs, ts, d)
    st = {k: jnp.stack(v) for k, v in outs.items()}
    return (y_p, y_s, st["a_conv_p"], st["a_conv_s"], st["a_ssm_p"], st["a_ssm_s"], st["b_k_p"], st["b_k_s"],
            st["b_v_p"], st["b_v_s"], st["b_i_p"], st["b_i_s"], st["c_conv_p"], st["c_conv_s"], st["c_h_p"],
            st["c_h_s"])
```

```python
import functools
import math

import jax
import jax.numpy as jnp
import numpy as np
from jax import lax
from jax.experimental import pallas as pl
from jax.experimental.pallas import tpu as pltpu

F32 = jnp.float32
BF16 = jnp.bfloat16
I32 = jnp.int32

N_MIXERS = 3
N_ADA = 9
CONV_W = 4
NORM_EPS = 1e-6
NEG_INF = -1e30
GDN_CHUNK = 64
TOPK_MAX = 256
Q_BLOCK = 128
ROPE_THETA = 10000.0
RG_C = 8.0
LRU_BLOCK = 256
SAMPLE_PAGES_PER_STEP = 4

LANES = 128
SUBLANES = 8
VMEM_LIMIT_BYTES = 56 * 2**20
INT32_MIN = -2**31
INT32_MAX = 2**31 - 1
KEY_OF_NEG_INF = int(np.float32(NEG_INF).view(np.int32)) ^ 0x7FFFFFFF


def _cparams(*sem):
    return pltpu.CompilerParams(dimension_semantics=sem, vmem_limit_bytes=VMEM_LIMIT_BYTES)


def _tile(n, target, align=LANES):
    if n <= target:
        return n
    t = (target // align) * align
    while t >= align:
        if n % t == 0:
            return t
        t -= align
    raise ValueError(f"no {align}-aligned tile of {n} below {target}")


def _sigmoid(x):
    return jax.nn.sigmoid(x)


def _silu(x):
    return x * _sigmoid(x)


def _dot(a, b):
    return jnp.dot(a, b, preferred_element_type=F32)


def _dot_nt(a, b):
    return lax.dot_general(a, b, (((1,), (1,)), ((), ())), preferred_element_type=F32)


def _dot_tn(a, b):
    return lax.dot_general(a, b, (((0,), (0,)), ((), ())), preferred_element_type=F32)


def _dot_f32(a, b):
    return jnp.dot(a, b, preferred_element_type=F32, precision=lax.Precision.HIGHEST)


def _norm_mod(x, nw, sh, sc):
    ms = jnp.mean(x * x, axis=-1, keepdims=True)
    return x * lax.rsqrt(ms + NORM_EPS) * (nw * (1.0 + sc)) + sh


def _ada_kernel(c_ref, w_ref, b_ref, o_ref):
    a = _silu(c_ref[...]).astype(BF16)
    o_ref[0] = _dot(a, w_ref[0].astype(BF16)) + b_ref[0]


def _ada(c_all, w, b):
    n_l, d, n = w.shape
    mp = c_all.shape[0]
    tn = _tile(n, 1024)
    return pl.pallas_call(
        _ada_kernel,
        grid=(n_l, n // tn),
        in_specs=[
            pl.BlockSpec((mp, d), lambda l, j: (0, 0)),
            pl.BlockSpec((1, d, tn), lambda l, j: (l, 0, j)),
            pl.BlockSpec((1, 1, tn), lambda l, j: (l, 0, j)),
        ],
        out_specs=pl.BlockSpec((1, mp, tn), lambda l, j: (l, 0, j)),
        out_shape=jax.ShapeDtypeStruct((n_l, mp, n), F32),
        name="ada_mod",
        compiler_params=_cparams("parallel", "parallel"),
    )(c_all, w, b.reshape(n_l, 1, n))


def _mod_spec(mod3, k, d, tm, rows_per_group):
    return pl.BlockSpec((None, mod3.shape[1], d), lambda i, j: ((i * tm) // rows_per_group, 0, k))


def _wspec(w, block, imap):
    arr, idx = w
    return pl.BlockSpec((None,) * len(idx) + block, lambda i, j: idx + imap(i, j))


def _wshape(w):
    return w[0].shape[len(w[1]):]


def _ffn_kernel(x_ref, nw_ref, sh_ref, sc_ref, g_ref, wa_ref, wb_ref, wo_ref, o_ref, xn_ref):
    f = pl.program_id(1)

    @pl.when(f == 0)
    def _():
        xn_ref[...] = _norm_mod(x_ref[...], nw_ref[...], sh_ref[...], sc_ref[...]).astype(BF16)

    xn = xn_ref[...]
    a = _dot(xn, wa_ref[...])
    b = _dot(xn, wb_ref[...])
    h = (_silu(a) * b).astype(BF16)

    @pl.when(f == 0)
    def _():
        o_ref[...] = _dot(h, wo_ref[...])

    @pl.when(f > 0)
    def _():
        o_ref[...] += _dot(h, wo_ref[...])

    @pl.when(f == pl.num_programs(1) - 1)
    def _():
        o_ref[...] = x_ref[...] + 0.5 * g_ref[...] * o_ref[...]


def _ffn(x, mod3, k0, nw, w_in, w_out, *, tm, tf_target):
    m, d = x.shape
    f = _wshape(w_out)[0]
    tf = _tile(f, tf_target)
    nf = f // tf
    rpg = m // mod3.shape[0]
    ms = lambda k: _mod_spec(mod3, k, d, tm, rpg)
    return pl.pallas_call(
        _ffn_kernel,
        grid=(m // tm, nf),
        in_specs=[
            pl.BlockSpec((tm, d), lambda i, j: (i, 0)),
            pl.BlockSpec((1, d), lambda i, j: (0, 0)),
            ms(k0), ms(k0 + 1), ms(k0 + 2),
            _wspec(w_in, (d, tf), lambda i, j: (0, j)),
            _wspec(w_in, (d, tf), lambda i, j: (0, j + nf)),
            _wspec(w_out, (tf, d), lambda i, j: (j, 0)),
        ],
        out_specs=pl.BlockSpec((tm, d), lambda i, j: (i, 0)),
        out_shape=jax.ShapeDtypeStruct((m, d), F32),
        scratch_shapes=[pltpu.VMEM((tm, d), BF16)],
        name="ffn",
        compiler_params=_cparams("parallel", "arbitrary"),
    )(x, nw, mod3, mod3, mod3, w_in[0], w_in[0], w_out[0])


def _normproj_kernel(x_ref, nw_ref, sh_ref, sc_ref, w_ref, o_ref, xn_ref):
    @pl.when(pl.program_id(1) == 0)
    def _():
        xn_ref[...] = _norm_mod(x_ref[...], nw_ref[...], sh_ref[...], sc_ref[...]).astype(BF16)

    o_ref[...] = _dot(xn_ref[...], w_ref[...])


def _normproj(x, mod3, k0, nw, w, *, tm, tn_target=1024):
    m, d = x.shape
    n = _wshape(w)[1]
    tn = _tile(n, tn_target)
    rpg = m // mod3.shape[0]
    ms = lambda k: _mod_spec(mod3, k, d, tm, rpg)
    return pl.pallas_call(
        _normproj_kernel,
        grid=(m // tm, n // tn),
        in_specs=[
            pl.BlockSpec((tm, d), lambda i, j: (i, 0)),
            pl.BlockSpec((1, d), lambda i, j: (0, 0)),
            ms(k0), ms(k0 + 1),
            _wspec(w, (d, tn), lambda i, j: (0, j)),
        ],
        out_specs=pl.BlockSpec((tm, tn), lambda i, j: (i, j)),
        out_shape=jax.ShapeDtypeStruct((m, n), F32),
        scratch_shapes=[pltpu.VMEM((tm, d), BF16)],
        name="normproj",
        compiler_params=_cparams("parallel", "arbitrary"),
    )(x, nw, mod3, mod3, w[0])


def _outproj_kernel(a_ref, w_ref, x_ref, g_ref, o_ref):
    o_ref[...] = x_ref[...] + g_ref[...] * _dot(a_ref[...], w_ref[...])


def _outproj(a, w, x, mod3, kg, *, tm, tn_target=1024):
    m, kdim = a.shape
    d = _wshape(w)[1]
    tn = _tile(d, tn_target)
    rpg = m // mod3.shape[0]
    r = mod3.shape[1]
    nd = d // tn
    return pl.pallas_call(
        _outproj_kernel,
        grid=(m // tm, nd),
        in_specs=[
            pl.BlockSpec((tm, kdim), lambda i, j: (i, 0)),
            _wspec(w, (kdim, tn), lambda i, j: (0, j)),
            pl.BlockSpec((tm, tn), lambda i, j: (i, j)),
            pl.BlockSpec((None, r, tn), lambda i, j: ((i * tm) // rpg, 0, kg * nd + j)),
        ],
        out_specs=pl.BlockSpec((tm, tn), lambda i, j: (i, j)),
        out_shape=jax.ShapeDtypeStruct((m, d), F32),
        name="outproj",
        compiler_params=_cparams("parallel", "arbitrary"),
    )(a, w[0], x, mod3)


def _final_kernel(x_ref, nw_ref, sh_ref, sc_ref, o_ref):
    o_ref[...] = _norm_mod(x_ref[...], nw_ref[...], sh_ref[...], sc_ref[...])


def _final_norm(x, mod3, nw, *, tm):
    m, d = x.shape
    rpg = m // mod3.shape[0]
    r = mod3.shape[1]
    ms = lambda k: pl.BlockSpec((None, r, d), lambda i: ((i * tm) // rpg, 0, k))
    return pl.pallas_call(
        _final_kernel,
        grid=(m // tm,),
        in_specs=[pl.BlockSpec((tm, d), lambda i: (i, 0)), pl.BlockSpec((1, d), lambda i: (0, 0)), ms(0), ms(1)],
        out_specs=pl.BlockSpec((tm, d), lambda i: (i, 0)),
        out_shape=jax.ShapeDtypeStruct((m, d), F32),
        name="final_norm",
        compiler_params=_cparams("parallel"),
    )(x, nw, mod3, mod3)


def _causal_conv(x, halo, w):
    def taps(rows, fix):
        acc = rows * w[CONV_W - 1:CONV_W]
        for j in range(1, CONV_W):
            acc = acc + fix(pltpu.roll(rows, j, 0), j) * w[CONV_W - 1 - j:CONV_W - j]
        return acc

    row = lax.broadcasted_iota(I32, halo.shape, 0)
    head = taps(x[:SUBLANES], lambda r, j: jnp.where(row < j, pltpu.roll(halo, j, 0), r))
    if x.shape[0] == SUBLANES:
        return head
    return jnp.concatenate([head, taps(x, lambda r, j: r)[SUBLANES:]], axis=0)


def _conv_specs(tt, tc, coff, boff):
    sub = tt // SUBLANES
    return [
        pl.BlockSpec((None, tt, tc), lambda b, t, c: (b, t, c + coff)),
        pl.BlockSpec((None, SUBLANES, tc), lambda b, t, c: (b, jnp.maximum(t * sub - 1, 0), c + coff)),
        pl.BlockSpec((None, SUBLANES, tc), lambda b, t, c: (b, 0, c + boff)),
    ]


def _pad_buf(buf):
    return jnp.pad(buf, ((0, 0), (SUBLANES - (CONV_W - 1), 0), (0, 0)))


def _gdn_prep_kernel(x_ref, halo_ref, buf_ref, w_ref, o_ref, *, norm):
    halo = jnp.where(pl.program_id(1) == 0, buf_ref[...], halo_ref[...])
    y = _silu(_causal_conv(x_ref[...], halo, w_ref[...]))
    if norm:
        for h in range(y.shape[1] // LANES):
            seg = y[:, h * LANES:(h + 1) * LANES]
            ss = jnp.sum(seg * seg, axis=-1, keepdims=True)
            o_ref[:, h * LANES:(h + 1) * LANES] = seg * lax.rsqrt(ss + NORM_EPS)
    else:
        o_ref[...] = y


def _gdn_prep(proj, buf8, conv_w, *, col0, ncols, norm, tt):
    bsz, t, _ = proj.shape
    tc = _tile(ncols, 512)
    return pl.pallas_call(
        functools.partial(_gdn_prep_kernel, norm=norm),
        grid=(bsz, t // tt, ncols // tc),
        in_specs=_conv_specs(tt, tc, col0 // tc, col0 // tc)
        + [pl.BlockSpec((CONV_W, tc), lambda b, i, c: (0, c + col0 // tc))],
        out_specs=pl.BlockSpec((None, tt, tc), lambda b, i, c: (b, i, c)),
        out_shape=jax.ShapeDtypeStruct((bsz, t, ncols), F32),
        name="gdn_prep",
        compiler_params=_cparams("parallel", "parallel", "parallel"),
    )(proj, proj, buf8, conv_w)


def _gdn_gate_kernel(x_ref, alog_ref, dtb_ref, beta_ref, gc_ref, *, hv, chunk):
    x = x_ref[...]
    tt = x.shape[0]
    beta_ref[...] = _sigmoid(x)
    z = x + dtb_ref[...]
    g = -jnp.exp(alog_ref[...]) * (jnp.maximum(z, 0.0) + jnp.log1p(jnp.exp(-jnp.abs(z))))
    row = lax.broadcasted_iota(I32, (tt, tt), 0)
    col = lax.broadcasted_iota(I32, (tt, tt), 1)
    tri = jnp.where((row >= col) & (row // chunk == col // chunk), 1.0, 0.0)
    gc_ref[...] = _dot_f32(tri, g)


def _gdn_gates(proj, a_log, dt_bias, *, col0, hv, chunk, tt):
    bsz, t, _ = proj.shape
    pad = lambda v: jnp.pad(v.astype(F32), (hv, LANES - 2 * hv)).reshape(1, LANES)
    blk = pl.BlockSpec((None, tt, LANES), lambda b, i: (b, i, col0 // LANES))
    out = pl.BlockSpec((None, tt, LANES), lambda b, i: (b, i, 0))
    par = pl.BlockSpec((1, LANES), lambda b, i: (0, 0))
    return pl.pallas_call(
        functools.partial(_gdn_gate_kernel, hv=hv, chunk=chunk),
        grid=(bsz, t // tt),
        in_specs=[blk, par, par],
        out_specs=[out, out],
        out_shape=[jax.ShapeDtypeStruct((bsz, t, LANES), F32)] * 2,
        name="gdn_gates",
        compiler_params=_cparams("parallel", "parallel"),
    )(proj, pad(a_log), pad(dt_bias))


def _split_bf16(a):
    hi = a.astype(BF16)
    return hi, (a - hi.astype(F32)).astype(BF16)


def _dot_split(a, b):
    ah, al = _split_bf16(a)
    bh, bl = _split_bf16(b)
    return _dot(jnp.concatenate([ah, al, ah], axis=1), jnp.concatenate([bh, bh, bl], axis=0))


def _dot_bf16(a, b):
    return _dot(a.astype(BF16), b.astype(BF16))


def _tri_inv_all(lmats, c, n_real):
    row = lax.broadcasted_iota(I32, (c, c), 0)
    col = lax.broadcasted_iota(I32, (c, c), 1)
    eye = jnp.where(row == col, 1.0, 0.0)
    base = min(16, c)
    ps = [-jnp.where(row // base == col // base, m, 0.0) for m in lmats]
    rs = [eye + p for p in ps]
    n = 2
    while n < base:
        ps = [_dot_bf16(p, p) for p in ps]
        rs = [r + _dot_bf16(r, p) for r, p in zip(rs, ps)]
        n *= 2
    s = base
    while s < min(c, n_real):
        off = (row // (2 * s) == col // (2 * s)) & (row // s != col // s)
        ts = [_dot_bf16(jnp.where(off, m, 0.0), r) for m, r in zip(lmats, rs)]
        rs = [r - _dot_bf16(r, t) for r, t in zip(rs, ts)]
        s *= 2
    res = [eye - r - _dot_split(m, r) for m, r in zip(lmats, rs)]
    return [r + _dot_bf16(r, e) for r, e in zip(rs, res)]


def _gdn_core_kernel(q_ref, k_ref, v_ref, z_ref, gcc_ref, gcr_ref, bc_ref, s0_ref, nw_ref, o_ref, so_ref, s_ref,
                     *, rep, c, ncb, hpb, dk, n_real):
    ci = pl.program_id(2)

    @pl.when(ci == 0)
    def _():
        s_ref[...] = s0_ref[...]

    row = lax.broadcasted_iota(I32, (c, c), 0)
    col = lax.broadcasted_iota(I32, (c, c), 1)
    causal = row >= col
    nhd = hpb * rep
    kinst = [(n, hh) for n in range(ncb) for hh in range(hpb)]
    inst = [(n, hh, r) for n in range(ncb) for hh in range(hpb) for r in range(rep)]
    rows = lambda n: slice(n * c, (n + 1) * c)
    lanes = lambda hh, r: slice((hh * rep + r) * LANES, (hh * rep + r + 1) * LANES)
    kidx = lambda n, hh: n * hpb + hh
    qs = [q_ref[rows(n), hh * LANES:(hh + 1) * LANES] * dk ** -0.5 for n, hh in kinst]
    ks = [k_ref[rows(n), hh * LANES:(hh + 1) * LANES] for n, hh in kinst]
    k16 = [k.astype(BF16) for k in ks]
    grams = [_dot_nt(kb, kb) for kb in k16]
    qk0s = [_dot_nt(q.astype(BF16), kb) for q, kb in zip(qs, k16)]
    gccs = [gcc_ref[hh, rows(n), r:r + 1] for n, hh, r in inst]
    gcrs = [gcr_ref[hh, r:r + 1, rows(n)] for n, hh, r in inst]
    betas = [bc_ref[hh, rows(n), r:r + 1] for n, hh, r in inst]
    decays = [jnp.where(causal, jnp.exp(jnp.where(causal, gc - gr, 0.0)), 0.0) for gc, gr in zip(gccs, gcrs)]
    lowers = [jnp.where(row > col, grams[kidx(n, hh)] * b * d, 0.0)
              for (n, hh, _), b, d in zip(inst, betas, decays)]
    tinvs = _tri_inv_all(lowers, c, n_real)
    egcs = [jnp.exp(gc) for gc in gccs]
    sols = [_dot_split(ti, jnp.concatenate([v_ref[rows(n), lanes(hh, r)] * b, ks[kidx(n, hh)] * (b * e)], axis=1))
            for ti, b, e, (n, hh, r) in zip(tinvs, betas, egcs, inst)]
    qg16 = [(qs[kidx(n, hh)] * e).astype(BF16) for (n, hh, _), e in zip(inst, egcs)]
    qk16 = [(qk0s[kidx(n, hh)] * d).astype(BF16) for (n, hh, _), d in zip(inst, decays)]
    g_last = [gc[c - 1:c, :] for gc in gccs]
    kd16 = [(ks[kidx(n, hh)] * jnp.exp(gl - gc)).astype(BF16) for (n, hh, _), gl, gc in zip(inst, g_last, gccs)]
    ss = [s_ref[i] for i in range(nhd)]
    for n in range(ncb):
        ids = range(n * nhd, (n + 1) * nhd)
        s16 = [s.astype(BF16) for s in ss]
        u16 = [(sols[i][:, :LANES] - _dot(sols[i][:, LANES:].astype(BF16), sb)).astype(BF16)
               for i, sb in zip(ids, s16)]
        ss = [s * jnp.exp(g_last[i]) + _dot_tn(kd16[i], ub) for s, i, ub in zip(ss, ids, u16)]
        os_ = [_dot(qg16[i], sb) + _dot(qk16[i], ub) for i, sb, ub in zip(ids, s16, u16)]
        for o, i in zip(os_, ids):
            _, hh, r = inst[i]
            on = o * lax.rsqrt(jnp.mean(o * o, axis=-1, keepdims=True) + NORM_EPS) * nw_ref[...]
            o_ref[rows(n), lanes(hh, r)] = (on * _silu(z_ref[rows(n), lanes(hh, r)])).astype(BF16)
    for i in range(nhd):
        s_ref[i] = ss[i]

    @pl.when(ci == pl.num_programs(2) - 1)
    def _():
        so_ref[...] = s_ref[...]


def _gdn_core(qk, v, proj, zcol0, gc, beta, s0, norm_w, *, c, ncb, hpb, n_real):
    bsz, t, val = v.shape
    hv = s0.shape[1]
    dk, dv = s0.shape[2], s0.shape[3]
    hk = qk.shape[2] // (2 * dk)
    rep = hv // hk
    tb = ncb * c
    assert dk == LANES and dv == LANES and t % tb == 0 and hk % hpb == 0 and zcol0 % (hpb * rep * dv) == 0
    heads = lambda a, lo: a[:, :, lo:lo + hv].reshape(bsz, t, hk, rep).transpose(0, 2, 1, 3)
    gcc = heads(gc, hv)
    bcc = heads(beta, 0)
    gcr = gcc.transpose(0, 1, 3, 2)
    nhb = hk // hpb
    zb = zcol0 // (hpb * rep * dv)
    colspec = pl.BlockSpec((None, hpb, tb, rep), lambda b, h, i: (b, h, i, 0))
    o, s_out = pl.pallas_call(
        functools.partial(_gdn_core_kernel, rep=rep, c=c, ncb=ncb, hpb=hpb, dk=dk, n_real=n_real),
        grid=(bsz, nhb, t // tb),
        in_specs=[
            pl.BlockSpec((None, tb, hpb * dk), lambda b, h, i: (b, i, h)),
            pl.BlockSpec((None, tb, hpb * dk), lambda b, h, i: (b, i, nhb + h)),
            pl.BlockSpec((None, tb, hpb * rep * dv), lambda b, h, i: (b, i, h)),
            pl.BlockSpec((None, tb, hpb * rep * dv), lambda b, h, i: (b, i, zb + h)),
            colspec,
            pl.BlockSpec((None, hpb, rep, tb), lambda b, h, i: (b, h, 0, i)),
            colspec,
            pl.BlockSpec((None, hpb * rep, dk, dv), lambda b, h, i: (b, h, 0, 0)),
            pl.BlockSpec((1, dv), lambda b, h, i: (0, 0)),
        ],
        out_specs=[
            pl.BlockSpec((None, tb, hpb * rep * dv), lambda b, h, i: (b, i, h)),
            pl.BlockSpec((None, hpb * rep, dk, dv), lambda b, h, i: (b, h, 0, 0)),
        ],
        out_shape=[jax.ShapeDtypeStruct((bsz, t, val), BF16), jax.ShapeDtypeStruct(s0.shape, F32)],
        scratch_shapes=[pltpu.VMEM((hpb * rep, dk, dv), F32)],
        name="gdn_core",
        compiler_params=_cparams("parallel", "parallel", "arbitrary"),
    )(qk, qk, v, proj, gcc, gcr, bcc, s0, norm_w.reshape(1, dv))
    return o, s_out


def _gdn_mixer(x, mod3, nw, buf, s0, w_in, conv_w, a_log, dt_bias, norm_w, w_out, *, tm, tm_in, tt, chunk):
    bsz, t, d = x.shape
    hv, dk, dv = s0.shape[1], s0.shape[2], s0.shape[3]
    val = hv * dv
    conv_dim = conv_w.shape[1]
    key = (conv_dim - val) // 2
    assert (conv_dim + val) % LANES == 0 and 2 * hv <= LANES and t >= CONV_W - 1
    x2 = x.reshape(bsz * t, d)
    proj = _normproj(x2, mod3, 3, nw, w_in, tm=tm_in, tn_target=512).reshape(bsz, t, -1)
    buf8 = _pad_buf(buf)
    qk = _gdn_prep(proj, buf8, conv_w, col0=0, ncols=2 * key, norm=True, tt=tt)
    v = _gdn_prep(proj, buf8, conv_w, col0=2 * key, ncols=val, norm=False, tt=tt)
    tp = -(-t // chunk) * chunk
    beta, gc = _gdn_gates(proj, a_log, dt_bias, col0=conv_dim + val, hv=hv, chunk=min(chunk, tt), tt=tt)
    if tp != t:
        padt = lambda a: jnp.pad(a, ((0, 0), (0, tp - t), (0, 0)))
        gc = jnp.concatenate([gc, jnp.broadcast_to(gc[:, -1:], (bsz, tp - t, LANES))], axis=1)
        qk, v, beta, projz = padt(qk), padt(v), padt(beta), padt(proj)
    else:
        projz = proj
    o, s_new = _gdn_core(qk, v, projz, conv_dim, gc, beta, s0, norm_w, c=chunk, ncb=math.gcd(tp // chunk, 2),
                         hpb=min(4, key // dk), n_real=min(t, chunk))
    o2 = o[:, :t].reshape(bsz * t, val)
    xo = _outproj(o2, w_out, x2, mod3, 5, tm=tm).reshape(bsz, t, d)
    new_buf = proj[:, t - (CONV_W - 1):, :conv_dim]
    return xo, new_buf, s_new


def _rope_tables(pos, half):
    inv_freq = ROPE_THETA ** (-jnp.arange(half, dtype=F32) / half)
    ang = pos.astype(F32)[:, None] * inv_freq[None, :]
    cos, sin = jnp.cos(ang), jnp.sin(ang)
    return jnp.concatenate([cos, cos], axis=-1), jnp.concatenate([-sin, sin], axis=-1)


def _dsa_prep_kernel(x_ref, cos_ref, sin_ref, q_ref, k_ref, k16_ref, v16_ref, qi_ref, ki_ref, ki16_ref, *tr_refs,
                     nh, nkv, nih):
    cos, sin = cos_ref[...], sin_ref[...]

    def rope(col):
        seg = x_ref[:, col * LANES:(col + 1) * LANES]
        return seg * cos + pltpu.roll(seg, LANES // 2, 1) * sin

    for h in range(nh):
        q_ref[:, h * LANES:(h + 1) * LANES] = (rope(h) * LANES ** -0.5).astype(BF16)
    for h in range(nkv):
        kr = rope(nh + h)
        k_ref[:, h * LANES:(h + 1) * LANES] = kr
        k16_ref[:, h * LANES:(h + 1) * LANES] = kr.astype(BF16)
    v0 = (nh + nkv) * LANES
    v16_ref[...] = x_ref[:, v0:v0 + nkv * LANES].astype(BF16)
    c0 = nh + 2 * nkv
    for h in range(nih):
        qi_ref[:, h * LANES:(h + 1) * LANES] = rope(c0 + h).astype(BF16)
    kir = rope(c0 + nih)
    ki_ref[...] = kir
    ki16_ref[...] = kir.astype(BF16)
    if tr_refs:
        wt_ref, vt_ref = tr_refs
        w0 = (c0 + nih + 1) * LANES
        wt_ref[...] = x_ref[:, w0:w0 + LANES].T[:wt_ref.shape[0], :]
        for h in range(nkv):
            vt_ref[h * LANES:(h + 1) * LANES, :] = x_ref[:, v0 + h * LANES:v0 + (h + 1) * LANES].T.astype(BF16)


def _dsa_prep(proj, pos, *, nh, nkv, nih, tt, with_wt):
    bsz, t, npj = proj.shape
    cos, sin = _rope_tables(pos, LANES // 2)
    row = lambda n, dt: jax.ShapeDtypeStruct((bsz, t, n * LANES), dt)
    ospec = lambda n: pl.BlockSpec((None, tt, n * LANES), lambda b, i: (b, i, 0))
    tab = pl.BlockSpec((tt, LANES), lambda b, i: (i, 0))
    nwt = -(-nih // SUBLANES) * SUBLANES
    return pl.pallas_call(
        functools.partial(_dsa_prep_kernel, nh=nh, nkv=nkv, nih=nih),
        grid=(bsz, t // tt),
        in_specs=[pl.BlockSpec((None, tt, npj), lambda b, i: (b, i, 0)), tab, tab],
        out_specs=[ospec(nh), ospec(nkv), ospec(nkv), ospec(nkv), ospec(nih), ospec(1), ospec(1)]
        + ([pl.BlockSpec((None, nwt, tt), lambda b, i: (b, 0, i)),
            pl.BlockSpec((None, None, nkv * LANES, tt), lambda b, i: (b, i, 0, 0))] if with_wt else []),
        out_shape=[row(nh, BF16), row(nkv, F32), row(nkv, BF16), row(nkv, BF16), row(nih, BF16), row(1, F32),
                   row(1, BF16)]
        + ([jax.ShapeDtypeStruct((bsz, nwt, t), F32),
            jax.ShapeDtypeStruct((bsz, t // tt, nkv * LANES, tt), BF16)] if with_wt else []),
        name="dsa_prep",
        compiler_params=_cparams("parallel", "parallel"),
    )(proj, cos, sin)


def _sort_key(s):
    bits = pltpu.bitcast(jnp.where(s == 0.0, 0.0, s), I32)
    return jnp.where(bits < 0, bits ^ 0x7FFFFFFF, bits)


def _topk_cut(count_where, shape, topk, idx_bits):
    def body(i, carry):
        t, n_t = carry
        cand = t + lax.shift_left(jnp.int32(1), 31 - i)
        n = count_where(lambda key, idx: key >= cand)
        ok = n >= topk
        return jnp.where(ok, cand, t), jnp.where(ok, n, n_t)

    thr, n_ge = lax.fori_loop(0, 32, body, (jnp.full(shape, INT32_MIN, I32), jnp.full(shape, INT32_MAX, I32)))
    tie = (n_ge > topk) & (thr > KEY_OF_NEG_INF)

    def cut():
        need = topk - count_where(lambda key, idx: key > thr)

        def jbody(i, j):
            cand = j + lax.shift_left(jnp.int32(1), idx_bits - 1 - i)
            below = count_where(lambda key, idx: (key == thr) & (idx < cand))
            return jnp.where(below < need, cand, j)

        return jnp.where(tie, lax.fori_loop(0, idx_bits, jbody, jnp.zeros(shape, I32)), INT32_MAX)

    jcut = lax.cond(jnp.any(tie), cut, lambda: jnp.full(shape, INT32_MAX, I32))
    return thr, jcut


def _topk_chosen(key, idx, thr, jcut):
    return (key > thr) | ((key == thr) & (idx <= jcut))


def _dsa_prompt_kernel(qi_ref, wt_ref, q_ref, ki_ref, k_ref, vt_ref, o_ref, key_ref, bias_ref, acc_ref,
                       *, nh, nkv, nih, topk, idx_scale):
    qb = pl.program_id(1)
    blk = Q_BLOCK
    kb = vt_ref.shape[2]
    ktiles = kb // blk
    nkb = (qb + ktiles) // ktiles
    rowk = lax.broadcasted_iota(I32, (kb, blk), 0)
    colq = lax.broadcasted_iota(I32, (kb, blk), 1)
    wt = wt_ref[...]
    hq = max(1, nih // 4)
    qis = [jnp.concatenate([qi_ref[:, h * LANES:(h + 1) * LANES] for h in range(h0, min(h0 + hq, nih))], axis=0)
           for h0 in range(0, nih, hq)]

    def rows_of(j):
        return pl.ds(pl.multiple_of(j * kb, kb), kb)

    def visible(j):
        return j * kb + rowk <= qb * blk + colq

    def score_body(j, carry):
        keys = ki_ref[rows_of(j), :]
        lgs = [_dot_nt(keys, qi) for qi in qis]
        acc = jnp.zeros((kb, blk), F32)
        for i, lg in enumerate(lgs):
            for hh in range(lg.shape[1] // blk):
                h = i * hq + hh
                acc = acc + jnp.maximum(lg[:, hh * blk:(hh + 1) * blk], 0.0) * wt[h:h + 1, :]
        key_ref[rows_of(j), :] = _sort_key(jnp.where(visible(j), acc * idx_scale, NEG_INF))
        return carry

    lax.fori_loop(0, nkb, score_body, 0)

    def count_where(pred):
        def body(j, c):
            hit = jnp.where(pred(key_ref[rows_of(j), :], j * kb + rowk), 1, 0)
            for i in range(ktiles):
                c = c + hit[i * blk:(i + 1) * blk]
            return c
        cnt = lax.fori_loop(0, nkb, body, jnp.zeros((blk, blk), I32))
        return jnp.sum(cnt, axis=0, keepdims=True)

    thr, jcut = _topk_cut(count_where, (1, blk), topk, (key_ref.shape[0] - 1).bit_length())

    def bias_body(j, carry):
        sel = _topk_chosen(key_ref[rows_of(j), :], j * kb + rowk, thr, jcut) & visible(j)
        bias_ref[rows_of(j), :] = jnp.where(sel, 0.0, NEG_INF)
        return carry

    lax.fori_loop(0, nkb, bias_body, 0)

    rep = nh // nkv
    qgs =[jnp.concatenate([q_ref[:, (g * rep + r) * LANES:(g * rep + r + 1) * LANES] for r in range(rep)], axis=0)
           for g in range(nkv)]
    acc_ref[...] = jnp.zeros_like(acc_ref)

    def att_body(j, carry):
        ms, ls = carry
        rows = rows_of(j)
        bias = jnp.concatenate([bias_ref[rows, :]] * rep, axis=1)
        ss = [_dot_nt(k_ref[rows, g * LANES:(g + 1) * LANES], qgs[g]) + bias for g in range(nkv)]
        m_new = [jnp.maximum(m, jnp.max(s, axis=0, keepdims=True)) for m, s in zip(ms, ss)]
        ps = [jnp.exp(s - m) for s, m in zip(ss, m_new)]
        pvs = [_dot(vt_ref[j, g * LANES:(g + 1) * LANES, :], p.astype(BF16)) for g, p in enumerate(ps)]
        alphas = [jnp.exp(m - mn) for m, mn in zip(ms, m_new)]
        for g in range(nkv):
            acc_ref[g] = alphas[g] * acc_ref[g] + pvs[g]
        l_new = [a * l + jnp.sum(p, axis=0, keepdims=True) for a, l, p in zip(alphas, ls, ps)]
        return tuple(m_new), tuple(l_new)

    row0 = lambda v: tuple(jnp.full((1, rep * blk), v, F32) for _ in range(nkv))
    _, ls = lax.fori_loop(0, nkb, att_body, (row0(NEG_INF), row0(0.0)))
    for g in range(nkv):
        og = acc_ref[g] / ls[g]
        for r in range(rep):
            h = g * rep + r
            o_ref[:, h * LANES:(h + 1) * LANES] = og[:, r * blk:(r + 1) * blk].T.astype(BF16)


def _dsa_prompt_attend(q16, qi16, wt, ki16, k16, vt16, *, nh, nkv, nih, topk):
    bsz, t, _ = q16.shape
    rep = nh // nkv
    full = lambda n: pl.BlockSpec((None, t, n * LANES), lambda b, i: (b, 0, 0))
    blk = lambda n: pl.BlockSpec((None, Q_BLOCK, n * LANES), lambda b, i: (b, i, 0))
    return pl.pallas_call(
        functools.partial(_dsa_prompt_kernel, nh=nh, nkv=nkv, nih=nih, topk=topk,
                          idx_scale=(LANES * nih) ** -0.5),
        grid=(bsz, t // Q_BLOCK),
        in_specs=[blk(nih), pl.BlockSpec((None, wt.shape[1], Q_BLOCK), lambda b, i: (b, 0, i)), blk(nh),
                  full(1), full(nkv), pl.BlockSpec((None,) + vt16.shape[1:], lambda b, i: (b, 0, 0, 0))],
        out_specs=blk(nh),
        out_shape=jax.ShapeDtypeStruct((bsz, t, nh * LANES), BF16),
        scratch_shapes=[pltpu.VMEM((t, Q_BLOCK), I32), pltpu.VMEM((t, Q_BLOCK), F32),
                        pltpu.VMEM((nkv, LANES, rep * Q_BLOCK), F32)],
        name="dsa_prompt_attend",
        compiler_params=_cparams("parallel", "arbitrary"),
    )(qi16, wt, q16, ki16, k16, vt16)


def _dsa_sample_score_kernel(pt_ref, qi_ref, wc_ref, *refs, n_steps, pps, nih, tq, past, idx_scale):
    page_refs, new_ref, o_ref = refs[:pps], refs[pps], refs[pps + 1]
    p = pl.program_id(1)
    keys = jnp.concatenate([r[...] for r in page_refs], axis=0).astype(BF16)
    keys = jnp.where(p == n_steps - 1, new_ref[...], keys)
    qi = jnp.concatenate([qi_ref[:, h * LANES:(h + 1) * LANES] for h in range(nih)], axis=0)
    w = jnp.maximum(_dot_nt(qi, keys), 0.0) * wc_ref[...]
    acc = w[0:tq]
    for h in range(1, nih):
        acc = acc + w[h * tq:(h + 1) * tq]
    s = p * keys.shape[0] + lax.broadcasted_iota(I32, acc.shape, 1)
    qpos = past + lax.broadcasted_iota(I32, acc.shape, 0)
    o_ref[...] = jnp.where(s <= qpos, acc * idx_scale, NEG_INF)


def _dsa_sample_select_kernel(s_ref, o_ref, *, topk):
    key = _sort_key(s_ref[...])
    idx = lax.broadcasted_iota(I32, key.shape, 1)
    count_where = lambda pred: jnp.sum(jnp.where(pred(key, idx), 1, 0), axis=-1, keepdims=True)
    thr, jcut = _topk_cut(count_where, (key.shape[0], 1), topk, (key.shape[1] - 1).bit_length())
    o_ref[...] = jnp.where(_topk_chosen(key, idx, thr, jcut) & (s_ref[...] > 0.5 * NEG_INF), 1.0, 0.0)


def _dsa_sample_attn_kernel(pt_ref, q_ref, sel_ref, *refs, n_steps, pps, nh, nkv, tq):
    kp_refs, vp_refs = refs[:pps], refs[pps:2 * pps]
    kn_ref, vn_ref, o_ref, m_ref, l_ref, acc_ref, hmask_ref, expand_ref = refs[2 * pps:]
    p = pl.program_id(1)
    rep = nh // nkv

    @pl.when(p == 0)
    def _():
        m_ref[...] = jnp.full_like(m_ref, NEG_INF)
        l_ref[...] = jnp.zeros_like(l_ref)
        acc_ref[...] = jnp.zeros_like(acc_ref)
        hrow = lax.broadcasted_iota(I32, hmask_ref.shape, 0) // (tq * rep)
        hcol = lax.broadcasted_iota(I32, hmask_ref.shape, 1) % nkv
        hmask_ref[...] = jnp.where(hrow == hcol, 0.0, NEG_INF)
        expand_ref[...] = (lax.broadcasted_iota(I32, expand_ref.shape, 1) // nkv
                           == lax.broadcasted_iota(I32, expand_ref.shape, 0)).astype(BF16)

    last = p == n_steps - 1
    kcat = jnp.where(last, kn_ref[...], jnp.concatenate([r[...] for r in kp_refs], axis=0).astype(BF16))
    vcat = jnp.where(last, vn_ref[...], jnp.concatenate([r[...] for r in vp_refs], axis=0).astype(BF16))
    q = jnp.concatenate([q_ref[:, h * LANES:(h + 1) * LANES] for h in range(nh)], axis=0)
    page = sel_ref.shape[1] // pps
    sel = sel_ref[...].astype(BF16)
    selx = jnp.concatenate([_dot(sel[:, i * page:(i + 1) * page], expand_ref[...]) for i in range(pps)], axis=1)
    unsel = (selx - 1.0) * -NEG_INF
    s = _dot_nt(q, kcat) + hmask_ref[...] + jnp.concatenate([unsel] * nh, axis=0)
    m = m_ref[...]
    m_new = jnp.maximum(m, jnp.max(s, axis=-1, keepdims=True))
    alpha = jnp.exp(m - m_new)
    pr = jnp.where(s > 0.5 * NEG_INF, jnp.exp(s - m_new), 0.0)
    l_ref[...] = alpha * l_ref[...] + jnp.sum(pr, axis=-1, keepdims=True)
    acc_ref[...] = alpha * acc_ref[...] + _dot(pr.astype(BF16), vcat)
    m_ref[...] = m_new

    @pl.when(last)
    def _():
        og = acc_ref[...] / l_ref[...]
        for h in range(nh):
            o_ref[:, h * LANES:(h + 1) * LANES] = og[h * tq:(h + 1) * tq].astype(BF16)


def _dsa_sample_attend(q16, qi16, wi, ki16, k16, v16, cache_k, cache_v, cache_i, layer, page_table,
                       *, nh, nkv, nih, topk):
    bsz, tq, _ = q16.shape
    n_pages = page_table.shape[1]
    n_layers, n_pool, page = cache_i.shape[:3]
    pps = math.gcd(n_pages, SAMPLE_PAGES_PER_STEP)
    assert page == LANES and tq == SUBLANES
    past = n_pages * page
    n_steps = n_pages // pps + 1
    ltot = n_steps * pps * page
    wcol = wi.transpose(0, 2, 1).reshape(bsz, nih * tq, 1)
    pidx = lambda i: (lambda b, p, pt: (layer * n_pool + pt[b, jnp.minimum(p * pps + i, n_pages - 1)], 0, 0))
    qspec = lambda n: pl.BlockSpec((None, tq, n * LANES), lambda b, p, pt: (b, 0, 0))
    idx_scale = (LANES * nih) ** -0.5
    cache_i = cache_i.reshape(n_layers * n_pool, page, LANES)
    ki_new = jnp.pad(ki16, ((0, 0), (0, pps * page - tq), (0, 0)))
    scores = pl.pallas_call(
        functools.partial(_dsa_sample_score_kernel, n_steps=n_steps, pps=pps, nih=nih, tq=tq, past=past,
                          idx_scale=idx_scale),
        grid_spec=pltpu.PrefetchScalarGridSpec(
            num_scalar_prefetch=1, grid=(bsz, n_steps),
            in_specs=[qspec(nih), pl.BlockSpec((None, nih * tq, 1), lambda b, p, pt: (b, 0, 0))]
            + [pl.BlockSpec((None, page, LANES), pidx(i)) for i in range(pps)]
            + [pl.BlockSpec((None, pps * page, LANES), lambda b, p, pt: (b, 0, 0))],
            out_specs=pl.BlockSpec((None, tq, pps * page), lambda b, p, pt: (b, 0, p))),
        out_shape=jax.ShapeDtypeStruct((bsz, tq, ltot), F32),
        name="dsa_sample_scores",
        compiler_params=_cparams("parallel", "arbitrary"),
    )(page_table, qi16, wcol, *([cache_i] * pps), ki_new)
    sel = pl.pallas_call(
        functools.partial(_dsa_sample_select_kernel, topk=topk),
        grid=(bsz,),
        in_specs=[pl.BlockSpec((None, tq, ltot), lambda b: (b, 0, 0))],
        out_specs=pl.BlockSpec((None, tq, ltot), lambda b: (b, 0, 0)),
        out_shape=jax.ShapeDtypeStruct((bsz, tq, ltot), F32),
        name="dsa_sample_select",
        compiler_params=_cparams("parallel"),
    )(scores)
    prow = page * nkv
    ck = cache_k.reshape(n_layers * n_pool, prow, LANES)
    cv = cache_v.reshape(n_layers * n_pool, prow, LANES)
    new_rows = lambda a: jnp.pad(a.reshape(bsz, tq * nkv, LANES), ((0, 0), (0, pps * prow - tq * nkv), (0, 0)))
    kvspecs = [pl.BlockSpec((None, prow, LANES), pidx(i)) for i in range(pps)]
    newspec = pl.BlockSpec((None, pps * prow, LANES), lambda b, p, pt: (b, 0, 0))
    return pl.pallas_call(
        functools.partial(_dsa_sample_attn_kernel, n_steps=n_steps, pps=pps, nh=nh, nkv=nkv, tq=tq),
        grid_spec=pltpu.PrefetchScalarGridSpec(
            num_scalar_prefetch=1, grid=(bsz, n_steps),
            in_specs=[qspec(nh), pl.BlockSpec((None, tq, pps * page), lambda b, p, pt: (b, 0, p))]
            + kvspecs + kvspecs + [newspec, newspec],
            out_specs=qspec(nh),
            scratch_shapes=[pltpu.VMEM((nh * tq, 1), F32), pltpu.VMEM((nh * tq, 1), F32),
                            pltpu.VMEM((nh * tq, LANES), F32), pltpu.VMEM((nh * tq, pps * prow), F32),
                            pltpu.VMEM((page, prow), BF16)]),
        out_shape=jax.ShapeDtypeStruct((bsz, tq, nh * LANES), BF16),
        name="dsa_sample_attend",
        compiler_params=_cparams("parallel", "arbitrary"),
    )(page_table, q16, sel, *([ck] * pps), *([cv] * pps), new_rows(k16), new_rows(v16))


def _dsa_mixer(x, mod3, nw, w_in, w_out, n_proj, *, nkv, tm, tm_in, tt, cache=None):
    bsz, t, d = x.shape
    nh = _wshape(w_out)[0] // LANES
    nih = (n_proj - (nh + 2 * nkv + 1) * LANES) // (LANES + 1)
    assert (nh + 2 * nkv + nih + 1) * LANES + nih == n_proj and nih <= LANES
    x2 = x.reshape(bsz * t, d)
    proj = _normproj(x2, mod3, 3, nw, w_in, tm=tm_in).reshape(bsz, t, -1)
    v = proj[:, :, (nh + nkv) * LANES:(nh + 2 * nkv) * LANES]
    if cache is None:
        assert t % Q_BLOCK == 0
        q16, k, k16, _, qi16, ki, ki16, wt, vt16 = _dsa_prep(proj, jnp.arange(t), nh=nh, nkv=nkv, nih=nih,
                                                             tt=_tile(t, 4 * Q_BLOCK, Q_BLOCK), with_wt=True)
        o = _dsa_prompt_attend(q16, qi16, wt, ki16, k16, vt16, nh=nh, nkv=nkv, nih=nih, topk=min(TOPK_MAX, t // 4))
    else:
        cache_k, cache_v, cache_i, layer, page_table = cache
        past = page_table.shape[1] * cache_i.shape[2]
        q16, k, k16, v16, qi16, ki, ki16 = _dsa_prep(proj, past + jnp.arange(t), nh=nh, nkv=nkv, nih=nih, tt=tt,
                                                    with_wt=False)
        w0 = (nh + 2 * nkv + nih + 1) * LANES
        o = _dsa_sample_attend(q16, qi16, proj[:, :, w0:w0 + nih], ki16, k16, v16, cache_k, cache_v, cache_i,
                               layer, page_table, nh=nh, nkv=nkv, nih=nih, topk=min(TOPK_MAX, (past + t) // 4))
    xo = _outproj(o.reshape(bsz * t, nh * LANES), w_out, x2, mod3, 5, tm=tm).reshape(bsz, t, d)
    return xo, k.reshape(bsz, t, nkv, LANES), v.reshape(bsz, t, nkv, LANES), ki


def _gelu_tanh(x):
    return 0.5 * x * (1.0 + jnp.tanh(math.sqrt(2.0 / math.pi) * (x + 0.044715 * (x * x * x))))


def _lru_kernel(gate_ref, xb_ref, halo_ref, buf_ref, cw_ref, cb_ref, wa_ref, ba_ref, wx_ref, bx_ref, lam_ref,
                h0_ref, y_ref, hl_ref, h_ref, *, nblk):
    ti = pl.program_id(1)

    @pl.when(ti == 0)
    def _():
        h_ref[...] = h0_ref[...]

    halo = jnp.where(ti == 0, buf_ref[...], halo_ref[...])
    xc = _causal_conv(xb_ref[...], halo, cw_ref[...]) + cb_ref[...]
    tt = xc.shape[0]
    xc16 = xc.astype(BF16)
    rs, xs = [], []
    for n in range(nblk):
        blk = xc16[:, n * LRU_BLOCK:(n + 1) * LRU_BLOCK]
        rs.append(_dot(blk, wa_ref[n]))
        xs.append(_dot(blk, wx_ref[n]))
    r = _sigmoid(jnp.concatenate(rs, axis=1) + ba_ref[...])
    ig = _sigmoid(jnp.concatenate(xs, axis=1) + bx_ref[...])
    lam = lam_ref[...]
    softplus_neg = jnp.maximum(-lam, 0.0) + jnp.log1p(jnp.exp(-jnp.abs(lam)))
    log_a = -RG_C * r * softplus_neg
    a = jnp.exp(log_a)
    b = jnp.sqrt(-jnp.tanh(log_a) * (a * a + 1.0)) * (ig * xc)
    row = lax.broadcasted_iota(I32, a.shape, 0)
    d = 1
    while d < tt:
        keep = row >= d
        a_sh = jnp.where(keep, pltpu.roll(a, d, 0), 1.0)
        b_sh = jnp.where(keep, pltpu.roll(b, d, 0), 0.0)
        b = a * b_sh + b
        a = a * a_sh
        d *= 2
    hs = b + a * h_ref[...]
    h_ref[...] = hs[tt - 1:tt, :]
    y_ref[...] = (hs * _gelu_tanh(gate_ref[...])).astype(BF16)

    @pl.when(ti == pl.num_programs(1) - 1)
    def _():
        hl_ref[...] = hs[tt - 1:tt, :]


def _lru_core(proj, buf8, h0, conv_w, conv_b, w_ga, b_ga, w_gx, b_gx, lam, *, tt):
    bsz, t, w2 = proj.shape
    w = w2 // 2
    nblk = w // LRU_BLOCK
    sub = tt // SUBLANES
    vec = lambda: pl.BlockSpec((1, w), lambda b, i: (0, 0))
    wsp = lambda: pl.BlockSpec((nblk, LRU_BLOCK, LRU_BLOCK), lambda b, i: (0, 0, 0))
    r1 = lambda v: v.reshape(1, w).astype(F32)
    y, hl = pl.pallas_call(
        functools.partial(_lru_kernel, nblk=nblk),
        grid=(bsz, t // tt),
        in_specs=[
            pl.BlockSpec((None, tt, w), lambda b, i: (b, i, 0)),
            pl.BlockSpec((None, tt, w), lambda b, i: (b, i, 1)),
            pl.BlockSpec((None, SUBLANES, w), lambda b, i: (b, jnp.maximum(i * sub - 1, 0), 1)),
            pl.BlockSpec((None, SUBLANES, w), lambda b, i: (b, 0, 0)),
            pl.BlockSpec((CONV_W, w), lambda b, i: (0, 0)),
            vec(), wsp(), vec(), wsp(), vec(), vec(),
            pl.BlockSpec((None, 1, w), lambda b, i: (b, 0, 0)),
        ],
        out_specs=[pl.BlockSpec((None, tt, w), lambda b, i: (b, i, 0)),
                   pl.BlockSpec((None, 1, w), lambda b, i: (b, 0, 0))],
        out_shape=[jax.ShapeDtypeStruct((bsz, t, w), BF16), jax.ShapeDtypeStruct((bsz, 1, w), F32)],
        scratch_shapes=[pltpu.VMEM((1, w), F32)],
        name="lru_core",
        compiler_params=_cparams("parallel", "arbitrary"),
    )(proj, proj, proj, buf8, conv_w, r1(conv_b), w_ga, r1(b_ga), w_gx, r1(b_gx), r1(lam), h0.reshape(bsz, 1, w))
    return y, hl.reshape(bsz, w)


def _lru_mixer(x, mod3, nw, buf, h0, w_in, conv_w, conv_b, w_ga, b_ga, w_gx, b_gx, lam, w_out, *, tm, tm_in, tt):
    bsz, t, d = x.shape
    assert t >= CONV_W - 1
    x2 = x.reshape(bsz * t, d)
    proj = _normproj(x2, mod3, 3, nw, w_in, tm=tm_in).reshape(bsz, t, -1)
    w = proj.shape[2] // 2
    y, hl = _lru_core(proj, _pad_buf(buf), h0, conv_w, conv_b, w_ga, b_ga, w_gx, b_gx, lam, tt=tt)
    xo = _outproj(y.reshape(bsz * t, w), w_out, x2, mod3, 5, tm=tm).reshape(bsz, t, d)
    return xo, proj[:, t - (CONV_W - 1):, w:], hl


def _bf16_padded(w, mult):
    n = w.shape[-1]
    return jnp.pad(w, ((0, 0), (0, 0), (0, -n % mult))).astype(BF16)


def kernel(x_prompt, x_sample, state_a_conv, state_a_ssm, cache_b_k, cache_b_v, cache_b_idx, state_c_conv, state_c_h, page_table, c_prompt, c_sample, w_ada, b_ada, norm_w, ffn_w_in, ffn_w_out, gdn_w_in, gdn_conv_w, gdn_a_log, gdn_dt_bias, gdn_norm_w, gdn_w_out, dsa_w_in, dsa_w_out, lru_w_in, lru_conv_w, lru_conv_b, lru_w_gate_a, lru_b_gate_a, lru_w_gate_x, lru_b_gate_x, lru_lambda, lru_w_out, w_ada_final, b_ada_final, final_norm_w):
    bp, seq, d = x_prompt.shape
    bs, ts, _ = x_sample.shape
    depth = w_ada.shape[0]
    nkv = cache_b_k.shape[3]
    tm_p = _tile(seq, 512, SUBLANES)
    tm_in_p = _tile(seq, 1024, SUBLANES)
    tt_p = _tile(seq, 256, SUBLANES)
    tm_s = bs * ts
    tiles_p = dict(tm=tm_p, tm_in=tm_in_p, tt=tt_p)
    tiles_s = dict(tm=tm_s, tm_in=tm_s, tt=ts)

    c_all = jnp.concatenate([c_prompt, c_sample], axis=0)
    c_all = jnp.pad(c_all, ((0, -c_all.shape[0] % SUBLANES), (0, 0)))
    mod = _ada(c_all, w_ada, b_ada)
    mod_f = _ada(c_all, w_ada_final[None], b_ada_final[None])[0]

    def groups(m):
        return m[:bp, None, :], jnp.repeat(m[bp:bp + bs], ts, axis=0)[None]

    ffn_w_in16, ffn_w_out16 = ffn_w_in.astype(BF16), ffn_w_out.astype(BF16)
    gdn_w_in16, gdn_w_out16 = _bf16_padded(gdn_w_in, 512), gdn_w_out.astype(BF16)
    dsa_w_in16, dsa_w_out16 = _bf16_padded(dsa_w_in, 512), dsa_w_out.astype(BF16)
    lru_w_in16, lru_w_out16 = lru_w_in.astype(BF16), lru_w_out.astype(BF16)

    xp, xs = x_prompt, x_sample
    outs = {k: [] for k in ("a_conv_p", "a_conv_s", "a_ssm_p", "a_ssm_s", "b_k_p", "b_k_s", "b_v_p", "b_v_s",
                            "b_i_p", "b_i_s", "c_conv_p", "c_conv_s", "c_h_p", "c_h_s")}
    for layer in range(depth):
        kind, j = layer % N_MIXERS, layer // N_MIXERS
        mod_p, mod_s = groups(mod[layer])
        nw = norm_w[layer]

        def ffn(x, m3, tm, tf, which):
            b, t, _ = x.shape
            return _ffn(x.reshape(b * t, d), m3, 6 * which, nw[2 * which:2 * which + 1],
                        (ffn_w_in16, (layer, which)), (ffn_w_out16, (layer, which)), tm=tm,
                        tf_target=tf).reshape(b, t, d)

        xp = ffn(xp, mod_p, tm_in_p, 256, 0)
        xs = ffn(xs, mod_s, tm_s, 512, 0)
        if kind == 0:
            prm = ((gdn_w_in16, (j,)), gdn_conv_w[j], gdn_a_log[j], gdn_dt_bias[j], gdn_norm_w[j],
                   (gdn_w_out16, (j,)))
            buf0 = jnp.zeros((bp,) + state_a_conv.shape[2:], F32)
            s0 = jnp.zeros((bp,) + state_a_ssm.shape[2:], F32)
            xp, buf, s = _gdn_mixer(xp, mod_p, nw[1:2], buf0, s0, *prm, chunk=2 * GDN_CHUNK, **tiles_p)
            outs["a_conv_p"].append(buf)
            outs["a_ssm_p"].append(s)
            xs, buf, s = _gdn_mixer(xs, mod_s, nw[1:2], state_a_conv[j], state_a_ssm[j], *prm, chunk=GDN_CHUNK,
                                    **tiles_s)
            outs["a_conv_s"].append(buf)
            outs["a_ssm_s"].append(s)
        elif kind == 1:
            w_in, w_out = (dsa_w_in16, (j,)), (dsa_w_out16, (j,))
            n_proj = dsa_w_in.shape[2]
            xp, k, v, ki = _dsa_mixer(xp, mod_p, nw[1:2], w_in, w_out, n_proj, nkv=nkv, **tiles_p)
            outs["b_k_p"].append(k)
            outs["b_v_p"].append(v)
            outs["b_i_p"].append(ki)
            xs, k, v, ki = _dsa_mixer(xs, mod_s, nw[1:2], w_in, w_out, n_proj, nkv=nkv,
                                      cache=(cache_b_k, cache_b_v, cache_b_idx, j, page_table), **tiles_s)
            outs["b_k_s"].append(k)
            outs["b_v_s"].append(v)
            outs["b_i_s"].append(ki)
        else:
            prm = ((lru_w_in16, (j,)), lru_conv_w[j], lru_conv_b[j], lru_w_gate_a[j].astype(BF16),
                   lru_b_gate_a[j], lru_w_gate_x[j].astype(BF16), lru_b_gate_x[j], lru_lambda[j],
                   (lru_w_out16, (j,)))
            buf0 = jnp.zeros((bp,) + state_c_conv.shape[2:], F32)
            h0 = jnp.zeros((bp,) + state_c_h.shape[2:], F32)
            xp, buf, hl = _lru_mixer(xp, mod_p, nw[1:2], buf0, h0, *prm, **tiles_p)
            outs["c_conv_p"].append(buf)
            outs["c_h_p"].append(hl)
            xs, buf, hl = _lru_mixer(xs, mod_s, nw[1:2], state_c_conv[j], state_c_h[j], *prm, **tiles_s)
            outs["c_conv_s"].append(buf)
            outs["c_h_s"].append(hl)
        xp = ffn(xp, mod_p, tm_in_p, 256, 1)
        xs = ffn(xs, mod_s, tm_s, 512, 1)

    modf_p, modf_s = groups(mod_f)
    fnw = final_norm_w.reshape(1, d)
    y_p = _final_norm(xp.reshape(bp * seq, d), modf_p, fnw, tm=tm_p).reshape(bp, seq, d)
    y_s = _final_norm(xs.reshape(bs * ts, d), modf_s, fnw, tm=tm_s).reshape(bs, ts, d)
    st = {k: jnp.stack(v) for k, v in outs.items()}
    return (y_p, y_s, st["a_conv_p"], st["a_conv_s"], st["a_ssm_p"], st["a_ssm_s"], st["b_k_p"], st["b_k_s"],
            st["b_v_p"], st["b_v_s"], st["b_i_p"], st["b_i_s"], st["c_conv_p"], st["c_conv_s"], st["c_h_p"],
            st["c_h_s"])
```

```python
import functools
import math

import jax
import jax.numpy as jnp
import numpy as np
from jax import lax
from jax.experimental import pallas as pl
from jax.experimental.pallas import tpu as pltpu

F32 = jnp.float32
BF16 = jnp.bfloat16
I32 = jnp.int32

N_MIXERS = 3
N_ADA = 9
CONV_W = 4
NORM_EPS = 1e-6
NEG_INF = -1e30
GDN_CHUNK = 64
TOPK_MAX = 256
Q_BLOCK = 128
ROPE_THETA = 10000.0
RG_C = 8.0
LRU_BLOCK = 256
SAMPLE_PAGES_PER_STEP = 4

LANES = 128
SUBLANES = 8
VMEM_LIMIT_BYTES = 56 * 2**20
INT32_MIN = -2**31
INT32_MAX = 2**31 - 1
KEY_OF_NEG_INF = int(np.float32(NEG_INF).view(np.int32)) ^ 0x7FFFFFFF


def _cparams(*sem):
    return pltpu.CompilerParams(dimension_semantics=sem, vmem_limit_bytes=VMEM_LIMIT_BYTES)


def _tile(n, target, align=LANES):
    if n <= target:
        return n
    t = (target // align) * align
    while t >= align:
        if n % t == 0:
            return t
        t -= align
    raise ValueError(f"no {align}-aligned tile of {n} below {target}")


def _sigmoid(x):
    return jax.nn.sigmoid(x)


def _silu(x):
    return x * _sigmoid(x)


def _dot(a, b):
    return jnp.dot(a, b, preferred_element_type=F32)


def _dot_nt(a, b):
    return lax.dot_general(a, b, (((1,), (1,)), ((), ())), preferred_element_type=F32)


def _dot_tn(a, b):
    return lax.dot_general(a, b, (((0,), (0,)), ((), ())), preferred_element_type=F32)


def _dot_f32(a, b):
    return jnp.dot(a, b, preferred_element_type=F32, precision=lax.Precision.HIGHEST)


def _norm_mod(x, nw, sh, sc):
    ms = jnp.mean(x * x, axis=-1, keepdims=True)
    return x * lax.rsqrt(ms + NORM_EPS) * (nw * (1.0 + sc)) + sh


def _ada_kernel(c_ref, w_ref, b_ref, o_ref):
    a = _silu(c_ref[...]).astype(BF16)
    o_ref[0] = _dot(a, w_ref[0].astype(BF16)) + b_ref[0]


def _ada(c_all, w, b):
    n_l, d, n = w.shape
    mp = c_all.shape[0]
    tn = _tile(n, 1024)
    return pl.pallas_call(
        _ada_kernel,
        grid=(n_l, n // tn),
        in_specs=[
            pl.BlockSpec((mp, d), lambda l, j: (0, 0)),
            pl.BlockSpec((1, d, tn), lambda l, j: (l, 0, j)),
            pl.BlockSpec((1, 1, tn), lambda l, j: (l, 0, j)),
        ],
        out_specs=pl.BlockSpec((1, mp, tn), lambda l, j: (l, 0, j)),
        out_shape=jax.ShapeDtypeStruct((n_l, mp, n), F32),
        name="ada_mod",
        compiler_params=_cparams("parallel", "parallel"),
    )(c_all, w, b.reshape(n_l, 1, n))


def _mod_spec(mod3, k, d, tm, rows_per_group):
    return pl.BlockSpec((None, mod3.shape[1], d), lambda i, j: ((i * tm) // rows_per_group, 0, k))


def _wspec(w, block, imap):
    arr, idx = w
    return pl.BlockSpec((None,) * len(idx) + block, lambda i, j: idx + imap(i, j))


def _wshape(w):
    return w[0].shape[len(w[1]):]


def _ffn_kernel(x_ref, nw_ref, sh_ref, sc_ref, g_ref, wa_ref, wb_ref, wo_ref, o_ref, xn_ref):
    f = pl.program_id(1)

    @pl.when(f == 0)
    def _():
        xn_ref[...] = _norm_mod(x_ref[...], nw_ref[...], sh_ref[...], sc_ref[...]).astype(BF16)
        o_ref[...] = jnp.zeros_like(o_ref)

    xn = xn_ref[...]
    a = _dot(xn, wa_ref[...])
    b = _dot(xn, wb_ref[...])
    o_ref[...] += _dot((_silu(a) * b).astype(BF16), wo_ref[...])

    @pl.when(f == pl.num_programs(1) - 1)
    def _():
        o_ref[...] = x_ref[...] + 0.5 * g_ref[...] * o_ref[...]


def _ffn(x, mod3, k0, nw, w_in, w_out, *, tm, tf_target):
    m, d = x.shape
    f = _wshape(w_out)[0]
    tf = _tile(f, tf_target)
    nf = f // tf
    rpg = m // mod3.shape[0]
    ms = lambda k: _mod_spec(mod3, k, d, tm, rpg)
    return pl.pallas_call(
        _ffn_kernel,
        grid=(m // tm, nf),
        in_specs=[
            pl.BlockSpec((tm, d), lambda i, j: (i, 0)),
            pl.BlockSpec((1, d), lambda i, j: (0, 0)),
            ms(k0), ms(k0 + 1), ms(k0 + 2),
            _wspec(w_in, (d, tf), lambda i, j: (0, j)),
            _wspec(w_in, (d, tf), lambda i, j: (0, j + nf)),
            _wspec(w_out, (tf, d), lambda i, j: (j, 0)),
        ],
        out_specs=pl.BlockSpec((tm, d), lambda i, j: (i, 0)),
        out_shape=jax.ShapeDtypeStruct((m, d), F32),
        scratch_shapes=[pltpu.VMEM((tm, d), BF16)],
        name="ffn",
        compiler_params=_cparams("parallel", "arbitrary"),
    )(x, nw, mod3, mod3, mod3, w_in[0], w_in[0], w_out[0])


def _normproj_kernel(x_ref, nw_ref, sh_ref, sc_ref, w_ref, o_ref, xn_ref):
    @pl.when(pl.program_id(1) == 0)
    def _():
        xn_ref[...] = _norm_mod(x_ref[...], nw_ref[...], sh_ref[...], sc_ref[...]).astype(BF16)

    o_ref[...] = _dot(xn_ref[...], w_ref[...])


def _normproj(x, mod3, k0, nw, w, *, tm, tn_target=1024):
    m, d = x.shape
    n = _wshape(w)[1]
    tn = _tile(n, tn_target)
    rpg = m // mod3.shape[0]
    ms = lambda k: _mod_spec(mod3, k, d, tm, rpg)
    return pl.pallas_call(
        _normproj_kernel,
        grid=(m // tm, n // tn),
        in_specs=[
            pl.BlockSpec((tm, d), lambda i, j: (i, 0)),
            pl.BlockSpec((1, d), lambda i, j: (0, 0)),
            ms(k0), ms(k0 + 1),
            _wspec(w, (d, tn), lambda i, j: (0, j)),
        ],
        out_specs=pl.BlockSpec((tm, tn), lambda i, j: (i, j)),
        out_shape=jax.ShapeDtypeStruct((m, n), F32),
        scratch_shapes=[pltpu.VMEM((tm, d), BF16)],
        name="normproj",
        compiler_params=_cparams("parallel", "arbitrary"),
    )(x, nw, mod3, mod3, w[0])


def _ffn_up_kernel(x_ref, nw_ref, sh_ref, sc_ref, wa_ref, wb_ref, h_ref, xn_ref):
    @pl.when(pl.program_id(1) == 0)
    def _():
        xn_ref[...] = _norm_mod(x_ref[...], nw_ref[...], sh_ref[...], sc_ref[...]).astype(BF16)

    xn = xn_ref[...]
    a = _dot(xn, wa_ref[...].astype(BF16))
    b = _dot(xn, wb_ref[...].astype(BF16))
    h_ref[...] = (_silu(a) * b).astype(BF16)


def _ffn_up(x, mod3, k0, nw, w_in, *, tm, tf_target=512):
    m, d = x.shape
    f = _wshape(w_in)[1] // 2
    tf = _tile(f, tf_target)
    nf = f // tf
    rpg = m // mod3.shape[0]
    ms = lambda k: _mod_spec(mod3, k, d, tm, rpg)
    return pl.pallas_call(
        _ffn_up_kernel,
        grid=(m // tm, nf),
        in_specs=[
            pl.BlockSpec((tm, d), lambda i, j: (i, 0)),
            pl.BlockSpec((1, d), lambda i, j: (0, 0)),
            ms(k0), ms(k0 + 1),
            _wspec(w_in, (d, tf), lambda i, j: (0, j)),
            _wspec(w_in, (d, tf), lambda i, j: (0, j + nf)),
        ],
        out_specs=pl.BlockSpec((tm, tf), lambda i, j: (i, j)),
        out_shape=jax.ShapeDtypeStruct((m, f), BF16),
        scratch_shapes=[pltpu.VMEM((tm, d), BF16)],
        name="ffn_up",
        compiler_params=_cparams("parallel", "arbitrary"),
    )(x, nw, mod3, mod3, w_in[0], w_in[0])


def _outproj_kernel(a_ref, w_ref, x_ref, g_ref, o_ref, *, gain):
    o_ref[...] = x_ref[...] + gain * g_ref[...] * _dot(a_ref[...], w_ref[...])


def _outproj(a, w, x, mod3, kg, *, tm, tn_target=1024, gain=1.0):
    m, kdim = a.shape
    d = _wshape(w)[1]
    tn = _tile(d, tn_target)
    rpg = m // mod3.shape[0]
    r = mod3.shape[1]
    nd = d // tn
    return pl.pallas_call(
        functools.partial(_outproj_kernel, gain=gain),
        grid=(m // tm, nd),
        in_specs=[
            pl.BlockSpec((tm, kdim), lambda i, j: (i, 0)),
            _wspec(w, (kdim, tn), lambda i, j: (0, j)),
            pl.BlockSpec((tm, tn), lambda i, j: (i, j)),
            pl.BlockSpec((None, r, tn), lambda i, j: ((i * tm) // rpg, 0, kg * nd + j)),
        ],
        out_specs=pl.BlockSpec((tm, tn), lambda i, j: (i, j)),
        out_shape=jax.ShapeDtypeStruct((m, d), F32),
        name="outproj",
        compiler_params=_cparams("parallel", "arbitrary"),
    )(a, w[0], x, mod3)


def _final_kernel(x_ref, nw_ref, sh_ref, sc_ref, o_ref):
    o_ref[...] = _norm_mod(x_ref[...], nw_ref[...], sh_ref[...], sc_ref[...])


def _final_norm(x, mod3, nw, *, tm):
    m, d = x.shape
    rpg = m // mod3.shape[0]
    r = mod3.shape[1]
    ms = lambda k: pl.BlockSpec((None, r, d), lambda i: ((i * tm) // rpg, 0, k))
    return pl.pallas_call(
        _final_kernel,
        grid=(m // tm,),
        in_specs=[pl.BlockSpec((tm, d), lambda i: (i, 0)), pl.BlockSpec((1, d), lambda i: (0, 0)), ms(0), ms(1)],
        out_specs=pl.BlockSpec((tm, d), lambda i: (i, 0)),
        out_shape=jax.ShapeDtypeStruct((m, d), F32),
        name="final_norm",
        compiler_params=_cparams("parallel"),
    )(x, nw, mod3, mod3)


def _causal_conv(x, halo, w):
    def taps(rows, fix):
        acc = rows * w[CONV_W - 1:CONV_W]
        for j in range(1, CONV_W):
            acc = acc + fix(pltpu.roll(rows, j, 0), j) * w[CONV_W - 1 - j:CONV_W - j]
        return acc

    row = lax.broadcasted_iota(I32, halo.shape, 0)
    head = taps(x[:SUBLANES], lambda r, j: jnp.where(row < j, pltpu.roll(halo, j, 0), r))
    if x.shape[0] == SUBLANES:
        return head
    return jnp.concatenate([head, taps(x, lambda r, j: r)[SUBLANES:]], axis=0)


def _conv_specs(tt, tc, coff, boff):
    sub = tt // SUBLANES
    return [
        pl.BlockSpec((None, tt, tc), lambda b, t, c: (b, t, c + coff)),
        pl.BlockSpec((None, SUBLANES, tc), lambda b, t, c: (b, jnp.maximum(t * sub - 1, 0), c + coff)),
        pl.BlockSpec((None, SUBLANES, tc), lambda b, t, c: (b, 0, c + boff)),
    ]


def _pad_buf(buf):
    return jnp.pad(buf, ((0, 0), (SUBLANES - (CONV_W - 1), 0), (0, 0)))


def _gdn_prep_kernel(x_ref, halo_ref, buf_ref, w_ref, o_ref, *, norm):
    halo = jnp.where(pl.program_id(1) == 0, buf_ref[...], halo_ref[...])
    y = _silu(_causal_conv(x_ref[...], halo, w_ref[...]))
    if norm:
        for h in range(y.shape[1] // LANES):
            seg = y[:, h * LANES:(h + 1) * LANES]
            ss = jnp.sum(seg * seg, axis=-1, keepdims=True)
            o_ref[:, h * LANES:(h + 1) * LANES] = seg * lax.rsqrt(ss + NORM_EPS)
    else:
        o_ref[...] = y


def _gdn_prep(proj, buf8, conv_w, *, col0, ncols, norm, tt):
    bsz, t, _ = proj.shape
    tc = _tile(ncols, 1024)
    return pl.pallas_call(
        functools.partial(_gdn_prep_kernel, norm=norm),
        grid=(bsz, t // tt, ncols // tc),
        in_specs=_conv_specs(tt, tc, col0 // tc, col0 // tc)
        + [pl.BlockSpec((CONV_W, tc), lambda b, i, c: (0, c + col0 // tc))],
        out_specs=pl.BlockSpec((None, tt, tc), lambda b, i, c: (b, i, c)),
        out_shape=jax.ShapeDtypeStruct((bsz, t, ncols), F32),
        name="gdn_prep",
        compiler_params=_cparams("parallel", "parallel", "parallel"),
    )(proj, proj, buf8, conv_w)


def _gdn_gate_kernel(x_ref, alog_ref, dtb_ref, beta_ref, gc_ref, *, hv, chunk):
    x = x_ref[...]
    tt = x.shape[0]
    beta_ref[...] = _sigmoid(x)
    z = x + dtb_ref[...]
    g = -jnp.exp(alog_ref[...]) * (jnp.maximum(z, 0.0) + jnp.log1p(jnp.exp(-jnp.abs(z))))
    row = lax.broadcasted_iota(I32, (tt, tt), 0)
    col = lax.broadcasted_iota(I32, (tt, tt), 1)
    tri = jnp.where((row >= col) & (row // chunk == col // chunk), 1.0, 0.0)
    gc_ref[...] = _dot_f32(tri, g)


def _gdn_gates(proj, a_log, dt_bias, *, col0, hv, chunk, tt):
    bsz, t, _ = proj.shape
    pad = lambda v: jnp.pad(v.astype(F32), (hv, LANES - 2 * hv)).reshape(1, LANES)
    blk = pl.BlockSpec((None, tt, LANES), lambda b, i: (b, i, col0 // LANES))
    out = pl.BlockSpec((None, tt, LANES), lambda b, i: (b, i, 0))
    par = pl.BlockSpec((1, LANES), lambda b, i: (0, 0))
    return pl.pallas_call(
        functools.partial(_gdn_gate_kernel, hv=hv, chunk=chunk),
        grid=(bsz, t // tt),
        in_specs=[blk, par, par],
        out_specs=[out, out],
        out_shape=[jax.ShapeDtypeStruct((bsz, t, LANES), F32)] * 2,
        name="gdn_gates",
        compiler_params=_cparams("parallel", "parallel"),
    )(proj, pad(a_log), pad(dt_bias))


def _split_bf16(a):
    hi = a.astype(BF16)
    return hi, (a - hi.astype(F32)).astype(BF16)


def _dot_split(a, b):
    ah, al = _split_bf16(a)
    bh, bl = _split_bf16(b)
    return _dot(jnp.concatenate([ah, al, ah], axis=1), jnp.concatenate([bh, bh, bl], axis=0))


def _dot_bf16(a, b):
    return _dot(a.astype(BF16), b.astype(BF16))


def _tri_inv_all(lmats, c, n_real):
    row = lax.broadcasted_iota(I32, (c, c), 0)
    col = lax.broadcasted_iota(I32, (c, c), 1)
    eye = jnp.where(row == col, 1.0, 0.0)
    base = min(16, c)
    ps = [-jnp.where(row // base == col // base, m, 0.0) for m in lmats]
    rs = [eye + p for p in ps]
    n = 2
    while n < base:
        ps = [_dot_bf16(p, p) for p in ps]
        rs = [r + _dot_bf16(r, p) for r, p in zip(rs, ps)]
        n *= 2
    s = base
    while s < min(c, n_real):
        off = (row // (2 * s) == col // (2 * s)) & (row // s != col // s)
        ts = [_dot_bf16(jnp.where(off, m, 0.0), r) for m, r in zip(lmats, rs)]
        rs = [r - _dot_bf16(r, t) for r, t in zip(rs, ts)]
        s *= 2
    res = [eye - r - _dot_split(m, r) for m, r in zip(lmats, rs)]
    return [r + _dot_bf16(r, e) for r, e in zip(rs, res)]


def _gdn_core_kernel(q_ref, k_ref, v_ref, z_ref, gcc_ref, gcr_ref, bc_ref, s0_ref, nw_ref, o_ref, so_ref, s_ref,
                     *, rep, c, ncb, hpb, dk, n_real):
    ci = pl.program_id(2)

    @pl.when(ci == 0)
    def _():
        s_ref[...] = s0_ref[...]

    row = lax.broadcasted_iota(I32, (c, c), 0)
    col = lax.broadcasted_iota(I32, (c, c), 1)
    causal = row >= col
    nhd = hpb * rep
    kinst = [(n, hh) for n in range(ncb) for hh in range(hpb)]
    inst = [(n, hh, r) for n in range(ncb) for hh in range(hpb) for r in range(rep)]
    rows = lambda n: slice(n * c, (n + 1) * c)
    lanes = lambda hh, r: slice((hh * rep + r) * LANES, (hh * rep + r + 1) * LANES)
    kidx = lambda n, hh: n * hpb + hh
    qs = [q_ref[rows(n), hh * LANES:(hh + 1) * LANES] * dk ** -0.5 for n, hh in kinst]
    ks = [k_ref[rows(n), hh * LANES:(hh + 1) * LANES] for n, hh in kinst]
    k16 = [k.astype(BF16) for k in ks]
    grams = [_dot_nt(kb, kb) for kb in k16]
    qk0s = [_dot_nt(q.astype(BF16), kb) for q, kb in zip(qs, k16)]
    gccs = [gcc_ref[hh, rows(n), r:r + 1] for n, hh, r in inst]
    gcrs = [gcr_ref[hh, r:r + 1, rows(n)] for n, hh, r in inst]
    betas = [bc_ref[hh, rows(n), r:r + 1] for n, hh, r in inst]
    decays = [jnp.where(causal, jnp.exp(jnp.where(causal, gc - gr, 0.0)), 0.0) for gc, gr in zip(gccs, gcrs)]
    lowers = [jnp.where(row > col, grams[kidx(n, hh)] * b * d, 0.0)
              for (n, hh, _), b, d in zip(inst, betas, decays)]
    tinvs = _tri_inv_all(lowers, c, n_real)
    egcs = [jnp.exp(gc) for gc in gccs]
    sols = [_dot_split(ti, jnp.concatenate([v_ref[rows(n), lanes(hh, r)] * b, ks[kidx(n, hh)] * (b * e)], axis=1))
            for ti, b, e, (n, hh, r) in zip(tinvs, betas, egcs, inst)]
    qg16 = [(qs[kidx(n, hh)] * e).astype(BF16) for (n, hh, _), e in zip(inst, egcs)]
    qk16 = [(qk0s[kidx(n, hh)] * d).astype(BF16) for (n, hh, _), d in zip(inst, decays)]
    g_last = [gc[c - 1:c, :] for gc in gccs]
    kd16 = [(ks[kidx(n, hh)] * jnp.exp(gl - gc)).astype(BF16) for (n, hh, _), gl, gc in zip(inst, g_last, gccs)]
    ss = [s_ref[i] for i in range(nhd)]
    for n in range(ncb):
        ids = range(n * nhd, (n + 1) * nhd)
        s16 = [s.astype(BF16) for s in ss]
        u16 = [(sols[i][:, :LANES] - _dot(sols[i][:, LANES:].astype(BF16), sb)).astype(BF16)
               for i, sb in zip(ids, s16)]
        ss = [s * jnp.exp(g_last[i]) + _dot_tn(kd16[i], ub) for s, i, ub in zip(ss, ids, u16)]
        os_ = [_dot(qg16[i], sb) + _dot(qk16[i], ub) for i, sb, ub in zip(ids, s16, u16)]
        for o, i in zip(os_, ids):
            _, hh, r = inst[i]
            on = o * lax.rsqrt(jnp.mean(o * o, axis=-1, keepdims=True) + NORM_EPS) * nw_ref[...]
            o_ref[rows(n), lanes(hh, r)] = (on * _silu(z_ref[rows(n), lanes(hh, r)])).astype(BF16)
    for i in range(nhd):
        s_ref[i] = ss[i]

    @pl.when(ci == pl.num_programs(2) - 1)
    def _():
        so_ref[...] = s_ref[...]


def _gdn_core(qk, v, proj, zcol0, gc, beta, s0, norm_w, *, c, ncb, hpb, n_real):
    bsz, t, val = v.shape
    hv = s0.shape[1]
    dk, dv = s0.shape[2], s0.shape[3]
    hk = qk.shape[2] // (2 * dk)
    rep = hv // hk
    tb = ncb * c
    assert dk == LANES and dv == LANES and t % tb == 0 and hk % hpb == 0 and zcol0 % (hpb * rep * dv) == 0
    heads = lambda a, lo: a[:, :, lo:lo + hv].reshape(bsz, t, hk, rep).transpose(0, 2, 1, 3)
    gcc = heads(gc, hv)
    bcc = heads(beta, 0)
    gcr = gcc.transpose(0, 1, 3, 2)
    nhb = hk // hpb
    zb = zcol0 // (hpb * rep * dv)
    colspec = pl.BlockSpec((None, hpb, tb, rep), lambda b, h, i: (b, h, i, 0))
    o, s_out = pl.pallas_call(
        functools.partial(_gdn_core_kernel, rep=rep, c=c, ncb=ncb, hpb=hpb, dk=dk, n_real=n_real),
        grid=(bsz, nhb, t // tb),
        in_specs=[
            pl.BlockSpec((None, tb, hpb * dk), lambda b, h, i: (b, i, h)),
            pl.BlockSpec((None, tb, hpb * dk), lambda b, h, i: (b, i, nhb + h)),
            pl.BlockSpec((None, tb, hpb * rep * dv), lambda b, h, i: (b, i, h)),
            pl.BlockSpec((None, tb, hpb * rep * dv), lambda b, h, i: (b, i, zb + h)),
            colspec,
            pl.BlockSpec((None, hpb, rep, tb), lambda b, h, i: (b, h, 0, i)),
            colspec,
            pl.BlockSpec((None, hpb * rep, dk, dv), lambda b, h, i: (b, h, 0, 0)),
            pl.BlockSpec((1, dv), lambda b, h, i: (0, 0)),
        ],
        out_specs=[
            pl.BlockSpec((None, tb, hpb * rep * dv), lambda b, h, i: (b, i, h)),
            pl.BlockSpec((None, hpb * rep, dk, dv), lambda b, h, i: (b, h, 0, 0)),
        ],
        out_shape=[jax.ShapeDtypeStruct((bsz, t, val), BF16), jax.ShapeDtypeStruct(s0.shape, F32)],
        scratch_shapes=[pltpu.VMEM((hpb * rep, dk, dv), F32)],
        name="gdn_core",
        compiler_params=_cparams("parallel", "parallel", "arbitrary"),
    )(qk, qk, v, proj, gcc, gcr, bcc, s0, norm_w.reshape(1, dv))
    return o, s_out


def _gdn_mixer(x, mod3, nw, buf, s0, w_in, conv_w, a_log, dt_bias, norm_w, w_out, *, tm, tm_in, tt, chunk):
    bsz, t, d = x.shape
    hv, dk, dv = s0.shape[1], s0.shape[2], s0.shape[3]
    val = hv * dv
    conv_dim = conv_w.shape[1]
    key = (conv_dim - val) // 2
    assert (conv_dim + val) % LANES == 0 and 2 * hv <= LANES and t >= CONV_W - 1
    x2 = x.reshape(bsz * t, d)
    proj = _normproj(x2, mod3, 3, nw, w_in, tm=tm_in, tn_target=512).reshape(bsz, t, -1)
    buf8 = _pad_buf(buf)
    qk = _gdn_prep(proj, buf8, conv_w, col0=0, ncols=2 * key, norm=True, tt=tt)
    v = _gdn_prep(proj, buf8, conv_w, col0=2 * key, ncols=val, norm=False, tt=tt)
    tp = -(-t // chunk) * chunk
    beta, gc = _gdn_gates(proj, a_log, dt_bias, col0=conv_dim + val, hv=hv, chunk=min(chunk, tt), tt=tt)
    if tp != t:
        padt = lambda a: jnp.pad(a, ((0, 0), (0, tp - t), (0, 0)))
        gc = jnp.concatenate([gc, jnp.broadcast_to(gc[:, -1:], (bsz, tp - t, LANES))], axis=1)
        qk, v, beta, projz = padt(qk), padt(v), padt(beta), padt(proj)
    else:
        projz = proj
    o, s_new = _gdn_core(qk, v, projz, conv_dim, gc, beta, s0, norm_w, c=chunk, ncb=math.gcd(tp // chunk, 2),
                         hpb=min(4, key // dk), n_real=min(t, chunk))
    o2 = o[:, :t].reshape(bsz * t, val)
    xo = _outproj(o2, w_out, x2, mod3, 5, tm=tm).reshape(bsz, t, d)
    new_buf = proj[:, t - (CONV_W - 1):, :conv_dim]
    return xo, new_buf, s_new


def _rope_tables(pos, half):
    inv_freq = ROPE_THETA ** (-jnp.arange(half, dtype=F32) / half)
    ang = pos.astype(F32)[:, None] * inv_freq[None, :]
    cos, sin = jnp.cos(ang), jnp.sin(ang)
    return jnp.concatenate([cos, cos], axis=-1), jnp.concatenate([-sin, sin], axis=-1)


def _dsa_prep_kernel(x_ref, cos_ref, sin_ref, q_ref, k_ref, k16_ref, v16_ref, qi_ref, ki_ref, ki16_ref, *tr_refs,
                     nh, nkv, nih):
    cos, sin = cos_ref[...], sin_ref[...]

    def rope(col):
        seg = x_ref[:, col * LANES:(col + 1) * LANES]
        return seg * cos + pltpu.roll(seg, LANES // 2, 1) * sin

    for h in range(nh):
        q_ref[:, h * LANES:(h + 1) * LANES] = (rope(h) * LANES ** -0.5).astype(BF16)
    for h in range(nkv):
        kr = rope(nh + h)
        k_ref[:, h * LANES:(h + 1) * LANES] = kr
        k16_ref[:, h * LANES:(h + 1) * LANES] = kr.astype(BF16)
    v0 = (nh + nkv) * LANES
    v16_ref[...] = x_ref[:, v0:v0 + nkv * LANES].astype(BF16)
    c0 = nh + 2 * nkv
    for h in range(nih):
        qi_ref[:, h * LANES:(h + 1) * LANES] = rope(c0 + h).astype(BF16)
    kir = rope(c0 + nih)
    ki_ref[...] = kir
    ki16_ref[...] = kir.astype(BF16)
    if tr_refs:
        wt_ref, vt_ref = tr_refs
        w0 = (c0 + nih + 1) * LANES
        wt_ref[...] = x_ref[:, w0:w0 + LANES].T[:wt_ref.shape[0], :]
        for h in range(nkv):
            vt_ref[h * LANES:(h + 1) * LANES, :] = x_ref[:, v0 + h * LANES:v0 + (h + 1) * LANES].T.astype(BF16)


def _dsa_prep(proj, pos, *, nh, nkv, nih, tt, with_wt):
    bsz, t, npj = proj.shape
    cos, sin = _rope_tables(pos, LANES // 2)
    row = lambda n, dt: jax.ShapeDtypeStruct((bsz, t, n * LANES), dt)
    ospec = lambda n: pl.BlockSpec((None, tt, n * LANES), lambda b, i: (b, i, 0))
    tab = pl.BlockSpec((tt, LANES), lambda b, i: (i, 0))
    nwt = -(-nih // SUBLANES) * SUBLANES
    return pl.pallas_call(
        functools.partial(_dsa_prep_kernel, nh=nh, nkv=nkv, nih=nih),
        grid=(bsz, t // tt),
        in_specs=[pl.BlockSpec((None, tt, npj), lambda b, i: (b, i, 0)), tab, tab],
        out_specs=[ospec(nh), ospec(nkv), ospec(nkv), ospec(nkv), ospec(nih), ospec(1), ospec(1)]
        + ([pl.BlockSpec((None, nwt, tt), lambda b, i: (b, 0, i)),
            pl.BlockSpec((None, None, nkv * LANES, tt), lambda b, i: (b, i, 0, 0))] if with_wt else []),
        out_shape=[row(nh, BF16), row(nkv, F32), row(nkv, BF16), row(nkv, BF16), row(nih, BF16), row(1, F32),
                   row(1, BF16)]
        + ([jax.ShapeDtypeStruct((bsz, nwt, t), F32),
            jax.ShapeDtypeStruct((bsz, t // tt, nkv * LANES, tt), BF16)] if with_wt else []),
        name="dsa_prep",
        compiler_params=_cparams("parallel", "parallel"),
    )(proj, cos, sin)


def _sort_key(s):
    bits = pltpu.bitcast(jnp.where(s == 0.0, 0.0, s), I32)
    return jnp.where(bits < 0, bits ^ 0x7FFFFFFF, bits)


def _topk_cut(count_where, shape, topk, idx_bits):
    def body(i, carry):
        t, n_t = carry
        cand = t + lax.shift_left(jnp.int32(1), 31 - i)
        n = count_where(lambda key, idx: key >= cand)
        ok = n >= topk
        return jnp.where(ok, cand, t), jnp.where(ok, n, n_t)

    thr, n_ge = lax.fori_loop(0, 32, body, (jnp.full(shape, INT32_MIN, I32), jnp.full(shape, INT32_MAX, I32)))
    tie = (n_ge > topk) & (thr > KEY_OF_NEG_INF)

    def cut():
        need = topk - count_where(lambda key, idx: key > thr)

        def jbody(i, j):
            cand = j + lax.shift_left(jnp.int32(1), idx_bits - 1 - i)
            below = count_where(lambda key, idx: (key == thr) & (idx < cand))
            return jnp.where(below < need, cand, j)

        return jnp.where(tie, lax.fori_loop(0, idx_bits, jbody, jnp.zeros(shape, I32)), INT32_MAX)

    jcut = lax.cond(jnp.any(tie), cut, lambda: jnp.full(shape, INT32_MAX, I32))
    return thr, jcut


def _topk_chosen(key, idx, thr, jcut):
    return (key > thr) | ((key == thr) & (idx <= jcut))


def _dsa_prompt_kernel(qi_ref, wt_ref, q_ref, ki_ref, k_ref, vt_ref, o_ref, key_ref, bias_ref, acc_ref,
                       *, nh, nkv, nih, topk, idx_scale):
    qb = pl.program_id(1)
    blk = Q_BLOCK
    kb = vt_ref.shape[2]
    ktiles = kb // blk
    nkb = (qb + ktiles) // ktiles
    rowk = lax.broadcasted_iota(I32, (kb, blk), 0)
    colq = lax.broadcasted_iota(I32, (kb, blk), 1)
    wt = wt_ref[...]
    hq = max(1, nih // 4)
    qis = [jnp.concatenate([qi_ref[:, h * LANES:(h + 1) * LANES] for h in range(h0, min(h0 + hq, nih))], axis=0)
           for h0 in range(0, nih, hq)]

    def rows_of(j):
        return pl.ds(pl.multiple_of(j * kb, kb), kb)

    def visible(j):
        return j * kb + rowk <= qb * blk + colq

    def score_body(j, carry):
        keys = ki_ref[rows_of(j), :]
        lgs = [_dot_nt(keys, qi) for qi in qis]
        acc = jnp.zeros((kb, blk), F32)
        for i, lg in enumerate(lgs):
            for hh in range(lg.shape[1] // blk):
                h = i * hq + hh
                acc = acc + jnp.maximum(lg[:, hh * blk:(hh + 1) * blk], 0.0) * wt[h:h + 1, :]
        key_ref[rows_of(j), :] = _sort_key(jnp.where(visible(j), acc * idx_scale, NEG_INF))
        return carry

    lax.fori_loop(0, nkb, score_body, 0)

    def count_where(pred):
        def body(j, c):
            hit = jnp.where(pred(key_ref[rows_of(j), :], j * kb + rowk), 1, 0)
            for i in range(ktiles):
                c = c + hit[i * blk:(i + 1) * blk]
            return c
        cnt = lax.fori_loop(0, nkb, body, jnp.zeros((blk, blk), I32))
        return jnp.sum(cnt, axis=0, keepdims=True)

    thr, jcut = _topk_cut(count_where, (1, blk), topk, (key_ref.shape[0] - 1).bit_length())

    def bias_body(j, carry):
        sel = _topk_chosen(key_ref[rows_of(j), :], j * kb + rowk, thr, jcut) & visible(j)
        bias_ref[rows_of(j), :] = jnp.where(sel, 0.0, NEG_INF)
        return carry

    lax.fori_loop(0, nkb, bias_body, 0)

    rep = nh // nkv
    qgs =[jnp.concatenate([q_ref[:, (g * rep + r) * LANES:(g * rep + r + 1) * LANES] for r in range(rep)], axis=0)
           for g in range(nkv)]
    acc_ref[...] = jnp.zeros_like(acc_ref)

    def att_body(j, carry):
        ms, ls = carry
        rows = rows_of(j)
        bias = jnp.concatenate([bias_ref[rows, :]] * rep, axis=1)
        ss = [_dot_nt(k_ref[rows, g * LANES:(g + 1) * LANES], qgs[g]) + bias for g in range(nkv)]
        m_new = [jnp.maximum(m, jnp.max(s, axis=0, keepdims=True)) for m, s in zip(ms, ss)]
        ps = [jnp.exp(s - m) for s, m in zip(ss, m_new)]
        pvs = [_dot(vt_ref[j, g * LANES:(g + 1) * LANES, :], p.astype(BF16)) for g, p in enumerate(ps)]
        alphas = [jnp.exp(m - mn) for m, mn in zip(ms, m_new)]
        for g in range(nkv):
            acc_ref[g] = alphas[g] * acc_ref[g] + pvs[g]
        l_new = [a * l + jnp.sum(p, axis=0, keepdims=True) for a, l, p in zip(alphas, ls, ps)]
        return tuple(m_new), tuple(l_new)

    row0 = lambda v: tuple(jnp.full((1, rep * blk), v, F32) for _ in range(nkv))
    _, ls = lax.fori_loop(0, nkb, att_body, (row0(NEG_INF), row0(0.0)))
    for g in range(nkv):
        og = acc_ref[g] / ls[g]
        for r in range(rep):
            h = g * rep + r
            o_ref[:, h * LANES:(h + 1) * LANES] = og[:, r * blk:(r + 1) * blk].T.astype(BF16)


def _dsa_prompt_attend(q16, qi16, wt, ki16, k16, vt16, *, nh, nkv, nih, topk):
    bsz, t, _ = q16.shape
    rep = nh // nkv
    full = lambda n: pl.BlockSpec((None, t, n * LANES), lambda b, i: (b, 0, 0))
    blk = lambda n: pl.BlockSpec((None, Q_BLOCK, n * LANES), lambda b, i: (b, i, 0))
    return pl.pallas_call(
        functools.partial(_dsa_prompt_kernel, nh=nh, nkv=nkv, nih=nih, topk=topk,
                          idx_scale=(LANES * nih) ** -0.5),
        grid=(bsz, t // Q_BLOCK),
        in_specs=[blk(nih), pl.BlockSpec((None, wt.shape[1], Q_BLOCK), lambda b, i: (b, 0, i)), blk(nh),
                  full(1), full(nkv), pl.BlockSpec((None,) + vt16.shape[1:], lambda b, i: (b, 0, 0, 0))],
        out_specs=blk(nh),
        out_shape=jax.ShapeDtypeStruct((bsz, t, nh * LANES), BF16),
        scratch_shapes=[pltpu.VMEM((t, Q_BLOCK), I32), pltpu.VMEM((t, Q_BLOCK), F32),
                        pltpu.VMEM((nkv, LANES, rep * Q_BLOCK), F32)],
        name="dsa_prompt_attend",
        compiler_params=_cparams("parallel", "arbitrary"),
    )(qi16, wt, q16, ki16, k16, vt16)


def _dsa_sample_score_kernel(pt_ref, qi_ref, wc_ref, *refs, n_steps, pps, nih, tq, past, idx_scale):
    page_refs, new_ref, o_ref = refs[:pps], refs[pps], refs[pps + 1]
    p = pl.program_id(1)
    keys = jnp.concatenate([r[...] for r in page_refs], axis=0).astype(BF16)
    keys = jnp.where(p == n_steps - 1, new_ref[...], keys)
    qi = jnp.concatenate([qi_ref[:, h * LANES:(h + 1) * LANES] for h in range(nih)], axis=0)
    w = jnp.maximum(_dot_nt(qi, keys), 0.0) * wc_ref[...]
    acc = w[0:tq]
    for h in range(1, nih):
        acc = acc + w[h * tq:(h + 1) * tq]
    s = p * keys.shape[0] + lax.broadcasted_iota(I32, acc.shape, 1)
    qpos = past + lax.broadcasted_iota(I32, acc.shape, 0)
    o_ref[...] = jnp.where(s <= qpos, acc * idx_scale, NEG_INF)


def _dsa_sample_select_kernel(s_ref, o_ref, *, topk):
    key = _sort_key(s_ref[...])
    idx = lax.broadcasted_iota(I32, key.shape, 1)
    count_where = lambda pred: jnp.sum(jnp.where(pred(key, idx), 1, 0), axis=-1, keepdims=True)
    thr, jcut = _topk_cut(count_where, (key.shape[0], 1), topk, (key.shape[1] - 1).bit_length())
    o_ref[...] = jnp.where(_topk_chosen(key, idx, thr, jcut) & (s_ref[...] > 0.5 * NEG_INF), 1.0, 0.0)


def _dsa_sample_attn_kernel(pt_ref, q_ref, sel_ref, *refs, n_steps, pps, nh, nkv, tq):
    kp_refs, vp_refs = refs[:pps], refs[pps:2 * pps]
    kn_ref, vn_ref, o_ref, m_ref, l_ref, acc_ref, hmask_ref, expand_ref = refs[2 * pps:]
    p = pl.program_id(1)
    rep = nh // nkv

    @pl.when(p == 0)
    def _():
        m_ref[...] = jnp.full_like(m_ref, NEG_INF)
        l_ref[...] = jnp.zeros_like(l_ref)
        acc_ref[...] = jnp.zeros_like(acc_ref)
        hrow = lax.broadcasted_iota(I32, hmask_ref.shape, 0) // (tq * rep)
        hcol = lax.broadcasted_iota(I32, hmask_ref.shape, 1) % nkv
        hmask_ref[...] = jnp.where(hrow == hcol, 0.0, NEG_INF)
        expand_ref[...] = (lax.broadcasted_iota(I32, expand_ref.shape, 1) // nkv
                           == lax.broadcasted_iota(I32, expand_ref.shape, 0)).astype(BF16)

    last = p == n_steps - 1
    kcat = jnp.where(last, kn_ref[...], jnp.concatenate([r[...] for r in kp_refs], axis=0).astype(BF16))
    vcat = jnp.where(last, vn_ref[...], jnp.concatenate([r[...] for r in vp_refs], axis=0).astype(BF16))
    q = jnp.concatenate([q_ref[:, h * LANES:(h + 1) * LANES] for h in range(nh)], axis=0)
    page = sel_ref.shape[1] // pps
    sel = sel_ref[...].astype(BF16)
    selx = jnp.concatenate([_dot(sel[:, i * page:(i + 1) * page], expand_ref[...]) for i in range(pps)], axis=1)
    unsel = (selx - 1.0) * -NEG_INF
    s = _dot_nt(q, kcat) + hmask_ref[...] + jnp.concatenate([unsel] * nh, axis=0)
    m = m_ref[...]
    m_new = jnp.maximum(m, jnp.max(s, axis=-1, keepdims=True))
    alpha = jnp.exp(m - m_new)
    pr = jnp.where(s > 0.5 * NEG_INF, jnp.exp(s - m_new), 0.0)
    l_ref[...] = alpha * l_ref[...] + jnp.sum(pr, axis=-1, keepdims=True)
    acc_ref[...] = alpha * acc_ref[...] + _dot(pr.astype(BF16), vcat)
    m_ref[...] = m_new

    @pl.when(last)
    def _():
        og = acc_ref[...] / l_ref[...]
        for h in range(nh):
            o_ref[:, h * LANES:(h + 1) * LANES] = og[h * tq:(h + 1) * tq].astype(BF16)


def _dsa_sample_attend(q16, qi16, wi, ki16, k16, v16, cache_k, cache_v, cache_i, layer, page_table,
                       *, nh, nkv, nih, topk):
    bsz, tq, _ = q16.shape
    n_pages = page_table.shape[1]
    n_layers, n_pool, page = cache_i.shape[:3]
    pps = math.gcd(n_pages, SAMPLE_PAGES_PER_STEP)
    assert page == LANES and tq == SUBLANES
    past = n_pages * page
    n_steps = n_pages // pps + 1
    ltot = n_steps * pps * page
    wcol = wi.transpose(0, 2, 1).reshape(bsz, nih * tq, 1)
    pidx = lambda i: (lambda b, p, pt: (layer * n_pool + pt[b, jnp.minimum(p * pps + i, n_pages - 1)], 0, 0))
    qspec = lambda n: pl.BlockSpec((None, tq, n * LANES), lambda b, p, pt: (b, 0, 0))
    idx_scale = (LANES * nih) ** -0.5
    cache_i = cache_i.reshape(n_layers * n_pool, page, LANES)
    ki_new = jnp.pad(ki16, ((0, 0), (0, pps * page - tq), (0, 0)))
    scores = pl.pallas_call(
        functools.partial(_dsa_sample_score_kernel, n_steps=n_steps, pps=pps, nih=nih, tq=tq, past=past,
                          idx_scale=idx_scale),
        grid_spec=pltpu.PrefetchScalarGridSpec(
            num_scalar_prefetch=1, grid=(bsz, n_steps),
            in_specs=[qspec(nih), pl.BlockSpec((None, nih * tq, 1), lambda b, p, pt: (b, 0, 0))]
            + [pl.BlockSpec((None, page, LANES), pidx(i)) for i in range(pps)]
            + [pl.BlockSpec((None, pps * page, LANES), lambda b, p, pt: (b, 0, 0))],
            out_specs=pl.BlockSpec((None, tq, pps * page), lambda b, p, pt: (b, 0, p))),
        out_shape=jax.ShapeDtypeStruct((bsz, tq, ltot), F32),
        name="dsa_sample_scores",
        compiler_params=_cparams("parallel", "arbitrary"),
    )(page_table, qi16, wcol, *([cache_i] * pps), ki_new)
    sel = pl.pallas_call(
        functools.partial(_dsa_sample_select_kernel, topk=topk),
        grid=(bsz,),
        in_specs=[pl.BlockSpec((None, tq, ltot), lambda b: (b, 0, 0))],
        out_specs=pl.BlockSpec((None, tq, ltot), lambda b: (b, 0, 0)),
        out_shape=jax.ShapeDtypeStruct((bsz, tq, ltot), F32),
        name="dsa_sample_select",
        compiler_params=_cparams("parallel"),
    )(scores)
    prow = page * nkv
    ck = cache_k.reshape(n_layers * n_pool, prow, LANES)
    cv = cache_v.reshape(n_layers * n_pool, prow, LANES)
    new_rows = lambda a: jnp.pad(a.reshape(bsz, tq * nkv, LANES), ((0, 0), (0, pps * prow - tq * nkv), (0, 0)))
    kvspecs = [pl.BlockSpec((None, prow, LANES), pidx(i)) for i in range(pps)]
    newspec = pl.BlockSpec((None, pps * prow, LANES), lambda b, p, pt: (b, 0, 0))
    return pl.pallas_call(
        functools.partial(_dsa_sample_attn_kernel, n_steps=n_steps, pps=pps, nh=nh, nkv=nkv, tq=tq),
        grid_spec=pltpu.PrefetchScalarGridSpec(
            num_scalar_prefetch=1, grid=(bsz, n_steps),
            in_specs=[qspec(nh), pl.BlockSpec((None, tq, pps * page), lambda b, p, pt: (b, 0, p))]
            + kvspecs + kvspecs + [newspec, newspec],
            out_specs=qspec(nh),
            scratch_shapes=[pltpu.VMEM((nh * tq, 1), F32), pltpu.VMEM((nh * tq, 1), F32),
                            pltpu.VMEM((nh * tq, LANES), F32), pltpu.VMEM((nh * tq, pps * prow), F32),
                            pltpu.VMEM((page, prow), BF16)]),
        out_shape=jax.ShapeDtypeStruct((bsz, tq, nh * LANES), BF16),
        name="dsa_sample_attend",
        compiler_params=_cparams("parallel", "arbitrary"),
    )(page_table, q16, sel, *([ck] * pps), *([cv] * pps), new_rows(k16), new_rows(v16))


def _dsa_mixer(x, mod3, nw, w_in, w_out, n_proj, *, nkv, tm, tm_in, tt, cache=None):
    bsz, t, d = x.shape
    nh = _wshape(w_out)[0] // LANES
    nih = (n_proj - (nh + 2 * nkv + 1) * LANES) // (LANES + 1)
    assert (nh + 2 * nkv + nih + 1) * LANES + nih == n_proj and nih <= LANES
    x2 = x.reshape(bsz * t, d)
    proj = _normproj(x2, mod3, 3, nw, w_in, tm=tm_in).reshape(bsz, t, -1)
    v = proj[:, :, (nh + nkv) * LANES:(nh + 2 * nkv) * LANES]
    if cache is None:
        assert t % Q_BLOCK == 0
        q16, k, k16, _, qi16, ki, ki16, wt, vt16 = _dsa_prep(proj, jnp.arange(t), nh=nh, nkv=nkv, nih=nih,
                                                             tt=_tile(t, 4 * Q_BLOCK, Q_BLOCK), with_wt=True)
        o = _dsa_prompt_attend(q16, qi16, wt, ki16, k16, vt16, nh=nh, nkv=nkv, nih=nih, topk=min(TOPK_MAX, t // 4))
    else:
        cache_k, cache_v, cache_i, layer, page_table = cache
        past = page_table.shape[1] * cache_i.shape[2]
        q16, k, k16, v16, qi16, ki, ki16 = _dsa_prep(proj, past + jnp.arange(t), nh=nh, nkv=nkv, nih=nih, tt=tt,
                                                    with_wt=False)
        w0 = (nh + 2 * nkv + nih + 1) * LANES
        o = _dsa_sample_attend(q16, qi16, proj[:, :, w0:w0 + nih], ki16, k16, v16, cache_k, cache_v, cache_i,
                               layer, page_table, nh=nh, nkv=nkv, nih=nih, topk=min(TOPK_MAX, (past + t) // 4))
    xo = _outproj(o.reshape(bsz * t, nh * LANES), w_out, x2, mod3, 5, tm=tm).reshape(bsz, t, d)
    return xo, k.reshape(bsz, t, nkv, LANES), v.reshape(bsz, t, nkv, LANES), ki


def _gelu_tanh(x):
    return 0.5 * x * (1.0 + jnp.tanh(math.sqrt(2.0 / math.pi) * (x + 0.044715 * (x * x * x))))


def _lru_kernel(gate_ref, xb_ref, halo_ref, buf_ref, cw_ref, cb_ref, wa_ref, ba_ref, wx_ref, bx_ref, lam_ref,
                h0_ref, y_ref, hl_ref, h_ref, *, nblk):
    ti = pl.program_id(1)

    @pl.when(ti == 0)
    def _():
        h_ref[...] = h0_ref[...]

    halo = jnp.where(ti == 0, buf_ref[...], halo_ref[...])
    xc = _causal_conv(xb_ref[...], halo, cw_ref[...]) + cb_ref[...]
    tt = xc.shape[0]
    xc16 = xc.astype(BF16)
    rs, xs = [], []
    for n in range(nblk):
        blk = xc16[:, n * LRU_BLOCK:(n + 1) * LRU_BLOCK]
        rs.append(_dot(blk, wa_ref[n]))
        xs.append(_dot(blk, wx_ref[n]))
    r = _sigmoid(jnp.concatenate(rs, axis=1) + ba_ref[...])
    ig = _sigmoid(jnp.concatenate(xs, axis=1) + bx_ref[...])
    lam = lam_ref[...]
    softplus_neg = jnp.maximum(-lam, 0.0) + jnp.log1p(jnp.exp(-jnp.abs(lam)))
    log_a = -RG_C * r * softplus_neg
    a = jnp.exp(log_a)
    b = jnp.sqrt(-jnp.tanh(log_a) * (a * a + 1.0)) * (ig * xc)
    row = lax.broadcasted_iota(I32, a.shape, 0)
    d = 1
    while d < tt:
        keep = row >= d
        a_sh = jnp.where(keep, pltpu.roll(a, d, 0), 1.0)
        b_sh = jnp.where(keep, pltpu.roll(b, d, 0), 0.0)
        b = a * b_sh + b
        a = a * a_sh
        d *= 2
    hs = b + a * h_ref[...]
    h_ref[...] = hs[tt - 1:tt, :]
    y_ref[...] = (hs * _gelu_tanh(gate_ref[...])).astype(BF16)

    @pl.when(ti == pl.num_programs(1) - 1)
    def _():
        hl_ref[...] = hs[tt - 1:tt, :]


def _lru_core(proj, buf8, h0, conv_w, conv_b, w_ga, b_ga, w_gx, b_gx, lam, *, tt):
    bsz, t, w2 = proj.shape
    w = w2 // 2
    nblk = w // LRU_BLOCK
    sub = tt // SUBLANES
    vec = lambda: pl.BlockSpec((1, w), lambda b, i: (0, 0))
    wsp = lambda: pl.BlockSpec((nblk, LRU_BLOCK, LRU_BLOCK), lambda b, i: (0, 0, 0))
    r1 = lambda v: v.reshape(1, w).astype(F32)
    y, hl = pl.pallas_call(
        functools.partial(_lru_kernel, nblk=nblk),
        grid=(bsz, t // tt),
        in_specs=[
            pl.BlockSpec((None, tt, w), lambda b, i: (b, i, 0)),
            pl.BlockSpec((None, tt, w), lambda b, i: (b, i, 1)),
            pl.BlockSpec((None, SUBLANES, w), lambda b, i: (b, jnp.maximum(i * sub - 1, 0), 1)),
            pl.BlockSpec((None, SUBLANES, w), lambda b, i: (b, 0, 0)),
            pl.BlockSpec((CONV_W, w), lambda b, i: (0, 0)),
            vec(), wsp(), vec(), wsp(), vec(), vec(),
            pl.BlockSpec((None, 1, w), lambda b, i: (b, 0, 0)),
        ],
        out_specs=[pl.BlockSpec((None, tt, w), lambda b, i: (b, i, 0)),
                   pl.BlockSpec((None, 1, w), lambda b, i: (b, 0, 0))],
        out_shape=[jax.ShapeDtypeStruct((bsz, t, w), BF16), jax.ShapeDtypeStruct((bsz, 1, w), F32)],
        scratch_shapes=[pltpu.VMEM((1, w), F32)],
        name="lru_core",
        compiler_params=_cparams("parallel", "arbitrary"),
    )(proj, proj, proj, buf8, conv_w, r1(conv_b), w_ga, r1(b_ga), w_gx, r1(b_gx), r1(lam), h0.reshape(bsz, 1, w))
    return y, hl.reshape(bsz, w)


def _lru_mixer(x, mod3, nw, buf, h0, w_in, conv_w, conv_b, w_ga, b_ga, w_gx, b_gx, lam, w_out, *, tm, tm_in, tt):
    bsz, t, d = x.shape
    assert t >= CONV_W - 1
    x2 = x.reshape(bsz * t, d)
    proj = _normproj(x2, mod3, 3, nw, w_in, tm=tm_in).reshape(bsz, t, -1)
    w = proj.shape[2] // 2
    y, hl = _lru_core(proj, _pad_buf(buf), h0, conv_w, conv_b, w_ga, b_ga, w_gx, b_gx, lam, tt=tt)
    xo = _outproj(y.reshape(bsz * t, w), w_out, x2, mod3, 5, tm=tm).reshape(bsz, t, d)
    return xo, proj[:, t - (CONV_W - 1):, w:], hl


def _bf16_padded(w, mult):
    n = w.shape[-1]
    return jnp.pad(w, ((0, 0), (0, 0), (0, -n % mult))).astype(BF16)


def kernel(x_prompt, x_sample, state_a_conv, state_a_ssm, cache_b_k, cache_b_v, cache_b_idx, state_c_conv, state_c_h, page_table, c_prompt, c_sample, w_ada, b_ada, norm_w, ffn_w_in, ffn_w_out, gdn_w_in, gdn_conv_w, gdn_a_log, gdn_dt_bias, gdn_norm_w, gdn_w_out, dsa_w_in, dsa_w_out, lru_w_in, lru_conv_w, lru_conv_b, lru_w_gate_a, lru_b_gate_a, lru_w_gate_x, lru_b_gate_x, lru_lambda, lru_w_out, w_ada_final, b_ada_final, final_norm_w):
    bp, seq, d = x_prompt.shape
    bs, ts, _ = x_sample.shape
    depth = w_ada.shape[0]
    nkv = cache_b_k.shape[3]
    tm_p = _tile(seq, 512, SUBLANES)
    tm_in_p = _tile(seq, 1024, SUBLANES)
    tt_p = _tile(seq, 256, SUBLANES)
    tm_s = bs * ts
    tiles_p = dict(tm=tm_p, tm_in=tm_in_p, tt=tt_p)
    tiles_s = dict(tm=tm_s, tm_in=tm_s, tt=ts)

    c_all = jnp.concatenate([c_prompt, c_sample], axis=0)
    c_all = jnp.pad(c_all, ((0, -c_all.shape[0] % SUBLANES), (0, 0)))
    mod = _ada(c_all, w_ada, b_ada)
    mod_f = _ada(c_all, w_ada_final[None], b_ada_final[None])[0]

    def groups(m):
        return m[:bp, None, :], jnp.repeat(m[bp:bp + bs], ts, axis=0)[None]

    ffn_w_out16 = ffn_w_out.astype(BF16)
    gdn_w_in16, gdn_w_out16 = _bf16_padded(gdn_w_in, 512), gdn_w_out.astype(BF16)
    dsa_w_in16, dsa_w_out16 = _bf16_padded(dsa_w_in, 512), dsa_w_out.astype(BF16)
    lru_w_in16, lru_w_out16 = lru_w_in.astype(BF16), lru_w_out.astype(BF16)

    xp, xs = x_prompt, x_sample
    outs = {k: [] for k in ("a_conv_p", "a_conv_s", "a_ssm_p", "a_ssm_s", "b_k_p", "b_k_s", "b_v_p", "b_v_s",
                            "b_i_p", "b_i_s", "c_conv_p", "c_conv_s", "c_h_p", "c_h_s")}
    for layer in range(depth):
        kind, j = layer % N_MIXERS, layer // N_MIXERS
        mod_p, mod_s = groups(mod[layer])
        nw = norm_w[layer]

        def ffn(x, m3, tm, which):
            b, t, _ = x.shape
            x2 = x.reshape(b * t, d)
            h = _ffn_up(x2, m3, 6 * which, nw[2 * which:2 * which + 1], (ffn_w_in, (layer, which)), tm=tm)
            return _outproj(h, (ffn_w_out16, (layer, which)), x2, m3, 6 * which + 2, tm=tm, tn_target=512,
                            gain=0.5).reshape(b, t, d)

        xp = ffn(xp, mod_p, tm_in_p, 0)
        xs = ffn(xs, mod_s, tm_s, 0)
        if kind == 0:
            prm = ((gdn_w_in16, (j,)), gdn_conv_w[j], gdn_a_log[j], gdn_dt_bias[j], gdn_norm_w[j],
                   (gdn_w_out16, (j,)))
            buf0 = jnp.zeros((bp,) + state_a_conv.shape[2:], F32)
            s0 = jnp.zeros((bp,) + state_a_ssm.shape[2:], F32)
            xp, buf, s = _gdn_mixer(xp, mod_p, nw[1:2], buf0, s0, *prm, chunk=2 * GDN_CHUNK, **tiles_p)
            outs["a_conv_p"].append(buf)
            outs["a_ssm_p"].append(s)
            xs, buf, s = _gdn_mixer(xs, mod_s, nw[1:2], state_a_conv[j], state_a_ssm[j], *prm, chunk=GDN_CHUNK,
                                    **tiles_s)
            outs["a_conv_s"].append(buf)
            outs["a_ssm_s"].append(s)
        elif kind == 1:
            w_in, w_out = (dsa_w_in16, (j,)), (dsa_w_out16, (j,))
            n_proj = dsa_w_in.shape[2]
            xp, k, v, ki = _dsa_mixer(xp, mod_p, nw[1:2], w_in, w_out, n_proj, nkv=nkv, **tiles_p)
            outs["b_k_p"].append(k)
            outs["b_v_p"].append(v)
            outs["b_i_p"].append(ki)
            xs, k, v, ki = _dsa_mixer(xs, mod_s, nw[1:2], w_in, w_out, n_proj, nkv=nkv,
                                      cache=(cache_b_k, cache_b_v, cache_b_idx, j, page_table), **tiles_s)
            outs["b_k_s"].append(k)
            outs["b_v_s"].append(v)
            outs["b_i_s"].append(ki)
        else:
            prm = ((lru_w_in16, (j,)), lru_conv_w[j], lru_conv_b[j], lru_w_gate_a[j].astype(BF16),
                   lru_b_gate_a[j], lru_w_gate_x[j].astype(BF16), lru_b_gate_x[j], lru_lambda[j],
                   (lru_w_out16, (j,)))
            buf0 = jnp.zeros((bp,) + state_c_conv.shape[2:], F32)
            h0 = jnp.zeros((bp,) + state_c_h.shape[2:], F32)
            xp, buf, hl = _lru_mixer(xp, mod_p, nw[1:2], buf0, h0, *prm, **tiles_p)
            outs["c_conv_p"].append(buf)
            outs["c_h_p"].append(hl)
            xs, buf, hl = _lru_mixer(xs, mod_s, nw[1:2], state_c_conv[j], state_c_h[j], *prm, **tiles_s)
            outs["c_conv_s"].append(buf)
            outs["c_h_s"].append(hl)
        xp = ffn(xp, mod_p, tm_in_p, 1)
        xs = ffn(xs, mod_s, tm_s, 1)

    modf_p, modf_s = groups(mod_f)
    fnw = final_norm_w.reshape(1, d)
    y_p = _final_norm(xp.reshape(bp * seq, d), modf_p, fnw, tm=tm_p).reshape(bp, seq, d)
    y_s = _final_norm(xs.reshape(bs * ts, d), modf_s, fnw, tm=tm_s).reshape(bs, ts, d)
    st = {k: jnp.stack(v) for k, v in outs.items()}
    return (y_p, y_s, st["a_conv_p"], st["a_conv_s"], st["a_ssm_p"], st["a_ssm_s"], st["b_k_p"], st["b_k_s"],
            st["b_v_p"], st["b_v_s"], st["b_i_p"], st["b_i_s"], st["c_conv_p"], st["c_conv_s"], st["c_h_p"],
            st["c_h_s"])
```

```python
import functools
import math

import jax
import jax.numpy as jnp
import numpy as np
from jax import lax
from jax.experimental import pallas as pl
from jax.experimental.pallas import tpu as pltpu

F32 = jnp.float32
BF16 = jnp.bfloat16
I32 = jnp.int32

N_MIXERS = 3
N_ADA = 9
CONV_W = 4
NORM_EPS = 1e-6
NEG_INF = -1e30
GDN_CHUNK = 64
TOPK_MAX = 256
Q_BLOCK = 128
ROPE_THETA = 10000.0
RG_C = 8.0
LRU_BLOCK = 256
SAMPLE_PAGES_PER_STEP = 8

LANES = 128
SUBLANES = 8
VMEM_LIMIT_BYTES = 56 * 2**20
VT_ROWS = LANES + 16
INT32_MIN = -2**31
INT32_MAX = 2**31 - 1
KEY_OF_NEG_INF = int(np.float32(NEG_INF).view(np.int32)) ^ 0x7FFFFFFF


def _cparams(*sem):
    return pltpu.CompilerParams(dimension_semantics=sem, vmem_limit_bytes=VMEM_LIMIT_BYTES)


def _tile(n, target, align=LANES):
    if n <= target:
        return n
    t = (target // align) * align
    while t >= align:
        if n % t == 0:
            return t
        t -= align
    raise ValueError(f"no {align}-aligned tile of {n} below {target}")


def _sigmoid(x):
    return jax.nn.sigmoid(x)


def _silu(x):
    return x * _sigmoid(x)


def _dot(a, b):
    return jnp.dot(a, b, preferred_element_type=F32)


def _dot_nt(a, b):
    return lax.dot_general(a, b, (((1,), (1,)), ((), ())), preferred_element_type=F32)


def _dot_tn(a, b):
    return lax.dot_general(a, b, (((0,), (0,)), ((), ())), preferred_element_type=F32)


def _dot_f32(a, b):
    return jnp.dot(a, b, preferred_element_type=F32, precision=lax.Precision.HIGHEST)


def _norm_mod(x, nw, sh, sc):
    ms = jnp.mean(x * x, axis=-1, keepdims=True)
    return x * lax.rsqrt(ms + NORM_EPS) * (nw * (1.0 + sc)) + sh


def _ada_kernel(c_ref, w_ref, b_ref, o_ref):
    a = _silu(c_ref[...]).astype(BF16)
    o_ref[0] = _dot(a, w_ref[0].astype(BF16)) + b_ref[0]


def _ada(c_all, w, b):
    n_l, d, n = w.shape
    mp = c_all.shape[0]
    tn = _tile(n, 1024)
    return pl.pallas_call(
        _ada_kernel,
        grid=(n_l, n // tn),
        in_specs=[
            pl.BlockSpec((mp, d), lambda l, j: (0, 0)),
            pl.BlockSpec((1, d, tn), lambda l, j: (l, 0, j)),
            pl.BlockSpec((1, 1, tn), lambda l, j: (l, 0, j)),
        ],
        out_specs=pl.BlockSpec((1, mp, tn), lambda l, j: (l, 0, j)),
        out_shape=jax.ShapeDtypeStruct((n_l, mp, n), F32),
        name="ada_mod",
        compiler_params=_cparams("parallel", "parallel"),
    )(c_all, w, b.reshape(n_l, 1, n))


def _mod_spec(mod3, k, d, tm, rows_per_group):
    return pl.BlockSpec((None, mod3.shape[1], d), lambda i, j: ((i * tm) // rows_per_group, 0, k))


def _wspec(w, block, imap):
    arr, idx = w
    return pl.BlockSpec((None,) * len(idx) + block, lambda i, j: idx + imap(i, j))


def _wshape(w):
    return w[0].shape[len(w[1]):]


def _ffn_kernel(x_ref, nw_ref, sh_ref, sc_ref, g_ref, wa_ref, wb_ref, wo_ref, o_ref, xn_ref):
    f = pl.program_id(1)

    @pl.when(f == 0)
    def _():
        xn_ref[...] = _norm_mod(x_ref[...], nw_ref[...], sh_ref[...], sc_ref[...]).astype(BF16)
        o_ref[...] = jnp.zeros_like(o_ref)

    xn = xn_ref[...]
    a = _dot(xn, wa_ref[...])
    b = _dot(xn, wb_ref[...])
    o_ref[...] += _dot((_silu(a) * b).astype(BF16), wo_ref[...])

    @pl.when(f == pl.num_programs(1) - 1)
    def _():
        o_ref[...] = x_ref[...] + 0.5 * g_ref[...] * o_ref[...]


def _ffn(x, mod3, k0, nw, w_in, w_out, *, tm, tf_target):
    m, d = x.shape
    f = _wshape(w_out)[0]
    tf = _tile(f, tf_target)
    nf = f // tf
    rpg = m // mod3.shape[0]
    ms = lambda k: _mod_spec(mod3, k, d, tm, rpg)
    return pl.pallas_call(
        _ffn_kernel,
        grid=(m // tm, nf),
        in_specs=[
            pl.BlockSpec((tm, d), lambda i, j: (i, 0)),
            pl.BlockSpec((1, d), lambda i, j: (0, 0)),
            ms(k0), ms(k0 + 1), ms(k0 + 2),
            _wspec(w_in, (d, tf), lambda i, j: (0, j)),
            _wspec(w_in, (d, tf), lambda i, j: (0, j + nf)),
            _wspec(w_out, (tf, d), lambda i, j: (j, 0)),
        ],
        out_specs=pl.BlockSpec((tm, d), lambda i, j: (i, 0)),
        out_shape=jax.ShapeDtypeStruct((m, d), F32),
        scratch_shapes=[pltpu.VMEM((tm, d), BF16)],
        name="ffn",
        compiler_params=_cparams("parallel", "arbitrary"),
    )(x, nw, mod3, mod3, mod3, w_in[0], w_in[0], w_out[0])


def _normproj_kernel(x_ref, nw_ref, sh_ref, sc_ref, w_ref, o_ref, xn_ref):
    @pl.when(pl.program_id(1) == 0)
    def _():
        xn_ref[...] = _norm_mod(x_ref[...], nw_ref[...], sh_ref[...], sc_ref[...]).astype(BF16)

    o_ref[...] = _dot(xn_ref[...], w_ref[...])


def _normproj(x, mod3, k0, nw, w, *, tm, tn_target=1024):
    m, d = x.shape
    n = _wshape(w)[1]
    tn = _tile(n, tn_target)
    rpg = m // mod3.shape[0]
    ms = lambda k: _mod_spec(mod3, k, d, tm, rpg)
    return pl.pallas_call(
        _normproj_kernel,
        grid=(m // tm, n // tn),
        in_specs=[
            pl.BlockSpec((tm, d), lambda i, j: (i, 0)),
            pl.BlockSpec((1, d), lambda i, j: (0, 0)),
            ms(k0), ms(k0 + 1),
            _wspec(w, (d, tn), lambda i, j: (0, j)),
        ],
        out_specs=pl.BlockSpec((tm, tn), lambda i, j: (i, j)),
        out_shape=jax.ShapeDtypeStruct((m, n), F32),
        scratch_shapes=[pltpu.VMEM((tm, d), BF16)],
        name="normproj",
        compiler_params=_cparams("parallel", "arbitrary"),
    )(x, nw, mod3, mod3, w[0])


def _ffn_up_kernel(x_ref, nw_ref, sh_ref, sc_ref, wa_ref, wb_ref, h_ref, xn_ref):
    @pl.when(pl.program_id(1) == 0)
    def _():
        xn_ref[...] = _norm_mod(x_ref[...], nw_ref[...], sh_ref[...], sc_ref[...]).astype(BF16)

    xn = xn_ref[...]
    a = _dot(xn, wa_ref[...].astype(BF16))
    b = _dot(xn, wb_ref[...].astype(BF16))
    h_ref[...] = (_silu(a) * b).astype(BF16)


def _ffn_up(x, mod3, k0, nw, w_in, *, tm, tf_target=512):
    m, d = x.shape
    f = _wshape(w_in)[1] // 2
    tf = _tile(f, tf_target)
    nf = f // tf
    rpg = m // mod3.shape[0]
    ms = lambda k: _mod_spec(mod3, k, d, tm, rpg)
    return pl.pallas_call(
        _ffn_up_kernel,
        grid=(m // tm, nf),
        in_specs=[
            pl.BlockSpec((tm, d), lambda i, j: (i, 0)),
            pl.BlockSpec((1, d), lambda i, j: (0, 0)),
            ms(k0), ms(k0 + 1),
            _wspec(w_in, (d, tf), lambda i, j: (0, j)),
            _wspec(w_in, (d, tf), lambda i, j: (0, j + nf)),
        ],
        out_specs=pl.BlockSpec((tm, tf), lambda i, j: (i, j)),
        out_shape=jax.ShapeDtypeStruct((m, f), BF16),
        scratch_shapes=[pltpu.VMEM((tm, d), BF16)],
        name="ffn_up",
        compiler_params=_cparams("parallel", "arbitrary"),
    )(x, nw, mod3, mod3, w_in[0], w_in[0])


def _outproj_kernel(a_ref, w_ref, x_ref, g_ref, o_ref, *, gain):
    o_ref[...] = x_ref[...] + gain * g_ref[...] * _dot(a_ref[...], w_ref[...])


def _outproj(a, w, x, mod3, kg, *, tm, tn_target=1024, gain=1.0):
    m, kdim = a.shape
    d = _wshape(w)[1]
    tn = _tile(d, tn_target)
    rpg = m // mod3.shape[0]
    r = mod3.shape[1]
    nd = d // tn
    return pl.pallas_call(
        functools.partial(_outproj_kernel, gain=gain),
        grid=(m // tm, nd),
        in_specs=[
            pl.BlockSpec((tm, kdim), lambda i, j: (i, 0)),
            _wspec(w, (kdim, tn), lambda i, j: (0, j)),
            pl.BlockSpec((tm, tn), lambda i, j: (i, j)),
            pl.BlockSpec((None, r, tn), lambda i, j: ((i * tm) // rpg, 0, kg * nd + j)),
        ],
        out_specs=pl.BlockSpec((tm, tn), lambda i, j: (i, j)),
        out_shape=jax.ShapeDtypeStruct((m, d), F32),
        name="outproj",
        compiler_params=_cparams("parallel", "arbitrary"),
    )(a, w[0], x, mod3)


def _final_kernel(x_ref, nw_ref, sh_ref, sc_ref, o_ref):
    o_ref[...] = _norm_mod(x_ref[...], nw_ref[...], sh_ref[...], sc_ref[...])


def _final_norm(x, mod3, nw, *, tm):
    m, d = x.shape
    rpg = m // mod3.shape[0]
    r = mod3.shape[1]
    ms = lambda k: pl.BlockSpec((None, r, d), lambda i: ((i * tm) // rpg, 0, k))
    return pl.pallas_call(
        _final_kernel,
        grid=(m // tm,),
        in_specs=[pl.BlockSpec((tm, d), lambda i: (i, 0)), pl.BlockSpec((1, d), lambda i: (0, 0)), ms(0), ms(1)],
        out_specs=pl.BlockSpec((tm, d), lambda i: (i, 0)),
        out_shape=jax.ShapeDtypeStruct((m, d), F32),
        name="final_norm",
        compiler_params=_cparams("parallel"),
    )(x, nw, mod3, mod3)


def _causal_conv(x, halo, w):
    def taps(rows, fix):
        acc = rows * w[CONV_W - 1:CONV_W]
        for j in range(1, CONV_W):
            acc = acc + fix(pltpu.roll(rows, j, 0), j) * w[CONV_W - 1 - j:CONV_W - j]
        return acc

    row = lax.broadcasted_iota(I32, halo.shape, 0)
    head = taps(x[:SUBLANES], lambda r, j: jnp.where(row < j, pltpu.roll(halo, j, 0), r))
    if x.shape[0] == SUBLANES:
        return head
    return jnp.concatenate([head, taps(x, lambda r, j: r)[SUBLANES:]], axis=0)


def _conv_specs(tt, tc, coff, boff):
    sub = tt // SUBLANES
    return [
        pl.BlockSpec((None, tt, tc), lambda b, t, c: (b, t, c + coff)),
        pl.BlockSpec((None, SUBLANES, tc), lambda b, t, c: (b, jnp.maximum(t * sub - 1, 0), c + coff)),
        pl.BlockSpec((None, SUBLANES, tc), lambda b, t, c: (b, 0, c + boff)),
    ]


def _pad_buf(buf):
    return jnp.pad(buf, ((0, 0), (SUBLANES - (CONV_W - 1), 0), (0, 0)))


def _gdn_prep_kernel(x_ref, halo_ref, buf_ref, w_ref, o_ref, *, norm):
    halo = jnp.where(pl.program_id(1) == 0, buf_ref[...], halo_ref[...])
    y = _silu(_causal_conv(x_ref[...], halo, w_ref[...]))
    if norm:
        for h in range(y.shape[1] // LANES):
            seg = y[:, h * LANES:(h + 1) * LANES]
            ss = jnp.sum(seg * seg, axis=-1, keepdims=True)
            o_ref[:, h * LANES:(h + 1) * LANES] = seg * lax.rsqrt(ss + NORM_EPS)
    else:
        o_ref[...] = y


def _gdn_prep(proj, buf8, conv_w, *, col0, ncols, norm, tt):
    bsz, t, _ = proj.shape
    tc = _tile(ncols, 1024)
    return pl.pallas_call(
        functools.partial(_gdn_prep_kernel, norm=norm),
        grid=(bsz, t // tt, ncols // tc),
        in_specs=_conv_specs(tt, tc, col0 // tc, col0 // tc)
        + [pl.BlockSpec((CONV_W, tc), lambda b, i, c: (0, c + col0 // tc))],
        out_specs=pl.BlockSpec((None, tt, tc), lambda b, i, c: (b, i, c)),
        out_shape=jax.ShapeDtypeStruct((bsz, t, ncols), F32),
        name="gdn_prep",
        compiler_params=_cparams("parallel", "parallel", "parallel"),
    )(proj, proj, buf8, conv_w)


def _gdn_gate_kernel(x_ref, alog_ref, dtb_ref, beta_ref, gc_ref, *, hv, chunk):
    x = x_ref[...]
    tt = x.shape[0]
    beta_ref[...] = _sigmoid(x)
    z = x + dtb_ref[...]
    g = -jnp.exp(alog_ref[...]) * (jnp.maximum(z, 0.0) + jnp.log1p(jnp.exp(-jnp.abs(z))))
    row = lax.broadcasted_iota(I32, (tt, tt), 0)
    col = lax.broadcasted_iota(I32, (tt, tt), 1)
    tri = jnp.where((row >= col) & (row // chunk == col // chunk), 1.0, 0.0)
    gc_ref[...] = _dot_f32(tri, g)


def _gdn_gates(proj, a_log, dt_bias, *, col0, hv, chunk, tt):
    bsz, t, _ = proj.shape
    pad = lambda v: jnp.pad(v.astype(F32), (hv, LANES - 2 * hv)).reshape(1, LANES)
    blk = pl.BlockSpec((None, tt, LANES), lambda b, i: (b, i, col0 // LANES))
    out = pl.BlockSpec((None, tt, LANES), lambda b, i: (b, i, 0))
    par = pl.BlockSpec((1, LANES), lambda b, i: (0, 0))
    return pl.pallas_call(
        functools.partial(_gdn_gate_kernel, hv=hv, chunk=chunk),
        grid=(bsz, t // tt),
        in_specs=[blk, par, par],
        out_specs=[out, out],
        out_shape=[jax.ShapeDtypeStruct((bsz, t, LANES), F32)] * 2,
        name="gdn_gates",
        compiler_params=_cparams("parallel", "parallel"),
    )(proj, pad(a_log), pad(dt_bias))


def _split_bf16(a):
    hi = a.astype(BF16)
    return hi, (a - hi.astype(F32)).astype(BF16)


def _dot_split(a, b):
    ah, al = _split_bf16(a)
    bh, bl = _split_bf16(b)
    return _dot(jnp.concatenate([ah, al, ah], axis=1), jnp.concatenate([bh, bh, bl], axis=0))


def _dot_bf16(a, b):
    return _dot(a.astype(BF16), b.astype(BF16))


def _tri_inv_all(lmats, c, n_real):
    row = lax.broadcasted_iota(I32, (c, c), 0)
    col = lax.broadcasted_iota(I32, (c, c), 1)
    eye = jnp.where(row == col, 1.0, 0.0)
    base = min(16, c)
    ps = [-jnp.where(row // base == col // base, m, 0.0) for m in lmats]
    rs = [eye + p for p in ps]
    n = 2
    while n < base:
        ps = [_dot_bf16(p, p) for p in ps]
        rs = [r + _dot_bf16(r, p) for r, p in zip(rs, ps)]
        n *= 2
    s = base
    while s < min(c, n_real):
        off = (row // (2 * s) == col // (2 * s)) & (row // s != col // s)
        ts = [_dot_bf16(jnp.where(off, m, 0.0), r) for m, r in zip(lmats, rs)]
        rs = [r - _dot_bf16(r, t) for r, t in zip(rs, ts)]
        s *= 2
    res = [eye - r - _dot_split(m, r) for m, r in zip(lmats, rs)]
    return [r + _dot_bf16(r, e) for r, e in zip(rs, res)]


def _gdn_core_kernel(q_ref, k_ref, v_ref, z_ref, gcc_ref, gcr_ref, bc_ref, s0_ref, nw_ref, o_ref, so_ref, s_ref,
                     *, rep, c, ncb, hpb, dk, n_real):
    ci = pl.program_id(2)

    @pl.when(ci == 0)
    def _():
        s_ref[...] = s0_ref[...]

    row = lax.broadcasted_iota(I32, (c, c), 0)
    col = lax.broadcasted_iota(I32, (c, c), 1)
    causal = row >= col
    nhd = hpb * rep
    kinst = [(n, hh) for n in range(ncb) for hh in range(hpb)]
    inst = [(n, hh, r) for n in range(ncb) for hh in range(hpb) for r in range(rep)]
    rows = lambda n: slice(n * c, (n + 1) * c)
    lanes = lambda hh, r: slice((hh * rep + r) * LANES, (hh * rep + r + 1) * LANES)
    kidx = lambda n, hh: n * hpb + hh
    qs = [q_ref[rows(n), hh * LANES:(hh + 1) * LANES] * dk ** -0.5 for n, hh in kinst]
    ks = [k_ref[rows(n), hh * LANES:(hh + 1) * LANES] for n, hh in kinst]
    k16 = [k.astype(BF16) for k in ks]
    grams = [_dot_nt(kb, kb) for kb in k16]
    qk0s = [_dot_nt(q.astype(BF16), kb) for q, kb in zip(qs, k16)]
    gccs = [gcc_ref[hh, rows(n), r:r + 1] for n, hh, r in inst]
    gcrs = [gcr_ref[hh, r:r + 1, rows(n)] for n, hh, r in inst]
    betas = [bc_ref[hh, rows(n), r:r + 1] for n, hh, r in inst]
    decays = [jnp.where(causal, jnp.exp(jnp.where(causal, gc - gr, 0.0)), 0.0) for gc, gr in zip(gccs, gcrs)]
    lowers = [jnp.where(row > col, grams[kidx(n, hh)] * b * d, 0.0)
              for (n, hh, _), b, d in zip(inst, betas, decays)]
    tinvs = _tri_inv_all(lowers, c, n_real)
    egcs = [jnp.exp(gc) for gc in gccs]
    sols = [_dot_split(ti, jnp.concatenate([v_ref[rows(n), lanes(hh, r)] * b, ks[kidx(n, hh)] * (b * e)], axis=1))
            for ti, b, e, (n, hh, r) in zip(tinvs, betas, egcs, inst)]
    qg16 = [(qs[kidx(n, hh)] * e).astype(BF16) for (n, hh, _), e in zip(inst, egcs)]
    qk16 = [(qk0s[kidx(n, hh)] * d).astype(BF16) for (n, hh, _), d in zip(inst, decays)]
    g_last = [gc[c - 1:c, :] for gc in gccs]
    kd16 = [(ks[kidx(n, hh)] * jnp.exp(gl - gc)).astype(BF16) for (n, hh, _), gl, gc in zip(inst, g_last, gccs)]
    ss = [s_ref[i] for i in range(nhd)]
    for n in range(ncb):
        ids = range(n * nhd, (n + 1) * nhd)
        s16 = [s.astype(BF16) for s in ss]
        u16 = [(sols[i][:, :LANES] - _dot(sols[i][:, LANES:].astype(BF16), sb)).astype(BF16)
               for i, sb in zip(ids, s16)]
        ss = [s * jnp.exp(g_last[i]) + _dot_tn(kd16[i], ub) for s, i, ub in zip(ss, ids, u16)]
        os_ = [_dot(qg16[i], sb) + _dot(qk16[i], ub) for i, sb, ub in zip(ids, s16, u16)]
        for o, i in zip(os_, ids):
            _, hh, r = inst[i]
            on = o * lax.rsqrt(jnp.mean(o * o, axis=-1, keepdims=True) + NORM_EPS) * nw_ref[...]
            o_ref[rows(n), lanes(hh, r)] = (on * _silu(z_ref[rows(n), lanes(hh, r)])).astype(BF16)
    for i in range(nhd):
        s_ref[i] = ss[i]

    @pl.when(ci == pl.num_programs(2) - 1)
    def _():
        so_ref[...] = s_ref[...]


def _gdn_core(qk, v, proj, zcol0, gc, beta, s0, norm_w, *, c, ncb, hpb, n_real):
    bsz, t, val = v.shape
    hv = s0.shape[1]
    dk, dv = s0.shape[2], s0.shape[3]
    hk = qk.shape[2] // (2 * dk)
    rep = hv // hk
    tb = ncb * c
    assert dk == LANES and dv == LANES and t % tb == 0 and hk % hpb == 0 and zcol0 % (hpb * rep * dv) == 0
    heads = lambda a, lo: a[:, :, lo:lo + hv].reshape(bsz, t, hk, rep).transpose(0, 2, 1, 3)
    gcc = heads(gc, hv)
    bcc = heads(beta, 0)
    gcr = gcc.transpose(0, 1, 3, 2)
    nhb = hk // hpb
    zb = zcol0 // (hpb * rep * dv)
    colspec = pl.BlockSpec((None, hpb, tb, rep), lambda b, h, i: (b, h, i, 0))
    o, s_out = pl.pallas_call(
        functools.partial(_gdn_core_kernel, rep=rep, c=c, ncb=ncb, hpb=hpb, dk=dk, n_real=n_real),
        grid=(bsz, nhb, t // tb),
        in_specs=[
            pl.BlockSpec((None, tb, hpb * dk), lambda b, h, i: (b, i, h)),
            pl.BlockSpec((None, tb, hpb * dk), lambda b, h, i: (b, i, nhb + h)),
            pl.BlockSpec((None, tb, hpb * rep * dv), lambda b, h, i: (b, i, h)),
            pl.BlockSpec((None, tb, hpb * rep * dv), lambda b, h, i: (b, i, zb + h)),
            colspec,
            pl.BlockSpec((None, hpb, rep, tb), lambda b, h, i: (b, h, 0, i)),
            colspec,
            pl.BlockSpec((None, hpb * rep, dk, dv), lambda b, h, i: (b, h, 0, 0)),
            pl.BlockSpec((1, dv), lambda b, h, i: (0, 0)),
        ],
        out_specs=[
            pl.BlockSpec((None, tb, hpb * rep * dv), lambda b, h, i: (b, i, h)),
            pl.BlockSpec((None, hpb * rep, dk, dv), lambda b, h, i: (b, h, 0, 0)),
        ],
        out_shape=[jax.ShapeDtypeStruct((bsz, t, val), BF16), jax.ShapeDtypeStruct(s0.shape, F32)],
        scratch_shapes=[pltpu.VMEM((hpb * rep, dk, dv), F32)],
        name="gdn_core",
        compiler_params=_cparams("parallel", "parallel", "arbitrary"),
    )(qk, qk, v, proj, gcc, gcr, bcc, s0, norm_w.reshape(1, dv))
    return o, s_out


def _gdn_mixer(x, mod3, nw, buf, s0, w_in, conv_w, a_log, dt_bias, norm_w, w_out, *, tm, tm_in, tt, chunk):
    bsz, t, d = x.shape
    hv, dk, dv = s0.shape[1], s0.shape[2], s0.shape[3]
    val = hv * dv
    conv_dim = conv_w.shape[1]
    key = (conv_dim - val) // 2
    assert (conv_dim + val) % LANES == 0 and 2 * hv <= LANES and t >= CONV_W - 1
    x2 = x.reshape(bsz * t, d)
    proj = _normproj(x2, mod3, 3, nw, w_in, tm=tm_in, tn_target=512).reshape(bsz, t, -1)
    buf8 = _pad_buf(buf)
    qk = _gdn_prep(proj, buf8, conv_w, col0=0, ncols=2 * key, norm=True, tt=tt)
    v = _gdn_prep(proj, buf8, conv_w, col0=2 * key, ncols=val, norm=False, tt=tt)
    tp = -(-t // chunk) * chunk
    beta, gc = _gdn_gates(proj, a_log, dt_bias, col0=conv_dim + val, hv=hv, chunk=min(chunk, tt), tt=tt)
    if tp != t:
        padt = lambda a: jnp.pad(a, ((0, 0), (0, tp - t), (0, 0)))
        gc = jnp.concatenate([gc, jnp.broadcast_to(gc[:, -1:], (bsz, tp - t, LANES))], axis=1)
        qk, v, beta, projz = padt(qk), padt(v), padt(beta), padt(proj)
    else:
        projz = proj
    o, s_new = _gdn_core(qk, v, projz, conv_dim, gc, beta, s0, norm_w, c=chunk, ncb=math.gcd(tp // chunk, 2),
                         hpb=min(4, key // dk), n_real=min(t, chunk))
    o2 = o[:, :t].reshape(bsz * t, val)
    xo = _outproj(o2, w_out, x2, mod3, 5, tm=tm).reshape(bsz, t, d)
    new_buf = proj[:, t - (CONV_W - 1):, :conv_dim]
    return xo, new_buf, s_new


def _rope_tables(pos, half):
    inv_freq = ROPE_THETA ** (-jnp.arange(half, dtype=F32) / half)
    ang = pos.astype(F32)[:, None] * inv_freq[None, :]
    cos, sin = jnp.cos(ang), jnp.sin(ang)
    return jnp.concatenate([cos, cos], axis=-1), jnp.concatenate([-sin, sin], axis=-1)


def _dsa_prep_kernel(x_ref, cos_ref, sin_ref, q_ref, k_ref, k16_ref, v16_ref, qi_ref, ki_ref, ki16_ref, *tr_refs,
                     nh, nkv, nih):
    cos, sin = cos_ref[...], sin_ref[...]

    def rope(col):
        seg = x_ref[:, col * LANES:(col + 1) * LANES]
        return seg * cos + pltpu.roll(seg, LANES // 2, 1) * sin

    for h in range(nh):
        q_ref[:, h * LANES:(h + 1) * LANES] = (rope(h) * LANES ** -0.5).astype(BF16)
    for h in range(nkv):
        kr = rope(nh + h)
        k_ref[:, h * LANES:(h + 1) * LANES] = kr
        k16_ref[:, h * LANES:(h + 1) * LANES] = kr.astype(BF16)
    v0 = (nh + nkv) * LANES
    v16_ref[...] = x_ref[:, v0:v0 + nkv * LANES].astype(BF16)
    c0 = nh + 2 * nkv
    for h in range(nih):
        qi_ref[:, h * LANES:(h + 1) * LANES] = rope(c0 + h).astype(BF16)
    kir = rope(c0 + nih)
    ki_ref[...] = kir
    ki16_ref[...] = kir.astype(BF16)
    if tr_refs:
        wt_ref, vt_ref = tr_refs
        w0 = (c0 + nih + 1) * LANES
        wt_ref[...] = x_ref[:, w0:w0 + LANES].T[:wt_ref.shape[0], :]
        for h in range(nkv):
            vt_ref[h * VT_ROWS:h * VT_ROWS + LANES, :] = x_ref[:, v0 + h * LANES:v0 + (h + 1) * LANES].T.astype(BF16)
            vt_ref[h * VT_ROWS + LANES:(h + 1) * VT_ROWS, :] = jnp.ones((VT_ROWS - LANES, vt_ref.shape[1]), BF16)


def _dsa_prep(proj, pos, *, nh, nkv, nih, tt, with_wt):
    bsz, t, npj = proj.shape
    cos, sin = _rope_tables(pos, LANES // 2)
    row = lambda n, dt: jax.ShapeDtypeStruct((bsz, t, n * LANES), dt)
    ospec = lambda n: pl.BlockSpec((None, tt, n * LANES), lambda b, i: (b, i, 0))
    tab = pl.BlockSpec((tt, LANES), lambda b, i: (i, 0))
    nwt = -(-nih // SUBLANES) * SUBLANES
    return pl.pallas_call(
        functools.partial(_dsa_prep_kernel, nh=nh, nkv=nkv, nih=nih),
        grid=(bsz, t // tt),
        in_specs=[pl.BlockSpec((None, tt, npj), lambda b, i: (b, i, 0)), tab, tab],
        out_specs=[ospec(nh), ospec(nkv), ospec(nkv), ospec(nkv), ospec(nih), ospec(1), ospec(1)]
        + ([pl.BlockSpec((None, nwt, tt), lambda b, i: (b, 0, i)),
            pl.BlockSpec((None, None, nkv * VT_ROWS, tt), lambda b, i: (b, i, 0, 0))] if with_wt else []),
        out_shape=[row(nh, BF16), row(nkv, F32), row(nkv, BF16), row(nkv, BF16), row(nih, BF16), row(1, F32),
                   row(1, BF16)]
        + ([jax.ShapeDtypeStruct((bsz, nwt, t), F32),
            jax.ShapeDtypeStruct((bsz, t // tt, nkv * VT_ROWS, tt), BF16)] if with_wt else []),
        name="dsa_prep",
        compiler_params=_cparams("parallel", "parallel"),
    )(proj, cos, sin)


def _sort_key(s):
    bits = pltpu.bitcast(jnp.where(s == 0.0, 0.0, s), I32)
    return jnp.where(bits < 0, bits ^ 0x7FFFFFFF, bits)


def _topk_cut(count_where, shape, topk, idx_bits):
    def body(i, carry):
        t, n_t = carry
        cand = t + lax.shift_left(jnp.int32(1), 31 - i)
        n = count_where(lambda key, idx: key >= cand)
        ok = n >= topk
        return jnp.where(ok, cand, t), jnp.where(ok, n, n_t)

    thr, n_ge = lax.fori_loop(0, 32, body, (jnp.full(shape, INT32_MIN, I32), jnp.full(shape, INT32_MAX, I32)))
    tie = (n_ge > topk) & (thr > KEY_OF_NEG_INF)

    def cut():
        need = topk - count_where(lambda key, idx: key > thr)

        def jbody(i, j):
            cand = j + lax.shift_left(jnp.int32(1), idx_bits - 1 - i)
            below = count_where(lambda key, idx: (key == thr) & (idx < cand))
            return jnp.where(below < need, cand, j)

        return jnp.where(tie, lax.fori_loop(0, idx_bits, jbody, jnp.zeros(shape, I32)), INT32_MAX)

    jcut = lax.cond(jnp.any(tie), cut, lambda: jnp.full(shape, INT32_MAX, I32))
    return thr, jcut


def _topk_chosen(key, idx, thr, jcut):
    return (key > thr) | ((key == thr) & (idx <= jcut))


def _dsa_prompt_kernel(qi_ref, wt_ref, q_ref, ki_ref, k_ref, vt_ref, o_ref, key_ref, bias_ref, acc_ref,
                       *, nh, nkv, nih, topk, idx_scale):
    qb = pl.program_id(1)
    blk = Q_BLOCK
    kb = vt_ref.shape[2]
    ktiles = kb // blk
    nkb = (qb + ktiles) // ktiles
    rowk = lax.broadcasted_iota(I32, (kb, blk), 0)
    colq = lax.broadcasted_iota(I32, (kb, blk), 1)
    wt = wt_ref[...]
    hq = max(1, nih // 4)
    qis = [jnp.concatenate([qi_ref[:, h * LANES:(h + 1) * LANES] for h in range(h0, min(h0 + hq, nih))], axis=0)
           for h0 in range(0, nih, hq)]

    def rows_of(j):
        return pl.ds(pl.multiple_of(j * kb, kb), kb)

    def visible(j):
        return j * kb + rowk <= qb * blk + colq

    def score_body(j, carry):
        keys = ki_ref[rows_of(j), :]
        lgs = [_dot_nt(keys, qi) for qi in qis]
        acc = jnp.zeros((kb, blk), F32)
        for i, lg in enumerate(lgs):
            for hh in range(lg.shape[1] // blk):
                h = i * hq + hh
                acc = acc + jnp.maximum(lg[:, hh * blk:(hh + 1) * blk], 0.0) * wt[h:h + 1, :]
        key_ref[rows_of(j), :] = _sort_key(jnp.where(visible(j), acc * idx_scale, NEG_INF))
        return carry

    lax.fori_loop(0, nkb, score_body, 0)

    def count_where(pred):
        def body(j, c):
            hit = jnp.where(pred(key_ref[rows_of(j), :], j * kb + rowk), 1, 0)
            for i in range(ktiles):
                c = c + hit[i * blk:(i + 1) * blk]
            return c
        cnt = lax.fori_loop(0, nkb, body, jnp.zeros((blk, blk), I32))
        return jnp.sum(cnt, axis=0, keepdims=True)

    thr, jcut = _topk_cut(count_where, (1, blk), topk, (key_ref.shape[0] - 1).bit_length())

    def bias_body(j, carry):
        sel = _topk_chosen(key_ref[rows_of(j), :], j * kb + rowk, thr, jcut) & visible(j)
        bias_ref[rows_of(j), :] = jnp.where(sel, 0.0, NEG_INF)
        return carry

    lax.fori_loop(0, nkb, bias_body, 0)

    rep = nh // nkv
    qgs =[jnp.concatenate([q_ref[:, (g * rep + r) * LANES:(g * rep + r + 1) * LANES] for r in range(rep)], axis=0)
           for g in range(nkv)]
    acc_ref[...] = jnp.zeros_like(acc_ref)

    def att_body(j, carry):
        ms, ls = carry
        rows = rows_of(j)
        bias = jnp.concatenate([bias_ref[rows, :]] * rep, axis=1)
        ss = [_dot_nt(k_ref[rows, g * LANES:(g + 1) * LANES], qgs[g]) + bias for g in range(nkv)]
        m_new = [jnp.maximum(m, jnp.max(s, axis=0, keepdims=True)) for m, s in zip(ms, ss)]
        ps = [jnp.exp((s - m).astype(BF16)) for s, m in zip(ss, m_new)]
        pvs = [_dot(vt_ref[j, g * VT_ROWS:(g + 1) * VT_ROWS, :], p) for g, p in enumerate(ps)]
        alphas = [jnp.exp(m - mn) for m, mn in zip(ms, m_new)]
        for g in range(nkv):
            acc_ref[g] = alphas[g] * acc_ref[g] + pvs[g][:LANES]
        l_new = [a * l + pv[LANES:LANES + 1] for a, l, pv in zip(alphas, ls, pvs)]
        return tuple(m_new), tuple(l_new)

    row0 = lambda v: tuple(jnp.full((1, rep * blk), v, F32) for _ in range(nkv))
    _, ls = lax.fori_loop(0, nkb, att_body, (row0(NEG_INF), row0(0.0)))
    for g in range(nkv):
        og = acc_ref[g] / ls[g]
        for r in range(rep):
            h = g * rep + r
            o_ref[:, h * LANES:(h + 1) * LANES] = og[:, r * blk:(r + 1) * blk].T.astype(BF16)


def _dsa_prompt_attend(q16, qi16, wt, ki16, k16, vt16, *, nh, nkv, nih, topk):
    bsz, t, _ = q16.shape
    rep = nh // nkv
    full = lambda n: pl.BlockSpec((None, t, n * LANES), lambda b, i: (b, 0, 0))
    blk = lambda n: pl.BlockSpec((None, Q_BLOCK, n * LANES), lambda b, i: (b, i, 0))
    return pl.pallas_call(
        functools.partial(_dsa_prompt_kernel, nh=nh, nkv=nkv, nih=nih, topk=topk,
                          idx_scale=(LANES * nih) ** -0.5),
        grid=(bsz, t // Q_BLOCK),
        in_specs=[blk(nih), pl.BlockSpec((None, wt.shape[1], Q_BLOCK), lambda b, i: (b, 0, i)), blk(nh),
                  full(1), full(nkv), pl.BlockSpec((None,) + vt16.shape[1:], lambda b, i: (b, 0, 0, 0))],
        out_specs=blk(nh),
        out_shape=jax.ShapeDtypeStruct((bsz, t, nh * LANES), BF16),
        scratch_shapes=[pltpu.VMEM((t, Q_BLOCK), I32), pltpu.VMEM((t, Q_BLOCK), F32),
                        pltpu.VMEM((nkv, LANES, rep * Q_BLOCK), F32)],
        name="dsa_prompt_attend",
        compiler_params=_cparams("parallel", "arbitrary"),
    )(qi16, wt, q16, ki16, k16, vt16)


def _dsa_sample_score_kernel(pt_ref, qi_ref, wc_ref, *refs, n_steps, pps, nih, tq, past, idx_scale):
    page_refs, new_ref, o_ref = refs[:pps], refs[pps], refs[pps + 1]
    p = pl.program_id(1)
    keys = jnp.concatenate([r[...] for r in page_refs], axis=0).astype(BF16)
    keys = jnp.where(p == n_steps - 1, new_ref[...], keys)
    qi = jnp.concatenate([qi_ref[:, h * LANES:(h + 1) * LANES] for h in range(nih)], axis=0)
    w = jnp.maximum(_dot_nt(qi, keys), 0.0) * wc_ref[...]
    acc = w[0:tq]
    for h in range(1, nih):
        acc = acc + w[h * tq:(h + 1) * tq]
    s = p * keys.shape[0] + lax.broadcasted_iota(I32, acc.shape, 1)
    qpos = past + lax.broadcasted_iota(I32, acc.shape, 0)
    o_ref[...] = jnp.where(s <= qpos, acc * idx_scale, NEG_INF)


def _dsa_sample_select_kernel(s_ref, o_ref, *, topk):
    key = _sort_key(s_ref[...])
    idx = lax.broadcasted_iota(I32, key.shape, 1)
    count_where = lambda pred: jnp.sum(jnp.where(pred(key, idx), 1, 0), axis=-1, keepdims=True)
    thr, jcut = _topk_cut(count_where, (key.shape[0], 1), topk, (key.shape[1] - 1).bit_length())
    o_ref[...] = jnp.where(_topk_chosen(key, idx, thr, jcut) & (s_ref[...] > 0.5 * NEG_INF), 1.0, 0.0)


def _dsa_sample_attn_kernel(pt_ref, q_ref, sel_ref, *refs, n_steps, pps, nh, nkv, tq):
    kp_refs, vp_refs = refs[:pps], refs[pps:2 * pps]
    kn_ref, vn_ref, o_ref, m_ref, l_ref, acc_ref = refs[2 * pps:]
    p = pl.program_id(1)
    rep = nh // nkv
    page = sel_ref.shape[1] // pps

    @pl.when(p == 0)
    def _():
        m_ref[...] = jnp.full_like(m_ref, NEG_INF)
        l_ref[...] = jnp.zeros_like(l_ref)
        acc_ref[...] = jnp.zeros_like(acc_ref)

    last = p == n_steps - 1

    def head_rows(page_refs, new_ref, g):
        cached = jnp.concatenate([r[pl.ds(g, page, stride=nkv), :] for r in page_refs], axis=0).astype(BF16)
        return jnp.where(last, new_ref[g], cached)

    rows = rep * tq
    q = jnp.concatenate([q_ref[:, h * LANES:(h + 1) * LANES] for h in range(nh)], axis=0)
    unsel = (sel_ref[...] - 1.0) * -NEG_INF
    s = jnp.concatenate([_dot_nt(q[g * rows:(g + 1) * rows], head_rows(kp_refs, kn_ref, g)) for g in range(nkv)],
                        axis=0) + jnp.concatenate([unsel] * nh, axis=0)
    m = m_ref[...]
    m_new = jnp.maximum(m, jnp.max(s, axis=-1, keepdims=True))
    alpha = jnp.exp(m - m_new)
    pr = jnp.where(s > 0.5 * NEG_INF, jnp.exp(s - m_new), 0.0)
    l_ref[...] = alpha * l_ref[...] + jnp.sum(pr, axis=-1, keepdims=True)
    pv = jnp.concatenate([_dot(pr[g * rows:(g + 1) * rows].astype(BF16), head_rows(vp_refs, vn_ref, g))
                          for g in range(nkv)], axis=0)
    acc_ref[...] = alpha * acc_ref[...] + pv
    m_ref[...] = m_new

    @pl.when(last)
    def _():
        og = acc_ref[...] / l_ref[...]
        for h in range(nh):
            o_ref[:, h * LANES:(h + 1) * LANES] = og[h * tq:(h + 1) * tq].astype(BF16)


def _dsa_sample_attend(q16, qi16, wi, ki16, k16, v16, cache_k, cache_v, cache_i, layer, page_table,
                       *, nh, nkv, nih, topk):
    bsz, tq, _ = q16.shape
    n_pages = page_table.shape[1]
    n_layers, n_pool, page = cache_i.shape[:3]
    pps = math.gcd(n_pages, SAMPLE_PAGES_PER_STEP)
    assert page == LANES and tq == SUBLANES
    past = n_pages * page
    n_steps = n_pages // pps + 1
    ltot = n_steps * pps * page
    wcol = wi.transpose(0, 2, 1).reshape(bsz, nih * tq, 1)
    pidx = lambda i: (lambda b, p, pt: (layer * n_pool + pt[b, jnp.minimum(p * pps + i, n_pages - 1)], 0, 0))
    qspec = lambda n: pl.BlockSpec((None, tq, n * LANES), lambda b, p, pt: (b, 0, 0))
    idx_scale = (LANES * nih) ** -0.5
    cache_i = cache_i.reshape(n_layers * n_pool, page, LANES)
    ki_new = jnp.pad(ki16, ((0, 0), (0, pps * page - tq), (0, 0)))
    scores = pl.pallas_call(
        functools.partial(_dsa_sample_score_kernel, n_steps=n_steps, pps=pps, nih=nih, tq=tq, past=past,
                          idx_scale=idx_scale),
        grid_spec=pltpu.PrefetchScalarGridSpec(
            num_scalar_prefetch=1, grid=(bsz, n_steps),
            in_specs=[qspec(nih), pl.BlockSpec((None, nih * tq, 1), lambda b, p, pt: (b, 0, 0))]
            + [pl.BlockSpec((None, page, LANES), pidx(i)) for i in range(pps)]
            + [pl.BlockSpec((None, pps * page, LANES), lambda b, p, pt: (b, 0, 0))],
            out_specs=pl.BlockSpec((None, tq, pps * page), lambda b, p, pt: (b, 0, p))),
        out_shape=jax.ShapeDtypeStruct((bsz, tq, ltot), F32),
        name="dsa_sample_scores",
        compiler_params=_cparams("parallel", "arbitrary"),
    )(page_table, qi16, wcol, *([cache_i] * pps), ki_new)
    sel = pl.pallas_call(
        functools.partial(_dsa_sample_select_kernel, topk=topk),
        grid=(bsz,),
        in_specs=[pl.BlockSpec((None, tq, ltot), lambda b: (b, 0, 0))],
        out_specs=pl.BlockSpec((None, tq, ltot), lambda b: (b, 0, 0)),
        out_shape=jax.ShapeDtypeStruct((bsz, tq, ltot), F32),
        name="dsa_sample_select",
        compiler_params=_cparams("parallel"),
    )(scores)
    prow = page * nkv
    ck = cache_k.reshape(n_layers * n_pool, prow, LANES)
    cv = cache_v.reshape(n_layers * n_pool, prow, LANES)
    new_rows = lambda a: jnp.pad(a.reshape(bsz, tq, nkv, LANES).transpose(0, 2, 1, 3),
                                 ((0, 0), (0, 0), (0, pps * page - tq), (0, 0)))
    kvspecs = [pl.BlockSpec((None, prow, LANES), pidx(i)) for i in range(pps)]
    newspec = pl.BlockSpec((None, nkv, pps * page, LANES), lambda b, p, pt: (b, 0, 0, 0))
    return pl.pallas_call(
        functools.partial(_dsa_sample_attn_kernel, n_steps=n_steps, pps=pps, nh=nh, nkv=nkv, tq=tq),
        grid_spec=pltpu.PrefetchScalarGridSpec(
            num_scalar_prefetch=1, grid=(bsz, n_steps),
            in_specs=[qspec(nh), pl.BlockSpec((None, tq, pps * page), lambda b, p, pt: (b, 0, p))]
            + kvspecs + kvspecs + [newspec, newspec],
            out_specs=qspec(nh),
            scratch_shapes=[pltpu.VMEM((nh * tq, 1), F32), pltpu.VMEM((nh * tq, 1), F32),
                            pltpu.VMEM((nh * tq, LANES), F32)]),
        out_shape=jax.ShapeDtypeStruct((bsz, tq, nh * LANES), BF16),
        name="dsa_sample_attend",
        compiler_params=_cparams("parallel", "arbitrary"),
    )(page_table, q16, sel, *([ck] * pps), *([cv] * pps), new_rows(k16), new_rows(v16))


def _dsa_mixer(x, mod3, nw, w_in, w_out, n_proj, *, nkv, tm, tm_in, tt, cache=None):
    bsz, t, d = x.shape
    nh = _wshape(w_out)[0] // LANES
    nih = (n_proj - (nh + 2 * nkv + 1) * LANES) // (LANES + 1)
    assert (nh + 2 * nkv + nih + 1) * LANES + nih == n_proj and nih <= LANES
    x2 = x.reshape(bsz * t, d)
    proj = _normproj(x2, mod3, 3, nw, w_in, tm=tm_in).reshape(bsz, t, -1)
    v = proj[:, :, (nh + nkv) * LANES:(nh + 2 * nkv) * LANES]
    if cache is None:
        assert t % Q_BLOCK == 0
        q16, k, k16, _, qi16, ki, ki16, wt, vt16 = _dsa_prep(proj, jnp.arange(t), nh=nh, nkv=nkv, nih=nih,
                                                             tt=_tile(t, 4 * Q_BLOCK, Q_BLOCK), with_wt=True)
        o = _dsa_prompt_attend(q16, qi16, wt, ki16, k16, vt16, nh=nh, nkv=nkv, nih=nih, topk=min(TOPK_MAX, t // 4))
    else:
        cache_k, cache_v, cache_i, layer, page_table = cache
        past = page_table.shape[1] * cache_i.shape[2]
        q16, k, k16, v16, qi16, ki, ki16 = _dsa_prep(proj, past + jnp.arange(t), nh=nh, nkv=nkv, nih=nih, tt=tt,
                                                    with_wt=False)
        w0 = (nh + 2 * nkv + nih + 1) * LANES
        o = _dsa_sample_attend(q16, qi16, proj[:, :, w0:w0 + nih], ki16, k16, v16, cache_k, cache_v, cache_i,
                               layer, page_table, nh=nh, nkv=nkv, nih=nih, topk=min(TOPK_MAX, (past + t) // 4))
    xo = _outproj(o.reshape(bsz * t, nh * LANES), w_out, x2, mod3, 5, tm=tm).reshape(bsz, t, d)
    return xo, k.reshape(bsz, t, nkv, LANES), v.reshape(bsz, t, nkv, LANES), ki


def _gelu_tanh(x):
    return 0.5 * x * (1.0 + jnp.tanh(math.sqrt(2.0 / math.pi) * (x + 0.044715 * (x * x * x))))


def _lru_kernel(gate_ref, xb_ref, halo_ref, buf_ref, cw_ref, cb_ref, wa_ref, ba_ref, wx_ref, bx_ref, lam_ref,
                h0_ref, y_ref, hl_ref, h_ref, *, nblk):
    ti = pl.program_id(1)

    @pl.when(ti == 0)
    def _():
        h_ref[...] = h0_ref[...]

    halo = jnp.where(ti == 0, buf_ref[...], halo_ref[...])
    xc = _causal_conv(xb_ref[...], halo, cw_ref[...]) + cb_ref[...]
    tt = xc.shape[0]
    xc16 = xc.astype(BF16)
    rs, xs = [], []
    for n in range(nblk):
        blk = xc16[:, n * LRU_BLOCK:(n + 1) * LRU_BLOCK]
        rs.append(_dot(blk, wa_ref[n]))
        xs.append(_dot(blk, wx_ref[n]))
    r = _sigmoid(jnp.concatenate(rs, axis=1) + ba_ref[...])
    ig = _sigmoid(jnp.concatenate(xs, axis=1) + bx_ref[...])
    lam = lam_ref[...]
    softplus_neg = jnp.maximum(-lam, 0.0) + jnp.log1p(jnp.exp(-jnp.abs(lam)))
    log_a = -RG_C * r * softplus_neg
    a = jnp.exp(log_a)
    b = jnp.sqrt(-jnp.tanh(log_a) * (a * a + 1.0)) * (ig * xc)
    row = lax.broadcasted_iota(I32, a.shape, 0)
    d = 1
    while d < tt:
        keep = row >= d
        a_sh = jnp.where(keep, pltpu.roll(a, d, 0), 1.0)
        b_sh = jnp.where(keep, pltpu.roll(b, d, 0), 0.0)
        b = a * b_sh + b
        a = a * a_sh
        d *= 2
    hs = b + a * h_ref[...]
    h_ref[...] = hs[tt - 1:tt, :]
    y_ref[...] = (hs * _gelu_tanh(gate_ref[...])).astype(BF16)

    @pl.when(ti == pl.num_programs(1) - 1)
    def _():
        hl_ref[...] = hs[tt - 1:tt, :]


def _lru_core(proj, buf8, h0, conv_w, conv_b, w_ga, b_ga, w_gx, b_gx, lam, *, tt):
    bsz, t, w2 = proj.shape
    w = w2 // 2
    nblk = w // LRU_BLOCK
    sub = tt // SUBLANES
    vec = lambda: pl.BlockSpec((1, w), lambda b, i: (0, 0))
    wsp = lambda: pl.BlockSpec((nblk, LRU_BLOCK, LRU_BLOCK), lambda b, i: (0, 0, 0))
    r1 = lambda v: v.reshape(1, w).astype(F32)
    y, hl = pl.pallas_call(
        functools.partial(_lru_kernel, nblk=nblk),
        grid=(bsz, t // tt),
        in_specs=[
            pl.BlockSpec((None, tt, w), lambda b, i: (b, i, 0)),
            pl.BlockSpec((None, tt, w), lambda b, i: (b, i, 1)),
            pl.BlockSpec((None, SUBLANES, w), lambda b, i: (b, jnp.maximum(i * sub - 1, 0), 1)),
            pl.BlockSpec((None, SUBLANES, w), lambda b, i: (b, 0, 0)),
            pl.BlockSpec((CONV_W, w), lambda b, i: (0, 0)),
            vec(), wsp(), vec(), wsp(), vec(), vec(),
            pl.BlockSpec((None, 1, w), lambda b, i: (b, 0, 0)),
        ],
        out_specs=[pl.BlockSpec((None, tt, w), lambda b, i: (b, i, 0)),
                   pl.BlockSpec((None, 1, w), lambda b, i: (b, 0, 0))],
        out_shape=[jax.ShapeDtypeStruct((bsz, t, w), BF16), jax.ShapeDtypeStruct((bsz, 1, w), F32)],
        scratch_shapes=[pltpu.VMEM((1, w), F32)],
        name="lru_core",
        compiler_params=_cparams("parallel", "arbitrary"),
    )(proj, proj, proj, buf8, conv_w, r1(conv_b), w_ga, r1(b_ga), w_gx, r1(b_gx), r1(lam), h0.reshape(bsz, 1, w))
    return y, hl.reshape(bsz, w)


def _lru_mixer(x, mod3, nw, buf, h0, w_in, conv_w, conv_b, w_ga, b_ga, w_gx, b_gx, lam, w_out, *, tm, tm_in, tt):
    bsz, t, d = x.shape
    assert t >= CONV_W - 1
    x2 = x.reshape(bsz * t, d)
    proj = _normproj(x2, mod3, 3, nw, w_in, tm=tm_in).reshape(bsz, t, -1)
    w = proj.shape[2] // 2
    y, hl = _lru_core(proj, _pad_buf(buf), h0, conv_w, conv_b, w_ga, b_ga, w_gx, b_gx, lam, tt=tt)
    xo = _outproj(y.reshape(bsz * t, w), w_out, x2, mod3, 5, tm=tm).reshape(bsz, t, d)
    return xo, proj[:, t - (CONV_W - 1):, w:], hl


def _bf16_padded(w, mult):
    n = w.shape[-1]
    return jnp.pad(w, ((0, 0), (0, 0), (0, -n % mult))).astype(BF16)


def kernel(x_prompt, x_sample, state_a_conv, state_a_ssm, cache_b_k, cache_b_v, cache_b_idx, state_c_conv, state_c_h, page_table, c_prompt, c_sample, w_ada, b_ada, norm_w, ffn_w_in, ffn_w_out, gdn_w_in, gdn_conv_w, gdn_a_log, gdn_dt_bias, gdn_norm_w, gdn_w_out, dsa_w_in, dsa_w_out, lru_w_in, lru_conv_w, lru_conv_b, lru_w_gate_a, lru_b_gate_a, lru_w_gate_x, lru_b_gate_x, lru_lambda, lru_w_out, w_ada_final, b_ada_final, final_norm_w):
    bp, seq, d = x_prompt.shape
    bs, ts, _ = x_sample.shape
    depth = w_ada.shape[0]
    nkv = cache_b_k.shape[3]
    tm_p = _tile(seq, 512, SUBLANES)
    tm_in_p = _tile(seq, 1024, SUBLANES)
    tt_p = _tile(seq, 256, SUBLANES)
    tm_s = bs * ts
    tiles_p = dict(tm=tm_p, tm_in=tm_in_p, tt=tt_p)
    tiles_s = dict(tm=tm_s, tm_in=tm_s, tt=ts)

    c_all = jnp.concatenate([c_prompt, c_sample], axis=0)
    c_all = jnp.pad(c_all, ((0, -c_all.shape[0] % SUBLANES), (0, 0)))
    mod = _ada(c_all, w_ada, b_ada)
    mod_f = _ada(c_all, w_ada_final[None], b_ada_final[None])[0]

    def groups(m):
        return m[:bp, None, :], jnp.repeat(m[bp:bp + bs], ts, axis=0)[None]

    ffn_w_out16 = ffn_w_out.astype(BF16)
    gdn_w_in16, gdn_w_out16 = _bf16_padded(gdn_w_in, 512), gdn_w_out.astype(BF16)
    dsa_w_in16, dsa_w_out16 = _bf16_padded(dsa_w_in, 512), dsa_w_out.astype(BF16)
    lru_w_in16, lru_w_out16 = lru_w_in.astype(BF16), lru_w_out.astype(BF16)

    xp, xs = x_prompt, x_sample
    outs = {k: [] for k in ("a_conv_p", "a_conv_s", "a_ssm_p", "a_ssm_s", "b_k_p", "b_k_s", "b_v_p", "b_v_s",
                            "b_i_p", "b_i_s", "c_conv_p", "c_conv_s", "c_h_p", "c_h_s")}
    for layer in range(depth):
        kind, j = layer % N_MIXERS, layer // N_MIXERS
        mod_p, mod_s = groups(mod[layer])
        nw = norm_w[layer]

        def ffn(x, m3, tm, which):
            b, t, _ = x.shape
            x2 = x.reshape(b * t, d)
            h = _ffn_up(x2, m3, 6 * which, nw[2 * which:2 * which + 1], (ffn_w_in, (layer, which)), tm=tm)
            return _outproj(h, (ffn_w_out16, (layer, which)), x2, m3, 6 * which + 2, tm=tm, tn_target=512,
                            gain=0.5).reshape(b, t, d)

        xp = ffn(xp, mod_p, tm_in_p, 0)
        xs = ffn(xs, mod_s, tm_s, 0)
        if kind == 0:
            prm = ((gdn_w_in16, (j,)), gdn_conv_w[j], gdn_a_log[j], gdn_dt_bias[j], gdn_norm_w[j],
                   (gdn_w_out16, (j,)))
            buf0 = jnp.zeros((bp,) + state_a_conv.shape[2:], F32)
            s0 = jnp.zeros((bp,) + state_a_ssm.shape[2:], F32)
            xp, buf, s = _gdn_mixer(xp, mod_p, nw[1:2], buf0, s0, *prm, chunk=2 * GDN_CHUNK, **tiles_p)
            outs["a_conv_p"].append(buf)
            outs["a_ssm_p"].append(s)
            xs, buf, s = _gdn_mixer(xs, mod_s, nw[1:2], state_a_conv[j], state_a_ssm[j], *prm, chunk=GDN_CHUNK,
                                    **tiles_s)
            outs["a_conv_s"].append(buf)
            outs["a_ssm_s"].append(s)
        elif kind == 1:
            w_in, w_out = (dsa_w_in16, (j,)), (dsa_w_out16, (j,))
            n_proj = dsa_w_in.shape[2]
            xp, k, v, ki = _dsa_mixer(xp, mod_p, nw[1:2], w_in, w_out, n_proj, nkv=nkv, **tiles_p)
            outs["b_k_p"].append(k)
            outs["b_v_p"].append(v)
            outs["b_i_p"].append(ki)
            xs, k, v, ki = _dsa_mixer(xs, mod_s, nw[1:2], w_in, w_out, n_proj, nkv=nkv,
                                      cache=(cache_b_k, cache_b_v, cache_b_idx, j, page_table), **tiles_s)
            outs["b_k_s"].append(k)
            outs["b_v_s"].append(v)
            outs["b_i_s"].append(ki)
        else:
            prm = ((lru_w_in16, (j,)), lru_conv_w[j], lru_conv_b[j], lru_w_gate_a[j].astype(BF16),
                   lru_b_gate_a[j], lru_w_gate_x[j].astype(BF16), lru_b_gate_x[j], lru_lambda[j],
                   (lru_w_out16, (j,)))
            buf0 = jnp.zeros((bp,) + state_c_conv.shape[2:], F32)
            h0 = jnp.zeros((bp,) + state_c_h.shape[2:], F32)
            xp, buf, hl = _lru_mixer(xp, mod_p, nw[1:2], buf0, h0, *prm, **tiles_p)
            outs["c_conv_p"].append(buf)
            outs["c_h_p"].append(hl)
            xs, buf, hl = _lru_mixer(xs, mod_s, nw[1:2], state_c_conv[j], state_c_h[j], *prm, **tiles_s)
            outs["c_conv_s"].append(buf)
            outs["c_h_s"].append(hl)
        xp = ffn(xp, mod_p, tm_in_p, 1)
        xs = ffn(xs, mod_s, tm_s, 1)

    modf_p, modf_s = groups(mod_f)
    fnw = final_norm_w.reshape(1, d)
    y_p = _final_norm(xp.reshape(bp * seq, d), modf_p, fnw, tm=tm_p).reshape(bp, seq, d)
    y_s = _final_norm(xs.reshape(bs * ts, d), modf_s, fnw, tm=tm_s).reshape(bs, ts, d)
    st = {k: jnp.stack(v) for k, v in outs.items()}
    return (y_p, y_s, st["a_conv_p"], st["a_conv_s"], st["a_ssm_p"], st["a_ssm_s"], st["b_k_p"], st["b_k_s"],
            st["b_v_p"], st["b_v_s"], st["b_i_p"], st["b_i_s"], st["c_conv_p"], st["c_conv_s"], st["c_h_p"],
            st["c_h_s"])
```

```python
import functools
import math

import jax
import jax.numpy as jnp
import numpy as np
from jax import lax
from jax.experimental import pallas as pl
from jax.experimental.pallas import tpu as pltpu

F32 = jnp.float32
BF16 = jnp.bfloat16
I32 = jnp.int32

N_MIXERS = 3
N_ADA = 9
CONV_W = 4
NORM_EPS = 1e-6
NEG_INF = -1e30
GDN_CHUNK = 64
TOPK_MAX = 256
Q_BLOCK = 128
ROPE_THETA = 10000.0
RG_C = 8.0
LRU_BLOCK = 256
SAMPLE_PAGES_PER_STEP = 8

LANES = 128
SUBLANES = 8
VMEM_LIMIT_BYTES = 56 * 2**20
VT_ROWS = LANES + 16
INT32_MIN = -2**31
INT32_MAX = 2**31 - 1
KEY_OF_NEG_INF = int(np.float32(NEG_INF).view(np.int32)) ^ 0x7FFFFFFF


def _cparams(*sem):
    return pltpu.CompilerParams(dimension_semantics=sem, vmem_limit_bytes=VMEM_LIMIT_BYTES)


def _tile(n, target, align=LANES):
    if n <= target:
        return n
    t = (target // align) * align
    while t >= align:
        if n % t == 0:
            return t
        t -= align
    raise ValueError(f"no {align}-aligned tile of {n} below {target}")


def _sigmoid(x):
    return jax.nn.sigmoid(x)


def _silu(x):
    return x * _sigmoid(x)


def _dot(a, b):
    return jnp.dot(a, b, preferred_element_type=F32)


def _dot_nt(a, b):
    return lax.dot_general(a, b, (((1,), (1,)), ((), ())), preferred_element_type=F32)


def _dot_tn(a, b):
    return lax.dot_general(a, b, (((0,), (0,)), ((), ())), preferred_element_type=F32)


def _dot_f32(a, b):
    return jnp.dot(a, b, preferred_element_type=F32, precision=lax.Precision.HIGHEST)


def _norm_mod(x, nw, sh, sc):
    ms = jnp.mean(x * x, axis=-1, keepdims=True)
    return x * lax.rsqrt(ms + NORM_EPS) * (nw * (1.0 + sc)) + sh


def _ada_kernel(c_ref, w_ref, b_ref, o_ref):
    a = _silu(c_ref[...]).astype(BF16)
    o_ref[0] = _dot(a, w_ref[0].astype(BF16)) + b_ref[0]


def _ada(c_all, w, b):
    n_l, d, n = w.shape
    mp = c_all.shape[0]
    tn = _tile(n, 1024)
    return pl.pallas_call(
        _ada_kernel,
        grid=(n_l, n // tn),
        in_specs=[
            pl.BlockSpec((mp, d), lambda l, j: (0, 0)),
            pl.BlockSpec((1, d, tn), lambda l, j: (l, 0, j)),
            pl.BlockSpec((1, 1, tn), lambda l, j: (l, 0, j)),
        ],
        out_specs=pl.BlockSpec((1, mp, tn), lambda l, j: (l, 0, j)),
        out_shape=jax.ShapeDtypeStruct((n_l, mp, n), F32),
        name="ada_mod",
        compiler_params=_cparams("parallel", "parallel"),
    )(c_all, w, b.reshape(n_l, 1, n))


def _mod_spec(mod3, k, d, tm, rows_per_group):
    return pl.BlockSpec((None, mod3.shape[1], d), lambda i, j: ((i * tm) // rows_per_group, 0, k))


def _wspec(w, block, imap):
    arr, idx = w
    return pl.BlockSpec((None,) * len(idx) + block, lambda i, j: idx + imap(i, j))


def _wshape(w):
    return w[0].shape[len(w[1]):]


def _ffn_kernel(x_ref, nw_ref, sh_ref, sc_ref, g_ref, wa_ref, wb_ref, wo_ref, o_ref, xn_ref):
    f = pl.program_id(1)

    @pl.when(f == 0)
    def _():
        xn_ref[...] = _norm_mod(x_ref[...], nw_ref[...], sh_ref[...], sc_ref[...]).astype(BF16)
        o_ref[...] = jnp.zeros_like(o_ref)

    xn = xn_ref[...]
    a = _dot(xn, wa_ref[...])
    b = _dot(xn, wb_ref[...])
    o_ref[...] += _dot((_silu(a) * b).astype(BF16), wo_ref[...])

    @pl.when(f == pl.num_programs(1) - 1)
    def _():
        o_ref[...] = x_ref[...] + 0.5 * g_ref[...] * o_ref[...]


def _ffn(x, mod3, k0, nw, w_in, w_out, *, tm, tf_target):
    m, d = x.shape
    f = _wshape(w_out)[0]
    tf = _tile(f, tf_target)
    nf = f // tf
    rpg = m // mod3.shape[0]
    ms = lambda k: _mod_spec(mod3, k, d, tm, rpg)
    return pl.pallas_call(
        _ffn_kernel,
        grid=(m // tm, nf),
        in_specs=[
            pl.BlockSpec((tm, d), lambda i, j: (i, 0)),
            pl.BlockSpec((1, d), lambda i, j: (0, 0)),
            ms(k0), ms(k0 + 1), ms(k0 + 2),
            _wspec(w_in, (d, tf), lambda i, j: (0, j)),
            _wspec(w_in, (d, tf), lambda i, j: (0, j + nf)),
            _wspec(w_out, (tf, d), lambda i, j: (j, 0)),
        ],
        out_specs=pl.BlockSpec((tm, d), lambda i, j: (i, 0)),
        out_shape=jax.ShapeDtypeStruct((m, d), F32),
        scratch_shapes=[pltpu.VMEM((tm, d), BF16)],
        name="ffn",
        compiler_params=_cparams("parallel", "arbitrary"),
    )(x, nw, mod3, mod3, mod3, w_in[0], w_in[0], w_out[0])


def _normproj_kernel(x_ref, nw_ref, sh_ref, sc_ref, w_ref, o_ref, xn_ref):
    @pl.when(pl.program_id(1) == 0)
    def _():
        xn_ref[...] = _norm_mod(x_ref[...], nw_ref[...], sh_ref[...], sc_ref[...]).astype(BF16)

    o_ref[...] = _dot(xn_ref[...], w_ref[...])


def _normproj(x, mod3, k0, nw, w, *, tm, tn_target=1024):
    m, d = x.shape
    n = _wshape(w)[1]
    tn = _tile(n, tn_target)
    rpg = m // mod3.shape[0]
    ms = lambda k: _mod_spec(mod3, k, d, tm, rpg)
    return pl.pallas_call(
        _normproj_kernel,
        grid=(m // tm, n // tn),
        in_specs=[
            pl.BlockSpec((tm, d), lambda i, j: (i, 0)),
            pl.BlockSpec((1, d), lambda i, j: (0, 0)),
            ms(k0), ms(k0 + 1),
            _wspec(w, (d, tn), lambda i, j: (0, j)),
        ],
        out_specs=pl.BlockSpec((tm, tn), lambda i, j: (i, j)),
        out_shape=jax.ShapeDtypeStruct((m, n), F32),
        scratch_shapes=[pltpu.VMEM((tm, d), BF16)],
        name="normproj",
        compiler_params=_cparams("parallel", "arbitrary"),
    )(x, nw, mod3, mod3, w[0])


def _ffn_up_kernel(x_ref, nw_ref, sh_ref, sc_ref, wa_ref, wb_ref, h_ref, xn_ref):
    @pl.when(pl.program_id(1) == 0)
    def _():
        xn_ref[...] = _norm_mod(x_ref[...], nw_ref[...], sh_ref[...], sc_ref[...]).astype(BF16)

    xn = xn_ref[...]
    a = _dot(xn, wa_ref[...].astype(BF16))
    b = _dot(xn, wb_ref[...].astype(BF16))
    h_ref[...] = (_silu(a) * b).astype(BF16)


def _ffn_up(x, mod3, k0, nw, w_in, *, tm, tf_target=512):
    m, d = x.shape
    f = _wshape(w_in)[1] // 2
    tf = _tile(f, tf_target)
    nf = f // tf
    rpg = m // mod3.shape[0]
    ms = lambda k: _mod_spec(mod3, k, d, tm, rpg)
    return pl.pallas_call(
        _ffn_up_kernel,
        grid=(m // tm, nf),
        in_specs=[
            pl.BlockSpec((tm, d), lambda i, j: (i, 0)),
            pl.BlockSpec((1, d), lambda i, j: (0, 0)),
            ms(k0), ms(k0 + 1),
            _wspec(w_in, (d, tf), lambda i, j: (0, j)),
            _wspec(w_in, (d, tf), lambda i, j: (0, j + nf)),
        ],
        out_specs=pl.BlockSpec((tm, tf), lambda i, j: (i, j)),
        out_shape=jax.ShapeDtypeStruct((m, f), BF16),
        scratch_shapes=[pltpu.VMEM((tm, d), BF16)],
        name="ffn_up",
        compiler_params=_cparams("parallel", "arbitrary"),
    )(x, nw, mod3, mod3, w_in[0], w_in[0])


def _outproj_kernel(a_ref, w_ref, x_ref, g_ref, o_ref, *, gain):
    o_ref[...] = x_ref[...] + gain * g_ref[...] * _dot(a_ref[...], w_ref[...])


def _outproj(a, w, x, mod3, kg, *, tm, tn_target=1024, gain=1.0):
    m, kdim = a.shape
    d = _wshape(w)[1]
    tn = _tile(d, tn_target)
    rpg = m // mod3.shape[0]
    r = mod3.shape[1]
    nd = d // tn
    return pl.pallas_call(
        functools.partial(_outproj_kernel, gain=gain),
        grid=(m // tm, nd),
        in_specs=[
            pl.BlockSpec((tm, kdim), lambda i, j: (i, 0)),
            _wspec(w, (kdim, tn), lambda i, j: (0, j)),
            pl.BlockSpec((tm, tn), lambda i, j: (i, j)),
            pl.BlockSpec((None, r, tn), lambda i, j: ((i * tm) // rpg, 0, kg * nd + j)),
        ],
        out_specs=pl.BlockSpec((tm, tn), lambda i, j: (i, j)),
        out_shape=jax.ShapeDtypeStruct((m, d), F32),
        name="outproj",
        compiler_params=_cparams("parallel", "arbitrary"),
    )(a, w[0], x, mod3)


def _final_kernel(x_ref, nw_ref, sh_ref, sc_ref, o_ref):
    o_ref[...] = _norm_mod(x_ref[...], nw_ref[...], sh_ref[...], sc_ref[...])


def _final_norm(x, mod3, nw, *, tm):
    m, d = x.shape
    rpg = m // mod3.shape[0]
    r = mod3.shape[1]
    ms = lambda k: pl.BlockSpec((None, r, d), lambda i: ((i * tm) // rpg, 0, k))
    return pl.pallas_call(
        _final_kernel,
        grid=(m // tm,),
        in_specs=[pl.BlockSpec((tm, d), lambda i: (i, 0)), pl.BlockSpec((1, d), lambda i: (0, 0)), ms(0), ms(1)],
        out_specs=pl.BlockSpec((tm, d), lambda i: (i, 0)),
        out_shape=jax.ShapeDtypeStruct((m, d), F32),
        name="final_norm",
        compiler_params=_cparams("parallel"),
    )(x, nw, mod3, mod3)


def _causal_conv(x, halo, w):
    def taps(rows, fix):
        acc = rows * w[CONV_W - 1:CONV_W]
        for j in range(1, CONV_W):
            acc = acc + fix(pltpu.roll(rows, j, 0), j) * w[CONV_W - 1 - j:CONV_W - j]
        return acc

    row = lax.broadcasted_iota(I32, halo.shape, 0)
    head = taps(x[:SUBLANES], lambda r, j: jnp.where(row < j, pltpu.roll(halo, j, 0), r))
    if x.shape[0] == SUBLANES:
        return head
    return jnp.concatenate([head, taps(x, lambda r, j: r)[SUBLANES:]], axis=0)


def _conv_specs(tt, tc, coff, boff):
    sub = tt // SUBLANES
    return [
        pl.BlockSpec((None, tt, tc), lambda b, t, c: (b, t, c + coff)),
        pl.BlockSpec((None, SUBLANES, tc), lambda b, t, c: (b, jnp.maximum(t * sub - 1, 0), c + coff)),
        pl.BlockSpec((None, SUBLANES, tc), lambda b, t, c: (b, 0, c + boff)),
    ]


def _pad_buf(buf):
    return jnp.pad(buf, ((0, 0), (SUBLANES - (CONV_W - 1), 0), (0, 0)))


def _gdn_prep_kernel(x_ref, halo_ref, buf_ref, w_ref, o_ref, *, norm):
    halo = jnp.where(pl.program_id(1) == 0, buf_ref[...], halo_ref[...])
    y = _silu(_causal_conv(x_ref[...], halo, w_ref[...]))
    if norm:
        for h in range(y.shape[1] // LANES):
            seg = y[:, h * LANES:(h + 1) * LANES]
            ss = jnp.sum(seg * seg, axis=-1, keepdims=True)
            o_ref[:, h * LANES:(h + 1) * LANES] = seg * lax.rsqrt(ss + NORM_EPS)
    else:
        o_ref[...] = y


def _gdn_prep(proj, buf8, conv_w, *, col0, ncols, norm, tt):
    bsz, t, _ = proj.shape
    tc = _tile(ncols, 1024)
    return pl.pallas_call(
        functools.partial(_gdn_prep_kernel, norm=norm),
        grid=(bsz, t // tt, ncols // tc),
        in_specs=_conv_specs(tt, tc, col0 // tc, col0 // tc)
        + [pl.BlockSpec((CONV_W, tc), lambda b, i, c: (0, c + col0 // tc))],
        out_specs=pl.BlockSpec((None, tt, tc), lambda b, i, c: (b, i, c)),
        out_shape=jax.ShapeDtypeStruct((bsz, t, ncols), F32),
        name="gdn_prep",
        compiler_params=_cparams("parallel", "parallel", "parallel"),
    )(proj, proj, buf8, conv_w)


def _gdn_gate_kernel(x_ref, alog_ref, dtb_ref, beta_ref, gc_ref, *, hv, chunk):
    x = x_ref[...]
    tt = x.shape[0]
    beta_ref[...] = _sigmoid(x)
    z = x + dtb_ref[...]
    g = -jnp.exp(alog_ref[...]) * (jnp.maximum(z, 0.0) + jnp.log1p(jnp.exp(-jnp.abs(z))))
    row = lax.broadcasted_iota(I32, (tt, tt), 0)
    col = lax.broadcasted_iota(I32, (tt, tt), 1)
    tri = jnp.where((row >= col) & (row // chunk == col // chunk), 1.0, 0.0)
    gc_ref[...] = _dot_f32(tri, g)


def _gdn_gates(proj, a_log, dt_bias, *, col0, hv, chunk, tt):
    bsz, t, _ = proj.shape
    pad = lambda v: jnp.pad(v.astype(F32), (hv, LANES - 2 * hv)).reshape(1, LANES)
    blk = pl.BlockSpec((None, tt, LANES), lambda b, i: (b, i, col0 // LANES))
    out = pl.BlockSpec((None, tt, LANES), lambda b, i: (b, i, 0))
    par = pl.BlockSpec((1, LANES), lambda b, i: (0, 0))
    return pl.pallas_call(
        functools.partial(_gdn_gate_kernel, hv=hv, chunk=chunk),
        grid=(bsz, t // tt),
        in_specs=[blk, par, par],
        out_specs=[out, out],
        out_shape=[jax.ShapeDtypeStruct((bsz, t, LANES), F32)] * 2,
        name="gdn_gates",
        compiler_params=_cparams("parallel", "parallel"),
    )(proj, pad(a_log), pad(dt_bias))


def _split_bf16(a):
    hi = a.astype(BF16)
    return hi, (a - hi.astype(F32)).astype(BF16)


def _dot_split(a, b):
    ah, al = _split_bf16(a)
    bh, bl = _split_bf16(b)
    return _dot(jnp.concatenate([ah, al, ah], axis=1), jnp.concatenate([bh, bh, bl], axis=0))


def _dot_bf16(a, b):
    return _dot(a.astype(BF16), b.astype(BF16))


def _tri_inv_all(lmats, c, n_real):
    row = lax.broadcasted_iota(I32, (c, c), 0)
    col = lax.broadcasted_iota(I32, (c, c), 1)
    eye = jnp.where(row == col, 1.0, 0.0)
    base = min(16, c)
    ps = [-jnp.where(row // base == col // base, m, 0.0) for m in lmats]
    rs = [eye + p for p in ps]
    n = 2
    while n < base:
        ps = [_dot_bf16(p, p) for p in ps]
        rs = [r + _dot_bf16(r, p) for r, p in zip(rs, ps)]
        n *= 2
    s = base
    while s < min(c, n_real):
        off = (row // (2 * s) == col // (2 * s)) & (row // s != col // s)
        ts = [_dot_bf16(jnp.where(off, m, 0.0), r) for m, r in zip(lmats, rs)]
        rs = [r - _dot_bf16(r, t) for r, t in zip(rs, ts)]
        s *= 2
    res = [eye - r - _dot_split(m, r) for m, r in zip(lmats, rs)]
    return [r + _dot_bf16(r, e) for r, e in zip(rs, res)]


def _gdn_core_kernel(q_ref, k_ref, v_ref, z_ref, gcc_ref, gcr_ref, bc_ref, s0_ref, nw_ref, o_ref, so_ref, s_ref,
                     *, rep, c, ncb, hpb, dk, n_real):
    ci = pl.program_id(2)

    @pl.when(ci == 0)
    def _():
        s_ref[...] = s0_ref[...]

    row = lax.broadcasted_iota(I32, (c, c), 0)
    col = lax.broadcasted_iota(I32, (c, c), 1)
    causal = row >= col
    nhd = hpb * rep
    kinst = [(n, hh) for n in range(ncb) for hh in range(hpb)]
    inst = [(n, hh, r) for n in range(ncb) for hh in range(hpb) for r in range(rep)]
    rows = lambda n: slice(n * c, (n + 1) * c)
    lanes = lambda hh, r: slice((hh * rep + r) * LANES, (hh * rep + r + 1) * LANES)
    kidx = lambda n, hh: n * hpb + hh
    qs = [q_ref[rows(n), hh * LANES:(hh + 1) * LANES] * dk ** -0.5 for n, hh in kinst]
    ks = [k_ref[rows(n), hh * LANES:(hh + 1) * LANES] for n, hh in kinst]
    k16 = [k.astype(BF16) for k in ks]
    grams = [_dot_nt(kb, kb) for kb in k16]
    qk0s = [_dot_nt(q.astype(BF16), kb) for q, kb in zip(qs, k16)]
    gccs = [gcc_ref[hh, rows(n), r:r + 1] for n, hh, r in inst]
    gcrs = [gcr_ref[hh, r:r + 1, rows(n)] for n, hh, r in inst]
    betas = [bc_ref[hh, rows(n), r:r + 1] for n, hh, r in inst]
    decays = [jnp.where(causal, jnp.exp(jnp.where(causal, gc - gr, 0.0)), 0.0) for gc, gr in zip(gccs, gcrs)]
    lowers = [jnp.where(row > col, grams[kidx(n, hh)] * b * d, 0.0)
              for (n, hh, _), b, d in zip(inst, betas, decays)]
    tinvs = _tri_inv_all(lowers, c, n_real)
    egcs = [jnp.exp(gc) for gc in gccs]
    sols = [_dot_split(ti, jnp.concatenate([v_ref[rows(n), lanes(hh, r)] * b, ks[kidx(n, hh)] * (b * e)], axis=1))
            for ti, b, e, (n, hh, r) in zip(tinvs, betas, egcs, inst)]
    qg16 = [(qs[kidx(n, hh)] * e).astype(BF16) for (n, hh, _), e in zip(inst, egcs)]
    qk16 = [(qk0s[kidx(n, hh)] * d).astype(BF16) for (n, hh, _), d in zip(inst, decays)]
    g_last = [gc[c - 1:c, :] for gc in gccs]
    kd16 = [(ks[kidx(n, hh)] * jnp.exp(gl - gc)).astype(BF16) for (n, hh, _), gl, gc in zip(inst, g_last, gccs)]
    ss = [s_ref[i] for i in range(nhd)]
    for n in range(ncb):
        ids = range(n * nhd, (n + 1) * nhd)
        s16 = [s.astype(BF16) for s in ss]
        u16 = [(sols[i][:, :LANES] - _dot(sols[i][:, LANES:].astype(BF16), sb)).astype(BF16)
               for i, sb in zip(ids, s16)]
        ss = [s * jnp.exp(g_last[i]) + _dot_tn(kd16[i], ub) for s, i, ub in zip(ss, ids, u16)]
        os_ = [_dot(qg16[i], sb) + _dot(qk16[i], ub) for i, sb, ub in zip(ids, s16, u16)]
        for o, i in zip(os_, ids):
            _, hh, r = inst[i]
            on = o * lax.rsqrt(jnp.mean(o * o, axis=-1, keepdims=True) + NORM_EPS) * nw_ref[...]
            o_ref[rows(n), lanes(hh, r)] = (on * _silu(z_ref[rows(n), lanes(hh, r)])).astype(BF16)
    for i in range(nhd):
        s_ref[i] = ss[i]

    @pl.when(ci == pl.num_programs(2) - 1)
    def _():
        so_ref[...] = s_ref[...]


def _gdn_core(qk, v, proj, zcol0, gc, beta, s0, norm_w, *, c, ncb, hpb, n_real):
    bsz, t, val = v.shape
    hv = s0.shape[1]
    dk, dv = s0.shape[2], s0.shape[3]
    hk = qk.shape[2] // (2 * dk)
    rep = hv // hk
    tb = ncb * c
    assert dk == LANES and dv == LANES and t % tb == 0 and hk % hpb == 0 and zcol0 % (hpb * rep * dv) == 0
    heads = lambda a, lo: a[:, :, lo:lo + hv].reshape(bsz, t, hk, rep).transpose(0, 2, 1, 3)
    gcc = heads(gc, hv)
    bcc = heads(beta, 0)
    gcr = gcc.transpose(0, 1, 3, 2)
    nhb = hk // hpb
    zb = zcol0 // (hpb * rep * dv)
    colspec = pl.BlockSpec((None, hpb, tb, rep), lambda b, h, i: (b, h, i, 0))
    o, s_out = pl.pallas_call(
        functools.partial(_gdn_core_kernel, rep=rep, c=c, ncb=ncb, hpb=hpb, dk=dk, n_real=n_real),
        grid=(bsz, nhb, t // tb),
        in_specs=[
            pl.BlockSpec((None, tb, hpb * dk), lambda b, h, i: (b, i, h)),
            pl.BlockSpec((None, tb, hpb * dk), lambda b, h, i: (b, i, nhb + h)),
            pl.BlockSpec((None, tb, hpb * rep * dv), lambda b, h, i: (b, i, h)),
            pl.BlockSpec((None, tb, hpb * rep * dv), lambda b, h, i: (b, i, zb + h)),
            colspec,
            pl.BlockSpec((None, hpb, rep, tb), lambda b, h, i: (b, h, 0, i)),
            colspec,
            pl.BlockSpec((None, hpb * rep, dk, dv), lambda b, h, i: (b, h, 0, 0)),
            pl.BlockSpec((1, dv), lambda b, h, i: (0, 0)),
        ],
        out_specs=[
            pl.BlockSpec((None, tb, hpb * rep * dv), lambda b, h, i: (b, i, h)),
            pl.BlockSpec((None, hpb * rep, dk, dv), lambda b, h, i: (b, h, 0, 0)),
        ],
        out_shape=[jax.ShapeDtypeStruct((bsz, t, val), BF16), jax.ShapeDtypeStruct(s0.shape, F32)],
        scratch_shapes=[pltpu.VMEM((hpb * rep, dk, dv), F32)],
        name="gdn_core",
        compiler_params=_cparams("parallel", "parallel", "arbitrary"),
    )(qk, qk, v, proj, gcc, gcr, bcc, s0, norm_w.reshape(1, dv))
    return o, s_out


def _gdn_mixer(x, mod3, nw, buf, s0, w_in, conv_w, a_log, dt_bias, norm_w, w_out, *, tm, tm_in, tt, chunk):
    bsz, t, d = x.shape
    hv, dk, dv = s0.shape[1], s0.shape[2], s0.shape[3]
    val = hv * dv
    conv_dim = conv_w.shape[1]
    key = (conv_dim - val) // 2
    assert (conv_dim + val) % LANES == 0 and 2 * hv <= LANES and t >= CONV_W - 1
    x2 = x.reshape(bsz * t, d)
    proj = _normproj(x2, mod3, 3, nw, w_in, tm=tm_in, tn_target=512).reshape(bsz, t, -1)
    buf8 = _pad_buf(buf)
    qk = _gdn_prep(proj, buf8, conv_w, col0=0, ncols=2 * key, norm=True, tt=tt)
    v = _gdn_prep(proj, buf8, conv_w, col0=2 * key, ncols=val, norm=False, tt=tt)
    tp = -(-t // chunk) * chunk
    beta, gc = _gdn_gates(proj, a_log, dt_bias, col0=conv_dim + val, hv=hv, chunk=min(chunk, tt), tt=tt)
    if tp != t:
        padt = lambda a: jnp.pad(a, ((0, 0), (0, tp - t), (0, 0)))
        gc = jnp.concatenate([gc, jnp.broadcast_to(gc[:, -1:], (bsz, tp - t, LANES))], axis=1)
        qk, v, beta, projz = padt(qk), padt(v), padt(beta), padt(proj)
    else:
        projz = proj
    o, s_new = _gdn_core(qk, v, projz, conv_dim, gc, beta, s0, norm_w, c=chunk, ncb=math.gcd(tp // chunk, 2),
                         hpb=min(4 if tp > chunk else 16, key // dk), n_real=min(t, chunk))
    o2 = o[:, :t].reshape(bsz * t, val)
    xo = _outproj(o2, w_out, x2, mod3, 5, tm=tm).reshape(bsz, t, d)
    new_buf = proj[:, t - (CONV_W - 1):, :conv_dim]
    return xo, new_buf, s_new


def _rope_tables(pos, half):
    inv_freq = ROPE_THETA ** (-jnp.arange(half, dtype=F32) / half)
    ang = pos.astype(F32)[:, None] * inv_freq[None, :]
    cos, sin = jnp.cos(ang), jnp.sin(ang)
    return jnp.concatenate([cos, cos], axis=-1), jnp.concatenate([-sin, sin], axis=-1)


def _dsa_prep_kernel(x_ref, cos_ref, sin_ref, q_ref, k_ref, k16_ref, v16_ref, qi_ref, ki_ref, ki16_ref, *tr_refs,
                     nh, nkv, nih):
    cos, sin = cos_ref[...], sin_ref[...]

    def rope(col):
        seg = x_ref[:, col * LANES:(col + 1) * LANES]
        return seg * cos + pltpu.roll(seg, LANES // 2, 1) * sin

    for h in range(nh):
        q_ref[:, h * LANES:(h + 1) * LANES] = (rope(h) * LANES ** -0.5).astype(BF16)
    for h in range(nkv):
        kr = rope(nh + h)
        k_ref[:, h * LANES:(h + 1) * LANES] = kr
        k16_ref[:, h * LANES:(h + 1) * LANES] = kr.astype(BF16)
    v0 = (nh + nkv) * LANES
    v16_ref[...] = x_ref[:, v0:v0 + nkv * LANES].astype(BF16)
    c0 = nh + 2 * nkv
    for h in range(nih):
        qi_ref[:, h * LANES:(h + 1) * LANES] = rope(c0 + h).astype(BF16)
    kir = rope(c0 + nih)
    ki_ref[...] = kir
    ki16_ref[...] = kir.astype(BF16)
    if tr_refs:
        wt_ref, vt_ref = tr_refs
        w0 = (c0 + nih + 1) * LANES
        wt_ref[...] = x_ref[:, w0:w0 + LANES].T[:wt_ref.shape[0], :]
        for h in range(nkv):
            vt_ref[h * VT_ROWS:h * VT_ROWS + LANES, :] = x_ref[:, v0 + h * LANES:v0 + (h + 1) * LANES].T.astype(BF16)
            vt_ref[h * VT_ROWS + LANES:(h + 1) * VT_ROWS, :] = jnp.ones((VT_ROWS - LANES, vt_ref.shape[1]), BF16)


def _dsa_prep(proj, pos, *, nh, nkv, nih, tt, with_wt):
    bsz, t, npj = proj.shape
    cos, sin = _rope_tables(pos, LANES // 2)
    row = lambda n, dt: jax.ShapeDtypeStruct((bsz, t, n * LANES), dt)
    ospec = lambda n: pl.BlockSpec((None, tt, n * LANES), lambda b, i: (b, i, 0))
    tab = pl.BlockSpec((tt, LANES), lambda b, i: (i, 0))
    nwt = -(-nih // SUBLANES) * SUBLANES
    return pl.pallas_call(
        functools.partial(_dsa_prep_kernel, nh=nh, nkv=nkv, nih=nih),
        grid=(bsz, t // tt),
        in_specs=[pl.BlockSpec((None, tt, npj), lambda b, i: (b, i, 0)), tab, tab],
        out_specs=[ospec(nh), ospec(nkv), ospec(nkv), ospec(nkv), ospec(nih), ospec(1), ospec(1)]
        + ([pl.BlockSpec((None, nwt, tt), lambda b, i: (b, 0, i)),
            pl.BlockSpec((None, None, nkv * VT_ROWS, tt), lambda b, i: (b, i, 0, 0))] if with_wt else []),
        out_shape=[row(nh, BF16), row(nkv, F32), row(nkv, BF16), row(nkv, BF16), row(nih, BF16), row(1, F32),
                   row(1, BF16)]
        + ([jax.ShapeDtypeStruct((bsz, nwt, t), F32),
            jax.ShapeDtypeStruct((bsz, t // tt, nkv * VT_ROWS, tt), BF16)] if with_wt else []),
        name="dsa_prep",
        compiler_params=_cparams("parallel", "parallel"),
    )(proj, cos, sin)


def _sort_key(s):
    bits = pltpu.bitcast(jnp.where(s == 0.0, 0.0, s), I32)
    return jnp.where(bits < 0, bits ^ 0x7FFFFFFF, bits)


def _topk_cut(count_where, shape, topk, idx_bits):
    def body(i, carry):
        t, n_t = carry
        cand = t + lax.shift_left(jnp.int32(1), 31 - i)
        n = count_where(lambda key, idx: key >= cand)
        ok = n >= topk
        return jnp.where(ok, cand, t), jnp.where(ok, n, n_t)

    thr, n_ge = lax.fori_loop(0, 32, body, (jnp.full(shape, INT32_MIN, I32), jnp.full(shape, INT32_MAX, I32)))
    tie = (n_ge > topk) & (thr > KEY_OF_NEG_INF)

    def cut():
        need = topk - count_where(lambda key, idx: key > thr)

        def jbody(i, j):
            cand = j + lax.shift_left(jnp.int32(1), idx_bits - 1 - i)
            below = count_where(lambda key, idx: (key == thr) & (idx < cand))
            return jnp.where(below < need, cand, j)

        return jnp.where(tie, lax.fori_loop(0, idx_bits, jbody, jnp.zeros(shape, I32)), INT32_MAX)

    jcut = lax.cond(jnp.any(tie), cut, lambda: jnp.full(shape, INT32_MAX, I32))
    return thr, jcut


def _topk_chosen(key, idx, thr, jcut):
    return (key > thr) | ((key == thr) & (idx <= jcut))


def _dsa_prompt_kernel(qi_ref, wt_ref, q_ref, ki_ref, k_ref, vt_ref, o_ref, key_ref, bias_ref, acc_ref,
                       *, nh, nkv, nih, topk, idx_scale):
    qb = pl.program_id(1)
    blk = Q_BLOCK
    kb = vt_ref.shape[2]
    ktiles = kb // blk
    nkb = (qb + ktiles) // ktiles
    rowk = lax.broadcasted_iota(I32, (kb, blk), 0)
    colq = lax.broadcasted_iota(I32, (kb, blk), 1)
    wt = wt_ref[...]
    hq = max(1, nih // 4)
    qis = [jnp.concatenate([qi_ref[:, h * LANES:(h + 1) * LANES] for h in range(h0, min(h0 + hq, nih))], axis=0)
           for h0 in range(0, nih, hq)]

    def rows_of(j):
        return pl.ds(pl.multiple_of(j * kb, kb), kb)

    def visible(j):
        return j * kb + rowk <= qb * blk + colq

    def score_body(j, carry):
        keys = ki_ref[rows_of(j), :]
        lgs = [_dot_nt(keys, qi) for qi in qis]
        acc = jnp.zeros((kb, blk), F32)
        for i, lg in enumerate(lgs):
            for hh in range(lg.shape[1] // blk):
                h = i * hq + hh
                acc = acc + jnp.maximum(lg[:, hh * blk:(hh + 1) * blk], 0.0) * wt[h:h + 1, :]
        key_ref[rows_of(j), :] = _sort_key(jnp.where(visible(j), acc * idx_scale, NEG_INF))
        return carry

    lax.fori_loop(0, nkb, score_body, 0)

    def count_where(pred):
        def body(j, c):
            hit = jnp.where(pred(key_ref[rows_of(j), :], j * kb + rowk), 1, 0)
            for i in range(ktiles):
                c = c + hit[i * blk:(i + 1) * blk]
            return c
        cnt = lax.fori_loop(0, nkb, body, jnp.zeros((blk, blk), I32))
        return jnp.sum(cnt, axis=0, keepdims=True)

    thr, jcut = _topk_cut(count_where, (1, blk), topk, (key_ref.shape[0] - 1).bit_length())

    def bias_body(j, carry):
        sel = _topk_chosen(key_ref[rows_of(j), :], j * kb + rowk, thr, jcut) & visible(j)
        bias_ref[rows_of(j), :] = jnp.where(sel, 0.0, NEG_INF)
        return carry

    lax.fori_loop(0, nkb, bias_body, 0)

    rep = nh // nkv
    qgs =[jnp.concatenate([q_ref[:, (g * rep + r) * LANES:(g * rep + r + 1) * LANES] for r in range(rep)], axis=0)
           for g in range(nkv)]
    acc_ref[...] = jnp.zeros_like(acc_ref)

    def att_body(j, carry):
        ms, ls = carry
        rows = rows_of(j)
        bias = jnp.concatenate([bias_ref[rows, :]] * rep, axis=1)
        ss = [_dot_nt(k_ref[rows, g * LANES:(g + 1) * LANES], qgs[g]) + bias for g in range(nkv)]
        m_new = [jnp.maximum(m, jnp.max(s, axis=0, keepdims=True)) for m, s in zip(ms, ss)]
        ps = [jnp.exp((s - m).astype(BF16)) for s, m in zip(ss, m_new)]
        pvs = [_dot(vt_ref[j, g * VT_ROWS:(g + 1) * VT_ROWS, :], p) for g, p in enumerate(ps)]
        alphas = [jnp.exp(m - mn) for m, mn in zip(ms, m_new)]
        for g in range(nkv):
            acc_ref[g] = alphas[g] * acc_ref[g] + pvs[g][:LANES]
        l_new = [a * l + pv[LANES:LANES + 1] for a, l, pv in zip(alphas, ls, pvs)]
        return tuple(m_new), tuple(l_new)

    row0 = lambda v: tuple(jnp.full((1, rep * blk), v, F32) for _ in range(nkv))
    _, ls = lax.fori_loop(0, nkb, att_body, (row0(NEG_INF), row0(0.0)))
    for g in range(nkv):
        og = acc_ref[g] / ls[g]
        for r in range(rep):
            h = g * rep + r
            o_ref[:, h * LANES:(h + 1) * LANES] = og[:, r * blk:(r + 1) * blk].T.astype(BF16)


def _dsa_prompt_attend(q16, qi16, wt, ki16, k16, vt16, *, nh, nkv, nih, topk):
    bsz, t, _ = q16.shape
    rep = nh // nkv
    full = lambda n: pl.BlockSpec((None, t, n * LANES), lambda b, i: (b, 0, 0))
    blk = lambda n: pl.BlockSpec((None, Q_BLOCK, n * LANES), lambda b, i: (b, i, 0))
    return pl.pallas_call(
        functools.partial(_dsa_prompt_kernel, nh=nh, nkv=nkv, nih=nih, topk=topk,
                          idx_scale=(LANES * nih) ** -0.5),
        grid=(bsz, t // Q_BLOCK),
        in_specs=[blk(nih), pl.BlockSpec((None, wt.shape[1], Q_BLOCK), lambda b, i: (b, 0, i)), blk(nh),
                  full(1), full(nkv), pl.BlockSpec((None,) + vt16.shape[1:], lambda b, i: (b, 0, 0, 0))],
        out_specs=blk(nh),
        out_shape=jax.ShapeDtypeStruct((bsz, t, nh * LANES), BF16),
        scratch_shapes=[pltpu.VMEM((t, Q_BLOCK), I32), pltpu.VMEM((t, Q_BLOCK), F32),
                        pltpu.VMEM((nkv, LANES, rep * Q_BLOCK), F32)],
        name="dsa_prompt_attend",
        compiler_params=_cparams("parallel", "arbitrary"),
    )(qi16, wt, q16, ki16, k16, vt16)


def _dsa_sample_score_kernel(pt_ref, qi_ref, wc_ref, *refs, n_steps, pps, nih, tq, past, idx_scale):
    page_refs, new_ref, o_ref = refs[:pps], refs[pps], refs[pps + 1]
    p = pl.program_id(1)
    keys = jnp.concatenate([r[...] for r in page_refs], axis=0).astype(BF16)
    keys = jnp.where(p == n_steps - 1, new_ref[...], keys)
    qi = jnp.concatenate([qi_ref[:, h * LANES:(h + 1) * LANES] for h in range(nih)], axis=0)
    w = jnp.maximum(_dot_nt(qi, keys), 0.0) * wc_ref[...]
    acc = w[0:tq]
    for h in range(1, nih):
        acc = acc + w[h * tq:(h + 1) * tq]
    s = p * keys.shape[0] + lax.broadcasted_iota(I32, acc.shape, 1)
    qpos = past + lax.broadcasted_iota(I32, acc.shape, 0)
    o_ref[...] = jnp.where(s <= qpos, acc * idx_scale, NEG_INF)


def _dsa_sample_select_kernel(s_ref, o_ref, *, topk):
    key = _sort_key(s_ref[...])
    idx = lax.broadcasted_iota(I32, key.shape, 1)
    count_where = lambda pred: jnp.sum(jnp.where(pred(key, idx), 1, 0), axis=-1, keepdims=True)
    thr, jcut = _topk_cut(count_where, (key.shape[0], 1), topk, (key.shape[1] - 1).bit_length())
    o_ref[...] = jnp.where(_topk_chosen(key, idx, thr, jcut) & (s_ref[...] > 0.5 * NEG_INF), 1.0, 0.0)


def _dsa_sample_attn_kernel(pt_ref, q_ref, sel_ref, *refs, n_steps, pps, nh, nkv, tq):
    kp_refs, vp_refs = refs[:pps], refs[pps:2 * pps]
    kn_ref, vn_ref, o_ref, m_ref, l_ref, acc_ref = refs[2 * pps:]
    p = pl.program_id(1)
    rep = nh // nkv
    page = sel_ref.shape[1] // pps

    @pl.when(p == 0)
    def _():
        m_ref[...] = jnp.full_like(m_ref, NEG_INF)
        l_ref[...] = jnp.zeros_like(l_ref)
        acc_ref[...] = jnp.zeros_like(acc_ref)

    last = p == n_steps - 1

    def head_rows(page_refs, new_ref, g):
        cached = jnp.concatenate([r[pl.ds(g, page, stride=nkv), :] for r in page_refs], axis=0).astype(BF16)
        return jnp.where(last, new_ref[g], cached)

    rows = rep * tq
    q = jnp.concatenate([q_ref[:, h * LANES:(h + 1) * LANES] for h in range(nh)], axis=0)
    unsel = (sel_ref[...] - 1.0) * -NEG_INF
    s = jnp.concatenate([_dot_nt(q[g * rows:(g + 1) * rows], head_rows(kp_refs, kn_ref, g)) for g in range(nkv)],
                        axis=0) + jnp.concatenate([unsel] * nh, axis=0)
    m = m_ref[...]
    m_new = jnp.maximum(m, jnp.max(s, axis=-1, keepdims=True))
    alpha = jnp.exp(m - m_new)
    pr = jnp.where(s > 0.5 * NEG_INF, jnp.exp(s - m_new), 0.0)
    l_ref[...] = alpha * l_ref[...] + jnp.sum(pr, axis=-1, keepdims=True)
    pv = jnp.concatenate([_dot(pr[g * rows:(g + 1) * rows].astype(BF16), head_rows(vp_refs, vn_ref, g))
                          for g in range(nkv)], axis=0)
    acc_ref[...] = alpha * acc_ref[...] + pv
    m_ref[...] = m_new

    @pl.when(last)
    def _():
        og = acc_ref[...] / l_ref[...]
        for h in range(nh):
            o_ref[:, h * LANES:(h + 1) * LANES] = og[h * tq:(h + 1) * tq].astype(BF16)


def _dsa_sample_attend(q16, qi16, wi, ki16, k16, v16, cache_k, cache_v, cache_i, layer, page_table,
                       *, nh, nkv, nih, topk):
    bsz, tq, _ = q16.shape
    n_pages = page_table.shape[1]
    n_layers, n_pool, page = cache_i.shape[:3]
    pps = math.gcd(n_pages, SAMPLE_PAGES_PER_STEP)
    assert page == LANES and tq == SUBLANES
    past = n_pages * page
    n_steps = n_pages // pps + 1
    ltot = n_steps * pps * page
    wcol = wi.transpose(0, 2, 1).reshape(bsz, nih * tq, 1)
    pidx = lambda i: (lambda b, p, pt: (layer * n_pool + pt[b, jnp.minimum(p * pps + i, n_pages - 1)], 0, 0))
    qspec = lambda n: pl.BlockSpec((None, tq, n * LANES), lambda b, p, pt: (b, 0, 0))
    idx_scale = (LANES * nih) ** -0.5
    cache_i = cache_i.reshape(n_layers * n_pool, page, LANES)
    ki_new = jnp.pad(ki16, ((0, 0), (0, pps * page - tq), (0, 0)))
    scores = pl.pallas_call(
        functools.partial(_dsa_sample_score_kernel, n_steps=n_steps, pps=pps, nih=nih, tq=tq, past=past,
                          idx_scale=idx_scale),
        grid_spec=pltpu.PrefetchScalarGridSpec(
            num_scalar_prefetch=1, grid=(bsz, n_steps),
            in_specs=[qspec(nih), pl.BlockSpec((None, nih * tq, 1), lambda b, p, pt: (b, 0, 0))]
            + [pl.BlockSpec((None, page, LANES), pidx(i)) for i in range(pps)]
            + [pl.BlockSpec((None, pps * page, LANES), lambda b, p, pt: (b, 0, 0))],
            out_specs=pl.BlockSpec((None, tq, pps * page), lambda b, p, pt: (b, 0, p))),
        out_shape=jax.ShapeDtypeStruct((bsz, tq, ltot), F32),
        name="dsa_sample_scores",
        compiler_params=_cparams("parallel", "arbitrary"),
    )(page_table, qi16, wcol, *([cache_i] * pps), ki_new)
    sel = pl.pallas_call(
        functools.partial(_dsa_sample_select_kernel, topk=topk),
        grid=(1,),
        in_specs=[pl.BlockSpec((bsz * tq, ltot), lambda i: (0, 0))],
        out_specs=pl.BlockSpec((bsz * tq, ltot), lambda i: (0, 0)),
        out_shape=jax.ShapeDtypeStruct((bsz * tq, ltot), F32),
        name="dsa_sample_select",
        compiler_params=_cparams("arbitrary"),
    )(scores.reshape(bsz * tq, ltot)).reshape(bsz, tq, ltot)
    prow = page * nkv
    ck = cache_k.reshape(n_layers * n_pool, prow, LANES)
    cv = cache_v.reshape(n_layers * n_pool, prow, LANES)
    new_rows = lambda a: jnp.pad(a.reshape(bsz, tq, nkv, LANES).transpose(0, 2, 1, 3),
                                 ((0, 0), (0, 0), (0, pps * page - tq), (0, 0)))
    kvspecs = [pl.BlockSpec((None, prow, LANES), pidx(i)) for i in range(pps)]
    newspec = pl.BlockSpec((None, nkv, pps * page, LANES), lambda b, p, pt: (b, 0, 0, 0))
    return pl.pallas_call(
        functools.partial(_dsa_sample_attn_kernel, n_steps=n_steps, pps=pps, nh=nh, nkv=nkv, tq=tq),
        grid_spec=pltpu.PrefetchScalarGridSpec(
            num_scalar_prefetch=1, grid=(bsz, n_steps),
            in_specs=[qspec(nh), pl.BlockSpec((None, tq, pps * page), lambda b, p, pt: (b, 0, p))]
            + kvspecs + kvspecs + [newspec, newspec],
            out_specs=qspec(nh),
            scratch_shapes=[pltpu.VMEM((nh * tq, 1), F32), pltpu.VMEM((nh * tq, 1), F32),
                            pltpu.VMEM((nh * tq, LANES), F32)]),
        out_shape=jax.ShapeDtypeStruct((bsz, tq, nh * LANES), BF16),
        name="dsa_sample_attend",
        compiler_params=_cparams("parallel", "arbitrary"),
    )(page_table, q16, sel, *([ck] * pps), *([cv] * pps), new_rows(k16), new_rows(v16))


def _dsa_mixer(x, mod3, nw, w_in, w_out, n_proj, *, nkv, tm, tm_in, tt, cache=None):
    bsz, t, d = x.shape
    nh = _wshape(w_out)[0] // LANES
    nih = (n_proj - (nh + 2 * nkv + 1) * LANES) // (LANES + 1)
    assert (nh + 2 * nkv + nih + 1) * LANES + nih == n_proj and nih <= LANES
    x2 = x.reshape(bsz * t, d)
    proj = _normproj(x2, mod3, 3, nw, w_in, tm=tm_in).reshape(bsz, t, -1)
    v = proj[:, :, (nh + nkv) * LANES:(nh + 2 * nkv) * LANES]
    if cache is None:
        assert t % Q_BLOCK == 0
        q16, k, k16, _, qi16, ki, ki16, wt, vt16 = _dsa_prep(proj, jnp.arange(t), nh=nh, nkv=nkv, nih=nih,
                                                             tt=_tile(t, 4 * Q_BLOCK, Q_BLOCK), with_wt=True)
        o = _dsa_prompt_attend(q16, qi16, wt, ki16, k16, vt16, nh=nh, nkv=nkv, nih=nih, topk=min(TOPK_MAX, t // 4))
    else:
        cache_k, cache_v, cache_i, layer, page_table = cache
        past = page_table.shape[1] * cache_i.shape[2]
        q16, k, k16, v16, qi16, ki, ki16 = _dsa_prep(proj, past + jnp.arange(t), nh=nh, nkv=nkv, nih=nih, tt=tt,
                                                    with_wt=False)
        w0 = (nh + 2 * nkv + nih + 1) * LANES
        o = _dsa_sample_attend(q16, qi16, proj[:, :, w0:w0 + nih], ki16, k16, v16, cache_k, cache_v, cache_i,
                               layer, page_table, nh=nh, nkv=nkv, nih=nih, topk=min(TOPK_MAX, (past + t) // 4))
    xo = _outproj(o.reshape(bsz * t, nh * LANES), w_out, x2, mod3, 5, tm=tm).reshape(bsz, t, d)
    return xo, k.reshape(bsz, t, nkv, LANES), v.reshape(bsz, t, nkv, LANES), ki


def _gelu_tanh(x):
    return 0.5 * x * (1.0 + jnp.tanh(math.sqrt(2.0 / math.pi) * (x + 0.044715 * (x * x * x))))


def _lru_kernel(gate_ref, xb_ref, halo_ref, buf_ref, cw_ref, cb_ref, wa_ref, ba_ref, wx_ref, bx_ref, lam_ref,
                h0_ref, y_ref, hl_ref, h_ref, *, nblk):
    ti = pl.program_id(1)

    @pl.when(ti == 0)
    def _():
        h_ref[...] = h0_ref[...]

    halo = jnp.where(ti == 0, buf_ref[...], halo_ref[...])
    xc = _causal_conv(xb_ref[...], halo, cw_ref[...]) + cb_ref[...]
    tt = xc.shape[0]
    xc16 = xc.astype(BF16)
    rs, xs = [], []
    for n in range(nblk):
        blk = xc16[:, n * LRU_BLOCK:(n + 1) * LRU_BLOCK]
        rs.append(_dot(blk, wa_ref[n]))
        xs.append(_dot(blk, wx_ref[n]))
    r = _sigmoid(jnp.concatenate(rs, axis=1) + ba_ref[...])
    ig = _sigmoid(jnp.concatenate(xs, axis=1) + bx_ref[...])
    lam = lam_ref[...]
    softplus_neg = jnp.maximum(-lam, 0.0) + jnp.log1p(jnp.exp(-jnp.abs(lam)))
    log_a = -RG_C * r * softplus_neg
    a = jnp.exp(log_a)
    b = jnp.sqrt(-jnp.tanh(log_a) * (a * a + 1.0)) * (ig * xc)
    row = lax.broadcasted_iota(I32, a.shape, 0)
    d = 1
    while d < tt:
        keep = row >= d
        a_sh = jnp.where(keep, pltpu.roll(a, d, 0), 1.0)
        b_sh = jnp.where(keep, pltpu.roll(b, d, 0), 0.0)
        b = a * b_sh + b
        a = a * a_sh
        d *= 2
    hs = b + a * h_ref[...]
    h_ref[...] = hs[tt - 1:tt, :]
    y_ref[...] = (hs * _gelu_tanh(gate_ref[...])).astype(BF16)

    @pl.when(ti == pl.num_programs(1) - 1)
    def _():
        hl_ref[...] = hs[tt - 1:tt, :]


def _lru_core(proj, buf8, h0, conv_w, conv_b, w_ga, b_ga, w_gx, b_gx, lam, *, tt):
    bsz, t, w2 = proj.shape
    w = w2 // 2
    nblk = w // LRU_BLOCK
    sub = tt // SUBLANES
    vec = lambda: pl.BlockSpec((1, w), lambda b, i: (0, 0))
    wsp = lambda: pl.BlockSpec((nblk, LRU_BLOCK, LRU_BLOCK), lambda b, i: (0, 0, 0))
    r1 = lambda v: v.reshape(1, w).astype(F32)
    y, hl = pl.pallas_call(
        functools.partial(_lru_kernel, nblk=nblk),
        grid=(bsz, t // tt),
        in_specs=[
            pl.BlockSpec((None, tt, w), lambda b, i: (b, i, 0)),
            pl.BlockSpec((None, tt, w), lambda b, i: (b, i, 1)),
            pl.BlockSpec((None, SUBLANES, w), lambda b, i: (b, jnp.maximum(i * sub - 1, 0), 1)),
            pl.BlockSpec((None, SUBLANES, w), lambda b, i: (b, 0, 0)),
            pl.BlockSpec((CONV_W, w), lambda b, i: (0, 0)),
            vec(), wsp(), vec(), wsp(), vec(), vec(),
            pl.BlockSpec((None, 1, w), lambda b, i: (b, 0, 0)),
        ],
        out_specs=[pl.BlockSpec((None, tt, w), lambda b, i: (b, i, 0)),
                   pl.BlockSpec((None, 1, w), lambda b, i: (b, 0, 0))],
        out_shape=[jax.ShapeDtypeStruct((bsz, t, w), BF16), jax.ShapeDtypeStruct((bsz, 1, w), F32)],
        scratch_shapes=[pltpu.VMEM((1, w), F32)],
        name="lru_core",
        compiler_params=_cparams("parallel", "arbitrary"),
    )(proj, proj, proj, buf8, conv_w, r1(conv_b), w_ga, r1(b_ga), w_gx, r1(b_gx), r1(lam), h0.reshape(bsz, 1, w))
    return y, hl.reshape(bsz, w)


def _lru_mixer(x, mod3, nw, buf, h0, w_in, conv_w, conv_b, w_ga, b_ga, w_gx, b_gx, lam, w_out, *, tm, tm_in, tt):
    bsz, t, d = x.shape
    assert t >= CONV_W - 1
    x2 = x.reshape(bsz * t, d)
    proj = _normproj(x2, mod3, 3, nw, w_in, tm=tm_in).reshape(bsz, t, -1)
    w = proj.shape[2] // 2
    y, hl = _lru_core(proj, _pad_buf(buf), h0, conv_w, conv_b, w_ga, b_ga, w_gx, b_gx, lam, tt=tt)
    xo = _outproj(y.reshape(bsz * t, w), w_out, x2, mod3, 5, tm=tm).reshape(bsz, t, d)
    return xo, proj[:, t - (CONV_W - 1):, w:], hl


def _bf16_padded(w, mult):
    n = w.shape[-1]
    return jnp.pad(w, ((0, 0), (0, 0), (0, -n % mult))).astype(BF16)


def kernel(x_prompt, x_sample, state_a_conv, state_a_ssm, cache_b_k, cache_b_v, cache_b_idx, state_c_conv, state_c_h, page_table, c_prompt, c_sample, w_ada, b_ada, norm_w, ffn_w_in, ffn_w_out, gdn_w_in, gdn_conv_w, gdn_a_log, gdn_dt_bias, gdn_norm_w, gdn_w_out, dsa_w_in, dsa_w_out, lru_w_in, lru_conv_w, lru_conv_b, lru_w_gate_a, lru_b_gate_a, lru_w_gate_x, lru_b_gate_x, lru_lambda, lru_w_out, w_ada_final, b_ada_final, final_norm_w):
    bp, seq, d = x_prompt.shape
    bs, ts, _ = x_sample.shape
    depth = w_ada.shape[0]
    nkv = cache_b_k.shape[3]
    tm_p = _tile(seq, 512, SUBLANES)
    tm_in_p = _tile(seq, 1024, SUBLANES)
    tt_p = _tile(seq, 256, SUBLANES)
    tm_s = bs * ts
    tiles_p = dict(tm=tm_p, tm_in=tm_in_p, tt=tt_p)
    tiles_s = dict(tm=tm_s, tm_in=tm_s, tt=ts)

    c_all = jnp.concatenate([c_prompt, c_sample], axis=0)
    c_all = jnp.pad(c_all, ((0, -c_all.shape[0] % SUBLANES), (0, 0)))
    mod = _ada(c_all, w_ada, b_ada)
    mod_f = _ada(c_all, w_ada_final[None], b_ada_final[None])[0]

    def groups(m):
        return m[:bp, None, :], jnp.repeat(m[bp:bp + bs], ts, axis=0)[None]

    ffn_w_out16 = ffn_w_out.astype(BF16)
    gdn_w_in16, gdn_w_out16 = _bf16_padded(gdn_w_in, 512), gdn_w_out.astype(BF16)
    dsa_w_in16, dsa_w_out16 = _bf16_padded(dsa_w_in, 512), dsa_w_out.astype(BF16)
    lru_w_in16, lru_w_out16 = lru_w_in.astype(BF16), lru_w_out.astype(BF16)

    xp, xs = x_prompt, x_sample
    outs = {k: [] for k in ("a_conv_p", "a_conv_s", "a_ssm_p", "a_ssm_s", "b_k_p", "b_k_s", "b_v_p", "b_v_s",
                            "b_i_p", "b_i_s", "c_conv_p", "c_conv_s", "c_h_p", "c_h_s")}
    for layer in range(depth):
        kind, j = layer % N_MIXERS, layer // N_MIXERS
        mod_p, mod_s = groups(mod[layer])
        nw = norm_w[layer]

        def ffn(x, m3, tm, which):
            b, t, _ = x.shape
            x2 = x.reshape(b * t, d)
            h = _ffn_up(x2, m3, 6 * which, nw[2 * which:2 * which + 1], (ffn_w_in, (layer, which)), tm=tm)
            return _outproj(h, (ffn_w_out16, (layer, which)), x2, m3, 6 * which + 2, tm=tm, tn_target=512,
                            gain=0.5).reshape(b, t, d)

        xp = ffn(xp, mod_p, tm_in_p, 0)
        xs = ffn(xs, mod_s, tm_s, 0)
        if kind == 0:
            prm = ((gdn_w_in16, (j,)), gdn_conv_w[j], gdn_a_log[j], gdn_dt_bias[j], gdn_norm_w[j],
                   (gdn_w_out16, (j,)))
            buf0 = jnp.zeros((bp,) + state_a_conv.shape[2:], F32)
            s0 = jnp.zeros((bp,) + state_a_ssm.shape[2:], F32)
            xp, buf, s = _gdn_mixer(xp, mod_p, nw[1:2], buf0, s0, *prm, chunk=2 * GDN_CHUNK, **tiles_p)
            outs["a_conv_p"].append(buf)
            outs["a_ssm_p"].append(s)
            xs, buf, s = _gdn_mixer(xs, mod_s, nw[1:2], state_a_conv[j], state_a_ssm[j], *prm, chunk=GDN_CHUNK,
                                    **tiles_s)
            outs["a_conv_s"].append(buf)
            outs["a_ssm_s"].append(s)
        elif kind == 1:
            w_in, w_out = (dsa_w_in16, (j,)), (dsa_w_out16, (j,))
            n_proj = dsa_w_in.shape[2]
            xp, k, v, ki = _dsa_mixer(xp, mod_p, nw[1:2], w_in, w_out, n_proj, nkv=nkv, **tiles_p)
            outs["b_k_p"].append(k)
            outs["b_v_p"].append(v)
            outs["b_i_p"].append(ki)
            xs, k, v, ki = _dsa_mixer(xs, mod_s, nw[1:2], w_in, w_out, n_proj, nkv=nkv,
                                      cache=(cache_b_k, cache_b_v, cache_b_idx, j, page_table), **tiles_s)
            outs["b_k_s"].append(k)
            outs["b_v_s"].append(v)
            outs["b_i_s"].append(ki)
        else:
            prm = ((lru_w_in16, (j,)), lru_conv_w[j], lru_conv_b[j], lru_w_gate_a[j].astype(BF16),
                   lru_b_gate_a[j], lru_w_gate_x[j].astype(BF16), lru_b_gate_x[j], lru_lambda[j],
                   (lru_w_out16, (j,)))
            buf0 = jnp.zeros((bp,) + state_c_conv.shape[2:], F32)
            h0 = jnp.zeros((bp,) + state_c_h.shape[2:], F32)
            xp, buf, hl = _lru_mixer(xp, mod_p, nw[1:2], buf0, h0, *prm, **tiles_p)
            outs["c_conv_p"].append(buf)
            outs["c_h_p"].append(hl)
            xs, buf, hl = _lru_mixer(xs, mod_s, nw[1:2], state_c_conv[j], state_c_h[j], *prm, **tiles_s)
            outs["c_conv_s"].append(buf)
            outs["c_h_s"].append(hl)
        xp = ffn(xp, mod_p, tm_in_p, 1)
        xs = ffn(xs, mod_s, tm_s, 1)

    modf_p, modf_s = groups(mod_f)
    fnw = final_norm_w.reshape(1, d)
    y_p = _final_norm(xp.reshape(bp * seq, d), modf_p, fnw, tm=tm_p).reshape(bp, seq, d)
    y_s = _final_norm(xs.reshape(bs * ts, d), modf_s, fnw, tm=tm_s).reshape(bs, ts, d)
    st = {k: jnp.stack(v) for k, v in outs.items()}
    return (y_p, y_s, st["a_conv_p"], st["a_conv_s"], st["a_ssm_p"], st["a_ssm_s"], st["b_k_p"], st["b_k_s"],
            st["b_v_p"], st["b_v_s"], st["b_i_p"], st["b_i_s"], st["c_conv_p"], st["c_conv_s"], st["c_h_p"],
            st["c_h_s"])
```

```python
import functools
import math

import jax
import jax.numpy as jnp
import numpy as np
from jax import lax
from jax.experimental import pallas as pl
from jax.experimental.pallas import tpu as pltpu

F32 = jnp.float32
BF16 = jnp.bfloat16
I32 = jnp.int32

N_MIXERS = 3
N_ADA = 9
CONV_W = 4
NORM_EPS = 1e-6
NEG_INF = -1e30
GDN_CHUNK = 64
TOPK_MAX = 256
Q_BLOCK = 128
ROPE_THETA = 10000.0
RG_C = 8.0
LRU_BLOCK = 256
SAMPLE_PAGES_PER_STEP = 8

LANES = 128
SUBLANES = 8
VMEM_LIMIT_BYTES = 56 * 2**20
VT_ROWS = LANES + 16
INT32_MIN = -2**31
INT32_MAX = 2**31 - 1
KEY_OF_NEG_INF = int(np.float32(NEG_INF).view(np.int32)) ^ 0x7FFFFFFF


def _cparams(*sem):
    return pltpu.CompilerParams(dimension_semantics=sem, vmem_limit_bytes=VMEM_LIMIT_BYTES)


def _tile(n, target, align=LANES):
    if n <= target:
        return n
    t = (target // align) * align
    while t >= align:
        if n % t == 0:
            return t
        t -= align
    raise ValueError(f"no {align}-aligned tile of {n} below {target}")


def _sigmoid(x):
    return jax.nn.sigmoid(x)


def _silu(x):
    return x * _sigmoid(x)


def _dot(a, b):
    return jnp.dot(a, b, preferred_element_type=F32)


def _dot_nt(a, b):
    return lax.dot_general(a, b, (((1,), (1,)), ((), ())), preferred_element_type=F32)


def _dot_tn(a, b):
    return lax.dot_general(a, b, (((0,), (0,)), ((), ())), preferred_element_type=F32)


def _dot_f32(a, b):
    return jnp.dot(a, b, preferred_element_type=F32, precision=lax.Precision.HIGHEST)


def _norm_mod(x, nw, sh, sc):
    ms = jnp.mean(x * x, axis=-1, keepdims=True)
    return x * lax.rsqrt(ms + NORM_EPS) * (nw * (1.0 + sc)) + sh


def _ada_kernel(c_ref, w_ref, b_ref, o_ref):
    a = _silu(c_ref[...]).astype(BF16)
    o_ref[0] = _dot(a, w_ref[0].astype(BF16)) + b_ref[0]


def _ada(c_all, w, b):
    n_l, d, n = w.shape
    mp = c_all.shape[0]
    tn = _tile(n, 1024)
    return pl.pallas_call(
        _ada_kernel,
        grid=(n_l, n // tn),
        in_specs=[
            pl.BlockSpec((mp, d), lambda l, j: (0, 0)),
            pl.BlockSpec((1, d, tn), lambda l, j: (l, 0, j)),
            pl.BlockSpec((1, 1, tn), lambda l, j: (l, 0, j)),
        ],
        out_specs=pl.BlockSpec((1, mp, tn), lambda l, j: (l, 0, j)),
        out_shape=jax.ShapeDtypeStruct((n_l, mp, n), F32),
        name="ada_mod",
        compiler_params=_cparams("parallel", "parallel"),
    )(c_all, w, b.reshape(n_l, 1, n))


def _mod_spec(mod3, k, d, tm, rows_per_group):
    return pl.BlockSpec((None, mod3.shape[1], d), lambda i, j: ((i * tm) // rows_per_group, 0, k))


def _wspec(w, block, imap):
    arr, idx = w
    return pl.BlockSpec((None,) * len(idx) + block, lambda i, j: idx + imap(i, j))


def _wshape(w):
    return w[0].shape[len(w[1]):]


def _ffn_kernel(x_ref, nw_ref, sh_ref, sc_ref, g_ref, wa_ref, wb_ref, wo_ref, o_ref, xn_ref):
    f = pl.program_id(1)

    @pl.when(f == 0)
    def _():
        xn_ref[...] = _norm_mod(x_ref[...], nw_ref[...], sh_ref[...], sc_ref[...]).astype(BF16)
        o_ref[...] = jnp.zeros_like(o_ref)

    xn = xn_ref[...]
    a = _dot(xn, wa_ref[...])
    b = _dot(xn, wb_ref[...])
    o_ref[...] += _dot((_silu(a) * b).astype(BF16), wo_ref[...])

    @pl.when(f == pl.num_programs(1) - 1)
    def _():
        o_ref[...] = x_ref[...] + 0.5 * g_ref[...] * o_ref[...]


def _ffn(x, mod3, k0, nw, w_in, w_out, *, tm, tf_target):
    m, d = x.shape
    f = _wshape(w_out)[0]
    tf = _tile(f, tf_target)
    nf = f // tf
    rpg = m // mod3.shape[0]
    ms = lambda k: _mod_spec(mod3, k, d, tm, rpg)
    return pl.pallas_call(
        _ffn_kernel,
        grid=(m // tm, nf),
        in_specs=[
            pl.BlockSpec((tm, d), lambda i, j: (i, 0)),
            pl.BlockSpec((1, d), lambda i, j: (0, 0)),
            ms(k0), ms(k0 + 1), ms(k0 + 2),
            _wspec(w_in, (d, tf), lambda i, j: (0, j)),
            _wspec(w_in, (d, tf), lambda i, j: (0, j + nf)),
            _wspec(w_out, (tf, d), lambda i, j: (j, 0)),
        ],
        out_specs=pl.BlockSpec((tm, d), lambda i, j: (i, 0)),
        out_shape=jax.ShapeDtypeStruct((m, d), F32),
        scratch_shapes=[pltpu.VMEM((tm, d), BF16)],
        name="ffn",
        compiler_params=_cparams("parallel", "arbitrary"),
    )(x, nw, mod3, mod3, mod3, w_in[0], w_in[0], w_out[0])


def _normproj_kernel(x_ref, nw_ref, sh_ref, sc_ref, w_ref, o_ref, xn_ref):
    @pl.when(pl.program_id(1) == 0)
    def _():
        xn_ref[...] = _norm_mod(x_ref[...], nw_ref[...], sh_ref[...], sc_ref[...]).astype(BF16)

    o_ref[...] = _dot(xn_ref[...], w_ref[...])


def _normproj(x, mod3, k0, nw, w, *, tm, tn_target=1024):
    m, d = x.shape
    n = _wshape(w)[1]
    tn = _tile(n, tn_target)
    rpg = m // mod3.shape[0]
    ms = lambda k: _mod_spec(mod3, k, d, tm, rpg)
    return pl.pallas_call(
        _normproj_kernel,
        grid=(m // tm, n // tn),
        in_specs=[
            pl.BlockSpec((tm, d), lambda i, j: (i, 0)),
            pl.BlockSpec((1, d), lambda i, j: (0, 0)),
            ms(k0), ms(k0 + 1),
            _wspec(w, (d, tn), lambda i, j: (0, j)),
        ],
        out_specs=pl.BlockSpec((tm, tn), lambda i, j: (i, j)),
        out_shape=jax.ShapeDtypeStruct((m, n), F32),
        scratch_shapes=[pltpu.VMEM((tm, d), BF16)],
        name="normproj",
        compiler_params=_cparams("parallel", "arbitrary"),
    )(x, nw, mod3, mod3, w[0])


def _ffn_up_kernel(x_ref, nw_ref, sh_ref, sc_ref, xs_ref, shs_ref, scs_ref, wa_ref, wb_ref, h_ref, hs_ref,
                   xn_ref, xsn_ref):
    i, j = pl.program_id(0), pl.program_id(1)

    @pl.when(j == 0)
    def _():
        xn_ref[...] = _norm_mod(x_ref[...], nw_ref[...], sh_ref[...], sc_ref[...]).astype(BF16)

    @pl.when((i == 0) & (j == 0))
    def _():
        xsn_ref[...] = _norm_mod(xs_ref[...], nw_ref[...], shs_ref[...], scs_ref[...]).astype(BF16)

    wa, wb = wa_ref[...].astype(BF16), wb_ref[...].astype(BF16)
    swiglu = lambda xn: (_silu(_dot(xn, wa)) * _dot(xn, wb)).astype(BF16)
    h_ref[...] = swiglu(xn_ref[...])

    @pl.when(i == 0)
    def _():
        hs_ref[...] = swiglu(xsn_ref[...])

    @pl.when(i > 0)
    def _():
        hs_ref[...] = jnp.zeros_like(hs_ref)


def _ffn_up(x, mod3, xs, mods3, k0, nw, w_in, *, tm, tf_target=512):
    m, d = x.shape
    msr = xs.shape[0]
    f = _wshape(w_in)[1] // 2
    tf = _tile(f, tf_target)
    nf = f // tf
    rpg = m // mod3.shape[0]
    ms = lambda k: _mod_spec(mod3, k, d, tm, rpg)
    mss = lambda k: pl.BlockSpec((None, msr, d), lambda i, j: (0, 0, k))
    return pl.pallas_call(
        _ffn_up_kernel,
        grid=(m // tm, nf),
        in_specs=[
            pl.BlockSpec((tm, d), lambda i, j: (i, 0)),
            pl.BlockSpec((1, d), lambda i, j: (0, 0)),
            ms(k0), ms(k0 + 1),
            pl.BlockSpec((msr, d), lambda i, j: (0, 0)),
            mss(k0), mss(k0 + 1),
            _wspec(w_in, (d, tf), lambda i, j: (0, j)),
            _wspec(w_in, (d, tf), lambda i, j: (0, j + nf)),
        ],
        out_specs=[pl.BlockSpec((tm, tf), lambda i, j: (i, j)),
                   pl.BlockSpec((msr, tf), lambda i, j: (jnp.minimum(i, 1), j))],
        out_shape=[jax.ShapeDtypeStruct((m, f), BF16), jax.ShapeDtypeStruct((2 * msr, f), BF16)],
        scratch_shapes=[pltpu.VMEM((tm, d), BF16), pltpu.VMEM((msr, d), BF16)],
        name="ffn_up",
        compiler_params=_cparams("arbitrary", "arbitrary"),
    )(x, nw, mod3, mod3, xs, mods3, mods3, w_in[0], w_in[0])


def _outproj_kernel(a_ref, w_ref, x_ref, g_ref, o_ref, *, gain):
    o_ref[...] = x_ref[...] + gain * g_ref[...] * _dot(a_ref[...], w_ref[...])


def _outproj(a, w, x, mod3, kg, *, tm, tn_target=1024, gain=1.0):
    m, kdim = x.shape[0], a.shape[1]
    d = _wshape(w)[1]
    tn = _tile(d, tn_target)
    rpg = m // mod3.shape[0]
    r = mod3.shape[1]
    nd = d // tn
    return pl.pallas_call(
        functools.partial(_outproj_kernel, gain=gain),
        grid=(m // tm, nd),
        in_specs=[
            pl.BlockSpec((tm, kdim), lambda i, j: (i, 0)),
            _wspec(w, (kdim, tn), lambda i, j: (0, j)),
            pl.BlockSpec((tm, tn), lambda i, j: (i, j)),
            pl.BlockSpec((None, r, tn), lambda i, j: ((i * tm) // rpg, 0, kg * nd + j)),
        ],
        out_specs=pl.BlockSpec((tm, tn), lambda i, j: (i, j)),
        out_shape=jax.ShapeDtypeStruct((m, d), F32),
        name="outproj",
        compiler_params=_cparams("parallel", "arbitrary"),
    )(a, w[0], x, mod3)


def _final_kernel(x_ref, nw_ref, sh_ref, sc_ref, o_ref):
    o_ref[...] = _norm_mod(x_ref[...], nw_ref[...], sh_ref[...], sc_ref[...])


def _final_norm(x, mod3, nw, *, tm):
    m, d = x.shape
    rpg = m // mod3.shape[0]
    r = mod3.shape[1]
    ms = lambda k: pl.BlockSpec((None, r, d), lambda i: ((i * tm) // rpg, 0, k))
    return pl.pallas_call(
        _final_kernel,
        grid=(m // tm,),
        in_specs=[pl.BlockSpec((tm, d), lambda i: (i, 0)), pl.BlockSpec((1, d), lambda i: (0, 0)), ms(0), ms(1)],
        out_specs=pl.BlockSpec((tm, d), lambda i: (i, 0)),
        out_shape=jax.ShapeDtypeStruct((m, d), F32),
        name="final_norm",
        compiler_params=_cparams("parallel"),
    )(x, nw, mod3, mod3)


def _causal_conv(x, halo, w):
    def taps(rows, fix):
        acc = rows * w[CONV_W - 1:CONV_W]
        for j in range(1, CONV_W):
            acc = acc + fix(pltpu.roll(rows, j, 0), j) * w[CONV_W - 1 - j:CONV_W - j]
        return acc

    row = lax.broadcasted_iota(I32, halo.shape, 0)
    head = taps(x[:SUBLANES], lambda r, j: jnp.where(row < j, pltpu.roll(halo, j, 0), r))
    if x.shape[0] == SUBLANES:
        return head
    return jnp.concatenate([head, taps(x, lambda r, j: r)[SUBLANES:]], axis=0)


def _conv_specs(tt, tc, coff, boff):
    sub = tt // SUBLANES
    return [
        pl.BlockSpec((None, tt, tc), lambda b, t, c: (b, t, c + coff)),
        pl.BlockSpec((None, SUBLANES, tc), lambda b, t, c: (b, jnp.maximum(t * sub - 1, 0), c + coff)),
        pl.BlockSpec((None, SUBLANES, tc), lambda b, t, c: (b, 0, c + boff)),
    ]


def _pad_buf(buf):
    return jnp.pad(buf, ((0, 0), (SUBLANES - (CONV_W - 1), 0), (0, 0)))


def _gdn_prep_kernel(x_ref, halo_ref, buf_ref, w_ref, o_ref, *, norm):
    halo = jnp.where(pl.program_id(1) == 0, buf_ref[...], halo_ref[...])
    y = _silu(_causal_conv(x_ref[...], halo, w_ref[...]))
    if norm:
        for h in range(y.shape[1] // LANES):
            seg = y[:, h * LANES:(h + 1) * LANES]
            ss = jnp.sum(seg * seg, axis=-1, keepdims=True)
            o_ref[:, h * LANES:(h + 1) * LANES] = seg * lax.rsqrt(ss + NORM_EPS)
    else:
        o_ref[...] = y


def _gdn_prep(proj, buf8, conv_w, *, col0, ncols, norm, tt):
    bsz, t, _ = proj.shape
    tc = _tile(ncols, 1024)
    return pl.pallas_call(
        functools.partial(_gdn_prep_kernel, norm=norm),
        grid=(bsz, t // tt, ncols // tc),
        in_specs=_conv_specs(tt, tc, col0 // tc, col0 // tc)
        + [pl.BlockSpec((CONV_W, tc), lambda b, i, c: (0, c + col0 // tc))],
        out_specs=pl.BlockSpec((None, tt, tc), lambda b, i, c: (b, i, c)),
        out_shape=jax.ShapeDtypeStruct((bsz, t, ncols), F32),
        name="gdn_prep",
        compiler_params=_cparams("parallel", "parallel", "parallel"),
    )(proj, proj, buf8, conv_w)


def _gdn_gate_kernel(x_ref, alog_ref, dtb_ref, beta_ref, gc_ref, *, hv, chunk):
    x = x_ref[...]
    tt = x.shape[0]
    beta_ref[...] = _sigmoid(x)
    z = x + dtb_ref[...]
    g = -jnp.exp(alog_ref[...]) * (jnp.maximum(z, 0.0) + jnp.log1p(jnp.exp(-jnp.abs(z))))
    row = lax.broadcasted_iota(I32, (tt, tt), 0)
    col = lax.broadcasted_iota(I32, (tt, tt), 1)
    tri = jnp.where((row >= col) & (row // chunk == col // chunk), 1.0, 0.0)
    gc_ref[...] = _dot_f32(tri, g)


def _gdn_gates(proj, a_log, dt_bias, *, col0, hv, chunk, tt):
    bsz, t, _ = proj.shape
    pad = lambda v: jnp.pad(v.astype(F32), (hv, LANES - 2 * hv)).reshape(1, LANES)
    blk = pl.BlockSpec((None, tt, LANES), lambda b, i: (b, i, col0 // LANES))
    out = pl.BlockSpec((None, tt, LANES), lambda b, i: (b, i, 0))
    par = pl.BlockSpec((1, LANES), lambda b, i: (0, 0))
    return pl.pallas_call(
        functools.partial(_gdn_gate_kernel, hv=hv, chunk=chunk),
        grid=(bsz, t // tt),
        in_specs=[blk, par, par],
        out_specs=[out, out],
        out_shape=[jax.ShapeDtypeStruct((bsz, t, LANES), F32)] * 2,
        name="gdn_gates",
        compiler_params=_cparams("parallel", "parallel"),
    )(proj, pad(a_log), pad(dt_bias))


def _split_bf16(a):
    hi = a.astype(BF16)
    return hi, (a - hi.astype(F32)).astype(BF16)


def _dot_split(a, b):
    ah, al = _split_bf16(a)
    bh, bl = _split_bf16(b)
    return _dot(jnp.concatenate([ah, al, ah], axis=1), jnp.concatenate([bh, bh, bl], axis=0))


def _dot_bf16(a, b):
    return _dot(a.astype(BF16), b.astype(BF16))


def _tri_inv_all(lmats, c, n_real):
    row = lax.broadcasted_iota(I32, (c, c), 0)
    col = lax.broadcasted_iota(I32, (c, c), 1)
    eye = jnp.where(row == col, 1.0, 0.0)
    base = min(16, c)
    ps = [-jnp.where(row // base == col // base, m, 0.0) for m in lmats]
    rs = [eye + p for p in ps]
    n = 2
    while n < base:
        ps = [_dot_bf16(p, p) for p in ps]
        rs = [r + _dot_bf16(r, p) for r, p in zip(rs, ps)]
        n *= 2
    s = base
    while s < min(c, n_real):
        off = (row // (2 * s) == col // (2 * s)) & (row // s != col // s)
        ts = [_dot_bf16(jnp.where(off, m, 0.0), r) for m, r in zip(lmats, rs)]
        rs = [r - _dot_bf16(r, t) for r, t in zip(rs, ts)]
        s *= 2
    res = [eye - r - _dot_split(m, r) for m, r in zip(lmats, rs)]
    return [r + _dot_bf16(r, e) for r, e in zip(rs, res)]


def _gdn_core_kernel(q_ref, k_ref, v_ref, z_ref, gcc_ref, gcr_ref, bc_ref, s0_ref, nw_ref, o_ref, so_ref, s_ref,
                     *, rep, c, ncb, hpb, dk, n_real):
    ci = pl.program_id(2)

    @pl.when(ci == 0)
    def _():
        s_ref[...] = s0_ref[...]

    row = lax.broadcasted_iota(I32, (c, c), 0)
    col = lax.broadcasted_iota(I32, (c, c), 1)
    causal = row >= col
    nhd = hpb * rep
    kinst = [(n, hh) for n in range(ncb) for hh in range(hpb)]
    inst = [(n, hh, r) for n in range(ncb) for hh in range(hpb) for r in range(rep)]
    rows = lambda n: slice(n * c, (n + 1) * c)
    lanes = lambda hh, r: slice((hh * rep + r) * LANES, (hh * rep + r + 1) * LANES)
    kidx = lambda n, hh: n * hpb + hh
    qs = [q_ref[rows(n), hh * LANES:(hh + 1) * LANES] * dk ** -0.5 for n, hh in kinst]
    ks = [k_ref[rows(n), hh * LANES:(hh + 1) * LANES] for n, hh in kinst]
    k16 = [k.astype(BF16) for k in ks]
    grams = [_dot_nt(kb, kb) for kb in k16]
    qk0s = [_dot_nt(q.astype(BF16), kb) for q, kb in zip(qs, k16)]
    gccs = [gcc_ref[hh, rows(n), r:r + 1] for n, hh, r in inst]
    gcrs = [gcr_ref[hh, r:r + 1, rows(n)] for n, hh, r in inst]
    betas = [bc_ref[hh, rows(n), r:r + 1] for n, hh, r in inst]
    decays = [jnp.where(causal, jnp.exp(jnp.where(causal, gc - gr, 0.0)), 0.0) for gc, gr in zip(gccs, gcrs)]
    lowers = [jnp.where(row > col, grams[kidx(n, hh)] * b * d, 0.0)
              for (n, hh, _), b, d in zip(inst, betas, decays)]
    tinvs = _tri_inv_all(lowers, c, n_real)
    egcs = [jnp.exp(gc) for gc in gccs]
    sols = [_dot_split(ti, jnp.concatenate([v_ref[rows(n), lanes(hh, r)] * b, ks[kidx(n, hh)] * (b * e)], axis=1))
            for ti, b, e, (n, hh, r) in zip(tinvs, betas, egcs, inst)]
    qg16 = [(qs[kidx(n, hh)] * e).astype(BF16) for (n, hh, _), e in zip(inst, egcs)]
    qk16 = [(qk0s[kidx(n, hh)] * d).astype(BF16) for (n, hh, _), d in zip(inst, decays)]
    g_last = [gc[c - 1:c, :] for gc in gccs]
    kd16 = [(ks[kidx(n, hh)] * jnp.exp(gl - gc)).astype(BF16) for (n, hh, _), gl, gc in zip(inst, g_last, gccs)]
    ss = [s_ref[i] for i in range(nhd)]
    for n in range(ncb):
        ids = range(n * nhd, (n + 1) * nhd)
        s16 = [s.astype(BF16) for s in ss]
        u16 = [(sols[i][:, :LANES] - _dot(sols[i][:, LANES:].astype(BF16), sb)).astype(BF16)
               for i, sb in zip(ids, s16)]
        ss = [s * jnp.exp(g_last[i]) + _dot_tn(kd16[i], ub) for s, i, ub in zip(ss, ids, u16)]
        os_ = [_dot(qg16[i], sb) + _dot(qk16[i], ub) for i, sb, ub in zip(ids, s16, u16)]
        for o, i in zip(os_, ids):
            _, hh, r = inst[i]
            on = o * lax.rsqrt(jnp.mean(o * o, axis=-1, keepdims=True) + NORM_EPS) * nw_ref[...]
            o_ref[rows(n), lanes(hh, r)] = (on * _silu(z_ref[rows(n), lanes(hh, r)])).astype(BF16)
    for i in range(nhd):
        s_ref[i] = ss[i]

    @pl.when(ci == pl.num_programs(2) - 1)
    def _():
        so_ref[...] = s_ref[...]


def _gdn_core(qk, v, proj, zcol0, gc, beta, s0, norm_w, *, c, ncb, hpb, n_real):
    bsz, t, val = v.shape
    hv = s0.shape[1]
    dk, dv = s0.shape[2], s0.shape[3]
    hk = qk.shape[2] // (2 * dk)
    rep = hv // hk
    tb = ncb * c
    assert dk == LANES and dv == LANES and t % tb == 0 and hk % hpb == 0 and zcol0 % (hpb * rep * dv) == 0
    heads = lambda a, lo: a[:, :, lo:lo + hv].reshape(bsz, t, hk, rep).transpose(0, 2, 1, 3)
    gcc = heads(gc, hv)
    bcc = heads(beta, 0)
    gcr = gcc.transpose(0, 1, 3, 2)
    nhb = hk // hpb
    zb = zcol0 // (hpb * rep * dv)
    colspec = pl.BlockSpec((None, hpb, tb, rep), lambda b, h, i: (b, h, i, 0))
    o, s_out = pl.pallas_call(
        functools.partial(_gdn_core_kernel, rep=rep, c=c, ncb=ncb, hpb=hpb, dk=dk, n_real=n_real),
        grid=(bsz, nhb, t // tb),
        in_specs=[
            pl.BlockSpec((None, tb, hpb * dk), lambda b, h, i: (b, i, h)),
            pl.BlockSpec((None, tb, hpb * dk), lambda b, h, i: (b, i, nhb + h)),
            pl.BlockSpec((None, tb, hpb * rep * dv), lambda b, h, i: (b, i, h)),
            pl.BlockSpec((None, tb, hpb * rep * dv), lambda b, h, i: (b, i, zb + h)),
            colspec,
            pl.BlockSpec((None, hpb, rep, tb), lambda b, h, i: (b, h, 0, i)),
            colspec,
            pl.BlockSpec((None, hpb * rep, dk, dv), lambda b, h, i: (b, h, 0, 0)),
            pl.BlockSpec((1, dv), lambda b, h, i: (0, 0)),
        ],
        out_specs=[
            pl.BlockSpec((None, tb, hpb * rep * dv), lambda b, h, i: (b, i, h)),
            pl.BlockSpec((None, hpb * rep, dk, dv), lambda b, h, i: (b, h, 0, 0)),
        ],
        out_shape=[jax.ShapeDtypeStruct((bsz, t, val), BF16), jax.ShapeDtypeStruct(s0.shape, F32)],
        scratch_shapes=[pltpu.VMEM((hpb * rep, dk, dv), F32)],
        name="gdn_core",
        compiler_params=_cparams("parallel", "parallel", "arbitrary"),
    )(qk, qk, v, proj, gcc, gcr, bcc, s0, norm_w.reshape(1, dv))
    return o, s_out


def _gdn_mixer(x, mod3, nw, buf, s0, w_in, conv_w, a_log, dt_bias, norm_w, w_out, *, tm, tm_in, tt, chunk):
    bsz, t, d = x.shape
    hv, dk, dv = s0.shape[1], s0.shape[2], s0.shape[3]
    val = hv * dv
    conv_dim = conv_w.shape[1]
    key = (conv_dim - val) // 2
    assert (conv_dim + val) % LANES == 0 and 2 * hv <= LANES and t >= CONV_W - 1
    x2 = x.reshape(bsz * t, d)
    proj = _normproj(x2, mod3, 3, nw, w_in, tm=tm_in, tn_target=512).reshape(bsz, t, -1)
    buf8 = _pad_buf(buf)
    qk = _gdn_prep(proj, buf8, conv_w, col0=0, ncols=2 * key, norm=True, tt=tt)
    v = _gdn_prep(proj, buf8, conv_w, col0=2 * key, ncols=val, norm=False, tt=tt)
    tp = -(-t // chunk) * chunk
    beta, gc = _gdn_gates(proj, a_log, dt_bias, col0=conv_dim + val, hv=hv, chunk=min(chunk, tt), tt=tt)
    if tp != t:
        padt = lambda a: jnp.pad(a, ((0, 0), (0, tp - t), (0, 0)))
        gc = jnp.concatenate([gc, jnp.broadcast_to(gc[:, -1:], (bsz, tp - t, LANES))], axis=1)
        qk, v, beta, projz = padt(qk), padt(v), padt(beta), padt(proj)
    else:
        projz = proj
    o, s_new = _gdn_core(qk, v, projz, conv_dim, gc, beta, s0, norm_w, c=chunk, ncb=math.gcd(tp // chunk, 2),
                         hpb=min(4 if tp > chunk else 16, key // dk), n_real=min(t, chunk))
    o2 = o[:, :t].reshape(bsz * t, val)
    xo = _outproj(o2, w_out, x2, mod3, 5, tm=tm).reshape(bsz, t, d)
    new_buf = proj[:, t - (CONV_W - 1):, :conv_dim]
    return xo, new_buf, s_new


def _rope_tables(pos, half):
    inv_freq = ROPE_THETA ** (-jnp.arange(half, dtype=F32) / half)
    ang = pos.astype(F32)[:, None] * inv_freq[None, :]
    cos, sin = jnp.cos(ang), jnp.sin(ang)
    return jnp.concatenate([cos, cos], axis=-1), jnp.concatenate([-sin, sin], axis=-1)


def _dsa_prep_kernel(x_ref, cos_ref, sin_ref, q_ref, k_ref, k16_ref, v16_ref, qi_ref, ki_ref, ki16_ref, *tr_refs,
                     nh, nkv, nih):
    cos, sin = cos_ref[...], sin_ref[...]

    def rope(col):
        seg = x_ref[:, col * LANES:(col + 1) * LANES]
        return seg * cos + pltpu.roll(seg, LANES // 2, 1) * sin

    for h in range(nh):
        q_ref[:, h * LANES:(h + 1) * LANES] = (rope(h) * LANES ** -0.5).astype(BF16)
    for h in range(nkv):
        kr = rope(nh + h)
        k_ref[:, h * LANES:(h + 1) * LANES] = kr
        k16_ref[:, h * LANES:(h + 1) * LANES] = kr.astype(BF16)
    v0 = (nh + nkv) * LANES
    v16_ref[...] = x_ref[:, v0:v0 + nkv * LANES].astype(BF16)
    c0 = nh + 2 * nkv
    for h in range(nih):
        qi_ref[:, h * LANES:(h + 1) * LANES] = rope(c0 + h).astype(BF16)
    kir = rope(c0 + nih)
    ki_ref[...] = kir
    ki16_ref[...] = kir.astype(BF16)
    if tr_refs:
        wt_ref, vt_ref = tr_refs
        w0 = (c0 + nih + 1) * LANES
        wt_ref[...] = x_ref[:, w0:w0 + LANES].T[:wt_ref.shape[0], :]
        for h in range(nkv):
            vt_ref[h * VT_ROWS:h * VT_ROWS + LANES, :] = x_ref[:, v0 + h * LANES:v0 + (h + 1) * LANES].T.astype(BF16)
            vt_ref[h * VT_ROWS + LANES:(h + 1) * VT_ROWS, :] = jnp.ones((VT_ROWS - LANES, vt_ref.shape[1]), BF16)


def _dsa_prep(proj, pos, *, nh, nkv, nih, tt, with_wt):
    bsz, t, npj = proj.shape
    cos, sin = _rope_tables(pos, LANES // 2)
    row = lambda n, dt: jax.ShapeDtypeStruct((bsz, t, n * LANES), dt)
    ospec = lambda n: pl.BlockSpec((None, tt, n * LANES), lambda b, i: (b, i, 0))
    tab = pl.BlockSpec((tt, LANES), lambda b, i: (i, 0))
    nwt = -(-nih // SUBLANES) * SUBLANES
    return pl.pallas_call(
        functools.partial(_dsa_prep_kernel, nh=nh, nkv=nkv, nih=nih),
        grid=(bsz, t // tt),
        in_specs=[pl.BlockSpec((None, tt, npj), lambda b, i: (b, i, 0)), tab, tab],
        out_specs=[ospec(nh), ospec(nkv), ospec(nkv), ospec(nkv), ospec(nih), ospec(1), ospec(1)]
        + ([pl.BlockSpec((None, nwt, tt), lambda b, i: (b, 0, i)),
            pl.BlockSpec((None, None, nkv * VT_ROWS, tt), lambda b, i: (b, i, 0, 0))] if with_wt else []),
        out_shape=[row(nh, BF16), row(nkv, F32), row(nkv, BF16), row(nkv, BF16), row(nih, BF16), row(1, F32),
                   row(1, BF16)]
        + ([jax.ShapeDtypeStruct((bsz, nwt, t), F32),
            jax.ShapeDtypeStruct((bsz, t // tt, nkv * VT_ROWS, tt), BF16)] if with_wt else []),
        name="dsa_prep",
        compiler_params=_cparams("parallel", "parallel"),
    )(proj, cos, sin)


def _sort_key(s):
    bits = pltpu.bitcast(jnp.where(s == 0.0, 0.0, s), I32)
    return jnp.where(bits < 0, bits ^ 0x7FFFFFFF, bits)


def _topk_cut(count_where, shape, topk, idx_bits):
    def body(i, carry):
        t, n_t = carry
        cand = t + lax.shift_left(jnp.int32(1), 31 - i)
        n = count_where(lambda key, idx: key >= cand)
        ok = n >= topk
        return jnp.where(ok, cand, t), jnp.where(ok, n, n_t)

    thr, n_ge = lax.fori_loop(0, 32, body, (jnp.full(shape, INT32_MIN, I32), jnp.full(shape, INT32_MAX, I32)))
    tie = (n_ge > topk) & (thr > KEY_OF_NEG_INF)

    def cut():
        need = topk - count_where(lambda key, idx: key > thr)

        def jbody(i, j):
            cand = j + lax.shift_left(jnp.int32(1), idx_bits - 1 - i)
            below = count_where(lambda key, idx: (key == thr) & (idx < cand))
            return jnp.where(below < need, cand, j)

        return jnp.where(tie, lax.fori_loop(0, idx_bits, jbody, jnp.zeros(shape, I32)), INT32_MAX)

    jcut = lax.cond(jnp.any(tie), cut, lambda: jnp.full(shape, INT32_MAX, I32))
    return thr, jcut


def _topk_chosen(key, idx, thr, jcut):
    return (key > thr) | ((key == thr) & (idx <= jcut))


def _dsa_prompt_kernel(qi_ref, wt_ref, q_ref, ki_ref, k_ref, vt_ref, o_ref, key_ref, bias_ref, acc_ref,
                       *, nh, nkv, nih, topk, idx_scale):
    qb = pl.program_id(1)
    blk = Q_BLOCK
    kb = vt_ref.shape[2]
    ktiles = kb // blk
    nkb = (qb + ktiles) // ktiles
    rowk = lax.broadcasted_iota(I32, (kb, blk), 0)
    colq = lax.broadcasted_iota(I32, (kb, blk), 1)
    wt = wt_ref[...]
    hq = max(1, nih // 4)
    qis = [jnp.concatenate([qi_ref[:, h * LANES:(h + 1) * LANES] for h in range(h0, min(h0 + hq, nih))], axis=0)
           for h0 in range(0, nih, hq)]

    def rows_of(j):
        return pl.ds(pl.multiple_of(j * kb, kb), kb)

    def visible(j):
        return j * kb + rowk <= qb * blk + colq

    def score_body(j, carry):
        keys = ki_ref[rows_of(j), :]
        lgs = [_dot_nt(keys, qi) for qi in qis]
        acc = jnp.zeros((kb, blk), F32)
        for i, lg in enumerate(lgs):
            for hh in range(lg.shape[1] // blk):
                h = i * hq + hh
                acc = acc + jnp.maximum(lg[:, hh * blk:(hh + 1) * blk], 0.0) * wt[h:h + 1, :]
        key_ref[rows_of(j), :] = _sort_key(jnp.where(visible(j), acc * idx_scale, NEG_INF))
        return carry

    lax.fori_loop(0, nkb, score_body, 0)

    def count_where(pred):
        def body(j, c):
            hit = jnp.where(pred(key_ref[rows_of(j), :], j * kb + rowk), 1, 0)
            for i in range(ktiles):
                c = c + hit[i * blk:(i + 1) * blk]
            return c
        cnt = lax.fori_loop(0, nkb, body, jnp.zeros((blk, blk), I32))
        return jnp.sum(cnt, axis=0, keepdims=True)

    thr, jcut = _topk_cut(count_where, (1, blk), topk, (key_ref.shape[0] - 1).bit_length())

    def bias_body(j, carry):
        sel = _topk_chosen(key_ref[rows_of(j), :], j * kb + rowk, thr, jcut) & visible(j)
        bias_ref[rows_of(j), :] = jnp.where(sel, 0.0, NEG_INF)
        return carry

    lax.fori_loop(0, nkb, bias_body, 0)

    rep = nh // nkv
    qgs =[jnp.concatenate([q_ref[:, (g * rep + r) * LANES:(g * rep + r + 1) * LANES] for r in range(rep)], axis=0)
           for g in range(nkv)]
    acc_ref[...] = jnp.zeros_like(acc_ref)

    def att_body(j, carry):
        ms, ls = carry
        rows = rows_of(j)
        bias = jnp.concatenate([bias_ref[rows, :]] * rep, axis=1)
        ss = [_dot_nt(k_ref[rows, g * LANES:(g + 1) * LANES], qgs[g]) + bias for g in range(nkv)]
        m_new = [jnp.maximum(m, jnp.max(s, axis=0, keepdims=True)) for m, s in zip(ms, ss)]
        ps = [jnp.exp((s - m).astype(BF16)) for s, m in zip(ss, m_new)]
        pvs = [_dot(vt_ref[j, g * VT_ROWS:(g + 1) * VT_ROWS, :], p) for g, p in enumerate(ps)]
        alphas = [jnp.exp(m - mn) for m, mn in zip(ms, m_new)]
        for g in range(nkv):
            acc_ref[g] = alphas[g] * acc_ref[g] + pvs[g][:LANES]
        l_new = [a * l + pv[LANES:LANES + 1] for a, l, pv in zip(alphas, ls, pvs)]
        return tuple(m_new), tuple(l_new)

    row0 = lambda v: tuple(jnp.full((1, rep * blk), v, F32) for _ in range(nkv))
    _, ls = lax.fori_loop(0, nkb, att_body, (row0(NEG_INF), row0(0.0)))
    for g in range(nkv):
        og = acc_ref[g] / ls[g]
        for r in range(rep):
            h = g * rep + r
            o_ref[:, h * LANES:(h + 1) * LANES] = og[:, r * blk:(r + 1) * blk].T.astype(BF16)


def _dsa_prompt_attend(q16, qi16, wt, ki16, k16, vt16, *, nh, nkv, nih, topk):
    bsz, t, _ = q16.shape
    rep = nh // nkv
    full = lambda n: pl.BlockSpec((None, t, n * LANES), lambda b, i: (b, 0, 0))
    blk = lambda n: pl.BlockSpec((None, Q_BLOCK, n * LANES), lambda b, i: (b, i, 0))
    return pl.pallas_call(
        functools.partial(_dsa_prompt_kernel, nh=nh, nkv=nkv, nih=nih, topk=topk,
                          idx_scale=(LANES * nih) ** -0.5),
        grid=(bsz, t // Q_BLOCK),
        in_specs=[blk(nih), pl.BlockSpec((None, wt.shape[1], Q_BLOCK), lambda b, i: (b, 0, i)), blk(nh),
                  full(1), full(nkv), pl.BlockSpec((None,) + vt16.shape[1:], lambda b, i: (b, 0, 0, 0))],
        out_specs=blk(nh),
        out_shape=jax.ShapeDtypeStruct((bsz, t, nh * LANES), BF16),
        scratch_shapes=[pltpu.VMEM((t, Q_BLOCK), I32), pltpu.VMEM((t, Q_BLOCK), F32),
                        pltpu.VMEM((nkv, LANES, rep * Q_BLOCK), F32)],
        name="dsa_prompt_attend",
        compiler_params=_cparams("parallel", "arbitrary"),
    )(qi16, wt, q16, ki16, k16, vt16)


def _dsa_sample_score_kernel(pt_ref, qi_ref, wc_ref, *refs, n_steps, pps, nih, tq, past, idx_scale):
    page_refs, new_ref, o_ref = refs[:pps], refs[pps], refs[pps + 1]
    p = pl.program_id(1)
    keys = jnp.concatenate([r[...] for r in page_refs], axis=0).astype(BF16)
    keys = jnp.where(p == n_steps - 1, new_ref[...], keys)
    qi = jnp.concatenate([qi_ref[:, h * LANES:(h + 1) * LANES] for h in range(nih)], axis=0)
    w = jnp.maximum(_dot_nt(qi, keys), 0.0) * wc_ref[...]
    acc = w[0:tq]
    for h in range(1, nih):
        acc = acc + w[h * tq:(h + 1) * tq]
    s = p * keys.shape[0] + lax.broadcasted_iota(I32, acc.shape, 1)
    qpos = past + lax.broadcasted_iota(I32, acc.shape, 0)
    o_ref[...] = jnp.where(s <= qpos, acc * idx_scale, NEG_INF)


def _dsa_sample_select_kernel(s_ref, o_ref, *, topk):
    key = _sort_key(s_ref[...])
    idx = lax.broadcasted_iota(I32, key.shape, 1)
    count_where = lambda pred: jnp.sum(jnp.where(pred(key, idx), 1, 0), axis=-1, keepdims=True)
    thr, jcut = _topk_cut(count_where, (key.shape[0], 1), topk, (key.shape[1] - 1).bit_length())
    o_ref[...] = jnp.where(_topk_chosen(key, idx, thr, jcut) & (s_ref[...] > 0.5 * NEG_INF), 1.0, 0.0)


def _dsa_sample_attn_kernel(pt_ref, q_ref, sel_ref, *refs, n_steps, pps, nh, nkv, tq):
    kp_refs, vp_refs = refs[:pps], refs[pps:2 * pps]
    kn_ref, vn_ref, o_ref, m_ref, l_ref, acc_ref = refs[2 * pps:]
    p = pl.program_id(1)
    rep = nh // nkv
    page = sel_ref.shape[1] // pps

    @pl.when(p == 0)
    def _():
        m_ref[...] = jnp.full_like(m_ref, NEG_INF)
        l_ref[...] = jnp.zeros_like(l_ref)
        acc_ref[...] = jnp.zeros_like(acc_ref)

    last = p == n_steps - 1

    def head_rows(page_refs, new_ref, g):
        cached = jnp.concatenate([r[pl.ds(g, page, stride=nkv), :] for r in page_refs], axis=0).astype(BF16)
        return jnp.where(last, new_ref[g], cached)

    rows = rep * tq
    q = jnp.concatenate([q_ref[:, h * LANES:(h + 1) * LANES] for h in range(nh)], axis=0)
    unsel = (sel_ref[...] - 1.0) * -NEG_INF
    s = jnp.concatenate([_dot_nt(q[g * rows:(g + 1) * rows], head_rows(kp_refs, kn_ref, g)) for g in range(nkv)],
                        axis=0) + jnp.concatenate([unsel] * nh, axis=0)
    m = m_ref[...]
    m_new = jnp.maximum(m, jnp.max(s, axis=-1, keepdims=True))
    alpha = jnp.exp(m - m_new)
    pr = jnp.where(s > 0.5 * NEG_INF, jnp.exp(s - m_new), 0.0)
    l_ref[...] = alpha * l_ref[...] + jnp.sum(pr, axis=-1, keepdims=True)
    pv = jnp.concatenate([_dot(pr[g * rows:(g + 1) * rows].astype(BF16), head_rows(vp_refs, vn_ref, g))
                          for g in range(nkv)], axis=0)
    acc_ref[...] = alpha * acc_ref[...] + pv
    m_ref[...] = m_new

    @pl.when(last)
    def _():
        og = acc_ref[...] / l_ref[...]
        for h in range(nh):
            o_ref[:, h * LANES:(h + 1) * LANES] = og[h * tq:(h + 1) * tq].astype(BF16)


def _dsa_sample_attend(q16, qi16, wi, ki16, k16, v16, cache_k, cache_v, cache_i, layer, page_table,
                       *, nh, nkv, nih, topk):
    bsz, tq, _ = q16.shape
    n_pages = page_table.shape[1]
    n_layers, n_pool, page = cache_i.shape[:3]
    pps = math.gcd(n_pages, SAMPLE_PAGES_PER_STEP)
    assert page == LANES and tq == SUBLANES
    past = n_pages * page
    n_steps = n_pages // pps + 1
    ltot = n_steps * pps * page
    wcol = wi.transpose(0, 2, 1).reshape(bsz, nih * tq, 1)
    pidx = lambda i: (lambda b, p, pt: (layer * n_pool + pt[b, jnp.minimum(p * pps + i, n_pages - 1)], 0, 0))
    qspec = lambda n: pl.BlockSpec((None, tq, n * LANES), lambda b, p, pt: (b, 0, 0))
    idx_scale = (LANES * nih) ** -0.5
    cache_i = cache_i.reshape(n_layers * n_pool, page, LANES)
    ki_new = jnp.pad(ki16, ((0, 0), (0, pps * page - tq), (0, 0)))
    scores = pl.pallas_call(
        functools.partial(_dsa_sample_score_kernel, n_steps=n_steps, pps=pps, nih=nih, tq=tq, past=past,
                          idx_scale=idx_scale),
        grid_spec=pltpu.PrefetchScalarGridSpec(
            num_scalar_prefetch=1, grid=(bsz, n_steps),
            in_specs=[qspec(nih), pl.BlockSpec((None, nih * tq, 1), lambda b, p, pt: (b, 0, 0))]
            + [pl.BlockSpec((None, page, LANES), pidx(i)) for i in range(pps)]
            + [pl.BlockSpec((None, pps * page, LANES), lambda b, p, pt: (b, 0, 0))],
            out_specs=pl.BlockSpec((None, tq, pps * page), lambda b, p, pt: (b, 0, p))),
        out_shape=jax.ShapeDtypeStruct((bsz, tq, ltot), F32),
        name="dsa_sample_scores",
        compiler_params=_cparams("parallel", "arbitrary"),
    )(page_table, qi16, wcol, *([cache_i] * pps), ki_new)
    sel = pl.pallas_call(
        functools.partial(_dsa_sample_select_kernel, topk=topk),
        grid=(1,),
        in_specs=[pl.BlockSpec((bsz * tq, ltot), lambda i: (0, 0))],
        out_specs=pl.BlockSpec((bsz * tq, ltot), lambda i: (0, 0)),
        out_shape=jax.ShapeDtypeStruct((bsz * tq, ltot), F32),
        name="dsa_sample_select",
        compiler_params=_cparams("arbitrary"),
    )(scores.reshape(bsz * tq, ltot)).reshape(bsz, tq, ltot)
    prow = page * nkv
    ck = cache_k.reshape(n_layers * n_pool, prow, LANES)
    cv = cache_v.reshape(n_layers * n_pool, prow, LANES)
    new_rows = lambda a: jnp.pad(a.reshape(bsz, tq, nkv, LANES).transpose(0, 2, 1, 3),
                                 ((0, 0), (0, 0), (0, pps * page - tq), (0, 0)))
    kvspecs = [pl.BlockSpec((None, prow, LANES), pidx(i)) for i in range(pps)]
    newspec = pl.BlockSpec((None, nkv, pps * page, LANES), lambda b, p, pt: (b, 0, 0, 0))
    return pl.pallas_call(
        functools.partial(_dsa_sample_attn_kernel, n_steps=n_steps, pps=pps, nh=nh, nkv=nkv, tq=tq),
        grid_spec=pltpu.PrefetchScalarGridSpec(
            num_scalar_prefetch=1, grid=(bsz, n_steps),
            in_specs=[qspec(nh), pl.BlockSpec((None, tq, pps * page), lambda b, p, pt: (b, 0, p))]
            + kvspecs + kvspecs + [newspec, newspec],
            out_specs=qspec(nh),
            scratch_shapes=[pltpu.VMEM((nh * tq, 1), F32), pltpu.VMEM((nh * tq, 1), F32),
                            pltpu.VMEM((nh * tq, LANES), F32)]),
        out_shape=jax.ShapeDtypeStruct((bsz, tq, nh * LANES), BF16),
        name="dsa_sample_attend",
        compiler_params=_cparams("parallel", "arbitrary"),
    )(page_table, q16, sel, *([ck] * pps), *([cv] * pps), new_rows(k16), new_rows(v16))


def _dsa_mixer(x, mod3, nw, w_in, w_out, n_proj, *, nkv, tm, tm_in, tt, cache=None):
    bsz, t, d = x.shape
    nh = _wshape(w_out)[0] // LANES
    nih = (n_proj - (nh + 2 * nkv + 1) * LANES) // (LANES + 1)
    assert (nh + 2 * nkv + nih + 1) * LANES + nih == n_proj and nih <= LANES
    x2 = x.reshape(bsz * t, d)
    proj = _normproj(x2, mod3, 3, nw, w_in, tm=tm_in).reshape(bsz, t, -1)
    v = proj[:, :, (nh + nkv) * LANES:(nh + 2 * nkv) * LANES]
    if cache is None:
        assert t % Q_BLOCK == 0
        q16, k, k16, _, qi16, ki, ki16, wt, vt16 = _dsa_prep(proj, jnp.arange(t), nh=nh, nkv=nkv, nih=nih,
                                                             tt=_tile(t, 4 * Q_BLOCK, Q_BLOCK), with_wt=True)
        o = _dsa_prompt_attend(q16, qi16, wt, ki16, k16, vt16, nh=nh, nkv=nkv, nih=nih, topk=min(TOPK_MAX, t // 4))
    else:
        cache_k, cache_v, cache_i, layer, page_table = cache
        past = page_table.shape[1] * cache_i.shape[2]
        q16, k, k16, v16, qi16, ki, ki16 = _dsa_prep(proj, past + jnp.arange(t), nh=nh, nkv=nkv, nih=nih, tt=tt,
                                                    with_wt=False)
        w0 = (nh + 2 * nkv + nih + 1) * LANES
        o = _dsa_sample_attend(q16, qi16, proj[:, :, w0:w0 + nih], ki16, k16, v16, cache_k, cache_v, cache_i,
                               layer, page_table, nh=nh, nkv=nkv, nih=nih, topk=min(TOPK_MAX, (past + t) // 4))
    xo = _outproj(o.reshape(bsz * t, nh * LANES), w_out, x2, mod3, 5, tm=tm).reshape(bsz, t, d)
    return xo, k.reshape(bsz, t, nkv, LANES), v.reshape(bsz, t, nkv, LANES), ki


def _gelu_tanh(x):
    return 0.5 * x * (1.0 + jnp.tanh(math.sqrt(2.0 / math.pi) * (x + 0.044715 * (x * x * x))))


def _lru_kernel(gate_ref, xb_ref, halo_ref, buf_ref, cw_ref, cb_ref, wa_ref, ba_ref, wx_ref, bx_ref, lam_ref,
                h0_ref, y_ref, hl_ref, h_ref, *, nblk):
    ti = pl.program_id(1)

    @pl.when(ti == 0)
    def _():
        h_ref[...] = h0_ref[...]

    halo = jnp.where(ti == 0, buf_ref[...], halo_ref[...])
    xc = _causal_conv(xb_ref[...], halo, cw_ref[...]) + cb_ref[...]
    tt = xc.shape[0]
    xc16 = xc.astype(BF16)
    rs, xs = [], []
    for n in range(nblk):
        blk = xc16[:, n * LRU_BLOCK:(n + 1) * LRU_BLOCK]
        rs.append(_dot(blk, wa_ref[n]))
        xs.append(_dot(blk, wx_ref[n]))
    r = _sigmoid(jnp.concatenate(rs, axis=1) + ba_ref[...])
    ig = _sigmoid(jnp.concatenate(xs, axis=1) + bx_ref[...])
    lam = lam_ref[...]
    softplus_neg = jnp.maximum(-lam, 0.0) + jnp.log1p(jnp.exp(-jnp.abs(lam)))
    log_a = -RG_C * r * softplus_neg
    a = jnp.exp(log_a)
    b = jnp.sqrt(-jnp.tanh(log_a) * (a * a + 1.0)) * (ig * xc)
    sub = lax.broadcasted_iota(I32, a.shape, 0) % SUBLANES
    d = 1
    while d < SUBLANES:
        keep = sub >= d
        a_sh = jnp.where(keep, pltpu.roll(a, d, 0), 1.0)
        b_sh = jnp.where(keep, pltpu.roll(b, d, 0), 0.0)
        b = a * b_sh + b
        a = a * a_sh
        d *= 2
    carry, groups = h_ref[...], []
    for g in range(tt // SUBLANES):
        rows = slice(g * SUBLANES, (g + 1) * SUBLANES)
        groups.append(b[rows] + a[rows] * carry)
        carry = groups[-1][SUBLANES - 1:SUBLANES]
    hs = jnp.concatenate(groups, axis=0)
    h_ref[...] = hs[tt - 1:tt, :]
    y_ref[...] = (hs * _gelu_tanh(gate_ref[...])).astype(BF16)

    @pl.when(ti == pl.num_programs(1) - 1)
    def _():
        hl_ref[...] = hs[tt - 1:tt, :]


def _lru_core(proj, buf8, h0, conv_w, conv_b, w_ga, b_ga, w_gx, b_gx, lam, *, tt):
    bsz, t, w2 = proj.shape
    w = w2 // 2
    nblk = w // LRU_BLOCK
    sub = tt // SUBLANES
    vec = lambda: pl.BlockSpec((1, w), lambda b, i: (0, 0))
    wsp = lambda: pl.BlockSpec((nblk, LRU_BLOCK, LRU_BLOCK), lambda b, i: (0, 0, 0))
    r1 = lambda v: v.reshape(1, w).astype(F32)
    y, hl = pl.pallas_call(
        functools.partial(_lru_kernel, nblk=nblk),
        grid=(bsz, t // tt),
        in_specs=[
            pl.BlockSpec((None, tt, w), lambda b, i: (b, i, 0)),
            pl.BlockSpec((None, tt, w), lambda b, i: (b, i, 1)),
            pl.BlockSpec((None, SUBLANES, w), lambda b, i: (b, jnp.maximum(i * sub - 1, 0), 1)),
            pl.BlockSpec((None, SUBLANES, w), lambda b, i: (b, 0, 0)),
            pl.BlockSpec((CONV_W, w), lambda b, i: (0, 0)),
            vec(), wsp(), vec(), wsp(), vec(), vec(),
            pl.BlockSpec((None, 1, w), lambda b, i: (b, 0, 0)),
        ],
        out_specs=[pl.BlockSpec((None, tt, w), lambda b, i: (b, i, 0)),
                   pl.BlockSpec((None, 1, w), lambda b, i: (b, 0, 0))],
        out_shape=[jax.ShapeDtypeStruct((bsz, t, w), BF16), jax.ShapeDtypeStruct((bsz, 1, w), F32)],
        scratch_shapes=[pltpu.VMEM((1, w), F32)],
        name="lru_core",
        compiler_params=_cparams("parallel", "arbitrary"),
    )(proj, proj, proj, buf8, conv_w, r1(conv_b), w_ga, r1(b_ga), w_gx, r1(b_gx), r1(lam), h0.reshape(bsz, 1, w))
    return y, hl.reshape(bsz, w)


def _lru_mixer(x, mod3, nw, buf, h0, w_in, conv_w, conv_b, w_ga, b_ga, w_gx, b_gx, lam, w_out, *, tm, tm_in, tt):
    bsz, t, d = x.shape
    assert t >= CONV_W - 1
    x2 = x.reshape(bsz * t, d)
    proj = _normproj(x2, mod3, 3, nw, w_in, tm=tm_in).reshape(bsz, t, -1)
    w = proj.shape[2] // 2
    y, hl = _lru_core(proj, _pad_buf(buf), h0, conv_w, conv_b, w_ga, b_ga, w_gx, b_gx, lam, tt=tt)
    xo = _outproj(y.reshape(bsz * t, w), w_out, x2, mod3, 5, tm=tm).reshape(bsz, t, d)
    return xo, proj[:, t - (CONV_W - 1):, w:], hl


def _bf16_padded(w, mult):
    n = w.shape[-1]
    return jnp.pad(w, ((0, 0), (0, 0), (0, -n % mult))).astype(BF16)


def kernel(x_prompt, x_sample, state_a_conv, state_a_ssm, cache_b_k, cache_b_v, cache_b_idx, state_c_conv, state_c_h, page_table, c_prompt, c_sample, w_ada, b_ada, norm_w, ffn_w_in, ffn_w_out, gdn_w_in, gdn_conv_w, gdn_a_log, gdn_dt_bias, gdn_norm_w, gdn_w_out, dsa_w_in, dsa_w_out, lru_w_in, lru_conv_w, lru_conv_b, lru_w_gate_a, lru_b_gate_a, lru_w_gate_x, lru_b_gate_x, lru_lambda, lru_w_out, w_ada_final, b_ada_final, final_norm_w):
    bp, seq, d = x_prompt.shape
    bs, ts, _ = x_sample.shape
    depth = w_ada.shape[0]
    nkv = cache_b_k.shape[3]
    tm_p = _tile(seq, 512, SUBLANES)
    tm_in_p = _tile(seq, 1024, SUBLANES)
    tt_p = _tile(seq, 256, SUBLANES)
    tm_s = bs * ts
    tiles_p = dict(tm=tm_p, tm_in=tm_in_p, tt=tt_p)
    tiles_s = dict(tm=tm_s, tm_in=tm_s, tt=ts)

    c_all = jnp.concatenate([c_prompt, c_sample], axis=0)
    c_all = jnp.pad(c_all, ((0, -c_all.shape[0] % SUBLANES), (0, 0)))
    mod = _ada(c_all, w_ada, b_ada)
    mod_f = _ada(c_all, w_ada_final[None], b_ada_final[None])[0]

    def groups(m):
        return m[:bp, None, :], jnp.repeat(m[bp:bp + bs], ts, axis=0)[None]

    ffn_w_out16 = ffn_w_out.astype(BF16)
    gdn_w_in16, gdn_w_out16 = _bf16_padded(gdn_w_in, 512), gdn_w_out.astype(BF16)
    dsa_w_in16, dsa_w_out16 = _bf16_padded(dsa_w_in, 512), dsa_w_out.astype(BF16)
    lru_w_in16, lru_w_out16 = lru_w_in.astype(BF16), lru_w_out.astype(BF16)

    xp, xs = x_prompt, x_sample
    outs = {k: [] for k in ("a_conv_p", "a_conv_s", "a_ssm_p", "a_ssm_s", "b_k_p", "b_k_s", "b_v_p", "b_v_s",
                            "b_i_p", "b_i_s", "c_conv_p", "c_conv_s", "c_h_p", "c_h_s")}
    for layer in range(depth):
        kind, j = layer % N_MIXERS, layer // N_MIXERS
        mod_p, mod_s = groups(mod[layer])
        nw = norm_w[layer]

        def ffn(xp, xs, which):
            xp2, xs2 = xp.reshape(bp * seq, d), xs.reshape(tm_s, d)
            w_out = (ffn_w_out16, (layer, which))
            hp, hs = _ffn_up(xp2, mod_p, xs2, mod_s, 6 * which, nw[2 * which:2 * which + 1],
                             (ffn_w_in, (layer, which)), tm=tm_in_p)
            down = functools.partial(_outproj, kg=6 * which + 2, tn_target=512, gain=0.5)
            return (down(hp, w_out, xp2, mod_p, tm=tm_in_p).reshape(xp.shape),
                    down(hs, w_out, xs2, mod_s, tm=tm_s).reshape(xs.shape))

        xp, xs = ffn(xp, xs, 0)
        if kind == 0:
            prm = ((gdn_w_in16, (j,)), gdn_conv_w[j], gdn_a_log[j], gdn_dt_bias[j], gdn_norm_w[j],
                   (gdn_w_out16, (j,)))
            buf0 = jnp.zeros((bp,) + state_a_conv.shape[2:], F32)
            s0 = jnp.zeros((bp,) + state_a_ssm.shape[2:], F32)
            xp, buf, s = _gdn_mixer(xp, mod_p, nw[1:2], buf0, s0, *prm, chunk=2 * GDN_CHUNK, **tiles_p)
            outs["a_conv_p"].append(buf)
            outs["a_ssm_p"].append(s)
            xs, buf, s = _gdn_mixer(xs, mod_s, nw[1:2], state_a_conv[j], state_a_ssm[j], *prm, chunk=GDN_CHUNK,
                                    **tiles_s)
            outs["a_conv_s"].append(buf)
            outs["a_ssm_s"].append(s)
        elif kind == 1:
            w_in, w_out = (dsa_w_in16, (j,)), (dsa_w_out16, (j,))
            n_proj = dsa_w_in.shape[2]
            xp, k, v, ki = _dsa_mixer(xp, mod_p, nw[1:2], w_in, w_out, n_proj, nkv=nkv, **tiles_p)
            outs["b_k_p"].append(k)
            outs["b_v_p"].append(v)
            outs["b_i_p"].append(ki)
            xs, k, v, ki = _dsa_mixer(xs, mod_s, nw[1:2], w_in, w_out, n_proj, nkv=nkv,
                                      cache=(cache_b_k, cache_b_v, cache_b_idx, j, page_table), **tiles_s)
            outs["b_k_s"].append(k)
            outs["b_v_s"].append(v)
            outs["b_i_s"].append(ki)
        else:
            prm = ((lru_w_in16, (j,)), lru_conv_w[j], lru_conv_b[j], lru_w_gate_a[j].astype(BF16),
                   lru_b_gate_a[j], lru_w_gate_x[j].astype(BF16), lru_b_gate_x[j], lru_lambda[j],
                   (lru_w_out16, (j,)))
            buf0 = jnp.zeros((bp,) + state_c_conv.shape[2:], F32)
            h0 = jnp.zeros((bp,) + state_c_h.shape[2:], F32)
            xp, buf, hl = _lru_mixer(xp, mod_p, nw[1:2], buf0, h0, *prm, **tiles_p)
            outs["c_conv_p"].append(buf)
            outs["c_h_p"].append(hl)
            xs, buf, hl = _lru_mixer(xs, mod_s, nw[1:2], state_c_conv[j], state_c_h[j], *prm, **tiles_s)
            outs["c_conv_s"].append(buf)
            outs["c_h_s"].append(hl)
        xp, xs = ffn(xp, xs, 1)

    modf_p, modf_s = groups(mod_f)
    fnw = final_norm_w.reshape(1, d)
    y_p = _final_norm(xp.reshape(bp * seq, d), modf_p, fnw, tm=tm_p).reshape(bp, seq, d)
    y_s = _final_norm(xs.reshape(bs * ts, d), modf_s, fnw, tm=tm_s).reshape(bs, ts, d)
    st = {k: jnp.stack(v) for k, v in outs.items()}
    return (y_p, y_s, st["a_conv_p"], st["a_conv_s"], st["a_ssm_p"], st["a_ssm_s"], st["b_k_p"], st["b_k_s"],
            st["b_v_p"], st["b_v_s"], st["b_i_p"], st["b_i_s"], st["c_conv_p"], st["c_conv_s"], st["c_h_p"],
            st["c_h_s"])
```

```python
import functools
import math

import jax
import jax.numpy as jnp
import numpy as np
from jax import lax
from jax.experimental import pallas as pl
from jax.experimental.pallas import tpu as pltpu

F32 = jnp.float32
BF16 = jnp.bfloat16
I32 = jnp.int32

N_MIXERS = 3
CONV_W = 4
NORM_EPS = 1e-6
NEG_INF = -1e30
GDN_CHUNK = 64
TOPK_MAX = 256
Q_BLOCK = 128
ROPE_THETA = 10000.0
RG_C = 8.0
LRU_BLOCK = 256
SAMPLE_PAGES_PER_STEP = 8

LANES = 128
SUBLANES = 8
VMEM_LIMIT_BYTES = 56 * 2**20
VT_ROWS = LANES + 16
INT32_MIN = -2**31
INT32_MAX = 2**31 - 1
KEY_OF_NEG_INF = int(np.float32(NEG_INF).view(np.int32)) ^ 0x7FFFFFFF


def _cparams(*sem):
    return pltpu.CompilerParams(dimension_semantics=sem, vmem_limit_bytes=VMEM_LIMIT_BYTES)


def _tile(n, target, align=LANES):
    if n <= target:
        return n
    t = (target // align) * align
    while t >= align:
        if n % t == 0:
            return t
        t -= align
    raise ValueError(f"no {align}-aligned tile of {n} below {target}")


def _sigmoid(x):
    return jax.nn.sigmoid(x)


def _silu(x):
    return x * _sigmoid(x)


def _dot(a, b):
    return jnp.dot(a, b, preferred_element_type=F32)


def _dot_nt(a, b):
    return lax.dot_general(a, b, (((1,), (1,)), ((), ())), preferred_element_type=F32)


def _dot_tn(a, b):
    return lax.dot_general(a, b, (((0,), (0,)), ((), ())), preferred_element_type=F32)


def _dot_f32(a, b):
    return jnp.dot(a, b, preferred_element_type=F32, precision=lax.Precision.HIGHEST)


def _norm_mod(x, nw, sh, sc):
    ms = jnp.mean(x * x, axis=-1, keepdims=True)
    return x * lax.rsqrt(ms + NORM_EPS) * (nw * (1.0 + sc)) + sh


def _ada_kernel(c_ref, w_ref, b_ref, o_ref):
    a = _silu(c_ref[...]).astype(BF16)
    o_ref[0] = _dot(a, w_ref[0].astype(BF16)) + b_ref[0]


def _ada(c_all, w, b):
    n_l, d, n = w.shape
    mp = c_all.shape[0]
    tn = _tile(n, 1024)
    return pl.pallas_call(
        _ada_kernel,
        grid=(n_l, n // tn),
        in_specs=[
            pl.BlockSpec((mp, d), lambda l, j: (0, 0)),
            pl.BlockSpec((1, d, tn), lambda l, j: (l, 0, j)),
            pl.BlockSpec((1, 1, tn), lambda l, j: (l, 0, j)),
        ],
        out_specs=pl.BlockSpec((1, mp, tn), lambda l, j: (l, 0, j)),
        out_shape=jax.ShapeDtypeStruct((n_l, mp, n), F32),
        name="ada_mod",
        compiler_params=_cparams("parallel", "parallel"),
    )(c_all, w, b.reshape(n_l, 1, n))


def _mod_spec(mod3, k, d, tm, rows_per_group):
    return pl.BlockSpec((None, mod3.shape[1], d), lambda i, j: ((i * tm) // rows_per_group, 0, k))


def _wspec(w, block, imap):
    arr, idx = w
    return pl.BlockSpec((None,) * len(idx) + block, lambda i, j: idx + imap(i, j))


def _wshape(w):
    return w[0].shape[len(w[1]):]


def _normproj_kernel(x_ref, nw_ref, sh_ref, sc_ref, w_ref, o_ref, xn_ref):
    @pl.when(pl.program_id(1) == 0)
    def _():
        xn_ref[...] = _norm_mod(x_ref[...], nw_ref[...], sh_ref[...], sc_ref[...]).astype(BF16)

    o_ref[...] = _dot(xn_ref[...], w_ref[...])


def _normproj(x, mod3, k0, nw, w, *, tm, tn_target=1024):
    m, d = x.shape
    n = _wshape(w)[1]
    tn = _tile(n, tn_target)
    rpg = m // mod3.shape[0]
    ms = lambda k: _mod_spec(mod3, k, d, tm, rpg)
    return pl.pallas_call(
        _normproj_kernel,
        grid=(m // tm, n // tn),
        in_specs=[
            pl.BlockSpec((tm, d), lambda i, j: (i, 0)),
            pl.BlockSpec((1, d), lambda i, j: (0, 0)),
            ms(k0), ms(k0 + 1),
            _wspec(w, (d, tn), lambda i, j: (0, j)),
        ],
        out_specs=pl.BlockSpec((tm, tn), lambda i, j: (i, j)),
        out_shape=jax.ShapeDtypeStruct((m, n), F32),
        scratch_shapes=[pltpu.VMEM((tm, d), BF16)],
        name="normproj",
        compiler_params=_cparams("parallel", "arbitrary"),
    )(x, nw, mod3, mod3, w[0])


def _ffn_up_kernel(x_ref, nw_ref, sh_ref, sc_ref, xs_ref, shs_ref, scs_ref, wa_ref, wb_ref, h_ref, hs_ref,
                   xn_ref, xsn_ref):
    i, j = pl.program_id(0), pl.program_id(1)

    @pl.when(j == 0)
    def _():
        xn_ref[...] = _norm_mod(x_ref[...], nw_ref[...], sh_ref[...], sc_ref[...]).astype(BF16)

    @pl.when((i == 0) & (j == 0))
    def _():
        xsn_ref[...] = _norm_mod(xs_ref[...], nw_ref[...], shs_ref[...], scs_ref[...]).astype(BF16)

    wa, wb = wa_ref[...].astype(BF16), wb_ref[...].astype(BF16)
    swiglu = lambda xn: (_silu(_dot(xn, wa)) * _dot(xn, wb)).astype(BF16)
    h_ref[...] = swiglu(xn_ref[...])

    @pl.when(i == 0)
    def _():
        hs_ref[...] = swiglu(xsn_ref[...])

    @pl.when(i > 0)
    def _():
        hs_ref[...] = jnp.zeros_like(hs_ref)


def _ffn_up(x, mod3, xs, mods3, k0, nw, w_in, *, tm, tf_target=512):
    m, d = x.shape
    msr = xs.shape[0]
    f = _wshape(w_in)[1] // 2
    tf = _tile(f, tf_target)
    nf = f // tf
    rpg = m // mod3.shape[0]
    ms = lambda k: _mod_spec(mod3, k, d, tm, rpg)
    mss = lambda k: pl.BlockSpec((None, msr, d), lambda i, j: (0, 0, k))
    return pl.pallas_call(
        _ffn_up_kernel,
        grid=(m // tm, nf),
        in_specs=[
            pl.BlockSpec((tm, d), lambda i, j: (i, 0)),
            pl.BlockSpec((1, d), lambda i, j: (0, 0)),
            ms(k0), ms(k0 + 1),
            pl.BlockSpec((msr, d), lambda i, j: (0, 0)),
            mss(k0), mss(k0 + 1),
            _wspec(w_in, (d, tf), lambda i, j: (0, j)),
            _wspec(w_in, (d, tf), lambda i, j: (0, j + nf)),
        ],
        out_specs=[pl.BlockSpec((tm, tf), lambda i, j: (i, j)),
                   pl.BlockSpec((msr, tf), lambda i, j: (jnp.minimum(i, 1), j))],
        out_shape=[jax.ShapeDtypeStruct((m, f), BF16), jax.ShapeDtypeStruct((2 * msr, f), BF16)],
        scratch_shapes=[pltpu.VMEM((tm, d), BF16), pltpu.VMEM((msr, d), BF16)],
        name="ffn_up",
        compiler_params=_cparams("arbitrary", "arbitrary"),
    )(x, nw, mod3, mod3, xs, mods3, mods3, w_in[0], w_in[0])


def _outproj_kernel(a_ref, w_ref, x_ref, g_ref, o_ref):
    o_ref[...] = x_ref[...] + g_ref[...] * _dot(a_ref[...], w_ref[...])


def _outproj(a, w, x, mod3, kg, *, tm, tn_target=1024):
    m, kdim = a.shape
    d = _wshape(w)[1]
    tn = _tile(d, tn_target)
    rpg = m // mod3.shape[0]
    r = mod3.shape[1]
    nd = d // tn
    return pl.pallas_call(
        _outproj_kernel,
        grid=(m // tm, nd),
        in_specs=[
            pl.BlockSpec((tm, kdim), lambda i, j: (i, 0)),
            _wspec(w, (kdim, tn), lambda i, j: (0, j)),
            pl.BlockSpec((tm, tn), lambda i, j: (i, j)),
            pl.BlockSpec((None, r, tn), lambda i, j: ((i * tm) // rpg, 0, kg * nd + j)),
        ],
        out_specs=pl.BlockSpec((tm, tn), lambda i, j: (i, j)),
        out_shape=jax.ShapeDtypeStruct((m, d), F32),
        name="outproj",
        compiler_params=_cparams("parallel", "arbitrary"),
    )(a, w[0], x, mod3)


def _ffn_down_kernel(a_ref, w_ref, x_ref, g_ref, as_ref, xs_ref, gs_ref, o_ref, os_ref):
    w = w_ref[...]
    o_ref[...] = x_ref[...] + 0.5 * g_ref[...] * _dot(a_ref[...], w)

    @pl.when(pl.program_id(0) == 0)
    def _():
        os_ref[...] = xs_ref[...] + 0.5 * gs_ref[...] * _dot(as_ref[...], w)

    @pl.when(pl.program_id(0) > 0)
    def _():
        os_ref[...] = jnp.zeros_like(os_ref)


def _ffn_down(a, a_s, w, x, mod3, xs, mods3, kg, *, tm, tn_target=512):
    m, kdim = a.shape
    msr = xs.shape[0]
    d = _wshape(w)[1]
    tn = _tile(d, tn_target)
    rpg = m // mod3.shape[0]
    nd = d // tn
    o, o_s = pl.pallas_call(
        _ffn_down_kernel,
        grid=(m // tm, nd),
        in_specs=[
            pl.BlockSpec((tm, kdim), lambda i, j: (i, 0)),
            _wspec(w, (kdim, tn), lambda i, j: (0, j)),
            pl.BlockSpec((tm, tn), lambda i, j: (i, j)),
            pl.BlockSpec((None, mod3.shape[1], tn), lambda i, j: ((i * tm) // rpg, 0, kg * nd + j)),
            pl.BlockSpec((msr, kdim), lambda i, j: (0, 0)),
            pl.BlockSpec((msr, tn), lambda i, j: (0, j)),
            pl.BlockSpec((None, msr, tn), lambda i, j: (0, 0, kg * nd + j)),
        ],
        out_specs=[pl.BlockSpec((tm, tn), lambda i, j: (i, j)),
                   pl.BlockSpec((msr, tn), lambda i, j: (jnp.minimum(i, 1), j))],
        out_shape=[jax.ShapeDtypeStruct((m, d), F32), jax.ShapeDtypeStruct((2 * msr, d), F32)],
        name="ffn_down",
        compiler_params=_cparams("arbitrary", "arbitrary"),
    )(a, w[0], x, mod3, a_s, xs, mods3)
    return o, o_s[:msr]


def _final_kernel(x_ref, nw_ref, sh_ref, sc_ref, o_ref):
    o_ref[...] = _norm_mod(x_ref[...], nw_ref[...], sh_ref[...], sc_ref[...])


def _final_norm(x, mod3, nw, *, tm):
    m, d = x.shape
    rpg = m // mod3.shape[0]
    r = mod3.shape[1]
    ms = lambda k: pl.BlockSpec((None, r, d), lambda i: ((i * tm) // rpg, 0, k))
    return pl.pallas_call(
        _final_kernel,
        grid=(m // tm,),
        in_specs=[pl.BlockSpec((tm, d), lambda i: (i, 0)), pl.BlockSpec((1, d), lambda i: (0, 0)), ms(0), ms(1)],
        out_specs=pl.BlockSpec((tm, d), lambda i: (i, 0)),
        out_shape=jax.ShapeDtypeStruct((m, d), F32),
        name="final_norm",
        compiler_params=_cparams("parallel"),
    )(x, nw, mod3, mod3)


def _causal_conv(x, halo, w):
    def taps(rows, fix):
        acc = rows * w[CONV_W - 1:CONV_W]
        for j in range(1, CONV_W):
            acc = acc + fix(pltpu.roll(rows, j, 0), j) * w[CONV_W - 1 - j:CONV_W - j]
        return acc

    row = lax.broadcasted_iota(I32, halo.shape, 0)
    head = taps(x[:SUBLANES], lambda r, j: jnp.where(row < j, pltpu.roll(halo, j, 0), r))
    if x.shape[0] == SUBLANES:
        return head
    return jnp.concatenate([head, taps(x, lambda r, j: r)[SUBLANES:]], axis=0)


def _conv_specs(tt, tc, coff, boff):
    sub = tt // SUBLANES
    return [
        pl.BlockSpec((None, tt, tc), lambda b, t, c: (b, t, c + coff)),
        pl.BlockSpec((None, SUBLANES, tc), lambda b, t, c: (b, jnp.maximum(t * sub - 1, 0), c + coff)),
        pl.BlockSpec((None, SUBLANES, tc), lambda b, t, c: (b, 0, c + boff)),
    ]


def _pad_buf(buf):
    return jnp.pad(buf, ((0, 0), (SUBLANES - (CONV_W - 1), 0), (0, 0)))


def _gdn_prep_kernel(x_ref, halo_ref, buf_ref, w_ref, o_ref, *, norm):
    halo = jnp.where(pl.program_id(1) == 0, buf_ref[...], halo_ref[...])
    y = _silu(_causal_conv(x_ref[...], halo, w_ref[...]))
    if norm:
        for h in range(y.shape[1] // LANES):
            seg = y[:, h * LANES:(h + 1) * LANES]
            ss = jnp.sum(seg * seg, axis=-1, keepdims=True)
            o_ref[:, h * LANES:(h + 1) * LANES] = seg * lax.rsqrt(ss + NORM_EPS)
    else:
        o_ref[...] = y


def _gdn_prep(proj, buf8, conv_w, *, col0, ncols, norm, tt):
    bsz, t, _ = proj.shape
    tc = _tile(ncols, 1024)
    return pl.pallas_call(
        functools.partial(_gdn_prep_kernel, norm=norm),
        grid=(bsz, t // tt, ncols // tc),
        in_specs=_conv_specs(tt, tc, col0 // tc, col0 // tc)
        + [pl.BlockSpec((CONV_W, tc), lambda b, i, c: (0, c + col0 // tc))],
        out_specs=pl.BlockSpec((None, tt, tc), lambda b, i, c: (b, i, c)),
        out_shape=jax.ShapeDtypeStruct((bsz, t, ncols), F32),
        name="gdn_prep",
        compiler_params=_cparams("parallel", "parallel", "parallel"),
    )(proj, proj, buf8, conv_w)


def _gdn_gate_kernel(x_ref, alog_ref, dtb_ref, beta_ref, gc_ref, *, hv, chunk):
    x = x_ref[...]
    tt = x.shape[0]
    beta_ref[...] = _sigmoid(x)
    z = x + dtb_ref[...]
    g = -jnp.exp(alog_ref[...]) * (jnp.maximum(z, 0.0) + jnp.log1p(jnp.exp(-jnp.abs(z))))
    row = lax.broadcasted_iota(I32, (tt, tt), 0)
    col = lax.broadcasted_iota(I32, (tt, tt), 1)
    tri = jnp.where((row >= col) & (row // chunk == col // chunk), 1.0, 0.0)
    gc_ref[...] = _dot_f32(tri, g)


def _gdn_gates(proj, a_log, dt_bias, *, col0, hv, chunk, tt):
    bsz, t, _ = proj.shape
    pad = lambda v: jnp.pad(v.astype(F32), (hv, LANES - 2 * hv)).reshape(1, LANES)
    blk = pl.BlockSpec((None, tt, LANES), lambda b, i: (b, i, col0 // LANES))
    out = pl.BlockSpec((None, tt, LANES), lambda b, i: (b, i, 0))
    par = pl.BlockSpec((1, LANES), lambda b, i: (0, 0))
    return pl.pallas_call(
        functools.partial(_gdn_gate_kernel, hv=hv, chunk=chunk),
        grid=(bsz, t // tt),
        in_specs=[blk, par, par],
        out_specs=[out, out],
        out_shape=[jax.ShapeDtypeStruct((bsz, t, LANES), F32)] * 2,
        name="gdn_gates",
        compiler_params=_cparams("parallel", "parallel"),
    )(proj, pad(a_log), pad(dt_bias))


def _split_bf16(a):
    hi = a.astype(BF16)
    return hi, (a - hi.astype(F32)).astype(BF16)


def _dot_split(a, b):
    ah, al = _split_bf16(a)
    bh, bl = _split_bf16(b)
    return _dot(jnp.concatenate([ah, al, ah], axis=1), jnp.concatenate([bh, bh, bl], axis=0))


def _dot_bf16(a, b):
    return _dot(a.astype(BF16), b.astype(BF16))


def _tri_inv_all(lmats, c, n_real):
    row = lax.broadcasted_iota(I32, (c, c), 0)
    col = lax.broadcasted_iota(I32, (c, c), 1)
    eye = jnp.where(row == col, 1.0, 0.0)
    base = min(16, c)
    ps = [-jnp.where(row // base == col // base, m, 0.0) for m in lmats]
    rs = [eye + p for p in ps]
    n = 2
    while n < base:
        ps = [_dot_bf16(p, p) for p in ps]
        rs = [r + _dot_bf16(r, p) for r, p in zip(rs, ps)]
        n *= 2
    s = base
    while s < min(c, n_real):
        off = (row // (2 * s) == col // (2 * s)) & (row // s != col // s)
        ts = [_dot_bf16(jnp.where(off, m, 0.0), r) for m, r in zip(lmats, rs)]
        rs = [r - _dot_bf16(r, t) for r, t in zip(rs, ts)]
        s *= 2
    res = [eye - r - _dot_split(m, r) for m, r in zip(lmats, rs)]
    return [r + _dot_bf16(r, e) for r, e in zip(rs, res)]


def _gdn_core_kernel(q_ref, k_ref, v_ref, z_ref, gcc_ref, gcr_ref, bc_ref, s0_ref, nw_ref, o_ref, so_ref, s_ref,
                     *, rep, c, ncb, hpb, dk, n_real):
    ci = pl.program_id(2)

    @pl.when(ci == 0)
    def _():
        s_ref[...] = s0_ref[...]

    row = lax.broadcasted_iota(I32, (c, c), 0)
    col = lax.broadcasted_iota(I32, (c, c), 1)
    causal = row >= col
    nhd = hpb * rep
    kinst = [(n, hh) for n in range(ncb) for hh in range(hpb)]
    inst = [(n, hh, r) for n in range(ncb) for hh in range(hpb) for r in range(rep)]
    rows = lambda n: slice(n * c, (n + 1) * c)
    lanes = lambda hh, r: slice((hh * rep + r) * LANES, (hh * rep + r + 1) * LANES)
    kidx = lambda n, hh: n * hpb + hh
    qs = [q_ref[rows(n), hh * LANES:(hh + 1) * LANES] * dk ** -0.5 for n, hh in kinst]
    ks = [k_ref[rows(n), hh * LANES:(hh + 1) * LANES] for n, hh in kinst]
    k16 = [k.astype(BF16) for k in ks]
    grams = [_dot_nt(kb, kb) for kb in k16]
    qk0s = [_dot_nt(q.astype(BF16), kb) for q, kb in zip(qs, k16)]
    gccs = [gcc_ref[hh, rows(n), r:r + 1] for n, hh, r in inst]
    gcrs = [gcr_ref[hh, r:r + 1, rows(n)] for n, hh, r in inst]
    betas = [bc_ref[hh, rows(n), r:r + 1] for n, hh, r in inst]
    decays = [jnp.where(causal, jnp.exp(jnp.where(causal, gc - gr, 0.0)), 0.0) for gc, gr in zip(gccs, gcrs)]
    lowers = [jnp.where(row > col, grams[kidx(n, hh)] * b * d, 0.0)
              for (n, hh, _), b, d in zip(inst, betas, decays)]
    tinvs = _tri_inv_all(lowers, c, n_real)
    egcs = [jnp.exp(gc) for gc in gccs]
    sols = [_dot_split(ti, jnp.concatenate([v_ref[rows(n), lanes(hh, r)] * b, ks[kidx(n, hh)] * (b * e)], axis=1))
            for ti, b, e, (n, hh, r) in zip(tinvs, betas, egcs, inst)]
    qg16 = [(qs[kidx(n, hh)] * e).astype(BF16) for (n, hh, _), e in zip(inst, egcs)]
    qk16 = [(qk0s[kidx(n, hh)] * d).astype(BF16) for (n, hh, _), d in zip(inst, decays)]
    g_last = [gc[c - 1:c, :] for gc in gccs]
    kd16 = [(ks[kidx(n, hh)] * jnp.exp(gl - gc)).astype(BF16) for (n, hh, _), gl, gc in zip(inst, g_last, gccs)]
    ss = [s_ref[i] for i in range(nhd)]
    for n in range(ncb):
        ids = range(n * nhd, (n + 1) * nhd)
        s16 = [s.astype(BF16) for s in ss]
        u16 = [(sols[i][:, :LANES] - _dot(sols[i][:, LANES:].astype(BF16), sb)).astype(BF16)
               for i, sb in zip(ids, s16)]
        ss = [s * jnp.exp(g_last[i]) + _dot_tn(kd16[i], ub) for s, i, ub in zip(ss, ids, u16)]
        os_ = [_dot(qg16[i], sb) + _dot(qk16[i], ub) for i, sb, ub in zip(ids, s16, u16)]
        for o, i in zip(os_, ids):
            _, hh, r = inst[i]
            on = o * lax.rsqrt(jnp.mean(o * o, axis=-1, keepdims=True) + NORM_EPS) * nw_ref[...]
            o_ref[rows(n), lanes(hh, r)] = (on * _silu(z_ref[rows(n), lanes(hh, r)])).astype(BF16)
    for i in range(nhd):
        s_ref[i] = ss[i]

    @pl.when(ci == pl.num_programs(2) - 1)
    def _():
        so_ref[...] = s_ref[...]


def _gdn_core(qk, v, proj, zcol0, gc, beta, s0, norm_w, *, c, ncb, hpb, n_real):
    bsz, t, val = v.shape
    hv = s0.shape[1]
    dk, dv = s0.shape[2], s0.shape[3]
    hk = qk.shape[2] // (2 * dk)
    rep = hv // hk
    tb = ncb * c
    assert dk == LANES and dv == LANES and t % tb == 0 and hk % hpb == 0 and zcol0 % (hpb * rep * dv) == 0
    heads = lambda a, lo: a[:, :, lo:lo + hv].reshape(bsz, t, hk, rep).transpose(0, 2, 1, 3)
    gcc = heads(gc, hv)
    bcc = heads(beta, 0)
    gcr = gcc.transpose(0, 1, 3, 2)
    nhb = hk // hpb
    zb = zcol0 // (hpb * rep * dv)
    colspec = pl.BlockSpec((None, hpb, tb, rep), lambda b, h, i: (b, h, i, 0))
    o, s_out = pl.pallas_call(
        functools.partial(_gdn_core_kernel, rep=rep, c=c, ncb=ncb, hpb=hpb, dk=dk, n_real=n_real),
        grid=(bsz, nhb, t // tb),
        in_specs=[
            pl.BlockSpec((None, tb, hpb * dk), lambda b, h, i: (b, i, h)),
            pl.BlockSpec((None, tb, hpb * dk), lambda b, h, i: (b, i, nhb + h)),
            pl.BlockSpec((None, tb, hpb * rep * dv), lambda b, h, i: (b, i, h)),
            pl.BlockSpec((None, tb, hpb * rep * dv), lambda b, h, i: (b, i, zb + h)),
            colspec,
            pl.BlockSpec((None, hpb, rep, tb), lambda b, h, i: (b, h, 0, i)),
            colspec,
            pl.BlockSpec((None, hpb * rep, dk, dv), lambda b, h, i: (b, h, 0, 0)),
            pl.BlockSpec((1, dv), lambda b, h, i: (0, 0)),
        ],
        out_specs=[
            pl.BlockSpec((None, tb, hpb * rep * dv), lambda b, h, i: (b, i, h)),
            pl.BlockSpec((None, hpb * rep, dk, dv), lambda b, h, i: (b, h, 0, 0)),
        ],
        out_shape=[jax.ShapeDtypeStruct((bsz, t, val), BF16), jax.ShapeDtypeStruct(s0.shape, F32)],
        scratch_shapes=[pltpu.VMEM((hpb * rep, dk, dv), F32)],
        name="gdn_core",
        compiler_params=_cparams("parallel", "parallel", "arbitrary"),
    )(qk, qk, v, proj, gcc, gcr, bcc, s0, norm_w.reshape(1, dv))
    return o, s_out


def _gdn_mixer(x, mod3, nw, buf, s0, w_in, conv_w, a_log, dt_bias, norm_w, w_out, *, tm, tm_in, tt, chunk):
    bsz, t, d = x.shape
    hv, dk, dv = s0.shape[1], s0.shape[2], s0.shape[3]
    val = hv * dv
    conv_dim = conv_w.shape[1]
    key = (conv_dim - val) // 2
    assert (conv_dim + val) % LANES == 0 and 2 * hv <= LANES and t >= CONV_W - 1
    x2 = x.reshape(bsz * t, d)
    proj = _normproj(x2, mod3, 3, nw, w_in, tm=tm_in, tn_target=512).reshape(bsz, t, -1)
    buf8 = _pad_buf(buf)
    qk = _gdn_prep(proj, buf8, conv_w, col0=0, ncols=2 * key, norm=True, tt=tt)
    v = _gdn_prep(proj, buf8, conv_w, col0=2 * key, ncols=val, norm=False, tt=tt)
    tp = -(-t // chunk) * chunk
    beta, gc = _gdn_gates(proj, a_log, dt_bias, col0=conv_dim + val, hv=hv, chunk=min(chunk, tt), tt=tt)
    if tp != t:
        padt = lambda a: jnp.pad(a, ((0, 0), (0, tp - t), (0, 0)))
        gc = jnp.concatenate([gc, jnp.broadcast_to(gc[:, -1:], (bsz, tp - t, LANES))], axis=1)
        qk, v, beta, projz = padt(qk), padt(v), padt(beta), padt(proj)
    else:
        projz = proj
    o, s_new = _gdn_core(qk, v, projz, conv_dim, gc, beta, s0, norm_w, c=chunk, ncb=math.gcd(tp // chunk, 2),
                         hpb=min(4 if tp > chunk else 16, key // dk), n_real=min(t, chunk))
    o2 = o[:, :t].reshape(bsz * t, val)
    xo = _outproj(o2, w_out, x2, mod3, 5, tm=tm).reshape(bsz, t, d)
    new_buf = proj[:, t - (CONV_W - 1):, :conv_dim]
    return xo, new_buf, s_new


def _rope_tables(pos, half):
    inv_freq = ROPE_THETA ** (-jnp.arange(half, dtype=F32) / half)
    ang = pos.astype(F32)[:, None] * inv_freq[None, :]
    cos, sin = jnp.cos(ang), jnp.sin(ang)
    return jnp.concatenate([cos, cos], axis=-1), jnp.concatenate([-sin, sin], axis=-1)


def _dsa_prep_kernel(x_ref, cos_ref, sin_ref, q_ref, k_ref, k16_ref, v16_ref, qi_ref, ki_ref, ki16_ref, *tr_refs,
                     nh, nkv, nih):
    cos, sin = cos_ref[...], sin_ref[...]

    def rope(col):
        seg = x_ref[:, col * LANES:(col + 1) * LANES]
        return seg * cos + pltpu.roll(seg, LANES // 2, 1) * sin

    for h in range(nh):
        q_ref[:, h * LANES:(h + 1) * LANES] = (rope(h) * LANES ** -0.5).astype(BF16)
    for h in range(nkv):
        kr = rope(nh + h)
        k_ref[:, h * LANES:(h + 1) * LANES] = kr
        k16_ref[:, h * LANES:(h + 1) * LANES] = kr.astype(BF16)
    v0 = (nh + nkv) * LANES
    v16_ref[...] = x_ref[:, v0:v0 + nkv * LANES].astype(BF16)
    c0 = nh + 2 * nkv
    for h in range(nih):
        qi_ref[:, h * LANES:(h + 1) * LANES] = rope(c0 + h).astype(BF16)
    kir = rope(c0 + nih)
    ki_ref[...] = kir
    ki16_ref[...] = kir.astype(BF16)
    if tr_refs:
        wt_ref, vt_ref = tr_refs
        w0 = (c0 + nih + 1) * LANES
        wt_ref[...] = x_ref[:, w0:w0 + LANES].T[:wt_ref.shape[0], :]
        for h in range(nkv):
            vt_ref[h * VT_ROWS:h * VT_ROWS + LANES, :] = x_ref[:, v0 + h * LANES:v0 + (h + 1) * LANES].T.astype(BF16)
            vt_ref[h * VT_ROWS + LANES:(h + 1) * VT_ROWS, :] = jnp.ones((VT_ROWS - LANES, vt_ref.shape[1]), BF16)


def _dsa_prep(proj, pos, *, nh, nkv, nih, tt, with_wt):
    bsz, t, npj = proj.shape
    cos, sin = _rope_tables(pos, LANES // 2)
    row = lambda n, dt: jax.ShapeDtypeStruct((bsz, t, n * LANES), dt)
    ospec = lambda n: pl.BlockSpec((None, tt, n * LANES), lambda b, i: (b, i, 0))
    tab = pl.BlockSpec((tt, LANES), lambda b, i: (i, 0))
    nwt = -(-nih // SUBLANES) * SUBLANES
    return pl.pallas_call(
        functools.partial(_dsa_prep_kernel, nh=nh, nkv=nkv, nih=nih),
        grid=(bsz, t // tt),
        in_specs=[pl.BlockSpec((None, tt, npj), lambda b, i: (b, i, 0)), tab, tab],
        out_specs=[ospec(nh), ospec(nkv), ospec(nkv), ospec(nkv), ospec(nih), ospec(1), ospec(1)]
        + ([pl.BlockSpec((None, nwt, tt), lambda b, i: (b, 0, i)),
            pl.BlockSpec((None, None, nkv * VT_ROWS, tt), lambda b, i: (b, i, 0, 0))] if with_wt else []),
        out_shape=[row(nh, BF16), row(nkv, F32), row(nkv, BF16), row(nkv, BF16), row(nih, BF16), row(1, F32),
                   row(1, BF16)]
        + ([jax.ShapeDtypeStruct((bsz, nwt, t), F32),
            jax.ShapeDtypeStruct((bsz, t // tt, nkv * VT_ROWS, tt), BF16)] if with_wt else []),
        name="dsa_prep",
        compiler_params=_cparams("parallel", "parallel"),
    )(proj, cos, sin)


def _sort_key(s):
    bits = pltpu.bitcast(jnp.where(s == 0.0, 0.0, s), I32)
    return jnp.where(bits < 0, bits ^ 0x7FFFFFFF, bits)


def _topk_cut(count_where, shape, topk, idx_bits):
    def body(i, carry):
        t, n_t = carry
        cand = t + lax.shift_left(jnp.int32(1), 31 - i)
        n = count_where(lambda key, idx: key >= cand)
        ok = n >= topk
        return jnp.where(ok, cand, t), jnp.where(ok, n, n_t)

    thr, n_ge = lax.fori_loop(0, 32, body, (jnp.full(shape, INT32_MIN, I32), jnp.full(shape, INT32_MAX, I32)))
    tie = (n_ge > topk) & (thr > KEY_OF_NEG_INF)

    def cut():
        need = topk - count_where(lambda key, idx: key > thr)

        def jbody(i, j):
            cand = j + lax.shift_left(jnp.int32(1), idx_bits - 1 - i)
            below = count_where(lambda key, idx: (key == thr) & (idx < cand))
            return jnp.where(below < need, cand, j)

        return jnp.where(tie, lax.fori_loop(0, idx_bits, jbody, jnp.zeros(shape, I32)), INT32_MAX)

    jcut = lax.cond(jnp.any(tie), cut, lambda: jnp.full(shape, INT32_MAX, I32))
    return thr, jcut


def _topk_chosen(key, idx, thr, jcut):
    return (key > thr) | ((key == thr) & (idx <= jcut))


def _dsa_prompt_kernel(qi_ref, wt_ref, q_ref, ki_ref, k_ref, vt_ref, o_ref, key_ref, bias_ref, acc_ref,
                       *, nh, nkv, nih, topk, idx_scale):
    qb = pl.program_id(1)
    blk = Q_BLOCK
    kb = vt_ref.shape[2]
    ktiles = kb // blk
    nkb = (qb + ktiles) // ktiles
    rowk = lax.broadcasted_iota(I32, (kb, blk), 0)
    colq = lax.broadcasted_iota(I32, (kb, blk), 1)
    wt = wt_ref[...]
    hq = max(1, nih // 4)
    qis = [jnp.concatenate([qi_ref[:, h * LANES:(h + 1) * LANES] for h in range(h0, min(h0 + hq, nih))], axis=0)
           for h0 in range(0, nih, hq)]

    def rows_of(j):
        return pl.ds(pl.multiple_of(j * kb, kb), kb)

    def visible(j):
        return j * kb + rowk <= qb * blk + colq

    def score_body(j, carry):
        keys = ki_ref[rows_of(j), :]
        lgs = [_dot_nt(keys, qi) for qi in qis]
        acc = jnp.zeros((kb, blk), F32)
        for i, lg in enumerate(lgs):
            for hh in range(lg.shape[1] // blk):
                h = i * hq + hh
                acc = acc + jnp.maximum(lg[:, hh * blk:(hh + 1) * blk], 0.0) * wt[h:h + 1, :]
        key_ref[rows_of(j), :] = _sort_key(jnp.where(visible(j), acc * idx_scale, NEG_INF))
        return carry

    lax.fori_loop(0, nkb, score_body, 0)

    def count_where(pred):
        def body(j, c):
            hit = jnp.where(pred(key_ref[rows_of(j), :], j * kb + rowk), 1, 0)
            for i in range(ktiles):
                c = c + hit[i * blk:(i + 1) * blk]
            return c
        cnt = lax.fori_loop(0, nkb, body, jnp.zeros((blk, blk), I32))
        return jnp.sum(cnt, axis=0, keepdims=True)

    thr, jcut = _topk_cut(count_where, (1, blk), topk, (key_ref.shape[0] - 1).bit_length())

    def bias_body(j, carry):
        sel = _topk_chosen(key_ref[rows_of(j), :], j * kb + rowk, thr, jcut) & visible(j)
        bias_ref[rows_of(j), :] = jnp.where(sel, 0.0, NEG_INF)
        return carry

    lax.fori_loop(0, nkb, bias_body, 0)

    rep = nh // nkv
    qgs =[jnp.concatenate([q_ref[:, (g * rep + r) * LANES:(g * rep + r + 1) * LANES] for r in range(rep)], axis=0)
           for g in range(nkv)]
    acc_ref[...] = jnp.zeros_like(acc_ref)

    def att_body(j, carry):
        ms, ls = carry
        rows = rows_of(j)
        bias = jnp.concatenate([bias_ref[rows, :]] * rep, axis=1)
        ss = [_dot_nt(k_ref[rows, g * LANES:(g + 1) * LANES], qgs[g]) + bias for g in range(nkv)]
        m_new = [jnp.maximum(m, jnp.max(s, axis=0, keepdims=True)) for m, s in zip(ms, ss)]
        ps = [jnp.exp((s - m).astype(BF16)) for s, m in zip(ss, m_new)]
        pvs = [_dot(vt_ref[j, g * VT_ROWS:(g + 1) * VT_ROWS, :], p) for g, p in enumerate(ps)]
        alphas = [jnp.exp(m - mn) for m, mn in zip(ms, m_new)]
        for g in range(nkv):
            acc_ref[g] = alphas[g] * acc_ref[g] + pvs[g][:LANES]
        l_new = [a * l + pv[LANES:LANES + 1] for a, l, pv in zip(alphas, ls, pvs)]
        return tuple(m_new), tuple(l_new)

    row0 = lambda v: tuple(jnp.full((1, rep * blk), v, F32) for _ in range(nkv))
    _, ls = lax.fori_loop(0, nkb, att_body, (row0(NEG_INF), row0(0.0)))
    for g in range(nkv):
        og = acc_ref[g] / ls[g]
        for r in range(rep):
            h = g * rep + r
            o_ref[:, h * LANES:(h + 1) * LANES] = og[:, r * blk:(r + 1) * blk].T.astype(BF16)


def _dsa_prompt_attend(q16, qi16, wt, ki16, k16, vt16, *, nh, nkv, nih, topk):
    bsz, t, _ = q16.shape
    rep = nh // nkv
    full = lambda n: pl.BlockSpec((None, t, n * LANES), lambda b, i: (b, 0, 0))
    blk = lambda n: pl.BlockSpec((None, Q_BLOCK, n * LANES), lambda b, i: (b, i, 0))
    return pl.pallas_call(
        functools.partial(_dsa_prompt_kernel, nh=nh, nkv=nkv, nih=nih, topk=topk,
                          idx_scale=(LANES * nih) ** -0.5),
        grid=(bsz, t // Q_BLOCK),
        in_specs=[blk(nih), pl.BlockSpec((None, wt.shape[1], Q_BLOCK), lambda b, i: (b, 0, i)), blk(nh),
                  full(1), full(nkv), pl.BlockSpec((None,) + vt16.shape[1:], lambda b, i: (b, 0, 0, 0))],
        out_specs=blk(nh),
        out_shape=jax.ShapeDtypeStruct((bsz, t, nh * LANES), BF16),
        scratch_shapes=[pltpu.VMEM((t, Q_BLOCK), I32), pltpu.VMEM((t, Q_BLOCK), F32),
                        pltpu.VMEM((nkv, LANES, rep * Q_BLOCK), F32)],
        name="dsa_prompt_attend",
        compiler_params=_cparams("parallel", "arbitrary"),
    )(qi16, wt, q16, ki16, k16, vt16)


def _dsa_sample_score_kernel(pt_ref, qi_ref, wc_ref, *refs, n_steps, pps, nih, tq, past, idx_scale):
    page_refs, new_ref, o_ref = refs[:pps], refs[pps], refs[pps + 1]
    p = pl.program_id(1)
    keys = jnp.concatenate([r[...] for r in page_refs], axis=0).astype(BF16)
    keys = jnp.where(p == n_steps - 1, new_ref[...], keys)
    qi = jnp.concatenate([qi_ref[:, h * LANES:(h + 1) * LANES] for h in range(nih)], axis=0)
    w = jnp.maximum(_dot_nt(qi, keys), 0.0) * wc_ref[...]
    acc = w[0:tq]
    for h in range(1, nih):
        acc = acc + w[h * tq:(h + 1) * tq]
    s = p * keys.shape[0] + lax.broadcasted_iota(I32, acc.shape, 1)
    qpos = past + lax.broadcasted_iota(I32, acc.shape, 0)
    o_ref[...] = jnp.where(s <= qpos, acc * idx_scale, NEG_INF)


def _dsa_sample_select_kernel(s_ref, o_ref, *, topk):
    key = _sort_key(s_ref[...])
    idx = lax.broadcasted_iota(I32, key.shape, 1)
    count_where = lambda pred: jnp.sum(jnp.where(pred(key, idx), 1, 0), axis=-1, keepdims=True)
    thr, jcut = _topk_cut(count_where, (key.shape[0], 1), topk, (key.shape[1] - 1).bit_length())
    o_ref[...] = jnp.where(_topk_chosen(key, idx, thr, jcut) & (s_ref[...] > 0.5 * NEG_INF), 1.0, 0.0)


def _dsa_sample_attn_kernel(pt_ref, q_ref, sel_ref, *refs, n_steps, pps, nh, nkv, tq):
    kp_refs, vp_refs = refs[:pps], refs[pps:2 * pps]
    kn_ref, vn_ref, o_ref, m_ref, l_ref, acc_ref = refs[2 * pps:]
    p = pl.program_id(1)
    rep = nh // nkv
    page = sel_ref.shape[1] // pps

    @pl.when(p == 0)
    def _():
        m_ref[...] = jnp.full_like(m_ref, NEG_INF)
        l_ref[...] = jnp.zeros_like(l_ref)
        acc_ref[...] = jnp.zeros_like(acc_ref)

    last = p == n_steps - 1

    def head_rows(page_refs, new_ref, g):
        cached = jnp.concatenate([r[pl.ds(g, page, stride=nkv), :] for r in page_refs], axis=0).astype(BF16)
        return jnp.where(last, new_ref[g], cached)

    rows = rep * tq
    q = jnp.concatenate([q_ref[:, h * LANES:(h + 1) * LANES] for h in range(nh)], axis=0)
    unsel = (sel_ref[...] - 1.0) * -NEG_INF
    s = jnp.concatenate([_dot_nt(q[g * rows:(g + 1) * rows], head_rows(kp_refs, kn_ref, g)) for g in range(nkv)],
                        axis=0) + jnp.concatenate([unsel] * nh, axis=0)
    m = m_ref[...]
    m_new = jnp.maximum(m, jnp.max(s, axis=-1, keepdims=True))
    alpha = jnp.exp(m - m_new)
    pr = jnp.where(s > 0.5 * NEG_INF, jnp.exp(s - m_new), 0.0)
    l_ref[...] = alpha * l_ref[...] + jnp.sum(pr, axis=-1, keepdims=True)
    pv = jnp.concatenate([_dot(pr[g * rows:(g + 1) * rows].astype(BF16), head_rows(vp_refs, vn_ref, g))
                          for g in range(nkv)], axis=0)
    acc_ref[...] = alpha * acc_ref[...] + pv
    m_ref[...] = m_new

    @pl.when(last)
    def _():
        og = acc_ref[...] / l_ref[...]
        for h in range(nh):
            o_ref[:, h * LANES:(h + 1) * LANES] = og[h * tq:(h + 1) * tq].astype(BF16)


def _dsa_sample_attend(q16, qi16, wi, ki16, k16, v16, cache_k, cache_v, cache_i, layer, page_table,
                       *, nh, nkv, nih, topk):
    bsz, tq, _ = q16.shape
    n_pages = page_table.shape[1]
    n_layers, n_pool, page = cache_i.shape[:3]
    pps = math.gcd(n_pages, SAMPLE_PAGES_PER_STEP)
    assert page == LANES and tq == SUBLANES
    past = n_pages * page
    n_steps = n_pages // pps + 1
    ltot = n_steps * pps * page
    wcol = wi.transpose(0, 2, 1).reshape(bsz, nih * tq, 1)
    pidx = lambda i: (lambda b, p, pt: (layer * n_pool + pt[b, jnp.minimum(p * pps + i, n_pages - 1)], 0, 0))
    qspec = lambda n: pl.BlockSpec((None, tq, n * LANES), lambda b, p, pt: (b, 0, 0))
    idx_scale = (LANES * nih) ** -0.5
    cache_i = cache_i.reshape(n_layers * n_pool, page, LANES)
    ki_new = jnp.pad(ki16, ((0, 0), (0, pps * page - tq), (0, 0)))
    scores = pl.pallas_call(
        functools.partial(_dsa_sample_score_kernel, n_steps=n_steps, pps=pps, nih=nih, tq=tq, past=past,
                          idx_scale=idx_scale),
        grid_spec=pltpu.PrefetchScalarGridSpec(
            num_scalar_prefetch=1, grid=(bsz, n_steps),
            in_specs=[qspec(nih), pl.BlockSpec((None, nih * tq, 1), lambda b, p, pt: (b, 0, 0))]
            + [pl.BlockSpec((None, page, LANES), pidx(i)) for i in range(pps)]
            + [pl.BlockSpec((None, pps * page, LANES), lambda b, p, pt: (b, 0, 0))],
            out_specs=pl.BlockSpec((None, tq, pps * page), lambda b, p, pt: (b, 0, p))),
        out_shape=jax.ShapeDtypeStruct((bsz, tq, ltot), F32),
        name="dsa_sample_scores",
        compiler_params=_cparams("parallel", "arbitrary"),
    )(page_table, qi16, wcol, *([cache_i] * pps), ki_new)
    sel = pl.pallas_call(
        functools.partial(_dsa_sample_select_kernel, topk=topk),
        grid=(1,),
        in_specs=[pl.BlockSpec((bsz * tq, ltot), lambda i: (0, 0))],
        out_specs=pl.BlockSpec((bsz * tq, ltot), lambda i: (0, 0)),
        out_shape=jax.ShapeDtypeStruct((bsz * tq, ltot), F32),
        name="dsa_sample_select",
        compiler_params=_cparams("arbitrary"),
    )(scores.reshape(bsz * tq, ltot)).reshape(bsz, tq, ltot)
    prow = page * nkv
    ck = cache_k.reshape(n_layers * n_pool, prow, LANES)
    cv = cache_v.reshape(n_layers * n_pool, prow, LANES)
    new_rows = lambda a: jnp.pad(a.reshape(bsz, tq, nkv, LANES).transpose(0, 2, 1, 3),
                                 ((0, 0), (0, 0), (0, pps * page - tq), (0, 0)))
    kvspecs = [pl.BlockSpec((None, prow, LANES), pidx(i)) for i in range(pps)]
    newspec = pl.BlockSpec((None, nkv, pps * page, LANES), lambda b, p, pt: (b, 0, 0, 0))
    return pl.pallas_call(
        functools.partial(_dsa_sample_attn_kernel, n_steps=n_steps, pps=pps, nh=nh, nkv=nkv, tq=tq),
        grid_spec=pltpu.PrefetchScalarGridSpec(
            num_scalar_prefetch=1, grid=(bsz, n_steps),
            in_specs=[qspec(nh), pl.BlockSpec((None, tq, pps * page), lambda b, p, pt: (b, 0, p))]
            + kvspecs + kvspecs + [newspec, newspec],
            out_specs=qspec(nh),
            scratch_shapes=[pltpu.VMEM((nh * tq, 1), F32), pltpu.VMEM((nh * tq, 1), F32),
                            pltpu.VMEM((nh * tq, LANES), F32)]),
        out_shape=jax.ShapeDtypeStruct((bsz, tq, nh * LANES), BF16),
        name="dsa_sample_attend",
        compiler_params=_cparams("parallel", "arbitrary"),
    )(page_table, q16, sel, *([ck] * pps), *([cv] * pps), new_rows(k16), new_rows(v16))


def _dsa_mixer(x, mod3, nw, w_in, w_out, n_proj, *, nkv, tm, tm_in, tt, cache=None):
    bsz, t, d = x.shape
    nh = _wshape(w_out)[0] // LANES
    nih = (n_proj - (nh + 2 * nkv + 1) * LANES) // (LANES + 1)
    assert (nh + 2 * nkv + nih + 1) * LANES + nih == n_proj and nih <= LANES
    x2 = x.reshape(bsz * t, d)
    proj = _normproj(x2, mod3, 3, nw, w_in, tm=tm_in).reshape(bsz, t, -1)
    v = proj[:, :, (nh + nkv) * LANES:(nh + 2 * nkv) * LANES]
    if cache is None:
        assert t % Q_BLOCK == 0
        q16, k, k16, _, qi16, ki, ki16, wt, vt16 = _dsa_prep(proj, jnp.arange(t), nh=nh, nkv=nkv, nih=nih,
                                                             tt=_tile(t, 4 * Q_BLOCK, Q_BLOCK), with_wt=True)
        o = _dsa_prompt_attend(q16, qi16, wt, ki16, k16, vt16, nh=nh, nkv=nkv, nih=nih, topk=min(TOPK_MAX, t // 4))
    else:
        cache_k, cache_v, cache_i, layer, page_table = cache
        past = page_table.shape[1] * cache_i.shape[2]
        q16, k, k16, v16, qi16, ki, ki16 = _dsa_prep(proj, past + jnp.arange(t), nh=nh, nkv=nkv, nih=nih, tt=tt,
                                                    with_wt=False)
        w0 = (nh + 2 * nkv + nih + 1) * LANES
        o = _dsa_sample_attend(q16, qi16, proj[:, :, w0:w0 + nih], ki16, k16, v16, cache_k, cache_v, cache_i,
                               layer, page_table, nh=nh, nkv=nkv, nih=nih, topk=min(TOPK_MAX, (past + t) // 4))
    xo = _outproj(o.reshape(bsz * t, nh * LANES), w_out, x2, mod3, 5, tm=tm).reshape(bsz, t, d)
    return xo, k.reshape(bsz, t, nkv, LANES), v.reshape(bsz, t, nkv, LANES), ki


def _gelu_tanh(x):
    return 0.5 * x * (1.0 + jnp.tanh(math.sqrt(2.0 / math.pi) * (x + 0.044715 * (x * x * x))))


def _lru_kernel(gate_ref, xb_ref, halo_ref, buf_ref, cw_ref, cb_ref, wa_ref, ba_ref, wx_ref, bx_ref, lam_ref,
                h0_ref, y_ref, hl_ref, h_ref, *, nblk):
    ti = pl.program_id(1)

    @pl.when(ti == 0)
    def _():
        h_ref[...] = h0_ref[...]

    halo = jnp.where(ti == 0, buf_ref[...], halo_ref[...])
    xc = _causal_conv(xb_ref[...], halo, cw_ref[...]) + cb_ref[...]
    tt = xc.shape[0]
    xc16 = xc.astype(BF16)
    rs, xs = [], []
    for n in range(nblk):
        blk = xc16[:, n * LRU_BLOCK:(n + 1) * LRU_BLOCK]
        rs.append(_dot(blk, wa_ref[n]))
        xs.append(_dot(blk, wx_ref[n]))
    r = _sigmoid(jnp.concatenate(rs, axis=1) + ba_ref[...])
    ig = _sigmoid(jnp.concatenate(xs, axis=1) + bx_ref[...])
    lam = lam_ref[...]
    softplus_neg = jnp.maximum(-lam, 0.0) + jnp.log1p(jnp.exp(-jnp.abs(lam)))
    log_a = -RG_C * r * softplus_neg
    a = jnp.exp(log_a)
    b = jnp.sqrt(-jnp.tanh(log_a) * (a * a + 1.0)) * (ig * xc)
    sub = lax.broadcasted_iota(I32, a.shape, 0) % SUBLANES
    d = 1
    while d < SUBLANES:
        keep = sub >= d
        a_sh = jnp.where(keep, pltpu.roll(a, d, 0), 1.0)
        b_sh = jnp.where(keep, pltpu.roll(b, d, 0), 0.0)
        b = a * b_sh + b
        a = a * a_sh
        d *= 2
    carry, groups = h_ref[...], []
    for g in range(tt // SUBLANES):
        rows = slice(g * SUBLANES, (g + 1) * SUBLANES)
        groups.append(b[rows] + a[rows] * carry)
        carry = groups[-1][SUBLANES - 1:SUBLANES]
    hs = jnp.concatenate(groups, axis=0)
    h_ref[...] = hs[tt - 1:tt, :]
    y_ref[...] = (hs * _gelu_tanh(gate_ref[...])).astype(BF16)

    @pl.when(ti == pl.num_programs(1) - 1)
    def _():
        hl_ref[...] = hs[tt - 1:tt, :]


def _lru_core(proj, buf8, h0, conv_w, conv_b, w_ga, b_ga, w_gx, b_gx, lam, *, tt):
    bsz, t, w2 = proj.shape
    w = w2 // 2
    nblk = w // LRU_BLOCK
    sub = tt // SUBLANES
    vec = lambda: pl.BlockSpec((1, w), lambda b, i: (0, 0))
    wsp = lambda: pl.BlockSpec((nblk, LRU_BLOCK, LRU_BLOCK), lambda b, i: (0, 0, 0))
    r1 = lambda v: v.reshape(1, w).astype(F32)
    y, hl = pl.pallas_call(
        functools.partial(_lru_kernel, nblk=nblk),
        grid=(bsz, t // tt),
        in_specs=[
            pl.BlockSpec((None, tt, w), lambda b, i: (b, i, 0)),
            pl.BlockSpec((None, tt, w), lambda b, i: (b, i, 1)),
            pl.BlockSpec((None, SUBLANES, w), lambda b, i: (b, jnp.maximum(i * sub - 1, 0), 1)),
            pl.BlockSpec((None, SUBLANES, w), lambda b, i: (b, 0, 0)),
            pl.BlockSpec((CONV_W, w), lambda b, i: (0, 0)),
            vec(), wsp(), vec(), wsp(), vec(), vec(),
            pl.BlockSpec((None, 1, w), lambda b, i: (b, 0, 0)),
        ],
        out_specs=[pl.BlockSpec((None, tt, w), lambda b, i: (b, i, 0)),
                   pl.BlockSpec((None, 1, w), lambda b, i: (b, 0, 0))],
        out_shape=[jax.ShapeDtypeStruct((bsz, t, w), BF16), jax.ShapeDtypeStruct((bsz, 1, w), F32)],
        scratch_shapes=[pltpu.VMEM((1, w), F32)],
        name="lru_core",
        compiler_params=_cparams("parallel", "arbitrary"),
    )(proj, proj, proj, buf8, conv_w, r1(conv_b), w_ga, r1(b_ga), w_gx, r1(b_gx), r1(lam), h0.reshape(bsz, 1, w))
    return y, hl.reshape(bsz, w)


def _lru_mixer(x, mod3, nw, buf, h0, w_in, conv_w, conv_b, w_ga, b_ga, w_gx, b_gx, lam, w_out, *, tm, tm_in, tt):
    bsz, t, d = x.shape
    assert t >= CONV_W - 1
    x2 = x.reshape(bsz * t, d)
    proj = _normproj(x2, mod3, 3, nw, w_in, tm=tm_in).reshape(bsz, t, -1)
    w = proj.shape[2] // 2
    y, hl = _lru_core(proj, _pad_buf(buf), h0, conv_w, conv_b, w_ga, b_ga, w_gx, b_gx, lam, tt=tt)
    xo = _outproj(y.reshape(bsz * t, w), w_out, x2, mod3, 5, tm=tm).reshape(bsz, t, d)
    return xo, proj[:, t - (CONV_W - 1):, w:], hl


def _bf16_padded(w, mult):
    n = w.shape[-1]
    return jnp.pad(w, ((0, 0), (0, 0), (0, -n % mult))).astype(BF16)


def kernel(x_prompt, x_sample, state_a_conv, state_a_ssm, cache_b_k, cache_b_v, cache_b_idx, state_c_conv, state_c_h, page_table, c_prompt, c_sample, w_ada, b_ada, norm_w, ffn_w_in, ffn_w_out, gdn_w_in, gdn_conv_w, gdn_a_log, gdn_dt_bias, gdn_norm_w, gdn_w_out, dsa_w_in, dsa_w_out, lru_w_in, lru_conv_w, lru_conv_b, lru_w_gate_a, lru_b_gate_a, lru_w_gate_x, lru_b_gate_x, lru_lambda, lru_w_out, w_ada_final, b_ada_final, final_norm_w):
    bp, seq, d = x_prompt.shape
    bs, ts, _ = x_sample.shape
    depth = w_ada.shape[0]
    nkv = cache_b_k.shape[3]
    tm_p = _tile(seq, 512, SUBLANES)
    tm_in_p = _tile(seq, 1024, SUBLANES)
    tt_p = _tile(seq, 256, SUBLANES)
    tm_s = bs * ts
    tiles_p = dict(tm=tm_p, tm_in=tm_in_p, tt=tt_p)
    tiles_s = dict(tm=tm_s, tm_in=tm_s, tt=ts)

    c_all = jnp.concatenate([c_prompt, c_sample], axis=0)
    c_all = jnp.pad(c_all, ((0, -c_all.shape[0] % SUBLANES), (0, 0)))
    mod = _ada(c_all, w_ada, b_ada)
    mod_f = _ada(c_all, w_ada_final[None], b_ada_final[None])[0]

    def groups(m):
        return m[:bp, None, :], jnp.repeat(m[bp:bp + bs], ts, axis=0)[None]

    ffn_w_out16 = ffn_w_out.astype(BF16)
    gdn_w_in16, gdn_w_out16 = _bf16_padded(gdn_w_in, 512), gdn_w_out.astype(BF16)
    dsa_w_in16, dsa_w_out16 = _bf16_padded(dsa_w_in, 512), dsa_w_out.astype(BF16)
    lru_w_in16, lru_w_out16 = lru_w_in.astype(BF16), lru_w_out.astype(BF16)

    xp, xs = x_prompt, x_sample
    outs = {k: [] for k in ("a_conv_p", "a_conv_s", "a_ssm_p", "a_ssm_s", "b_k_p", "b_k_s", "b_v_p", "b_v_s",
                            "b_i_p", "b_i_s", "c_conv_p", "c_conv_s", "c_h_p", "c_h_s")}
    for layer in range(depth):
        kind, j = layer % N_MIXERS, layer // N_MIXERS
        mod_p, mod_s = groups(mod[layer])
        nw = norm_w[layer]

        def ffn(xp, xs, which):
            xp2, xs2 = xp.reshape(bp * seq, d), xs.reshape(tm_s, d)
            w_out = (ffn_w_out16, (layer, which))
            hp, hs = _ffn_up(xp2, mod_p, xs2, mod_s, 6 * which, nw[2 * which:2 * which + 1],
                             (ffn_w_in, (layer, which)), tm=tm_in_p)
            op, os_ = _ffn_down(hp, hs, w_out, xp2, mod_p, xs2, mod_s, 6 * which + 2, tm=tm_in_p)
            return op.reshape(xp.shape), os_.reshape(xs.shape)

        xp, xs = ffn(xp, xs, 0)
        if kind == 0:
            prm = ((gdn_w_in16, (j,)), gdn_conv_w[j], gdn_a_log[j], gdn_dt_bias[j], gdn_norm_w[j],
                   (gdn_w_out16, (j,)))
            buf0 = jnp.zeros((bp,) + state_a_conv.shape[2:], F32)
            s0 = jnp.zeros((bp,) + state_a_ssm.shape[2:], F32)
            xp, buf, s = _gdn_mixer(xp, mod_p, nw[1:2], buf0, s0, *prm, chunk=2 * GDN_CHUNK, **tiles_p)
            outs["a_conv_p"].append(buf)
            outs["a_ssm_p"].append(s)
            xs, buf, s = _gdn_mixer(xs, mod_s, nw[1:2], state_a_conv[j], state_a_ssm[j], *prm, chunk=GDN_CHUNK,
                                    **tiles_s)
            outs["a_conv_s"].append(buf)
            outs["a_ssm_s"].append(s)
        elif kind == 1:
            w_in, w_out = (dsa_w_in16, (j,)), (dsa_w_out16, (j,))
            n_proj = dsa_w_in.shape[2]
            xp, k, v, ki = _dsa_mixer(xp, mod_p, nw[1:2], w_in, w_out, n_proj, nkv=nkv, **tiles_p)
            outs["b_k_p"].append(k)
            outs["b_v_p"].append(v)
            outs["b_i_p"].append(ki)
            xs, k, v, ki = _dsa_mixer(xs, mod_s, nw[1:2], w_in, w_out, n_proj, nkv=nkv,
                                      cache=(cache_b_k, cache_b_v, cache_b_idx, j, page_table), **tiles_s)
            outs["b_k_s"].append(k)
            outs["b_v_s"].append(v)
            outs["b_i_s"].append(ki)
        else:
            prm = ((lru_w_in16, (j,)), lru_conv_w[j], lru_conv_b[j], lru_w_gate_a[j].astype(BF16),
                   lru_b_gate_a[j], lru_w_gate_x[j].astype(BF16), lru_b_gate_x[j], lru_lambda[j],
                   (lru_w_out16, (j,)))
            buf0 = jnp.zeros((bp,) + state_c_conv.shape[2:], F32)
            h0 = jnp.zeros((bp,) + state_c_h.shape[2:], F32)
            xp, buf, hl = _lru_mixer(xp, mod_p, nw[1:2], buf0, h0, *prm, **tiles_p)
            outs["c_conv_p"].append(buf)
            outs["c_h_p"].append(hl)
            xs, buf, hl = _lru_mixer(xs, mod_s, nw[1:2], state_c_conv[j], state_c_h[j], *prm, **tiles_s)
            outs["c_conv_s"].append(buf)
            outs["c_h_s"].append(hl)
        xp, xs = ffn(xp, xs, 1)

    modf_p, modf_s = groups(mod_f)
    fnw = final_norm_w.reshape(1, d)
    y_p = _final_norm(xp.reshape(bp * seq, d), modf_p, fnw, tm=tm_p).reshape(bp, seq, d)
    y_s = _final_norm(xs.reshape(bs * ts, d), modf_s, fnw, tm=tm_s).reshape(bs, ts, d)
    st = {k: jnp.stack(v) for k, v in outs.items()}
    return (y_p, y_s, st["a_conv_p"], st["a_conv_s"], st["a_ssm_p"], st["a_ssm_s"], st["b_k_p"], st["b_k_s"],
            st["b_v_p"], st["b_v_s"], st["b_i_p"], st["b_i_s"], st["c_conv_p"], st["c_conv_s"], st["c_h_p"],
            st["c_h_s"])
```

```python
import functools
import math

import jax
import jax.numpy as jnp
import numpy as np
from jax import lax
from jax.experimental import pallas as pl
from jax.experimental.pallas import tpu as pltpu

F32 = jnp.float32
BF16 = jnp.bfloat16
I32 = jnp.int32

N_MIXERS = 3
CONV_W = 4
NORM_EPS = 1e-6
NEG_INF = -1e30
GDN_CHUNK = 64
TOPK_MAX = 256
Q_BLOCK = 128
ROPE_THETA = 10000.0
RG_C = 8.0
LRU_BLOCK = 256
SAMPLE_PAGES_PER_STEP = 8

LANES = 128
SUBLANES = 8
VMEM_LIMIT_BYTES = 56 * 2**20
VT_ROWS = LANES + 16
INT32_MIN = -2**31
INT32_MAX = 2**31 - 1
KEY_OF_NEG_INF = int(np.float32(NEG_INF).view(np.int32)) ^ 0x7FFFFFFF


def _cparams(*sem):
    return pltpu.CompilerParams(dimension_semantics=sem, vmem_limit_bytes=VMEM_LIMIT_BYTES)


def _tile(n, target, align=LANES):
    if n <= target:
        return n
    t = (target // align) * align
    while t >= align:
        if n % t == 0:
            return t
        t -= align
    raise ValueError(f"no {align}-aligned tile of {n} below {target}")


def _sigmoid(x):
    return jax.nn.sigmoid(x)


def _silu(x):
    return x * _sigmoid(x)


def _dot(a, b):
    return jnp.dot(a, b, preferred_element_type=F32)


def _dot_nt(a, b):
    return lax.dot_general(a, b, (((1,), (1,)), ((), ())), preferred_element_type=F32)


def _dot_tn(a, b):
    return lax.dot_general(a, b, (((0,), (0,)), ((), ())), preferred_element_type=F32)


def _dot_f32(a, b):
    return jnp.dot(a, b, preferred_element_type=F32, precision=lax.Precision.HIGHEST)


def _norm_mod(x, nw, sh, sc):
    ms = jnp.mean(x * x, axis=-1, keepdims=True)
    return x * lax.rsqrt(ms + NORM_EPS) * (nw * (1.0 + sc)) + sh


def _ada_kernel(c_ref, w_ref, b_ref, o_ref):
    a = _silu(c_ref[...]).astype(BF16)
    o_ref[0] = _dot(a, w_ref[0].astype(BF16)) + b_ref[0]


def _ada(c_all, w, b):
    n_l, d, n = w.shape
    mp = c_all.shape[0]
    tn = _tile(n, 1024)
    return pl.pallas_call(
        _ada_kernel,
        grid=(n_l, n // tn),
        in_specs=[
            pl.BlockSpec((mp, d), lambda l, j: (0, 0)),
            pl.BlockSpec((1, d, tn), lambda l, j: (l, 0, j)),
            pl.BlockSpec((1, 1, tn), lambda l, j: (l, 0, j)),
        ],
        out_specs=pl.BlockSpec((1, mp, tn), lambda l, j: (l, 0, j)),
        out_shape=jax.ShapeDtypeStruct((n_l, mp, n), F32),
        name="ada_mod",
        compiler_params=_cparams("parallel", "parallel"),
    )(c_all, w, b.reshape(n_l, 1, n))


def _mod_spec(mod3, k, d, tm, rows_per_group):
    return pl.BlockSpec((None, mod3.shape[1], d), lambda i, j: ((i * tm) // rows_per_group, 0, k))


def _wspec(w, block, imap):
    arr, idx = w
    return pl.BlockSpec((None,) * len(idx) + block, lambda i, j: idx + imap(i, j))


def _wshape(w):
    return w[0].shape[len(w[1]):]


def _normproj_kernel(x_ref, nw_ref, sh_ref, sc_ref, w_ref, o_ref, xn_ref):
    @pl.when(pl.program_id(1) == 0)
    def _():
        xn_ref[...] = _norm_mod(x_ref[...], nw_ref[...], sh_ref[...], sc_ref[...]).astype(BF16)

    o_ref[...] = _dot(xn_ref[...], w_ref[...])


def _normproj(x, mod3, k0, nw, w, *, tm, tn_target=1024):
    m, d = x.shape
    n = _wshape(w)[1]
    tn = _tile(n, tn_target)
    rpg = m // mod3.shape[0]
    ms = lambda k: _mod_spec(mod3, k, d, tm, rpg)
    return pl.pallas_call(
        _normproj_kernel,
        grid=(m // tm, n // tn),
        in_specs=[
            pl.BlockSpec((tm, d), lambda i, j: (i, 0)),
            pl.BlockSpec((1, d), lambda i, j: (0, 0)),
            ms(k0), ms(k0 + 1),
            _wspec(w, (d, tn), lambda i, j: (0, j)),
        ],
        out_specs=pl.BlockSpec((tm, tn), lambda i, j: (i, j)),
        out_shape=jax.ShapeDtypeStruct((m, n), F32),
        scratch_shapes=[pltpu.VMEM((tm, d), BF16)],
        name="normproj",
        compiler_params=_cparams("parallel", "arbitrary"),
    )(x, nw, mod3, mod3, w[0])


def _ffn_up_kernel(x_ref, nw_ref, sh_ref, sc_ref, xs_ref, shs_ref, scs_ref, wa_ref, wb_ref, h_ref, hs_ref,
                   xn_ref, xsn_ref):
    i, j = pl.program_id(0), pl.program_id(1)

    @pl.when(j == 0)
    def _():
        xn_ref[...] = _norm_mod(x_ref[...], nw_ref[...], sh_ref[...], sc_ref[...]).astype(BF16)

    @pl.when((i == 0) & (j == 0))
    def _():
        xsn_ref[...] = _norm_mod(xs_ref[...], nw_ref[...], shs_ref[...], scs_ref[...]).astype(BF16)

    wa, wb = wa_ref[...].astype(BF16), wb_ref[...].astype(BF16)
    swiglu = lambda xn: (_silu(_dot(xn, wa)) * _dot(xn, wb)).astype(BF16)
    h_ref[...] = swiglu(xn_ref[...])

    @pl.when(i == 0)
    def _():
        hs_ref[...] = swiglu(xsn_ref[...])

    @pl.when(i > 0)
    def _():
        hs_ref[...] = jnp.zeros_like(hs_ref)


def _ffn_up(x, mod3, xs, mods3, k0, nw, w_in, *, tm, tf_target=512):
    m, d = x.shape
    msr = xs.shape[0]
    f = _wshape(w_in)[1] // 2
    tf = _tile(f, tf_target)
    nf = f // tf
    rpg = m // mod3.shape[0]
    ms = lambda k: _mod_spec(mod3, k, d, tm, rpg)
    mss = lambda k: pl.BlockSpec((None, msr, d), lambda i, j: (0, 0, k))
    return pl.pallas_call(
        _ffn_up_kernel,
        grid=(m // tm, nf),
        in_specs=[
            pl.BlockSpec((tm, d), lambda i, j: (i, 0)),
            pl.BlockSpec((1, d), lambda i, j: (0, 0)),
            ms(k0), ms(k0 + 1),
            pl.BlockSpec((msr, d), lambda i, j: (0, 0)),
            mss(k0), mss(k0 + 1),
            _wspec(w_in, (d, tf), lambda i, j: (0, j)),
            _wspec(w_in, (d, tf), lambda i, j: (0, j + nf)),
        ],
        out_specs=[pl.BlockSpec((tm, tf), lambda i, j: (i, j)),
                   pl.BlockSpec((msr, tf), lambda i, j: (i, j))],
        out_shape=[jax.ShapeDtypeStruct((m, f), BF16), jax.ShapeDtypeStruct((m // tm * msr, f), BF16)],
        scratch_shapes=[pltpu.VMEM((tm, d), BF16), pltpu.VMEM((msr, d), BF16)],
        name="ffn_up",
        compiler_params=_cparams("arbitrary", "arbitrary"),
    )(x, nw, mod3, mod3, xs, mods3, mods3, w_in[0], w_in[0])


def _outproj_kernel(a_ref, w_ref, x_ref, g_ref, o_ref):
    o_ref[...] = x_ref[...] + g_ref[...] * _dot(a_ref[...], w_ref[...])


def _outproj(a, w, x, mod3, kg, *, tm, tn_target=1024):
    m, kdim = a.shape
    d = _wshape(w)[1]
    tn = _tile(d, tn_target)
    rpg = m // mod3.shape[0]
    r = mod3.shape[1]
    nd = d // tn
    return pl.pallas_call(
        _outproj_kernel,
        grid=(m // tm, nd),
        in_specs=[
            pl.BlockSpec((tm, kdim), lambda i, j: (i, 0)),
            _wspec(w, (kdim, tn), lambda i, j: (0, j)),
            pl.BlockSpec((tm, tn), lambda i, j: (i, j)),
            pl.BlockSpec((None, r, tn), lambda i, j: ((i * tm) // rpg, 0, kg * nd + j)),
        ],
        out_specs=pl.BlockSpec((tm, tn), lambda i, j: (i, j)),
        out_shape=jax.ShapeDtypeStruct((m, d), F32),
        name="outproj",
        compiler_params=_cparams("parallel", "arbitrary"),
    )(a, w[0], x, mod3)


def _ffn_down_kernel(a_ref, w_ref, x_ref, g_ref, as_ref, xs_ref, gs_ref, o_ref, os_ref):
    w = w_ref[...]
    o_ref[...] = x_ref[...] + 0.5 * g_ref[...] * _dot(a_ref[...], w)

    @pl.when(pl.program_id(0) == 0)
    def _():
        os_ref[...] = xs_ref[...] + 0.5 * gs_ref[...] * _dot(as_ref[...], w)

    @pl.when(pl.program_id(0) > 0)
    def _():
        os_ref[...] = jnp.zeros_like(os_ref)


def _ffn_down(a, a_s, w, x, mod3, xs, mods3, kg, *, tm, tn_target=512):
    m, kdim = a.shape
    msr = xs.shape[0]
    d = _wshape(w)[1]
    tn = _tile(d, tn_target)
    rpg = m // mod3.shape[0]
    nd = d // tn
    o, o_s = pl.pallas_call(
        _ffn_down_kernel,
        grid=(m // tm, nd),
        in_specs=[
            pl.BlockSpec((tm, kdim), lambda i, j: (i, 0)),
            _wspec(w, (kdim, tn), lambda i, j: (0, j)),
            pl.BlockSpec((tm, tn), lambda i, j: (i, j)),
            pl.BlockSpec((None, mod3.shape[1], tn), lambda i, j: ((i * tm) // rpg, 0, kg * nd + j)),
            pl.BlockSpec((msr, kdim), lambda i, j: (0, 0)),
            pl.BlockSpec((msr, tn), lambda i, j: (0, j)),
            pl.BlockSpec((None, msr, tn), lambda i, j: (0, 0, kg * nd + j)),
        ],
        out_specs=[pl.BlockSpec((tm, tn), lambda i, j: (i, j)),
                   pl.BlockSpec((msr, tn), lambda i, j: (i, j))],
        out_shape=[jax.ShapeDtypeStruct((m, d), F32), jax.ShapeDtypeStruct((m // tm * msr, d), F32)],
        name="ffn_down",
        compiler_params=_cparams("arbitrary", "arbitrary"),
    )(a, w[0], x, mod3, a_s, xs, mods3)
    return o, o_s[:msr]


def _final_kernel(x_ref, nw_ref, sh_ref, sc_ref, o_ref):
    o_ref[...] = _norm_mod(x_ref[...], nw_ref[...], sh_ref[...], sc_ref[...])


def _final_norm(x, mod3, nw, *, tm):
    m, d = x.shape
    rpg = m // mod3.shape[0]
    r = mod3.shape[1]
    ms = lambda k: pl.BlockSpec((None, r, d), lambda i: ((i * tm) // rpg, 0, k))
    return pl.pallas_call(
        _final_kernel,
        grid=(m // tm,),
        in_specs=[pl.BlockSpec((tm, d), lambda i: (i, 0)), pl.BlockSpec((1, d), lambda i: (0, 0)), ms(0), ms(1)],
        out_specs=pl.BlockSpec((tm, d), lambda i: (i, 0)),
        out_shape=jax.ShapeDtypeStruct((m, d), F32),
        name="final_norm",
        compiler_params=_cparams("parallel"),
    )(x, nw, mod3, mod3)


def _causal_conv(x, halo, w):
    def taps(rows, fix):
        acc = rows * w[CONV_W - 1:CONV_W]
        for j in range(1, CONV_W):
            acc = acc + fix(pltpu.roll(rows, j, 0), j) * w[CONV_W - 1 - j:CONV_W - j]
        return acc

    row = lax.broadcasted_iota(I32, halo.shape, 0)
    head = taps(x[:SUBLANES], lambda r, j: jnp.where(row < j, pltpu.roll(halo, j, 0), r))
    if x.shape[0] == SUBLANES:
        return head
    return jnp.concatenate([head, taps(x, lambda r, j: r)[SUBLANES:]], axis=0)


def _conv_specs(tt, tc, coff, boff):
    sub = tt // SUBLANES
    return [
        pl.BlockSpec((None, tt, tc), lambda b, t, c: (b, t, c + coff)),
        pl.BlockSpec((None, SUBLANES, tc), lambda b, t, c: (b, jnp.maximum(t * sub - 1, 0), c + coff)),
        pl.BlockSpec((None, SUBLANES, tc), lambda b, t, c: (b, 0, c + boff)),
    ]


def _pad_buf(buf):
    return jnp.pad(buf, ((0, 0), (SUBLANES - (CONV_W - 1), 0), (0, 0)))


def _gdn_prep_kernel(x_ref, halo_ref, buf_ref, w_ref, o_ref, *, norm):
    halo = jnp.where(pl.program_id(1) == 0, buf_ref[...], halo_ref[...])
    y = _silu(_causal_conv(x_ref[...], halo, w_ref[...]))
    if norm:
        for h in range(y.shape[1] // LANES):
            seg = y[:, h * LANES:(h + 1) * LANES]
            ss = jnp.sum(seg * seg, axis=-1, keepdims=True)
            o_ref[:, h * LANES:(h + 1) * LANES] = seg * lax.rsqrt(ss + NORM_EPS)
    else:
        o_ref[...] = y


def _gdn_prep(proj, buf8, conv_w, *, col0, ncols, norm, tt):
    bsz, t, _ = proj.shape
    tc = _tile(ncols, 1024)
    return pl.pallas_call(
        functools.partial(_gdn_prep_kernel, norm=norm),
        grid=(bsz, t // tt, ncols // tc),
        in_specs=_conv_specs(tt, tc, col0 // tc, col0 // tc)
        + [pl.BlockSpec((CONV_W, tc), lambda b, i, c: (0, c + col0 // tc))],
        out_specs=pl.BlockSpec((None, tt, tc), lambda b, i, c: (b, i, c)),
        out_shape=jax.ShapeDtypeStruct((bsz, t, ncols), F32),
        name="gdn_prep",
        compiler_params=_cparams("parallel", "parallel", "parallel"),
    )(proj, proj, buf8, conv_w)


def _gdn_gate_kernel(x_ref, alog_ref, dtb_ref, beta_ref, gc_ref, *, hv, chunk):
    x = x_ref[...]
    tt = x.shape[0]
    beta_ref[...] = _sigmoid(x)
    z = x + dtb_ref[...]
    g = -jnp.exp(alog_ref[...]) * (jnp.maximum(z, 0.0) + jnp.log1p(jnp.exp(-jnp.abs(z))))
    row = lax.broadcasted_iota(I32, (tt, tt), 0)
    col = lax.broadcasted_iota(I32, (tt, tt), 1)
    tri = jnp.where((row >= col) & (row // chunk == col // chunk), 1.0, 0.0)
    gc_ref[...] = _dot_f32(tri, g)


def _gdn_gates(proj, a_log, dt_bias, *, col0, hv, chunk, tt):
    bsz, t, _ = proj.shape
    pad = lambda v: jnp.pad(v.astype(F32), (hv, LANES - 2 * hv)).reshape(1, LANES)
    blk = pl.BlockSpec((None, tt, LANES), lambda b, i: (b, i, col0 // LANES))
    out = pl.BlockSpec((None, tt, LANES), lambda b, i: (b, i, 0))
    par = pl.BlockSpec((1, LANES), lambda b, i: (0, 0))
    return pl.pallas_call(
        functools.partial(_gdn_gate_kernel, hv=hv, chunk=chunk),
        grid=(bsz, t // tt),
        in_specs=[blk, par, par],
        out_specs=[out, out],
        out_shape=[jax.ShapeDtypeStruct((bsz, t, LANES), F32)] * 2,
        name="gdn_gates",
        compiler_params=_cparams("parallel", "parallel"),
    )(proj, pad(a_log), pad(dt_bias))


def _split_bf16(a):
    hi = a.astype(BF16)
    return hi, (a - hi.astype(F32)).astype(BF16)


def _dot_split(a, b):
    ah, al = _split_bf16(a)
    bh, bl = _split_bf16(b)
    return _dot(jnp.concatenate([ah, al, ah], axis=1), jnp.concatenate([bh, bh, bl], axis=0))


def _dot_bf16(a, b):
    return _dot(a.astype(BF16), b.astype(BF16))


def _tri_inv_all(lmats, c, n_real):
    row = lax.broadcasted_iota(I32, (c, c), 0)
    col = lax.broadcasted_iota(I32, (c, c), 1)
    eye = jnp.where(row == col, 1.0, 0.0)
    base = min(16, c)
    ps = [-jnp.where(row // base == col // base, m, 0.0) for m in lmats]
    rs = [eye + p for p in ps]
    n = 2
    while n < base:
        ps = [_dot_bf16(p, p) for p in ps]
        rs = [r + _dot_bf16(r, p) for r, p in zip(rs, ps)]
        n *= 2
    s = base
    while s < min(c, n_real):
        off = (row // (2 * s) == col // (2 * s)) & (row // s != col // s)
        ts = [_dot_bf16(jnp.where(off, m, 0.0), r) for m, r in zip(lmats, rs)]
        rs = [r - _dot_bf16(r, t) for r, t in zip(rs, ts)]
        s *= 2
    res = [eye - r - _dot_split(m, r) for m, r in zip(lmats, rs)]
    return [r + _dot_bf16(r, e) for r, e in zip(rs, res)]


def _gdn_core_kernel(q_ref, k_ref, v_ref, z_ref, gcc_ref, gcr_ref, bc_ref, s0_ref, nw_ref, o_ref, so_ref, s_ref,
                     *, rep, c, ncb, hpb, dk, n_real):
    ci = pl.program_id(2)

    @pl.when(ci == 0)
    def _():
        s_ref[...] = s0_ref[...]

    row = lax.broadcasted_iota(I32, (c, c), 0)
    col = lax.broadcasted_iota(I32, (c, c), 1)
    causal = row >= col
    nhd = hpb * rep
    kinst = [(n, hh) for n in range(ncb) for hh in range(hpb)]
    inst = [(n, hh, r) for n in range(ncb) for hh in range(hpb) for r in range(rep)]
    rows = lambda n: slice(n * c, (n + 1) * c)
    lanes = lambda hh, r: slice((hh * rep + r) * LANES, (hh * rep + r + 1) * LANES)
    kidx = lambda n, hh: n * hpb + hh
    qs = [q_ref[rows(n), hh * LANES:(hh + 1) * LANES] * dk ** -0.5 for n, hh in kinst]
    ks = [k_ref[rows(n), hh * LANES:(hh + 1) * LANES] for n, hh in kinst]
    k16 = [k.astype(BF16) for k in ks]
    grams = [_dot_nt(kb, kb) for kb in k16]
    qk0s = [_dot_nt(q.astype(BF16), kb) for q, kb in zip(qs, k16)]
    gccs = [gcc_ref[hh, rows(n), r:r + 1] for n, hh, r in inst]
    gcrs = [gcr_ref[hh, r:r + 1, rows(n)] for n, hh, r in inst]
    betas = [bc_ref[hh, rows(n), r:r + 1] for n, hh, r in inst]
    decays = [jnp.where(causal, jnp.exp(jnp.where(causal, gc - gr, 0.0)), 0.0) for gc, gr in zip(gccs, gcrs)]
    lowers = [jnp.where(row > col, grams[kidx(n, hh)] * b * d, 0.0)
              for (n, hh, _), b, d in zip(inst, betas, decays)]
    tinvs = _tri_inv_all(lowers, c, n_real)
    egcs = [jnp.exp(gc) for gc in gccs]
    sols = [_dot_split(ti, jnp.concatenate([v_ref[rows(n), lanes(hh, r)] * b, ks[kidx(n, hh)] * (b * e)], axis=1))
            for ti, b, e, (n, hh, r) in zip(tinvs, betas, egcs, inst)]
    qg16 = [(qs[kidx(n, hh)] * e).astype(BF16) for (n, hh, _), e in zip(inst, egcs)]
    qk16 = [(qk0s[kidx(n, hh)] * d).astype(BF16) for (n, hh, _), d in zip(inst, decays)]
    g_last = [gc[c - 1:c, :] for gc in gccs]
    kd16 = [(ks[kidx(n, hh)] * jnp.exp(gl - gc)).astype(BF16) for (n, hh, _), gl, gc in zip(inst, g_last, gccs)]
    ss = [s_ref[i] for i in range(nhd)]
    for n in range(ncb):
        ids = range(n * nhd, (n + 1) * nhd)
        s16 = [s.astype(BF16) for s in ss]
        u16 = [(sols[i][:, :LANES] - _dot(sols[i][:, LANES:].astype(BF16), sb)).astype(BF16)
               for i, sb in zip(ids, s16)]
        ss = [s * jnp.exp(g_last[i]) + _dot_tn(kd16[i], ub) for s, i, ub in zip(ss, ids, u16)]
        os_ = [_dot(qg16[i], sb) + _dot(qk16[i], ub) for i, sb, ub in zip(ids, s16, u16)]
        for o, i in zip(os_, ids):
            _, hh, r = inst[i]
            on = o * lax.rsqrt(jnp.mean(o * o, axis=-1, keepdims=True) + NORM_EPS) * nw_ref[...]
            o_ref[rows(n), lanes(hh, r)] = (on * _silu(z_ref[rows(n), lanes(hh, r)])).astype(BF16)
    for i in range(nhd):
        s_ref[i] = ss[i]

    @pl.when(ci == pl.num_programs(2) - 1)
    def _():
        so_ref[...] = s_ref[...]


def _gdn_core(qk, v, proj, zcol0, gc, beta, s0, norm_w, *, c, ncb, hpb, n_real):
    bsz, t, val = v.shape
    hv = s0.shape[1]
    dk, dv = s0.shape[2], s0.shape[3]
    hk = qk.shape[2] // (2 * dk)
    rep = hv // hk
    tb = ncb * c
    assert dk == LANES and dv == LANES and t % tb == 0 and hk % hpb == 0 and zcol0 % (hpb * rep * dv) == 0
    heads = lambda a, lo: a[:, :, lo:lo + hv].reshape(bsz, t, hk, rep).transpose(0, 2, 1, 3)
    gcc = heads(gc, hv)
    bcc = heads(beta, 0)
    gcr = gcc.transpose(0, 1, 3, 2)
    nhb = hk // hpb
    zb = zcol0 // (hpb * rep * dv)
    colspec = pl.BlockSpec((None, hpb, tb, rep), lambda b, h, i: (b, h, i, 0))
    o, s_out = pl.pallas_call(
        functools.partial(_gdn_core_kernel, rep=rep, c=c, ncb=ncb, hpb=hpb, dk=dk, n_real=n_real),
        grid=(bsz, nhb, t // tb),
        in_specs=[
            pl.BlockSpec((None, tb, hpb * dk), lambda b, h, i: (b, i, h)),
            pl.BlockSpec((None, tb, hpb * dk), lambda b, h, i: (b, i, nhb + h)),
            pl.BlockSpec((None, tb, hpb * rep * dv), lambda b, h, i: (b, i, h)),
            pl.BlockSpec((None, tb, hpb * rep * dv), lambda b, h, i: (b, i, zb + h)),
            colspec,
            pl.BlockSpec((None, hpb, rep, tb), lambda b, h, i: (b, h, 0, i)),
            colspec,
            pl.BlockSpec((None, hpb * rep, dk, dv), lambda b, h, i: (b, h, 0, 0)),
            pl.BlockSpec((1, dv), lambda b, h, i: (0, 0)),
        ],
        out_specs=[
            pl.BlockSpec((None, tb, hpb * rep * dv), lambda b, h, i: (b, i, h)),
            pl.BlockSpec((None, hpb * rep, dk, dv), lambda b, h, i: (b, h, 0, 0)),
        ],
        out_shape=[jax.ShapeDtypeStruct((bsz, t, val), BF16), jax.ShapeDtypeStruct(s0.shape, F32)],
        scratch_shapes=[pltpu.VMEM((hpb * rep, dk, dv), F32)],
        name="gdn_core",
        compiler_params=_cparams("parallel", "parallel", "arbitrary"),
    )(qk, qk, v, proj, gcc, gcr, bcc, s0, norm_w.reshape(1, dv))
    return o, s_out


def _gdn_mixer(x, mod3, nw, buf, s0, w_in, conv_w, a_log, dt_bias, norm_w, w_out, *, tm, tm_in, tt, chunk):
    bsz, t, d = x.shape
    hv, dk, dv = s0.shape[1], s0.shape[2], s0.shape[3]
    val = hv * dv
    conv_dim = conv_w.shape[1]
    key = (conv_dim - val) // 2
    assert (conv_dim + val) % LANES == 0 and 2 * hv <= LANES and t >= CONV_W - 1
    x2 = x.reshape(bsz * t, d)
    proj = _normproj(x2, mod3, 3, nw, w_in, tm=tm_in, tn_target=512).reshape(bsz, t, -1)
    buf8 = _pad_buf(buf)
    qk = _gdn_prep(proj, buf8, conv_w, col0=0, ncols=2 * key, norm=True, tt=tt)
    v = _gdn_prep(proj, buf8, conv_w, col0=2 * key, ncols=val, norm=False, tt=tt)
    tp = -(-t // chunk) * chunk
    beta, gc = _gdn_gates(proj, a_log, dt_bias, col0=conv_dim + val, hv=hv, chunk=min(chunk, tt), tt=tt)
    if tp != t:
        padt = lambda a: jnp.pad(a, ((0, 0), (0, tp - t), (0, 0)))
        gc = jnp.concatenate([gc, jnp.broadcast_to(gc[:, -1:], (bsz, tp - t, LANES))], axis=1)
        qk, v, beta, projz = padt(qk), padt(v), padt(beta), padt(proj)
    else:
        projz = proj
    o, s_new = _gdn_core(qk, v, projz, conv_dim, gc, beta, s0, norm_w, c=chunk, ncb=math.gcd(tp // chunk, 2),
                         hpb=min(4 if tp > chunk else 16, key // dk), n_real=min(t, chunk))
    o2 = o[:, :t].reshape(bsz * t, val)
    xo = _outproj(o2, w_out, x2, mod3, 5, tm=tm).reshape(bsz, t, d)
    new_buf = proj[:, t - (CONV_W - 1):, :conv_dim]
    return xo, new_buf, s_new


def _rope_tables(pos, half):
    inv_freq = ROPE_THETA ** (-jnp.arange(half, dtype=F32) / half)
    ang = pos.astype(F32)[:, None] * inv_freq[None, :]
    cos, sin = jnp.cos(ang), jnp.sin(ang)
    return jnp.concatenate([cos, cos], axis=-1), jnp.concatenate([-sin, sin], axis=-1)


def _dsa_prep_kernel(x_ref, cos_ref, sin_ref, q_ref, k_ref, k16_ref, v16_ref, qi_ref, ki_ref, ki16_ref, *tr_refs,
                     nh, nkv, nih):
    cos, sin = cos_ref[...], sin_ref[...]

    def rope(col):
        seg = x_ref[:, col * LANES:(col + 1) * LANES]
        return seg * cos + pltpu.roll(seg, LANES // 2, 1) * sin

    for h in range(nh):
        q_ref[:, h * LANES:(h + 1) * LANES] = (rope(h) * LANES ** -0.5).astype(BF16)
    for h in range(nkv):
        kr = rope(nh + h)
        k_ref[:, h * LANES:(h + 1) * LANES] = kr
        k16_ref[:, h * LANES:(h + 1) * LANES] = kr.astype(BF16)
    v0 = (nh + nkv) * LANES
    v16_ref[...] = x_ref[:, v0:v0 + nkv * LANES].astype(BF16)
    c0 = nh + 2 * nkv
    for h in range(nih):
        qi_ref[:, h * LANES:(h + 1) * LANES] = rope(c0 + h).astype(BF16)
    kir = rope(c0 + nih)
    ki_ref[...] = kir
    ki16_ref[...] = kir.astype(BF16)
    if tr_refs:
        wt_ref, vt_ref = tr_refs
        w0 = (c0 + nih + 1) * LANES
        wt_ref[...] = x_ref[:, w0:w0 + LANES].T[:wt_ref.shape[0], :]
        for h in range(nkv):
            vt_ref[h * VT_ROWS:h * VT_ROWS + LANES, :] = x_ref[:, v0 + h * LANES:v0 + (h + 1) * LANES].T.astype(BF16)
            vt_ref[h * VT_ROWS + LANES:(h + 1) * VT_ROWS, :] = jnp.ones((VT_ROWS - LANES, vt_ref.shape[1]), BF16)


def _dsa_prep(proj, pos, *, nh, nkv, nih, tt, with_wt):
    bsz, t, npj = proj.shape
    cos, sin = _rope_tables(pos, LANES // 2)
    row = lambda n, dt: jax.ShapeDtypeStruct((bsz, t, n * LANES), dt)
    ospec = lambda n: pl.BlockSpec((None, tt, n * LANES), lambda b, i: (b, i, 0))
    tab = pl.BlockSpec((tt, LANES), lambda b, i: (i, 0))
    nwt = -(-nih // SUBLANES) * SUBLANES
    return pl.pallas_call(
        functools.partial(_dsa_prep_kernel, nh=nh, nkv=nkv, nih=nih),
        grid=(bsz, t // tt),
        in_specs=[pl.BlockSpec((None, tt, npj), lambda b, i: (b, i, 0)), tab, tab],
        out_specs=[ospec(nh), ospec(nkv), ospec(nkv), ospec(nkv), ospec(nih), ospec(1), ospec(1)]
        + ([pl.BlockSpec((None, nwt, tt), lambda b, i: (b, 0, i)),
            pl.BlockSpec((None, None, nkv * VT_ROWS, tt), lambda b, i: (b, i, 0, 0))] if with_wt else []),
        out_shape=[row(nh, BF16), row(nkv, F32), row(nkv, BF16), row(nkv, BF16), row(nih, BF16), row(1, F32),
                   row(1, BF16)]
        + ([jax.ShapeDtypeStruct((bsz, nwt, t), F32),
            jax.ShapeDtypeStruct((bsz, t // tt, nkv * VT_ROWS, tt), BF16)] if with_wt else []),
        name="dsa_prep",
        compiler_params=_cparams("parallel", "parallel"),
    )(proj, cos, sin)


def _sort_key(s):
    bits = pltpu.bitcast(jnp.where(s == 0.0, 0.0, s), I32)
    return jnp.where(bits < 0, bits ^ 0x7FFFFFFF, bits)


def _topk_cut(count_where, shape, topk, idx_bits):
    def body(i, carry):
        t, n_t = carry
        cand = t + lax.shift_left(jnp.int32(1), 31 - i)
        n = count_where(lambda key, idx: key >= cand)
        ok = n >= topk
        return jnp.where(ok, cand, t), jnp.where(ok, n, n_t)

    thr, n_ge = lax.fori_loop(0, 32, body, (jnp.full(shape, INT32_MIN, I32), jnp.full(shape, INT32_MAX, I32)))
    tie = (n_ge > topk) & (thr > KEY_OF_NEG_INF)

    def cut():
        need = topk - count_where(lambda key, idx: key > thr)

        def jbody(i, j):
            cand = j + lax.shift_left(jnp.int32(1), idx_bits - 1 - i)
            below = count_where(lambda key, idx: (key == thr) & (idx < cand))
            return jnp.where(below < need, cand, j)

        return jnp.where(tie, lax.fori_loop(0, idx_bits, jbody, jnp.zeros(shape, I32)), INT32_MAX)

    jcut = lax.cond(jnp.any(tie), cut, lambda: jnp.full(shape, INT32_MAX, I32))
    return thr, jcut


def _topk_chosen(key, idx, thr, jcut):
    return (key > thr) | ((key == thr) & (idx <= jcut))


def _dsa_prompt_kernel(qi_ref, wt_ref, q_ref, ki_ref, k_ref, vt_ref, o_ref, key_ref, bias_ref, acc_ref,
                       *, nh, nkv, nih, topk, idx_scale):
    qb = pl.program_id(1)
    blk = Q_BLOCK
    kb = vt_ref.shape[2]
    ktiles = kb // blk
    nkb = (qb + ktiles) // ktiles
    rowk = lax.broadcasted_iota(I32, (kb, blk), 0)
    colq = lax.broadcasted_iota(I32, (kb, blk), 1)
    wt = wt_ref[...]
    hq = max(1, nih // 4)
    qis = [jnp.concatenate([qi_ref[:, h * LANES:(h + 1) * LANES] for h in range(h0, min(h0 + hq, nih))], axis=0)
           for h0 in range(0, nih, hq)]

    def rows_of(j):
        return pl.ds(pl.multiple_of(j * kb, kb), kb)

    def visible(j):
        return j * kb + rowk <= qb * blk + colq

    def score_body(j, carry):
        keys = ki_ref[rows_of(j), :]
        lgs = [_dot_nt(keys, qi) for qi in qis]
        acc = jnp.zeros((kb, blk), F32)
        for i, lg in enumerate(lgs):
            for hh in range(lg.shape[1] // blk):
                h = i * hq + hh
                acc = acc + jnp.maximum(lg[:, hh * blk:(hh + 1) * blk], 0.0) * wt[h:h + 1, :]
        key_ref[rows_of(j), :] = _sort_key(jnp.where(visible(j), acc * idx_scale, NEG_INF))
        return carry

    lax.fori_loop(0, nkb, score_body, 0)

    def count_where(pred):
        def body(j, c):
            hit = jnp.where(pred(key_ref[rows_of(j), :], j * kb + rowk), 1, 0)
            for i in range(ktiles):
                c = c + hit[i * blk:(i + 1) * blk]
            return c
        cnt = lax.fori_loop(0, nkb, body, jnp.zeros((blk, blk), I32))
        return jnp.sum(cnt, axis=0, keepdims=True)

    thr, jcut = _topk_cut(count_where, (1, blk), topk, (key_ref.shape[0] - 1).bit_length())

    def bias_body(j, carry):
        sel = _topk_chosen(key_ref[rows_of(j), :], j * kb + rowk, thr, jcut) & visible(j)
        bias_ref[rows_of(j), :] = jnp.where(sel, 0.0, NEG_INF)
        return carry

    lax.fori_loop(0, nkb, bias_body, 0)

    rep = nh // nkv
    qgs =[jnp.concatenate([q_ref[:, (g * rep + r) * LANES:(g * rep + r + 1) * LANES] for r in range(rep)], axis=0)
           for g in range(nkv)]
    acc_ref[...] = jnp.zeros_like(acc_ref)

    def att_body(j, carry):
        ms, ls = carry
        rows = rows_of(j)
        bias = jnp.concatenate([bias_ref[rows, :]] * rep, axis=1)
        ss = [_dot_nt(k_ref[rows, g * LANES:(g + 1) * LANES], qgs[g]) + bias for g in range(nkv)]
        m_new = [jnp.maximum(m, jnp.max(s, axis=0, keepdims=True)) for m, s in zip(ms, ss)]
        ps = [jnp.exp((s - m).astype(BF16)) for s, m in zip(ss, m_new)]
        pvs = [_dot(vt_ref[j, g * VT_ROWS:(g + 1) * VT_ROWS, :], p) for g, p in enumerate(ps)]
        alphas = [jnp.exp(m - mn) for m, mn in zip(ms, m_new)]
        for g in range(nkv):
            acc_ref[g] = alphas[g] * acc_ref[g] + pvs[g][:LANES]
        l_new = [a * l + pv[LANES:LANES + 1] for a, l, pv in zip(alphas, ls, pvs)]
        return tuple(m_new), tuple(l_new)

    row0 = lambda v: tuple(jnp.full((1, rep * blk), v, F32) for _ in range(nkv))
    _, ls = lax.fori_loop(0, nkb, att_body, (row0(NEG_INF), row0(0.0)))
    for g in range(nkv):
        og = acc_ref[g] / ls[g]
        for r in range(rep):
            h = g * rep + r
            o_ref[:, h * LANES:(h + 1) * LANES] = og[:, r * blk:(r + 1) * blk].T.astype(BF16)


def _dsa_prompt_attend(q16, qi16, wt, ki16, k16, vt16, *, nh, nkv, nih, topk):
    bsz, t, _ = q16.shape
    rep = nh // nkv
    full = lambda n: pl.BlockSpec((None, t, n * LANES), lambda b, i: (b, 0, 0))
    blk = lambda n: pl.BlockSpec((None, Q_BLOCK, n * LANES), lambda b, i: (b, i, 0))
    return pl.pallas_call(
        functools.partial(_dsa_prompt_kernel, nh=nh, nkv=nkv, nih=nih, topk=topk,
                          idx_scale=(LANES * nih) ** -0.5),
        grid=(bsz, t // Q_BLOCK),
        in_specs=[blk(nih), pl.BlockSpec((None, wt.shape[1], Q_BLOCK), lambda b, i: (b, 0, i)), blk(nh),
                  full(1), full(nkv), pl.BlockSpec((None,) + vt16.shape[1:], lambda b, i: (b, 0, 0, 0))],
        out_specs=blk(nh),
        out_shape=jax.ShapeDtypeStruct((bsz, t, nh * LANES), BF16),
        scratch_shapes=[pltpu.VMEM((t, Q_BLOCK), I32), pltpu.VMEM((t, Q_BLOCK), F32),
                        pltpu.VMEM((nkv, LANES, rep * Q_BLOCK), F32)],
        name="dsa_prompt_attend",
        compiler_params=_cparams("parallel", "arbitrary"),
    )(qi16, wt, q16, ki16, k16, vt16)


def _dsa_sample_score_kernel(pt_ref, qi_ref, wc_ref, *refs, n_steps, pps, nih, tq, past, idx_scale):
    page_refs, new_ref, o_ref = refs[:pps], refs[pps], refs[pps + 1]
    p = pl.program_id(1)
    keys = jnp.concatenate([r[...] for r in page_refs], axis=0).astype(BF16)
    keys = jnp.where(p == n_steps - 1, new_ref[...], keys)
    qi = jnp.concatenate([qi_ref[:, h * LANES:(h + 1) * LANES] for h in range(nih)], axis=0)
    w = jnp.maximum(_dot_nt(qi, keys), 0.0) * wc_ref[...]
    acc = w[0:tq]
    for h in range(1, nih):
        acc = acc + w[h * tq:(h + 1) * tq]
    s = p * keys.shape[0] + lax.broadcasted_iota(I32, acc.shape, 1)
    qpos = past + lax.broadcasted_iota(I32, acc.shape, 0)
    o_ref[...] = jnp.where(s <= qpos, acc * idx_scale, NEG_INF)


def _dsa_sample_select_kernel(s_ref, o_ref, *, topk):
    key = _sort_key(s_ref[...])
    idx = lax.broadcasted_iota(I32, key.shape, 1)
    count_where = lambda pred: jnp.sum(jnp.where(pred(key, idx), 1, 0), axis=-1, keepdims=True)
    thr, jcut = _topk_cut(count_where, (key.shape[0], 1), topk, (key.shape[1] - 1).bit_length())
    o_ref[...] = jnp.where(_topk_chosen(key, idx, thr, jcut) & (s_ref[...] > 0.5 * NEG_INF), 1.0, 0.0)


def _dsa_sample_attn_kernel(pt_ref, q_ref, sel_ref, *refs, n_steps, pps, nh, nkv, tq):
    kp_refs, vp_refs = refs[:pps], refs[pps:2 * pps]
    kn_ref, vn_ref, o_ref, m_ref, l_ref, acc_ref = refs[2 * pps:]
    p = pl.program_id(1)
    rep = nh // nkv
    page = sel_ref.shape[1] // pps

    @pl.when(p == 0)
    def _():
        m_ref[...] = jnp.full_like(m_ref, NEG_INF)
        l_ref[...] = jnp.zeros_like(l_ref)
        acc_ref[...] = jnp.zeros_like(acc_ref)

    last = p == n_steps - 1

    def head_rows(page_refs, new_ref, g):
        cached = jnp.concatenate([r[pl.ds(g, page, stride=nkv), :] for r in page_refs], axis=0).astype(BF16)
        return jnp.where(last, new_ref[g], cached)

    rows = rep * tq
    q = jnp.concatenate([q_ref[:, h * LANES:(h + 1) * LANES] for h in range(nh)], axis=0)
    unsel = (sel_ref[...] - 1.0) * -NEG_INF
    s = jnp.concatenate([_dot_nt(q[g * rows:(g + 1) * rows], head_rows(kp_refs, kn_ref, g)) for g in range(nkv)],
                        axis=0) + jnp.concatenate([unsel] * nh, axis=0)
    m = m_ref[...]
    m_new = jnp.maximum(m, jnp.max(s, axis=-1, keepdims=True))
    alpha = jnp.exp(m - m_new)
    pr = jnp.where(s > 0.5 * NEG_INF, jnp.exp(s - m_new), 0.0)
    l_ref[...] = alpha * l_ref[...] + jnp.sum(pr, axis=-1, keepdims=True)
    pv = jnp.concatenate([_dot(pr[g * rows:(g + 1) * rows].astype(BF16), head_rows(vp_refs, vn_ref, g))
                          for g in range(nkv)], axis=0)
    acc_ref[...] = alpha * acc_ref[...] + pv
    m_ref[...] = m_new

    @pl.when(last)
    def _():
        og = acc_ref[...] / l_ref[...]
        for h in range(nh):
            o_ref[:, h * LANES:(h + 1) * LANES] = og[h * tq:(h + 1) * tq].astype(BF16)


def _dsa_sample_attend(q16, qi16, wi, ki16, k16, v16, cache_k, cache_v, cache_i, layer, page_table,
                       *, nh, nkv, nih, topk):
    bsz, tq, _ = q16.shape
    n_pages = page_table.shape[1]
    n_layers, n_pool, page = cache_i.shape[:3]
    pps = math.gcd(n_pages, SAMPLE_PAGES_PER_STEP)
    assert page == LANES and tq == SUBLANES
    past = n_pages * page
    n_steps = n_pages // pps + 1
    ltot = n_steps * pps * page
    wcol = wi.transpose(0, 2, 1).reshape(bsz, nih * tq, 1)
    pidx = lambda i: (lambda b, p, pt: (layer * n_pool + pt[b, jnp.minimum(p * pps + i, n_pages - 1)], 0, 0))
    qspec = lambda n: pl.BlockSpec((None, tq, n * LANES), lambda b, p, pt: (b, 0, 0))
    idx_scale = (LANES * nih) ** -0.5
    cache_i = cache_i.reshape(n_layers * n_pool, page, LANES)
    ki_new = jnp.pad(ki16, ((0, 0), (0, pps * page - tq), (0, 0)))
    scores = pl.pallas_call(
        functools.partial(_dsa_sample_score_kernel, n_steps=n_steps, pps=pps, nih=nih, tq=tq, past=past,
                          idx_scale=idx_scale),
        grid_spec=pltpu.PrefetchScalarGridSpec(
            num_scalar_prefetch=1, grid=(bsz, n_steps),
            in_specs=[qspec(nih), pl.BlockSpec((None, nih * tq, 1), lambda b, p, pt: (b, 0, 0))]
            + [pl.BlockSpec((None, page, LANES), pidx(i)) for i in range(pps)]
            + [pl.BlockSpec((None, pps * page, LANES), lambda b, p, pt: (b, 0, 0))],
            out_specs=pl.BlockSpec((None, tq, pps * page), lambda b, p, pt: (b, 0, p))),
        out_shape=jax.ShapeDtypeStruct((bsz, tq, ltot), F32),
        name="dsa_sample_scores",
        compiler_params=_cparams("parallel", "arbitrary"),
    )(page_table, qi16, wcol, *([cache_i] * pps), ki_new)
    sel = pl.pallas_call(
        functools.partial(_dsa_sample_select_kernel, topk=topk),
        grid=(1,),
        in_specs=[pl.BlockSpec((bsz * tq, ltot), lambda i: (0, 0))],
        out_specs=pl.BlockSpec((bsz * tq, ltot), lambda i: (0, 0)),
        out_shape=jax.ShapeDtypeStruct((bsz * tq, ltot), F32),
        name="dsa_sample_select",
        compiler_params=_cparams("arbitrary"),
    )(scores.reshape(bsz * tq, ltot)).reshape(bsz, tq, ltot)
    prow = page * nkv
    ck = cache_k.reshape(n_layers * n_pool, prow, LANES)
    cv = cache_v.reshape(n_layers * n_pool, prow, LANES)
    new_rows = lambda a: jnp.pad(a.reshape(bsz, tq, nkv, LANES).transpose(0, 2, 1, 3),
                                 ((0, 0), (0, 0), (0, pps * page - tq), (0, 0)))
    kvspecs = [pl.BlockSpec((None, prow, LANES), pidx(i)) for i in range(pps)]
    newspec = pl.BlockSpec((None, nkv, pps * page, LANES), lambda b, p, pt: (b, 0, 0, 0))
    return pl.pallas_call(
        functools.partial(_dsa_sample_attn_kernel, n_steps=n_steps, pps=pps, nh=nh, nkv=nkv, tq=tq),
        grid_spec=pltpu.PrefetchScalarGridSpec(
            num_scalar_prefetch=1, grid=(bsz, n_steps),
            in_specs=[qspec(nh), pl.BlockSpec((None, tq, pps * page), lambda b, p, pt: (b, 0, p))]
            + kvspecs + kvspecs + [newspec, newspec],
            out_specs=qspec(nh),
            scratch_shapes=[pltpu.VMEM((nh * tq, 1), F32), pltpu.VMEM((nh * tq, 1), F32),
                            pltpu.VMEM((nh * tq, LANES), F32)]),
        out_shape=jax.ShapeDtypeStruct((bsz, tq, nh * LANES), BF16),
        name="dsa_sample_attend",
        compiler_params=_cparams("parallel", "arbitrary"),
    )(page_table, q16, sel, *([ck] * pps), *([cv] * pps), new_rows(k16), new_rows(v16))


def _dsa_mixer(x, mod3, nw, w_in, w_out, n_proj, *, nkv, tm, tm_in, tt, cache=None):
    bsz, t, d = x.shape
    nh = _wshape(w_out)[0] // LANES
    nih = (n_proj - (nh + 2 * nkv + 1) * LANES) // (LANES + 1)
    assert (nh + 2 * nkv + nih + 1) * LANES + nih == n_proj and nih <= LANES
    x2 = x.reshape(bsz * t, d)
    proj = _normproj(x2, mod3, 3, nw, w_in, tm=tm_in).reshape(bsz, t, -1)
    v = proj[:, :, (nh + nkv) * LANES:(nh + 2 * nkv) * LANES]
    if cache is None:
        assert t % Q_BLOCK == 0
        q16, k, k16, _, qi16, ki, ki16, wt, vt16 = _dsa_prep(proj, jnp.arange(t), nh=nh, nkv=nkv, nih=nih,
                                                             tt=_tile(t, 4 * Q_BLOCK, Q_BLOCK), with_wt=True)
        o = _dsa_prompt_attend(q16, qi16, wt, ki16, k16, vt16, nh=nh, nkv=nkv, nih=nih, topk=min(TOPK_MAX, t // 4))
    else:
        cache_k, cache_v, cache_i, layer, page_table = cache
        past = page_table.shape[1] * cache_i.shape[2]
        q16, k, k16, v16, qi16, ki, ki16 = _dsa_prep(proj, past + jnp.arange(t), nh=nh, nkv=nkv, nih=nih, tt=tt,
                                                    with_wt=False)
        w0 = (nh + 2 * nkv + nih + 1) * LANES
        o = _dsa_sample_attend(q16, qi16, proj[:, :, w0:w0 + nih], ki16, k16, v16, cache_k, cache_v, cache_i,
                               layer, page_table, nh=nh, nkv=nkv, nih=nih, topk=min(TOPK_MAX, (past + t) // 4))
    xo = _outproj(o.reshape(bsz * t, nh * LANES), w_out, x2, mod3, 5, tm=tm).reshape(bsz, t, d)
    return xo, k.reshape(bsz, t, nkv, LANES), v.reshape(bsz, t, nkv, LANES), ki


def _gelu_tanh(x):
    return 0.5 * x * (1.0 + jnp.tanh(math.sqrt(2.0 / math.pi) * (x + 0.044715 * (x * x * x))))


def _lru_kernel(gate_ref, xb_ref, halo_ref, buf_ref, cw_ref, cb_ref, wa_ref, ba_ref, wx_ref, bx_ref, lam_ref,
                h0_ref, y_ref, hl_ref, h_ref, *, nblk):
    ti = pl.program_id(1)

    @pl.when(ti == 0)
    def _():
        h_ref[...] = h0_ref[...]

    halo = jnp.where(ti == 0, buf_ref[...], halo_ref[...])
    xc = _causal_conv(xb_ref[...], halo, cw_ref[...]) + cb_ref[...]
    tt = xc.shape[0]
    xc16 = xc.astype(BF16)
    rs, xs = [], []
    for n in range(nblk):
        blk = xc16[:, n * LRU_BLOCK:(n + 1) * LRU_BLOCK]
        rs.append(_dot(blk, wa_ref[n]))
        xs.append(_dot(blk, wx_ref[n]))
    r = _sigmoid(jnp.concatenate(rs, axis=1) + ba_ref[...])
    ig = _sigmoid(jnp.concatenate(xs, axis=1) + bx_ref[...])
    lam = lam_ref[...]
    softplus_neg = jnp.maximum(-lam, 0.0) + jnp.log1p(jnp.exp(-jnp.abs(lam)))
    log_a = -RG_C * r * softplus_neg
    a = jnp.exp(log_a)
    b = jnp.sqrt(-jnp.tanh(log_a) * (a * a + 1.0)) * (ig * xc)
    sub = lax.broadcasted_iota(I32, a.shape, 0) % SUBLANES
    d = 1
    while d < SUBLANES:
        keep = sub >= d
        a_sh = jnp.where(keep, pltpu.roll(a, d, 0), 1.0)
        b_sh = jnp.where(keep, pltpu.roll(b, d, 0), 0.0)
        b = a * b_sh + b
        a = a * a_sh
        d *= 2
    carry, groups = h_ref[...], []
    for g in range(tt // SUBLANES):
        rows = slice(g * SUBLANES, (g + 1) * SUBLANES)
        groups.append(b[rows] + a[rows] * carry)
        carry = groups[-1][SUBLANES - 1:SUBLANES]
    hs = jnp.concatenate(groups, axis=0)
    h_ref[...] = hs[tt - 1:tt, :]
    y_ref[...] = (hs * _gelu_tanh(gate_ref[...])).astype(BF16)

    @pl.when(ti == pl.num_programs(1) - 1)
    def _():
        hl_ref[...] = hs[tt - 1:tt, :]


def _lru_core(proj, buf8, h0, conv_w, conv_b, w_ga, b_ga, w_gx, b_gx, lam, *, tt):
    bsz, t, w2 = proj.shape
    w = w2 // 2
    nblk = w // LRU_BLOCK
    sub = tt // SUBLANES
    vec = lambda: pl.BlockSpec((1, w), lambda b, i: (0, 0))
    wsp = lambda: pl.BlockSpec((nblk, LRU_BLOCK, LRU_BLOCK), lambda b, i: (0, 0, 0))
    r1 = lambda v: v.reshape(1, w).astype(F32)
    y, hl = pl.pallas_call(
        functools.partial(_lru_kernel, nblk=nblk),
        grid=(bsz, t // tt),
        in_specs=[
            pl.BlockSpec((None, tt, w), lambda b, i: (b, i, 0)),
            pl.BlockSpec((None, tt, w), lambda b, i: (b, i, 1)),
            pl.BlockSpec((None, SUBLANES, w), lambda b, i: (b, jnp.maximum(i * sub - 1, 0), 1)),
            pl.BlockSpec((None, SUBLANES, w), lambda b, i: (b, 0, 0)),
            pl.BlockSpec((CONV_W, w), lambda b, i: (0, 0)),
            vec(), wsp(), vec(), wsp(), vec(), vec(),
            pl.BlockSpec((None, 1, w), lambda b, i: (b, 0, 0)),
        ],
        out_specs=[pl.BlockSpec((None, tt, w), lambda b, i: (b, i, 0)),
                   pl.BlockSpec((None, 1, w), lambda b, i: (b, 0, 0))],
        out_shape=[jax.ShapeDtypeStruct((bsz, t, w), BF16), jax.ShapeDtypeStruct((bsz, 1, w), F32)],
        scratch_shapes=[pltpu.VMEM((1, w), F32)],
        name="lru_core",
        compiler_params=_cparams("parallel", "arbitrary"),
    )(proj, proj, proj, buf8, conv_w, r1(conv_b), w_ga, r1(b_ga), w_gx, r1(b_gx), r1(lam), h0.reshape(bsz, 1, w))
    return y, hl.reshape(bsz, w)


def _lru_mixer(x, mod3, nw, buf, h0, w_in, conv_w, conv_b, w_ga, b_ga, w_gx, b_gx, lam, w_out, *, tm, tm_in, tt):
    bsz, t, d = x.shape
    assert t >= CONV_W - 1
    x2 = x.reshape(bsz * t, d)
    proj = _normproj(x2, mod3, 3, nw, w_in, tm=tm_in).reshape(bsz, t, -1)
    w = proj.shape[2] // 2
    y, hl = _lru_core(proj, _pad_buf(buf), h0, conv_w, conv_b, w_ga, b_ga, w_gx, b_gx, lam, tt=tt)
    xo = _outproj(y.reshape(bsz * t, w), w_out, x2, mod3, 5, tm=tm).reshape(bsz, t, d)
    return xo, proj[:, t - (CONV_W - 1):, w:], hl


def _bf16_padded(w, mult):
    n = w.shape[-1]
    return jnp.pad(w, ((0, 0), (0, 0), (0, -n % mult))).astype(BF16)


def kernel(x_prompt, x_sample, state_a_conv, state_a_ssm, cache_b_k, cache_b_v, cache_b_idx, state_c_conv, state_c_h, page_table, c_prompt, c_sample, w_ada, b_ada, norm_w, ffn_w_in, ffn_w_out, gdn_w_in, gdn_conv_w, gdn_a_log, gdn_dt_bias, gdn_norm_w, gdn_w_out, dsa_w_in, dsa_w_out, lru_w_in, lru_conv_w, lru_conv_b, lru_w_gate_a, lru_b_gate_a, lru_w_gate_x, lru_b_gate_x, lru_lambda, lru_w_out, w_ada_final, b_ada_final, final_norm_w):
    bp, seq, d = x_prompt.shape
    bs, ts, _ = x_sample.shape
    depth = w_ada.shape[0]
    nkv = cache_b_k.shape[3]
    tm_p = _tile(seq, 512, SUBLANES)
    tm_in_p = _tile(seq, 1024, SUBLANES)
    tt_p = _tile(seq, 256, SUBLANES)
    tm_s = bs * ts
    tiles_p = dict(tm=tm_p, tm_in=tm_in_p, tt=tt_p)
    tiles_s = dict(tm=tm_s, tm_in=tm_s, tt=ts)

    c_all = jnp.concatenate([c_prompt, c_sample], axis=0)
    c_all = jnp.pad(c_all, ((0, -c_all.shape[0] % SUBLANES), (0, 0)))
    mod = _ada(c_all, w_ada, b_ada)
    mod_f = _ada(c_all, w_ada_final[None], b_ada_final[None])[0]

    def groups(m):
        return m[:bp, None, :], jnp.repeat(m[bp:bp + bs], ts, axis=0)[None]

    ffn_w_out16 = ffn_w_out.astype(BF16)
    gdn_w_in16, gdn_w_out16 = _bf16_padded(gdn_w_in, 512), gdn_w_out.astype(BF16)
    dsa_w_in16, dsa_w_out16 = _bf16_padded(dsa_w_in, 512), dsa_w_out.astype(BF16)
    lru_w_in16, lru_w_out16 = lru_w_in.astype(BF16), lru_w_out.astype(BF16)

    xp, xs = x_prompt, x_sample
    outs = {k: [] for k in ("a_conv_p", "a_conv_s", "a_ssm_p", "a_ssm_s", "b_k_p", "b_k_s", "b_v_p", "b_v_s",
                            "b_i_p", "b_i_s", "c_conv_p", "c_conv_s", "c_h_p", "c_h_s")}
    for layer in range(depth):
        kind, j = layer % N_MIXERS, layer // N_MIXERS
        mod_p, mod_s = groups(mod[layer])
        nw = norm_w[layer]

        def ffn(xp, xs, which):
            xp2, xs2 = xp.reshape(bp * seq, d), xs.reshape(tm_s, d)
            w_out = (ffn_w_out16, (layer, which))
            hp, hs = _ffn_up(xp2, mod_p, xs2, mod_s, 6 * which, nw[2 * which:2 * which + 1],
                             (ffn_w_in, (layer, which)), tm=tm_in_p)
            op, os_ = _ffn_down(hp, hs, w_out, xp2, mod_p, xs2, mod_s, 6 * which + 2, tm=tm_in_p)
            return op.reshape(xp.shape), os_.reshape(xs.shape)

        xp, xs = ffn(xp, xs, 0)
        if kind == 0:
            prm = ((gdn_w_in16, (j,)), gdn_conv_w[j], gdn_a_log[j], gdn_dt_bias[j], gdn_norm_w[j],
                   (gdn_w_out16, (j,)))
            buf0 = jnp.zeros((bp,) + state_a_conv.shape[2:], F32)
            s0 = jnp.zeros((bp,) + state_a_ssm.shape[2:], F32)
            xp, buf, s = _gdn_mixer(xp, mod_p, nw[1:2], buf0, s0, *prm, chunk=2 * GDN_CHUNK, **tiles_p)
            outs["a_conv_p"].append(buf)
            outs["a_ssm_p"].append(s)
            xs, buf, s = _gdn_mixer(xs, mod_s, nw[1:2], state_a_conv[j], state_a_ssm[j], *prm, chunk=GDN_CHUNK,
                                    **tiles_s)
            outs["a_conv_s"].append(buf)
            outs["a_ssm_s"].append(s)
        elif kind == 1:
            w_in, w_out = (dsa_w_in16, (j,)), (dsa_w_out16, (j,))
            n_proj = dsa_w_in.shape[2]
            xp, k, v, ki = _dsa_mixer(xp, mod_p, nw[1:2], w_in, w_out, n_proj, nkv=nkv, **tiles_p)
            outs["b_k_p"].append(k)
            outs["b_v_p"].append(v)
            outs["b_i_p"].append(ki)
            xs, k, v, ki = _dsa_mixer(xs, mod_s, nw[1:2], w_in, w_out, n_proj, nkv=nkv,
                                      cache=(cache_b_k, cache_b_v, cache_b_idx, j, page_table), **tiles_s)
            outs["b_k_s"].append(k)
            outs["b_v_s"].append(v)
            outs["b_i_s"].append(ki)
        else:
            prm = ((lru_w_in16, (j,)), lru_conv_w[j], lru_conv_b[j], lru_w_gate_a[j].astype(BF16),
                   lru_b_gate_a[j], lru_w_gate_x[j].astype(BF16), lru_b_gate_x[j], lru_lambda[j],
                   (lru_w_out16, (j,)))
            buf0 = jnp.zeros((bp,) + state_c_conv.shape[2:], F32)
            h0 = jnp.zeros((bp,) + state_c_h.shape[2:], F32)
            xp, buf, hl = _lru_mixer(xp, mod_p, nw[1:2], buf0, h0, *prm, **tiles_p)
            outs["c_conv_p"].append(buf)
            outs["c_h_p"].append(hl)
            xs, buf, hl = _lru_mixer(xs, mod_s, nw[1:2], state_c_conv[j], state_c_h[j], *prm, **tiles_s)
            outs["c_conv_s"].append(buf)
            outs["c_h_s"].append(hl)
        xp, xs = ffn(xp, xs, 1)

    modf_p, modf_s = groups(mod_f)
    fnw = final_norm_w.reshape(1, d)
    y_p = _final_norm(xp.reshape(bp * seq, d), modf_p, fnw, tm=tm_p).reshape(bp, seq, d)
    y_s = _final_norm(xs.reshape(bs * ts, d), modf_s, fnw, tm=tm_s).reshape(bs, ts, d)
    st = {k: jnp.stack(v) for k, v in outs.items()}
    return (y_p, y_s, st["a_conv_p"], st["a_conv_s"], st["a_ssm_p"], st["a_ssm_s"], st["b_k_p"], st["b_k_s"],
            st["b_v_p"], st["b_v_s"], st["b_i_p"], st["b_i_s"], st["c_conv_p"], st["c_conv_s"], st["c_h_p"],
            st["c_h_s"])
```

```python
import functools
import math

import jax
import jax.numpy as jnp
import numpy as np
from jax import lax
from jax.experimental import pallas as pl
from jax.experimental.pallas import tpu as pltpu

F32 = jnp.float32
BF16 = jnp.bfloat16
I32 = jnp.int32

N_MIXERS = 3
CONV_W = 4
NORM_EPS = 1e-6
NEG_INF = -1e30
GDN_CHUNK = 64
TOPK_MAX = 256
Q_BLOCK = 128
ROPE_THETA = 10000.0
RG_C = 8.0
LRU_BLOCK = 256
SAMPLE_PAGES_PER_STEP = 8

LANES = 128
SUBLANES = 8
VMEM_LIMIT_BYTES = 56 * 2**20
VT_ROWS = LANES + 16
INT32_MIN = -2**31
INT32_MAX = 2**31 - 1
KEY_OF_NEG_INF = int(np.float32(NEG_INF).view(np.int32)) ^ 0x7FFFFFFF


def _cparams(*sem):
    return pltpu.CompilerParams(dimension_semantics=sem, vmem_limit_bytes=VMEM_LIMIT_BYTES)


def _tile(n, target, align=LANES):
    if n <= target:
        return n
    t = (target // align) * align
    while t >= align:
        if n % t == 0:
            return t
        t -= align
    raise ValueError(f"no {align}-aligned tile of {n} below {target}")


def _sigmoid(x):
    return jax.nn.sigmoid(x)


def _silu(x):
    return x * _sigmoid(x)


def _dot(a, b):
    return jnp.dot(a, b, preferred_element_type=F32)


def _dot_nt(a, b):
    return lax.dot_general(a, b, (((1,), (1,)), ((), ())), preferred_element_type=F32)


def _dot_tn(a, b):
    return lax.dot_general(a, b, (((0,), (0,)), ((), ())), preferred_element_type=F32)


def _dot_f32(a, b):
    return jnp.dot(a, b, preferred_element_type=F32, precision=lax.Precision.HIGHEST)


def _norm_mod(x, nw, sh, sc):
    ms = jnp.mean(x * x, axis=-1, keepdims=True)
    return x * lax.rsqrt(ms + NORM_EPS) * (nw * (1.0 + sc)) + sh


def _ada_kernel(c_ref, w_ref, b_ref, o_ref):
    a = _silu(c_ref[...]).astype(BF16)
    o_ref[0] = _dot(a, w_ref[0].astype(BF16)) + b_ref[0]


def _ada(c_all, w, b):
    n_l, d, n = w.shape
    mp = c_all.shape[0]
    tn = _tile(n, 1024)
    return pl.pallas_call(
        _ada_kernel,
        grid=(n_l, n // tn),
        in_specs=[
            pl.BlockSpec((mp, d), lambda l, j: (0, 0)),
            pl.BlockSpec((1, d, tn), lambda l, j: (l, 0, j)),
            pl.BlockSpec((1, 1, tn), lambda l, j: (l, 0, j)),
        ],
        out_specs=pl.BlockSpec((1, mp, tn), lambda l, j: (l, 0, j)),
        out_shape=jax.ShapeDtypeStruct((n_l, mp, n), F32),
        name="ada_mod",
        compiler_params=_cparams("parallel", "parallel"),
    )(c_all, w, b.reshape(n_l, 1, n))


def _mod_spec(mod3, k, d, tm, rows_per_group):
    return pl.BlockSpec((None, mod3.shape[1], d), lambda i, j: ((i * tm) // rows_per_group, 0, k))


def _wspec(w, block, imap):
    arr, idx = w
    return pl.BlockSpec((None,) * len(idx) + block, lambda i, j: idx + imap(i, j))


def _wshape(w):
    return w[0].shape[len(w[1]):]


def _normproj_kernel(x_ref, nw_ref, sh_ref, sc_ref, w_ref, o_ref, xn_ref):
    @pl.when(pl.program_id(1) == 0)
    def _():
        xn_ref[...] = _norm_mod(x_ref[...], nw_ref[...], sh_ref[...], sc_ref[...]).astype(BF16)

    o_ref[...] = _dot(xn_ref[...], w_ref[...])


def _normproj(x, mod3, k0, nw, w, *, tm, tn_target=1024):
    m, d = x.shape
    n = _wshape(w)[1]
    tn = _tile(n, tn_target)
    rpg = m // mod3.shape[0]
    ms = lambda k: _mod_spec(mod3, k, d, tm, rpg)
    return pl.pallas_call(
        _normproj_kernel,
        grid=(m // tm, n // tn),
        in_specs=[
            pl.BlockSpec((tm, d), lambda i, j: (i, 0)),
            pl.BlockSpec((1, d), lambda i, j: (0, 0)),
            ms(k0), ms(k0 + 1),
            _wspec(w, (d, tn), lambda i, j: (0, j)),
        ],
        out_specs=pl.BlockSpec((tm, tn), lambda i, j: (i, j)),
        out_shape=jax.ShapeDtypeStruct((m, n), F32),
        scratch_shapes=[pltpu.VMEM((tm, d), BF16)],
        name="normproj",
        compiler_params=_cparams("parallel", "arbitrary"),
    )(x, nw, mod3, mod3, w[0])


def _ffn_up_kernel(x_ref, nw_ref, sh_ref, sc_ref, xs_ref, shs_ref, scs_ref, wa_ref, wb_ref, h_ref, hs_ref,
                   xn_ref, xsn_ref):
    i, j = pl.program_id(0), pl.program_id(1)

    @pl.when(j == 0)
    def _():
        xn_ref[...] = _norm_mod(x_ref[...], nw_ref[...], sh_ref[...], sc_ref[...]).astype(BF16)

    @pl.when((i == 0) & (j == 0))
    def _():
        xsn_ref[...] = _norm_mod(xs_ref[...], nw_ref[...], shs_ref[...], scs_ref[...]).astype(BF16)

    wa, wb = wa_ref[...].astype(BF16), wb_ref[...].astype(BF16)
    swiglu = lambda xn: (_silu(_dot(xn, wa)) * _dot(xn, wb)).astype(BF16)
    h_ref[...] = swiglu(xn_ref[...])

    @pl.when(i == 0)
    def _():
        hs_ref[...] = swiglu(xsn_ref[...])

    @pl.when(i > 0)
    def _():
        hs_ref[...] = jnp.zeros_like(hs_ref)


def _ffn_up(x, mod3, xs, mods3, k0, nw, w_in, *, tm, tf_target=512):
    m, d = x.shape
    msr = xs.shape[0]
    f = _wshape(w_in)[1] // 2
    tf = _tile(f, tf_target)
    nf = f // tf
    rpg = m // mod3.shape[0]
    ms = lambda k: _mod_spec(mod3, k, d, tm, rpg)
    mss = lambda k: pl.BlockSpec((None, msr, d), lambda i, j: (0, 0, k))
    return pl.pallas_call(
        _ffn_up_kernel,
        grid=(m // tm, nf),
        in_specs=[
            pl.BlockSpec((tm, d), lambda i, j: (i, 0)),
            pl.BlockSpec((1, d), lambda i, j: (0, 0)),
            ms(k0), ms(k0 + 1),
            pl.BlockSpec((msr, d), lambda i, j: (0, 0)),
            mss(k0), mss(k0 + 1),
            _wspec(w_in, (d, tf), lambda i, j: (0, j)),
            _wspec(w_in, (d, tf), lambda i, j: (0, j + nf)),
        ],
        out_specs=[pl.BlockSpec((tm, tf), lambda i, j: (i, j)),
                   pl.BlockSpec((msr, tf), lambda i, j: (i, j))],
        out_shape=[jax.ShapeDtypeStruct((m, f), BF16), jax.ShapeDtypeStruct((m // tm * msr, f), BF16)],
        scratch_shapes=[pltpu.VMEM((tm, d), BF16), pltpu.VMEM((msr, d), BF16)],
        name="ffn_up",
        compiler_params=_cparams("arbitrary", "arbitrary"),
    )(x, nw, mod3, mod3, xs, mods3, mods3, w_in[0], w_in[0])


def _outproj_kernel(a_ref, w_ref, x_ref, g_ref, o_ref):
    o_ref[...] = x_ref[...] + g_ref[...] * _dot(a_ref[...], w_ref[...])


def _outproj(a, w, x, mod3, kg, *, tm, tn_target=1024):
    m, kdim = a.shape
    d = _wshape(w)[1]
    tn = _tile(d, tn_target)
    rpg = m // mod3.shape[0]
    r = mod3.shape[1]
    nd = d // tn
    return pl.pallas_call(
        _outproj_kernel,
        grid=(m // tm, nd),
        in_specs=[
            pl.BlockSpec((tm, kdim), lambda i, j: (i, 0)),
            _wspec(w, (kdim, tn), lambda i, j: (0, j)),
            pl.BlockSpec((tm, tn), lambda i, j: (i, j)),
            pl.BlockSpec((None, r, tn), lambda i, j: ((i * tm) // rpg, 0, kg * nd + j)),
        ],
        out_specs=pl.BlockSpec((tm, tn), lambda i, j: (i, j)),
        out_shape=jax.ShapeDtypeStruct((m, d), F32),
        name="outproj",
        compiler_params=_cparams("parallel", "arbitrary"),
    )(a, w[0], x, mod3)


def _ffn_down_kernel(a_ref, w_ref, x_ref, g_ref, as_ref, xs_ref, gs_ref, o_ref, os_ref):
    w = w_ref[...]
    o_ref[...] = x_ref[...] + 0.5 * g_ref[...] * _dot(a_ref[...], w)

    @pl.when(pl.program_id(0) == 0)
    def _():
        os_ref[...] = xs_ref[...] + 0.5 * gs_ref[...] * _dot(as_ref[...], w)

    @pl.when(pl.program_id(0) > 0)
    def _():
        os_ref[...] = jnp.zeros_like(os_ref)


def _ffn_down(a, a_s, w, x, mod3, xs, mods3, kg, *, tm, tn_target=512):
    m, kdim = a.shape
    msr = xs.shape[0]
    d = _wshape(w)[1]
    tn = _tile(d, tn_target)
    rpg = m // mod3.shape[0]
    nd = d // tn
    o, o_s = pl.pallas_call(
        _ffn_down_kernel,
        grid=(m // tm, nd),
        in_specs=[
            pl.BlockSpec((tm, kdim), lambda i, j: (i, 0)),
            _wspec(w, (kdim, tn), lambda i, j: (0, j)),
            pl.BlockSpec((tm, tn), lambda i, j: (i, j)),
            pl.BlockSpec((None, mod3.shape[1], tn), lambda i, j: ((i * tm) // rpg, 0, kg * nd + j)),
            pl.BlockSpec((msr, kdim), lambda i, j: (0, 0)),
            pl.BlockSpec((msr, tn), lambda i, j: (0, j)),
            pl.BlockSpec((None, msr, tn), lambda i, j: (0, 0, kg * nd + j)),
        ],
        out_specs=[pl.BlockSpec((tm, tn), lambda i, j: (i, j)),
                   pl.BlockSpec((msr, tn), lambda i, j: (i, j))],
        out_shape=[jax.ShapeDtypeStruct((m, d), F32), jax.ShapeDtypeStruct((m // tm * msr, d), F32)],
        name="ffn_down",
        compiler_params=_cparams("arbitrary", "arbitrary"),
    )(a, w[0], x, mod3, a_s, xs, mods3)
    return o, o_s[:msr]


def _final_kernel(x_ref, nw_ref, sh_ref, sc_ref, o_ref):
    o_ref[...] = _norm_mod(x_ref[...], nw_ref[...], sh_ref[...], sc_ref[...])


def _final_norm(x, mod3, nw, *, tm):
    m, d = x.shape
    rpg = m // mod3.shape[0]
    r = mod3.shape[1]
    ms = lambda k: pl.BlockSpec((None, r, d), lambda i: ((i * tm) // rpg, 0, k))
    return pl.pallas_call(
        _final_kernel,
        grid=(m // tm,),
        in_specs=[pl.BlockSpec((tm, d), lambda i: (i, 0)), pl.BlockSpec((1, d), lambda i: (0, 0)), ms(0), ms(1)],
        out_specs=pl.BlockSpec((tm, d), lambda i: (i, 0)),
        out_shape=jax.ShapeDtypeStruct((m, d), F32),
        name="final_norm",
        compiler_params=_cparams("parallel"),
    )(x, nw, mod3, mod3)


def _causal_conv(x, halo, w):
    def taps(rows, fix):
        acc = rows * w[CONV_W - 1:CONV_W]
        for j in range(1, CONV_W):
            acc = acc + fix(pltpu.roll(rows, j, 0), j) * w[CONV_W - 1 - j:CONV_W - j]
        return acc

    row = lax.broadcasted_iota(I32, halo.shape, 0)
    head = taps(x[:SUBLANES], lambda r, j: jnp.where(row < j, pltpu.roll(halo, j, 0), r))
    if x.shape[0] == SUBLANES:
        return head
    return jnp.concatenate([head, taps(x, lambda r, j: r)[SUBLANES:]], axis=0)


def _conv_specs(tt, tc, coff, boff):
    sub = tt // SUBLANES
    return [
        pl.BlockSpec((None, tt, tc), lambda b, t, c: (b, t, c + coff)),
        pl.BlockSpec((None, SUBLANES, tc), lambda b, t, c: (b, jnp.maximum(t * sub - 1, 0), c + coff)),
        pl.BlockSpec((None, SUBLANES, tc), lambda b, t, c: (b, 0, c + boff)),
    ]


def _pad_buf(buf):
    return jnp.pad(buf, ((0, 0), (SUBLANES - (CONV_W - 1), 0), (0, 0)))


def _gdn_prep_kernel(x_ref, halo_ref, buf_ref, w_ref, o_ref, *, norm):
    halo = jnp.where(pl.program_id(1) == 0, buf_ref[...], halo_ref[...])
    y = _silu(_causal_conv(x_ref[...], halo, w_ref[...]))
    if norm:
        for h in range(y.shape[1] // LANES):
            seg = y[:, h * LANES:(h + 1) * LANES]
            ss = jnp.sum(seg * seg, axis=-1, keepdims=True)
            o_ref[:, h * LANES:(h + 1) * LANES] = seg * lax.rsqrt(ss + NORM_EPS)
    else:
        o_ref[...] = y


def _gdn_prep(proj, buf8, conv_w, *, col0, ncols, norm, tt):
    bsz, t, _ = proj.shape
    tc = _tile(ncols, 1024)
    return pl.pallas_call(
        functools.partial(_gdn_prep_kernel, norm=norm),
        grid=(bsz, t // tt, ncols // tc),
        in_specs=_conv_specs(tt, tc, col0 // tc, col0 // tc)
        + [pl.BlockSpec((CONV_W, tc), lambda b, i, c: (0, c + col0 // tc))],
        out_specs=pl.BlockSpec((None, tt, tc), lambda b, i, c: (b, i, c)),
        out_shape=jax.ShapeDtypeStruct((bsz, t, ncols), F32),
        name="gdn_prep",
        compiler_params=_cparams("parallel", "parallel", "parallel"),
    )(proj, proj, buf8, conv_w)


def _gdn_gate_kernel(x_ref, alog_ref, dtb_ref, beta_ref, gc_ref, *, hv, chunk):
    x = x_ref[...]
    tt = x.shape[0]
    beta_ref[...] = _sigmoid(x)
    z = x + dtb_ref[...]
    g = -jnp.exp(alog_ref[...]) * (jnp.maximum(z, 0.0) + jnp.log1p(jnp.exp(-jnp.abs(z))))
    row = lax.broadcasted_iota(I32, (tt, tt), 0)
    col = lax.broadcasted_iota(I32, (tt, tt), 1)
    tri = jnp.where((row >= col) & (row // chunk == col // chunk), 1.0, 0.0)
    gc_ref[...] = _dot_f32(tri, g)


def _gdn_gates(proj, a_log, dt_bias, *, col0, hv, chunk, tt):
    bsz, t, _ = proj.shape
    pad = lambda v: jnp.pad(v.astype(F32), (hv, LANES - 2 * hv)).reshape(1, LANES)
    blk = pl.BlockSpec((None, tt, LANES), lambda b, i: (b, i, col0 // LANES))
    out = pl.BlockSpec((None, tt, LANES), lambda b, i: (b, i, 0))
    par = pl.BlockSpec((1, LANES), lambda b, i: (0, 0))
    return pl.pallas_call(
        functools.partial(_gdn_gate_kernel, hv=hv, chunk=chunk),
        grid=(bsz, t // tt),
        in_specs=[blk, par, par],
        out_specs=[out, out],
        out_shape=[jax.ShapeDtypeStruct((bsz, t, LANES), F32)] * 2,
        name="gdn_gates",
        compiler_params=_cparams("parallel", "parallel"),
    )(proj, pad(a_log), pad(dt_bias))


def _split_bf16(a):
    hi = a.astype(BF16)
    return hi, (a - hi.astype(F32)).astype(BF16)


def _dot_split(a, b):
    ah, al = _split_bf16(a)
    bh, bl = _split_bf16(b)
    return _dot(jnp.concatenate([ah, al, ah], axis=1), jnp.concatenate([bh, bh, bl], axis=0))


def _dot_bf16(a, b):
    return _dot(a.astype(BF16), b.astype(BF16))


def _tri_inv_all(lmats, c, n_real):
    row = lax.broadcasted_iota(I32, (c, c), 0)
    col = lax.broadcasted_iota(I32, (c, c), 1)
    eye = jnp.where(row == col, 1.0, 0.0)
    base = min(16, c)
    ps = [-jnp.where(row // base == col // base, m, 0.0) for m in lmats]
    rs = [eye + p for p in ps]
    n = 2
    while n < base:
        ps = [_dot_bf16(p, p) for p in ps]
        rs = [r + _dot_bf16(r, p) for r, p in zip(rs, ps)]
        n *= 2
    s = base
    while s < min(c, n_real):
        off = (row // (2 * s) == col // (2 * s)) & (row // s != col // s)
        ts = [_dot_bf16(jnp.where(off, m, 0.0), r) for m, r in zip(lmats, rs)]
        rs = [r - _dot_bf16(r, t) for r, t in zip(rs, ts)]
        s *= 2
    res = [eye - r - _dot_split(m, r) for m, r in zip(lmats, rs)]
    return [r + _dot_bf16(r, e) for r, e in zip(rs, res)]


def _gdn_core_kernel(q_ref, k_ref, v_ref, z_ref, gcc_ref, gcr_ref, bc_ref, s0_ref, nw_ref, o_ref, so_ref, s_ref,
                     *, rep, c, ncb, hpb, dk, n_real):
    ci = pl.program_id(2)

    @pl.when(ci == 0)
    def _():
        s_ref[...] = s0_ref[...]

    row = lax.broadcasted_iota(I32, (c, c), 0)
    col = lax.broadcasted_iota(I32, (c, c), 1)
    causal = row >= col
    nhd = hpb * rep
    kinst = [(n, hh) for n in range(ncb) for hh in range(hpb)]
    inst = [(n, hh, r) for n in range(ncb) for hh in range(hpb) for r in range(rep)]
    rows = lambda n: slice(n * c, (n + 1) * c)
    lanes = lambda hh, r: slice((hh * rep + r) * LANES, (hh * rep + r + 1) * LANES)
    kidx = lambda n, hh: n * hpb + hh
    qs = [q_ref[rows(n), hh * LANES:(hh + 1) * LANES] * dk ** -0.5 for n, hh in kinst]
    ks = [k_ref[rows(n), hh * LANES:(hh + 1) * LANES] for n, hh in kinst]
    k16 = [k.astype(BF16) for k in ks]
    grams = [_dot_nt(kb, kb) for kb in k16]
    qk0s = [_dot_nt(q.astype(BF16), kb) for q, kb in zip(qs, k16)]
    gccs = [gcc_ref[hh, rows(n), r:r + 1] for n, hh, r in inst]
    gcrs = [gcr_ref[hh, r:r + 1, rows(n)] for n, hh, r in inst]
    betas = [bc_ref[hh, rows(n), r:r + 1] for n, hh, r in inst]
    decays = [jnp.where(causal, jnp.exp(jnp.where(causal, gc - gr, 0.0)), 0.0) for gc, gr in zip(gccs, gcrs)]
    lowers = [jnp.where(row > col, grams[kidx(n, hh)] * b * d, 0.0)
              for (n, hh, _), b, d in zip(inst, betas, decays)]
    tinvs = _tri_inv_all(lowers, c, n_real)
    egcs = [jnp.exp(gc) for gc in gccs]
    sols = [_dot_split(ti, jnp.concatenate([v_ref[rows(n), lanes(hh, r)] * b, ks[kidx(n, hh)] * (b * e)], axis=1))
            for ti, b, e, (n, hh, r) in zip(tinvs, betas, egcs, inst)]
    qg16 = [(qs[kidx(n, hh)] * e).astype(BF16) for (n, hh, _), e in zip(inst, egcs)]
    qk16 = [(qk0s[kidx(n, hh)] * d).astype(BF16) for (n, hh, _), d in zip(inst, decays)]
    g_last = [gc[c - 1:c, :] for gc in gccs]
    kd16 = [(ks[kidx(n, hh)] * jnp.exp(gl - gc)).astype(BF16) for (n, hh, _), gl, gc in zip(inst, g_last, gccs)]
    ss = [s_ref[i] for i in range(nhd)]
    for n in range(ncb):
        ids = range(n * nhd, (n + 1) * nhd)
        s16 = [s.astype(BF16) for s in ss]
        u16 = [(sols[i][:, :LANES] - _dot(sols[i][:, LANES:].astype(BF16), sb)).astype(BF16)
               for i, sb in zip(ids, s16)]
        ss = [s * jnp.exp(g_last[i]) + _dot_tn(kd16[i], ub) for s, i, ub in zip(ss, ids, u16)]
        os_ = [_dot(qg16[i], sb) + _dot(qk16[i], ub) for i, sb, ub in zip(ids, s16, u16)]
        for o, i in zip(os_, ids):
            _, hh, r = inst[i]
            on = o * lax.rsqrt(jnp.mean(o * o, axis=-1, keepdims=True) + NORM_EPS) * nw_ref[...]
            o_ref[rows(n), lanes(hh, r)] = (on * _silu(z_ref[rows(n), lanes(hh, r)])).astype(BF16)
    for i in range(nhd):
        s_ref[i] = ss[i]

    @pl.when(ci == pl.num_programs(2) - 1)
    def _():
        so_ref[...] = s_ref[...]


def _gdn_core(qk, v, proj, zcol0, gc, beta, s0, norm_w, *, c, ncb, hpb, n_real):
    bsz, t, val = v.shape
    hv = s0.shape[1]
    dk, dv = s0.shape[2], s0.shape[3]
    hk = qk.shape[2] // (2 * dk)
    rep = hv // hk
    tb = ncb * c
    assert dk == LANES and dv == LANES and t % tb == 0 and hk % hpb == 0 and zcol0 % (hpb * rep * dv) == 0
    heads = lambda a, lo: a[:, :, lo:lo + hv].reshape(bsz, t, hk, rep).transpose(0, 2, 1, 3)
    gcc = heads(gc, hv)
    bcc = heads(beta, 0)
    gcr = gcc.transpose(0, 1, 3, 2)
    nhb = hk // hpb
    zb = zcol0 // (hpb * rep * dv)
    colspec = pl.BlockSpec((None, hpb, tb, rep), lambda b, h, i: (b, h, i, 0))
    o, s_out = pl.pallas_call(
        functools.partial(_gdn_core_kernel, rep=rep, c=c, ncb=ncb, hpb=hpb, dk=dk, n_real=n_real),
        grid=(bsz, nhb, t // tb),
        in_specs=[
            pl.BlockSpec((None, tb, hpb * dk), lambda b, h, i: (b, i, h)),
            pl.BlockSpec((None, tb, hpb * dk), lambda b, h, i: (b, i, nhb + h)),
            pl.BlockSpec((None, tb, hpb * rep * dv), lambda b, h, i: (b, i, h)),
            pl.BlockSpec((None, tb, hpb * rep * dv), lambda b, h, i: (b, i, zb + h)),
            colspec,
            pl.BlockSpec((None, hpb, rep, tb), lambda b, h, i: (b, h, 0, i)),
            colspec,
            pl.BlockSpec((None, hpb * rep, dk, dv), lambda b, h, i: (b, h, 0, 0)),
            pl.BlockSpec((1, dv), lambda b, h, i: (0, 0)),
        ],
        out_specs=[
            pl.BlockSpec((None, tb, hpb * rep * dv), lambda b, h, i: (b, i, h)),
            pl.BlockSpec((None, hpb * rep, dk, dv), lambda b, h, i: (b, h, 0, 0)),
        ],
        out_shape=[jax.ShapeDtypeStruct((bsz, t, val), BF16), jax.ShapeDtypeStruct(s0.shape, F32)],
        scratch_shapes=[pltpu.VMEM((hpb * rep, dk, dv), F32)],
        name="gdn_core",
        compiler_params=_cparams("parallel", "parallel", "arbitrary"),
    )(qk, qk, v, proj, gcc, gcr, bcc, s0, norm_w.reshape(1, dv))
    return o, s_out


def _gdn_mixer(x, mod3, nw, buf, s0, w_in, conv_w, a_log, dt_bias, norm_w, w_out, *, tm_in, tt, chunk):
    bsz, t, d = x.shape
    hv, dk, dv = s0.shape[1], s0.shape[2], s0.shape[3]
    val = hv * dv
    conv_dim = conv_w.shape[1]
    key = (conv_dim - val) // 2
    assert (conv_dim + val) % LANES == 0 and 2 * hv <= LANES and t >= CONV_W - 1
    x2 = x.reshape(bsz * t, d)
    proj = _normproj(x2, mod3, 3, nw, w_in, tm=tm_in, tn_target=512).reshape(bsz, t, -1)
    buf8 = _pad_buf(buf)
    tt_prep = _tile(t, 2 * tt, SUBLANES)
    qk = _gdn_prep(proj, buf8, conv_w, col0=0, ncols=2 * key, norm=True, tt=tt_prep)
    v = _gdn_prep(proj, buf8, conv_w, col0=2 * key, ncols=val, norm=False, tt=tt_prep)
    tp = -(-t // chunk) * chunk
    beta, gc = _gdn_gates(proj, a_log, dt_bias, col0=conv_dim + val, hv=hv, chunk=min(chunk, tt), tt=tt)
    if tp != t:
        padt = lambda a: jnp.pad(a, ((0, 0), (0, tp - t), (0, 0)))
        gc = jnp.concatenate([gc, jnp.broadcast_to(gc[:, -1:], (bsz, tp - t, LANES))], axis=1)
        qk, v, beta, projz = padt(qk), padt(v), padt(beta), padt(proj)
    else:
        projz = proj
    o, s_new = _gdn_core(qk, v, projz, conv_dim, gc, beta, s0, norm_w, c=chunk, ncb=math.gcd(tp // chunk, 2),
                         hpb=min(4 if tp > chunk else 16, key // dk), n_real=min(t, chunk))
    o2 = o[:, :t].reshape(bsz * t, val)
    xo = _outproj(o2, w_out, x2, mod3, 5, tm=tm_in).reshape(bsz, t, d)
    new_buf = proj[:, t - (CONV_W - 1):, :conv_dim]
    return xo, new_buf, s_new


def _rope_tables(pos, half):
    inv_freq = ROPE_THETA ** (-jnp.arange(half, dtype=F32) / half)
    ang = pos.astype(F32)[:, None] * inv_freq[None, :]
    cos, sin = jnp.cos(ang), jnp.sin(ang)
    return jnp.concatenate([cos, cos], axis=-1), jnp.concatenate([-sin, sin], axis=-1)


def _dsa_prep_kernel(x_ref, cos_ref, sin_ref, q_ref, k_ref, k16_ref, v16_ref, qi_ref, ki_ref, ki16_ref, *tr_refs,
                     nh, nkv, nih):
    cos, sin = cos_ref[...], sin_ref[...]

    def rope(col):
        seg = x_ref[:, col * LANES:(col + 1) * LANES]
        return seg * cos + pltpu.roll(seg, LANES // 2, 1) * sin

    for h in range(nh):
        q_ref[:, h * LANES:(h + 1) * LANES] = (rope(h) * LANES ** -0.5).astype(BF16)
    for h in range(nkv):
        kr = rope(nh + h)
        k_ref[:, h * LANES:(h + 1) * LANES] = kr
        k16_ref[:, h * LANES:(h + 1) * LANES] = kr.astype(BF16)
    v0 = (nh + nkv) * LANES
    v16_ref[...] = x_ref[:, v0:v0 + nkv * LANES].astype(BF16)
    c0 = nh + 2 * nkv
    for h in range(nih):
        qi_ref[:, h * LANES:(h + 1) * LANES] = rope(c0 + h).astype(BF16)
    kir = rope(c0 + nih)
    ki_ref[...] = kir
    ki16_ref[...] = kir.astype(BF16)
    if tr_refs:
        wt_ref, vt_ref = tr_refs
        w0 = (c0 + nih + 1) * LANES
        wt_ref[...] = x_ref[:, w0:w0 + LANES].T[:wt_ref.shape[0], :]
        for h in range(nkv):
            vt_ref[h * VT_ROWS:h * VT_ROWS + LANES, :] = x_ref[:, v0 + h * LANES:v0 + (h + 1) * LANES].T.astype(BF16)
            vt_ref[h * VT_ROWS + LANES:(h + 1) * VT_ROWS, :] = jnp.ones((VT_ROWS - LANES, vt_ref.shape[1]), BF16)


def _dsa_prep(proj, pos, *, nh, nkv, nih, tt, with_wt):
    bsz, t, npj = proj.shape
    cos, sin = _rope_tables(pos, LANES // 2)
    row = lambda n, dt: jax.ShapeDtypeStruct((bsz, t, n * LANES), dt)
    ospec = lambda n: pl.BlockSpec((None, tt, n * LANES), lambda b, i: (b, i, 0))
    tab = pl.BlockSpec((tt, LANES), lambda b, i: (i, 0))
    nwt = -(-nih // SUBLANES) * SUBLANES
    return pl.pallas_call(
        functools.partial(_dsa_prep_kernel, nh=nh, nkv=nkv, nih=nih),
        grid=(bsz, t // tt),
        in_specs=[pl.BlockSpec((None, tt, npj), lambda b, i: (b, i, 0)), tab, tab],
        out_specs=[ospec(nh), ospec(nkv), ospec(nkv), ospec(nkv), ospec(nih), ospec(1), ospec(1)]
        + ([pl.BlockSpec((None, nwt, tt), lambda b, i: (b, 0, i)),
            pl.BlockSpec((None, None, nkv * VT_ROWS, tt), lambda b, i: (b, i, 0, 0))] if with_wt else []),
        out_shape=[row(nh, BF16), row(nkv, F32), row(nkv, BF16), row(nkv, BF16), row(nih, BF16), row(1, F32),
                   row(1, BF16)]
        + ([jax.ShapeDtypeStruct((bsz, nwt, t), F32),
            jax.ShapeDtypeStruct((bsz, t // tt, nkv * VT_ROWS, tt), BF16)] if with_wt else []),
        name="dsa_prep",
        compiler_params=_cparams("parallel", "parallel"),
    )(proj, cos, sin)


def _sort_key(s):
    bits = pltpu.bitcast(jnp.where(s == 0.0, 0.0, s), I32)
    return jnp.where(bits < 0, bits ^ 0x7FFFFFFF, bits)


def _topk_cut(count_where, shape, topk, idx_bits):
    def body(i, carry):
        t, n_t = carry
        cand = t + lax.shift_left(jnp.int32(1), 31 - i)
        n = count_where(lambda key, idx: key >= cand)
        ok = n >= topk
        return jnp.where(ok, cand, t), jnp.where(ok, n, n_t)

    thr, n_ge = lax.fori_loop(0, 32, body, (jnp.full(shape, INT32_MIN, I32), jnp.full(shape, INT32_MAX, I32)))
    tie = (n_ge > topk) & (thr > KEY_OF_NEG_INF)

    def cut():
        need = topk - count_where(lambda key, idx: key > thr)

        def jbody(i, j):
            cand = j + lax.shift_left(jnp.int32(1), idx_bits - 1 - i)
            below = count_where(lambda key, idx: (key == thr) & (idx < cand))
            return jnp.where(below < need, cand, j)

        return jnp.where(tie, lax.fori_loop(0, idx_bits, jbody, jnp.zeros(shape, I32)), INT32_MAX)

    jcut = lax.cond(jnp.any(tie), cut, lambda: jnp.full(shape, INT32_MAX, I32))
    return thr, jcut


def _topk_chosen(key, idx, thr, jcut):
    return (key > thr) | ((key == thr) & (idx <= jcut))


def _dsa_prompt_kernel(qi_ref, wt_ref, q_ref, ki_ref, k_ref, vt_ref, o_ref, key_ref, bias_ref, acc_ref,
                       *, nh, nkv, nih, topk, idx_scale):
    qb = pl.program_id(1)
    blk = Q_BLOCK
    kb = vt_ref.shape[2]
    ktiles = kb // blk
    nkb = (qb + ktiles) // ktiles
    rowk = lax.broadcasted_iota(I32, (kb, blk), 0)
    colq = lax.broadcasted_iota(I32, (kb, blk), 1)
    wt = wt_ref[...]
    hq = max(1, nih // 4)
    qis = [jnp.concatenate([qi_ref[:, h * LANES:(h + 1) * LANES] for h in range(h0, min(h0 + hq, nih))], axis=0)
           for h0 in range(0, nih, hq)]

    def rows_of(j):
        return pl.ds(pl.multiple_of(j * kb, kb), kb)

    def visible(j):
        return j * kb + rowk <= qb * blk + colq

    def score_body(j, carry):
        keys = ki_ref[rows_of(j), :]
        lgs = [_dot_nt(keys, qi) for qi in qis]
        acc = jnp.zeros((kb, blk), F32)
        for i, lg in enumerate(lgs):
            for hh in range(lg.shape[1] // blk):
                h = i * hq + hh
                acc = acc + jnp.maximum(lg[:, hh * blk:(hh + 1) * blk], 0.0) * wt[h:h + 1, :]
        key_ref[rows_of(j), :] = _sort_key(jnp.where(visible(j), acc * idx_scale, NEG_INF))
        return carry

    lax.fori_loop(0, nkb, score_body, 0)

    def count_where(pred):
        def body(j, c):
            hit = jnp.where(pred(key_ref[rows_of(j), :], j * kb + rowk), 1, 0)
            for i in range(ktiles):
                c = c + hit[i * blk:(i + 1) * blk]
            return c
        cnt = lax.fori_loop(0, nkb, body, jnp.zeros((blk, blk), I32))
        return jnp.sum(cnt, axis=0, keepdims=True)

    thr, jcut = _topk_cut(count_where, (1, blk), topk, (key_ref.shape[0] - 1).bit_length())

    def bias_body(j, carry):
        sel = _topk_chosen(key_ref[rows_of(j), :], j * kb + rowk, thr, jcut) & visible(j)
        bias_ref[rows_of(j), :] = jnp.where(sel, 0.0, NEG_INF)
        return carry

    lax.fori_loop(0, nkb, bias_body, 0)

    rep = nh // nkv
    qgs =[jnp.concatenate([q_ref[:, (g * rep + r) * LANES:(g * rep + r + 1) * LANES] for r in range(rep)], axis=0)
           for g in range(nkv)]
    acc_ref[...] = jnp.zeros_like(acc_ref)

    def att_body(j, carry):
        ms, ls = carry
        rows = rows_of(j)
        bias = jnp.concatenate([bias_ref[rows, :]] * rep, axis=1)
        ss = [_dot_nt(k_ref[rows, g * LANES:(g + 1) * LANES], qgs[g]) + bias for g in range(nkv)]
        m_new = [jnp.maximum(m, jnp.max(s, axis=0, keepdims=True)) for m, s in zip(ms, ss)]
        ps = [jnp.exp((s - m).astype(BF16)) for s, m in zip(ss, m_new)]
        pvs = [_dot(vt_ref[j, g * VT_ROWS:(g + 1) * VT_ROWS, :], p) for g, p in enumerate(ps)]
        alphas = [jnp.exp(m - mn) for m, mn in zip(ms, m_new)]
        for g in range(nkv):
            acc_ref[g] = alphas[g] * acc_ref[g] + pvs[g][:LANES]
        l_new = [a * l + pv[LANES:LANES + 1] for a, l, pv in zip(alphas, ls, pvs)]
        return tuple(m_new), tuple(l_new)

    row0 = lambda v: tuple(jnp.full((1, rep * blk), v, F32) for _ in range(nkv))
    _, ls = lax.fori_loop(0, nkb, att_body, (row0(NEG_INF), row0(0.0)))
    for g in range(nkv):
        og = acc_ref[g] / ls[g]
        for r in range(rep):
            h = g * rep + r
            o_ref[:, h * LANES:(h + 1) * LANES] = og[:, r * blk:(r + 1) * blk].T.astype(BF16)


def _dsa_prompt_attend(q16, qi16, wt, ki16, k16, vt16, *, nh, nkv, nih, topk):
    bsz, t, _ = q16.shape
    rep = nh // nkv
    full = lambda n: pl.BlockSpec((None, t, n * LANES), lambda b, i: (b, 0, 0))
    blk = lambda n: pl.BlockSpec((None, Q_BLOCK, n * LANES), lambda b, i: (b, i, 0))
    return pl.pallas_call(
        functools.partial(_dsa_prompt_kernel, nh=nh, nkv=nkv, nih=nih, topk=topk,
                          idx_scale=(LANES * nih) ** -0.5),
        grid=(bsz, t // Q_BLOCK),
        in_specs=[blk(nih), pl.BlockSpec((None, wt.shape[1], Q_BLOCK), lambda b, i: (b, 0, i)), blk(nh),
                  full(1), full(nkv), pl.BlockSpec((None,) + vt16.shape[1:], lambda b, i: (b, 0, 0, 0))],
        out_specs=blk(nh),
        out_shape=jax.ShapeDtypeStruct((bsz, t, nh * LANES), BF16),
        scratch_shapes=[pltpu.VMEM((t, Q_BLOCK), I32), pltpu.VMEM((t, Q_BLOCK), F32),
                        pltpu.VMEM((nkv, LANES, rep * Q_BLOCK), F32)],
        name="dsa_prompt_attend",
        compiler_params=_cparams("parallel", "arbitrary"),
    )(qi16, wt, q16, ki16, k16, vt16)


def _dsa_sample_score_kernel(pt_ref, qi_ref, wc_ref, *refs, n_steps, pps, nih, tq, past, idx_scale):
    page_refs, new_ref, o_ref = refs[:pps], refs[pps], refs[pps + 1]
    p = pl.program_id(1)
    keys = jnp.concatenate([r[...] for r in page_refs], axis=0).astype(BF16)
    keys = jnp.where(p == n_steps - 1, new_ref[...], keys)
    qi = jnp.concatenate([qi_ref[:, h * LANES:(h + 1) * LANES] for h in range(nih)], axis=0)
    w = jnp.maximum(_dot_nt(qi, keys), 0.0) * wc_ref[...]
    acc = w[0:tq]
    for h in range(1, nih):
        acc = acc + w[h * tq:(h + 1) * tq]
    s = p * keys.shape[0] + lax.broadcasted_iota(I32, acc.shape, 1)
    qpos = past + lax.broadcasted_iota(I32, acc.shape, 0)
    o_ref[...] = jnp.where(s <= qpos, acc * idx_scale, NEG_INF)


def _dsa_sample_select_kernel(s_ref, o_ref, *, topk):
    key = _sort_key(s_ref[...])
    idx = lax.broadcasted_iota(I32, key.shape, 1)
    count_where = lambda pred: jnp.sum(jnp.where(pred(key, idx), 1, 0), axis=-1, keepdims=True)
    thr, jcut = _topk_cut(count_where, (key.shape[0], 1), topk, (key.shape[1] - 1).bit_length())
    o_ref[...] = jnp.where(_topk_chosen(key, idx, thr, jcut) & (s_ref[...] > 0.5 * NEG_INF), 1.0, 0.0)


def _dsa_sample_attn_kernel(pt_ref, q_ref, sel_ref, *refs, n_steps, pps, nh, nkv, tq):
    kp_refs, vp_refs = refs[:pps], refs[pps:2 * pps]
    kn_ref, vn_ref, o_ref, m_ref, l_ref, acc_ref = refs[2 * pps:]
    p = pl.program_id(1)
    rep = nh // nkv
    page = sel_ref.shape[1] // pps

    @pl.when(p == 0)
    def _():
        m_ref[...] = jnp.full_like(m_ref, NEG_INF)
        l_ref[...] = jnp.zeros_like(l_ref)
        acc_ref[...] = jnp.zeros_like(acc_ref)

    last = p == n_steps - 1

    def head_rows(page_refs, new_ref, g):
        cached = jnp.concatenate([r[pl.ds(g, page, stride=nkv), :] for r in page_refs], axis=0).astype(BF16)
        return jnp.where(last, new_ref[g], cached)

    rows = rep * tq
    q = jnp.concatenate([q_ref[:, h * LANES:(h + 1) * LANES] for h in range(nh)], axis=0)
    unsel = (sel_ref[...] - 1.0) * -NEG_INF
    s = jnp.concatenate([_dot_nt(q[g * rows:(g + 1) * rows], head_rows(kp_refs, kn_ref, g)) for g in range(nkv)],
                        axis=0) + jnp.concatenate([unsel] * nh, axis=0)
    m = m_ref[...]
    m_new = jnp.maximum(m, jnp.max(s, axis=-1, keepdims=True))
    alpha = jnp.exp(m - m_new)
    pr = jnp.where(s > 0.5 * NEG_INF, jnp.exp(s - m_new), 0.0)
    l_ref[...] = alpha * l_ref[...] + jnp.sum(pr, axis=-1, keepdims=True)
    pv = jnp.concatenate([_dot(pr[g * rows:(g + 1) * rows].astype(BF16), head_rows(vp_refs, vn_ref, g))
                          for g in range(nkv)], axis=0)
    acc_ref[...] = alpha * acc_ref[...] + pv
    m_ref[...] = m_new

    @pl.when(last)
    def _():
        og = acc_ref[...] / l_ref[...]
        for h in range(nh):
            o_ref[:, h * LANES:(h + 1) * LANES] = og[h * tq:(h + 1) * tq].astype(BF16)


def _dsa_sample_attend(q16, qi16, wi, ki16, k16, v16, cache_k, cache_v, cache_i, layer, page_table,
                       *, nh, nkv, nih, topk):
    bsz, tq, _ = q16.shape
    n_pages = page_table.shape[1]
    n_layers, n_pool, page = cache_i.shape[:3]
    pps = math.gcd(n_pages, SAMPLE_PAGES_PER_STEP)
    assert page == LANES and tq == SUBLANES
    past = n_pages * page
    n_steps = n_pages // pps + 1
    ltot = n_steps * pps * page
    wcol = wi.transpose(0, 2, 1).reshape(bsz, nih * tq, 1)
    pidx = lambda i: (lambda b, p, pt: (layer * n_pool + pt[b, jnp.minimum(p * pps + i, n_pages - 1)], 0, 0))
    qspec = lambda n: pl.BlockSpec((None, tq, n * LANES), lambda b, p, pt: (b, 0, 0))
    idx_scale = (LANES * nih) ** -0.5
    cache_i = cache_i.reshape(n_layers * n_pool, page, LANES)
    ki_new = jnp.pad(ki16, ((0, 0), (0, pps * page - tq), (0, 0)))
    scores = pl.pallas_call(
        functools.partial(_dsa_sample_score_kernel, n_steps=n_steps, pps=pps, nih=nih, tq=tq, past=past,
                          idx_scale=idx_scale),
        grid_spec=pltpu.PrefetchScalarGridSpec(
            num_scalar_prefetch=1, grid=(bsz, n_steps),
            in_specs=[qspec(nih), pl.BlockSpec((None, nih * tq, 1), lambda b, p, pt: (b, 0, 0))]
            + [pl.BlockSpec((None, page, LANES), pidx(i)) for i in range(pps)]
            + [pl.BlockSpec((None, pps * page, LANES), lambda b, p, pt: (b, 0, 0))],
            out_specs=pl.BlockSpec((None, tq, pps * page), lambda b, p, pt: (b, 0, p))),
        out_shape=jax.ShapeDtypeStruct((bsz, tq, ltot), F32),
        name="dsa_sample_scores",
        compiler_params=_cparams("parallel", "arbitrary"),
    )(page_table, qi16, wcol, *([cache_i] * pps), ki_new)
    sel = pl.pallas_call(
        functools.partial(_dsa_sample_select_kernel, topk=topk),
        grid=(1,),
        in_specs=[pl.BlockSpec((bsz * tq, ltot), lambda i: (0, 0))],
        out_specs=pl.BlockSpec((bsz * tq, ltot), lambda i: (0, 0)),
        out_shape=jax.ShapeDtypeStruct((bsz * tq, ltot), F32),
        name="dsa_sample_select",
        compiler_params=_cparams("arbitrary"),
    )(scores.reshape(bsz * tq, ltot)).reshape(bsz, tq, ltot)
    prow = page * nkv
    ck = cache_k.reshape(n_layers * n_pool, prow, LANES)
    cv = cache_v.reshape(n_layers * n_pool, prow, LANES)
    new_rows = lambda a: jnp.pad(a.reshape(bsz, tq, nkv, LANES).transpose(0, 2, 1, 3),
                                 ((0, 0), (0, 0), (0, pps * page - tq), (0, 0)))
    kvspecs = [pl.BlockSpec((None, prow, LANES), pidx(i)) for i in range(pps)]
    newspec = pl.BlockSpec((None, nkv, pps * page, LANES), lambda b, p, pt: (b, 0, 0, 0))
    return pl.pallas_call(
        functools.partial(_dsa_sample_attn_kernel, n_steps=n_steps, pps=pps, nh=nh, nkv=nkv, tq=tq),
        grid_spec=pltpu.PrefetchScalarGridSpec(
            num_scalar_prefetch=1, grid=(bsz, n_steps),
            in_specs=[qspec(nh), pl.BlockSpec((None, tq, pps * page), lambda b, p, pt: (b, 0, p))]
            + kvspecs + kvspecs + [newspec, newspec],
            out_specs=qspec(nh),
            scratch_shapes=[pltpu.VMEM((nh * tq, 1), F32), pltpu.VMEM((nh * tq, 1), F32),
                            pltpu.VMEM((nh * tq, LANES), F32)]),
        out_shape=jax.ShapeDtypeStruct((bsz, tq, nh * LANES), BF16),
        name="dsa_sample_attend",
        compiler_params=_cparams("parallel", "arbitrary"),
    )(page_table, q16, sel, *([ck] * pps), *([cv] * pps), new_rows(k16), new_rows(v16))


def _dsa_mixer(x, mod3, nw, w_in, w_out, n_proj, *, nkv, tm_in, tt, cache=None):
    bsz, t, d = x.shape
    nh = _wshape(w_out)[0] // LANES
    nih = (n_proj - (nh + 2 * nkv + 1) * LANES) // (LANES + 1)
    assert (nh + 2 * nkv + nih + 1) * LANES + nih == n_proj and nih <= LANES
    x2 = x.reshape(bsz * t, d)
    proj = _normproj(x2, mod3, 3, nw, w_in, tm=tm_in).reshape(bsz, t, -1)
    v = proj[:, :, (nh + nkv) * LANES:(nh + 2 * nkv) * LANES]
    if cache is None:
        assert t % Q_BLOCK == 0
        q16, k, k16, _, qi16, ki, ki16, wt, vt16 = _dsa_prep(proj, jnp.arange(t), nh=nh, nkv=nkv, nih=nih,
                                                             tt=_tile(t, 4 * Q_BLOCK, Q_BLOCK), with_wt=True)
        o = _dsa_prompt_attend(q16, qi16, wt, ki16, k16, vt16, nh=nh, nkv=nkv, nih=nih, topk=min(TOPK_MAX, t // 4))
    else:
        cache_k, cache_v, cache_i, layer, page_table = cache
        past = page_table.shape[1] * cache_i.shape[2]
        q16, k, k16, v16, qi16, ki, ki16 = _dsa_prep(proj, past + jnp.arange(t), nh=nh, nkv=nkv, nih=nih, tt=tt,
                                                    with_wt=False)
        w0 = (nh + 2 * nkv + nih + 1) * LANES
        o = _dsa_sample_attend(q16, qi16, proj[:, :, w0:w0 + nih], ki16, k16, v16, cache_k, cache_v, cache_i,
                               layer, page_table, nh=nh, nkv=nkv, nih=nih, topk=min(TOPK_MAX, (past + t) // 4))
    xo = _outproj(o.reshape(bsz * t, nh * LANES), w_out, x2, mod3, 5, tm=tm_in).reshape(bsz, t, d)
    return xo, k.reshape(bsz, t, nkv, LANES), v.reshape(bsz, t, nkv, LANES), ki


def _gelu_tanh(x):
    return 0.5 * x * (1.0 + jnp.tanh(math.sqrt(2.0 / math.pi) * (x + 0.044715 * (x * x * x))))


def _lru_kernel(gate_ref, xb_ref, halo_ref, buf_ref, cw_ref, cb_ref, wa_ref, ba_ref, wx_ref, bx_ref, lam_ref,
                h0_ref, y_ref, hl_ref, h_ref, *, nblk):
    ti = pl.program_id(1)

    @pl.when(ti == 0)
    def _():
        h_ref[...] = h0_ref[...]

    halo = jnp.where(ti == 0, buf_ref[...], halo_ref[...])
    xc = _causal_conv(xb_ref[...], halo, cw_ref[...]) + cb_ref[...]
    tt = xc.shape[0]
    xc16 = xc.astype(BF16)
    rs, xs = [], []
    for n in range(nblk):
        blk = xc16[:, n * LRU_BLOCK:(n + 1) * LRU_BLOCK]
        rs.append(_dot(blk, wa_ref[n]))
        xs.append(_dot(blk, wx_ref[n]))
    r = _sigmoid(jnp.concatenate(rs, axis=1) + ba_ref[...])
    ig = _sigmoid(jnp.concatenate(xs, axis=1) + bx_ref[...])
    lam = lam_ref[...]
    softplus_neg = jnp.maximum(-lam, 0.0) + jnp.log1p(jnp.exp(-jnp.abs(lam)))
    log_a = -RG_C * r * softplus_neg
    a = jnp.exp(log_a)
    b = jnp.sqrt(-jnp.tanh(log_a) * (a * a + 1.0)) * (ig * xc)
    sub = lax.broadcasted_iota(I32, a.shape, 0) % SUBLANES
    d = 1
    while d < SUBLANES:
        keep = sub >= d
        a_sh = jnp.where(keep, pltpu.roll(a, d, 0), 1.0)
        b_sh = jnp.where(keep, pltpu.roll(b, d, 0), 0.0)
        b = a * b_sh + b
        a = a * a_sh
        d *= 2
    carry, groups = h_ref[...], []
    for g in range(tt // SUBLANES):
        rows = slice(g * SUBLANES, (g + 1) * SUBLANES)
        groups.append(b[rows] + a[rows] * carry)
        carry = groups[-1][SUBLANES - 1:SUBLANES]
    hs = jnp.concatenate(groups, axis=0)
    h_ref[...] = hs[tt - 1:tt, :]
    y_ref[...] = (hs * _gelu_tanh(gate_ref[...])).astype(BF16)

    @pl.when(ti == pl.num_programs(1) - 1)
    def _():
        hl_ref[...] = hs[tt - 1:tt, :]


def _lru_core(proj, buf8, h0, conv_w, conv_b, w_ga, b_ga, w_gx, b_gx, lam, *, tt):
    bsz, t, w2 = proj.shape
    w = w2 // 2
    nblk = w // LRU_BLOCK
    sub = tt // SUBLANES
    vec = lambda: pl.BlockSpec((1, w), lambda b, i: (0, 0))
    wsp = lambda: pl.BlockSpec((nblk, LRU_BLOCK, LRU_BLOCK), lambda b, i: (0, 0, 0))
    r1 = lambda v: v.reshape(1, w).astype(F32)
    y, hl = pl.pallas_call(
        functools.partial(_lru_kernel, nblk=nblk),
        grid=(bsz, t // tt),
        in_specs=[
            pl.BlockSpec((None, tt, w), lambda b, i: (b, i, 0)),
            pl.BlockSpec((None, tt, w), lambda b, i: (b, i, 1)),
            pl.BlockSpec((None, SUBLANES, w), lambda b, i: (b, jnp.maximum(i * sub - 1, 0), 1)),
            pl.BlockSpec((None, SUBLANES, w), lambda b, i: (b, 0, 0)),
            pl.BlockSpec((CONV_W, w), lambda b, i: (0, 0)),
            vec(), wsp(), vec(), wsp(), vec(), vec(),
            pl.BlockSpec((None, 1, w), lambda b, i: (b, 0, 0)),
        ],
        out_specs=[pl.BlockSpec((None, tt, w), lambda b, i: (b, i, 0)),
                   pl.BlockSpec((None, 1, w), lambda b, i: (b, 0, 0))],
        out_shape=[jax.ShapeDtypeStruct((bsz, t, w), BF16), jax.ShapeDtypeStruct((bsz, 1, w), F32)],
        scratch_shapes=[pltpu.VMEM((1, w), F32)],
        name="lru_core",
        compiler_params=_cparams("parallel", "arbitrary"),
    )(proj, proj, proj, buf8, conv_w, r1(conv_b), w_ga, r1(b_ga), w_gx, r1(b_gx), r1(lam), h0.reshape(bsz, 1, w))
    return y, hl.reshape(bsz, w)


def _lru_mixer(x, mod3, nw, buf, h0, w_in, conv_w, conv_b, w_ga, b_ga, w_gx, b_gx, lam, w_out, *, tm_in, tt):
    bsz, t, d = x.shape
    assert t >= CONV_W - 1
    x2 = x.reshape(bsz * t, d)
    proj = _normproj(x2, mod3, 3, nw, w_in, tm=tm_in).reshape(bsz, t, -1)
    w = proj.shape[2] // 2
    y, hl = _lru_core(proj, _pad_buf(buf), h0, conv_w, conv_b, w_ga, b_ga, w_gx, b_gx, lam, tt=tt)
    xo = _outproj(y.reshape(bsz * t, w), w_out, x2, mod3, 5, tm=tm_in).reshape(bsz, t, d)
    return xo, proj[:, t - (CONV_W - 1):, w:], hl


def _bf16_padded(w, mult):
    n = w.shape[-1]
    return jnp.pad(w, ((0, 0), (0, 0), (0, -n % mult))).astype(BF16)


def kernel(x_prompt, x_sample, state_a_conv, state_a_ssm, cache_b_k, cache_b_v, cache_b_idx, state_c_conv, state_c_h, page_table, c_prompt, c_sample, w_ada, b_ada, norm_w, ffn_w_in, ffn_w_out, gdn_w_in, gdn_conv_w, gdn_a_log, gdn_dt_bias, gdn_norm_w, gdn_w_out, dsa_w_in, dsa_w_out, lru_w_in, lru_conv_w, lru_conv_b, lru_w_gate_a, lru_b_gate_a, lru_w_gate_x, lru_b_gate_x, lru_lambda, lru_w_out, w_ada_final, b_ada_final, final_norm_w):
    bp, seq, d = x_prompt.shape
    bs, ts, _ = x_sample.shape
    depth = w_ada.shape[0]
    nkv = cache_b_k.shape[3]
    tm_p = _tile(seq, 512, SUBLANES)
    tm_in_p = _tile(seq, 1024, SUBLANES)
    tt_p = _tile(seq, 256, SUBLANES)
    tm_s = bs * ts
    tiles_p = dict(tm_in=tm_in_p, tt=tt_p)
    tiles_s = dict(tm_in=tm_s, tt=ts)

    c_all = jnp.concatenate([c_prompt, c_sample], axis=0)
    c_all = jnp.pad(c_all, ((0, -c_all.shape[0] % SUBLANES), (0, 0)))
    mod = _ada(c_all, w_ada, b_ada)
    mod_f = _ada(c_all, w_ada_final[None], b_ada_final[None])[0]

    def groups(m):
        return m[:bp, None, :], jnp.repeat(m[bp:bp + bs], ts, axis=0)[None]

    ffn_w_out16 = ffn_w_out.astype(BF16)
    gdn_w_in16, gdn_w_out16 = _bf16_padded(gdn_w_in, 512), gdn_w_out.astype(BF16)
    dsa_w_in16, dsa_w_out16 = _bf16_padded(dsa_w_in, 512), dsa_w_out.astype(BF16)
    lru_w_in16, lru_w_out16 = lru_w_in.astype(BF16), lru_w_out.astype(BF16)

    xp, xs = x_prompt, x_sample
    outs = {k: [] for k in ("a_conv_p", "a_conv_s", "a_ssm_p", "a_ssm_s", "b_k_p", "b_k_s", "b_v_p", "b_v_s",
                            "b_i_p", "b_i_s", "c_conv_p", "c_conv_s", "c_h_p", "c_h_s")}
    for layer in range(depth):
        kind, j = layer % N_MIXERS, layer // N_MIXERS
        mod_p, mod_s = groups(mod[layer])
        nw = norm_w[layer]

        def ffn(xp, xs, which):
            xp2, xs2 = xp.reshape(bp * seq, d), xs.reshape(tm_s, d)
            w_out = (ffn_w_out16, (layer, which))
            hp, hs = _ffn_up(xp2, mod_p, xs2, mod_s, 6 * which, nw[2 * which:2 * which + 1],
                             (ffn_w_in, (layer, which)), tm=tm_in_p)
            op, os_ = _ffn_down(hp, hs, w_out, xp2, mod_p, xs2, mod_s, 6 * which + 2, tm=tm_in_p)
            return op.reshape(xp.shape), os_.reshape(xs.shape)

        xp, xs = ffn(xp, xs, 0)
        if kind == 0:
            prm = ((gdn_w_in16, (j,)), gdn_conv_w[j], gdn_a_log[j], gdn_dt_bias[j], gdn_norm_w[j],
                   (gdn_w_out16, (j,)))
            buf0 = jnp.zeros((bp,) + state_a_conv.shape[2:], F32)
            s0 = jnp.zeros((bp,) + state_a_ssm.shape[2:], F32)
            xp, buf, s = _gdn_mixer(xp, mod_p, nw[1:2], buf0, s0, *prm, chunk=2 * GDN_CHUNK, **tiles_p)
            outs["a_conv_p"].append(buf)
            outs["a_ssm_p"].append(s)
            xs, buf, s = _gdn_mixer(xs, mod_s, nw[1:2], state_a_conv[j], state_a_ssm[j], *prm, chunk=GDN_CHUNK,
                                    **tiles_s)
            outs["a_conv_s"].append(buf)
            outs["a_ssm_s"].append(s)
        elif kind == 1:
            w_in, w_out = (dsa_w_in16, (j,)), (dsa_w_out16, (j,))
            n_proj = dsa_w_in.shape[2]
            xp, k, v, ki = _dsa_mixer(xp, mod_p, nw[1:2], w_in, w_out, n_proj, nkv=nkv, **tiles_p)
            outs["b_k_p"].append(k)
            outs["b_v_p"].append(v)
            outs["b_i_p"].append(ki)
            xs, k, v, ki = _dsa_mixer(xs, mod_s, nw[1:2], w_in, w_out, n_proj, nkv=nkv,
                                      cache=(cache_b_k, cache_b_v, cache_b_idx, j, page_table), **tiles_s)
            outs["b_k_s"].append(k)
            outs["b_v_s"].append(v)
            outs["b_i_s"].append(ki)
        else:
            prm = ((lru_w_in16, (j,)), lru_conv_w[j], lru_conv_b[j], lru_w_gate_a[j].astype(BF16),
                   lru_b_gate_a[j], lru_w_gate_x[j].astype(BF16), lru_b_gate_x[j], lru_lambda[j],
                   (lru_w_out16, (j,)))
            buf0 = jnp.zeros((bp,) + state_c_conv.shape[2:], F32)
            h0 = jnp.zeros((bp,) + state_c_h.shape[2:], F32)
            xp, buf, hl = _lru_mixer(xp, mod_p, nw[1:2], buf0, h0, *prm, **tiles_p)
            outs["c_conv_p"].append(buf)
            outs["c_h_p"].append(hl)
            xs, buf, hl = _lru_mixer(xs, mod_s, nw[1:2], state_c_conv[j], state_c_h[j], *prm, **tiles_s)
            outs["c_conv_s"].append(buf)
            outs["c_h_s"].append(hl)
        xp, xs = ffn(xp, xs, 1)

    modf_p, modf_s = groups(mod_f)
    fnw = final_norm_w.reshape(1, d)
    y_p = _final_norm(xp.reshape(bp * seq, d), modf_p, fnw, tm=tm_p).reshape(bp, seq, d)
    y_s = _final_norm(xs.reshape(bs * ts, d), modf_s, fnw, tm=tm_s).reshape(bs, ts, d)
    st = {k: jnp.stack(v) for k, v in outs.items()}
    return (y_p, y_s, st["a_conv_p"], st["a_conv_s"], st["a_ssm_p"], st["a_ssm_s"], st["b_k_p"], st["b_k_s"],
            st["b_v_p"], st["b_v_s"], st["b_i_p"], st["b_i_s"], st["c_conv_p"], st["c_conv_s"], st["c_h_p"],
            st["c_h_s"])
```
